```python
import jax
import jax.numpy as jnp
from jax import lax
import numpy as np

D_MODEL = 1024
BATCH = 2
SEQ = 16384
DEPTH = 2

CTX_LEN = 256
GRID_W = 64
BLOCK = 128
HEAD_DIM = 64
ROPE_THETA = 10000.0
EPS = 1e-6
ML_HEADS = 4
ML_DIM = 128
ML_WIDTH = ML_HEADS * ML_DIM
WIN_Q_HEADS = 8
WIN_KV_HEADS = 2
WINDOW = 128
WIN_WIDTH = WIN_Q_HEADS * HEAD_DIM
WIN_KV_WIDTH = WIN_KV_HEADS * HEAD_DIM
GLB_Q_HEADS = 8
GLB_KV_HEADS = 2
GLB_WIDTH = GLB_Q_HEADS * HEAD_DIM
GLB_KV_WIDTH = GLB_KV_HEADS * HEAD_DIM
N_BRANCH = 3
BRANCH_WIDTH = 512
D_FF = ((8 * D_MODEL + 3 * 256 - 1) // (3 * 256)) * 256
SPLITS = (ML_WIDTH, ML_WIDTH, ML_WIDTH, ML_WIDTH, 2 * ML_HEADS, 2 * ML_HEADS,
          WIN_WIDTH, WIN_KV_WIDTH, WIN_KV_WIDTH,
          GLB_WIDTH, GLB_KV_WIDTH, GLB_KV_WIDTH,
          N_BRANCH * D_MODEL)
D_IN = sum(SPLITS)
ML_F_OFF = 4 * ML_WIDTH + 2 * ML_HEADS

kernel_name = 'hybrid_mlstm_swa_axial_gqa_dit'


def rmsnorm(x, g):
    xf = x.astype(jnp.float32)
    y = xf * lax.rsqrt(jnp.mean(xf * xf, axis=-1, keepdims=True) + EPS)
    return (y * g.astype(jnp.float32)).astype(x.dtype)


def modulate(h, shift, scale):
    return h * (1 + scale) + shift


def split_cols(p):
    idx = np.cumsum(SPLITS)[:-1].tolist()
    return jnp.split(p, idx, axis=-1)


def split_heads(a, n_heads):
    b, t, _ = a.shape
    return a.reshape(b, t, n_heads, -1).transpose(0, 2, 1, 3)


def merge_heads(a):
    b, h, t, d = a.shape
    return a.transpose(0, 2, 1, 3).reshape(b, t, h * d)


def axial_angles(n_tok):
    rows = n_tok // GRID_W
    row = jnp.broadcast_to(jnp.arange(rows)[:, None], (rows, GRID_W)).reshape(-1)
    col = jnp.broadcast_to(jnp.arange(GRID_W)[None, :], (rows, GRID_W)).reshape(-1)
    half = HEAD_DIM // 2
    inv = ROPE_THETA ** (-jnp.arange(0, half, 2, dtype=jnp.float32) / half)
    return row.astype(jnp.float32)[:, None] * inv, col.astype(jnp.float32)[:, None] * inv


def rope_1d(x, ang):
    x1, x2 = jnp.split(x, 2, axis=-1)
    cos = jnp.cos(ang).astype(x.dtype)
    sin = jnp.sin(ang).astype(x.dtype)
    return jnp.concatenate([x1 * cos - x2 * sin, x1 * sin + x2 * cos], axis=-1)


def rope_axial(x, ang_row, ang_col):
    xr, xc = jnp.split(x, 2, axis=-1)
    return jnp.concatenate([rope_1d(xr, ang_row), rope_1d(xc, ang_col)], axis=-1)


def zero_state(b, h, d):
    return (jnp.zeros((b, h, d, d), jnp.float32), jnp.zeros((b, h, d), jnp.float32),
            jnp.zeros((b, h), jnp.float32))


def mlstm_scan(q, k, v, ig, lf, state):
    b, h, t, d = q.shape
    nc = t // BLOCK
    out_dtype = q.dtype

    def to_chunks(a):
        a = a.astype(jnp.float32)
        return jnp.moveaxis(a.reshape(b, h, nc, BLOCK, *a.shape[3:]), 2, 0)

    lower = jnp.tril(jnp.ones((BLOCK, BLOCK), bool))
    scale = d ** -0.5

    def step(carry, xs):
        C, n, m = carry
        qc, kc, vc, ic, fc = xs
        qc = qc * scale
        bcum = jnp.cumsum(fc, axis=-1)
        dlog = jnp.where(lower, bcum[..., :, None] - bcum[..., None, :] + ic[..., None, :], -jnp.inf)
        inter = bcum + m[..., None]
        m_t = jnp.maximum(inter, jnp.max(dlog, axis=-1))
        w = jnp.exp(dlog - m_t[..., None])
        a = jnp.exp(inter - m_t)
        s = jnp.einsum('bhtd,bhsd->bhts', qc, kc) * w
        num = a[..., None] * jnp.einsum('bhtd,bhde->bhte', qc, C) + jnp.einsum('bhts,bhse->bhte', s, vc)
        den = a * jnp.einsum('bhtd,bhd->bht', qc, n) + jnp.sum(s, axis=-1)
        hc = num / jnp.maximum(jnp.abs(den), jnp.exp(-m_t))[..., None]
        b_end = bcum[..., -1]
        wlog = b_end[..., None] - bcum + ic
        m_new = jnp.maximum(b_end + m, jnp.max(wlog, axis=-1))
        a_c = jnp.exp(b_end + m - m_new)
        wk = jnp.exp(wlog - m_new[..., None])
        C_new = a_c[..., None, None] * C + jnp.einsum('bhs,bhsd,bhse->bhde', wk, kc, vc)
        n_new = a_c[..., None] * n + jnp.einsum('bhs,bhsd->bhd', wk, kc)
        return (C_new, n_new, m_new), hc

    xs = (to_chunks(q), to_chunks(k), to_chunks(v), to_chunks(ig), to_chunks(lf))
    state, hs = lax.scan(step, state, xs)
    return jnp.moveaxis(hs, 0, 2).reshape(b, h, t, d).astype(out_dtype), state


def flip_if(a, rev):
    return jnp.flip(a, axis=2) if rev else a


def gate_dirs(a):
    b, t, _ = a.shape
    return a.reshape(b, t, 2, ML_HEADS).transpose(2, 0, 3, 1)


def mlstm_branch(lat, ctx, norm_g, ctx_out):
    def prep(parts):
        q, k, v, _, ig, fg = parts
        return (split_heads(q, ML_HEADS), split_heads(k, ML_HEADS), split_heads(v, ML_HEADS),
                gate_dirs(ig), jax.nn.log_sigmoid(gate_dirs(fg).astype(jnp.float32)))

    lq, lk, lv, lig, llf = prep(lat)
    cq, ck, cv, cig, clf = prep(ctx)
    b = lq.shape[0]
    h_lat, h_ctx = [], []
    for dr, rev in enumerate((False, True)):
        hc, st = mlstm_scan(flip_if(cq, rev), flip_if(ck, rev), flip_if(cv, rev),
                            flip_if(cig[dr], rev), flip_if(clf[dr], rev), zero_state(b, ML_HEADS, ML_DIM))
        hl, _ = mlstm_scan(flip_if(lq, rev), flip_if(lk, rev), flip_if(lv, rev),
                           flip_if(lig[dr], rev), flip_if(llf[dr], rev), st)
        h_ctx.append(flip_if(hc, rev))
        h_lat.append(flip_if(hl, rev))

    def finish(hs, o):
        return merge_heads(rmsnorm(hs[0] + hs[1], norm_g[:, None, :])) * jax.nn.sigmoid(o)

    y_lat = finish(h_lat, lat[3])
    y_ctx = finish(h_ctx, ctx[3]) if ctx_out else None
    return y_lat, y_ctx


def window_attention(q, k, v, k_ctx, v_ctx, sink):
    b, hq, n_tok, d = q.shape
    g = k.shape[1]
    r = hq // g
    nb = n_tok // BLOCK
    n_ctx = k_ctx.shape[2]
    scale = d ** -0.5
    qb = jnp.moveaxis(q.reshape(b, g, r, nb, BLOCK, d), 3, 0)

    def band(a):
        ap = jnp.pad(a, ((0, 0), (0, 0), (BLOCK, BLOCK), (0, 0))).reshape(b, g, nb + 2, BLOCK, d)
        a3 = jnp.concatenate([ap[:, :, :-2], ap[:, :, 1:-1], ap[:, :, 2:]], axis=3)
        return jnp.moveaxis(a3, 2, 0)

    kb, vb = band(k), band(v)
    qpos = jnp.arange(n_tok).reshape(nb, BLOCK)
    kpos = jnp.arange(nb)[:, None] * BLOCK - BLOCK + jnp.arange(3 * BLOCK)[None, :]
    valid = ((jnp.abs(qpos[:, :, None] - kpos[:, None, :]) <= WINDOW)
             & (kpos[:, None, :] >= 0) & (kpos[:, None, :] < n_tok))
    sink_g = sink.astype(jnp.float32).reshape(1, g, r, 1, 1)

    def one_block(args):
        qblk, kblk, vblk, ok = args
        s_loc = jnp.where(ok, jnp.einsum('bgrqd,bgkd->bgrqk', qblk, kblk).astype(jnp.float32) * scale, -jnp.inf)
        s_ctx = jnp.einsum('bgrqd,bgkd->bgrqk', qblk, k_ctx).astype(jnp.float32) * scale
        s_snk = jnp.broadcast_to(sink_g, s_ctx.shape[:-1] + (1,))
        p = jax.nn.softmax(jnp.concatenate([s_loc, s_ctx, s_snk], axis=-1), axis=-1).astype(v.dtype)
        return (jnp.einsum('bgrqk,bgkd->bgrqd', p[..., :3 * BLOCK], vblk)
                + jnp.einsum('bgrqk,bgkd->bgrqd', p[..., 3 * BLOCK:3 * BLOCK + n_ctx], v_ctx))

    o = lax.map(one_block, (qb, kb, vb, valid))
    return jnp.moveaxis(o, 0, 3).reshape(b, hq, n_tok, d)


def global_attention(q, k_all, v_all):
    b, hq, n_tok, d = q.shape
    g = k_all.shape[1]
    r = hq // g
    nb = n_tok // BLOCK
    scale = d ** -0.5
    qb = jnp.moveaxis(q.reshape(b, g, r, nb, BLOCK, d), 3, 0)

    def one_block(qblk):
        s = jnp.einsum('bgrqd,bgkd->bgrqk', qblk, k_all).astype(jnp.float32) * scale
        p = jax.nn.softmax(s, axis=-1).astype(v_all.dtype)
        return jnp.einsum('bgrqk,bgkd->bgrqd', p, v_all)

    o = lax.map(one_block, qb)
    return jnp.moveaxis(o, 0, 3).reshape(b, hq, n_tok, d)


def ctx_attention(q, k, v, sink):
    b, hq, n_ctx, d = q.shape
    g = k.shape[1]
    r = hq // g
    s = jnp.einsum('bgrqd,bgkd->bgrqk', q.reshape(b, g, r, n_ctx, d), k).astype(jnp.float32) * d ** -0.5
    if sink is not None:
        s_snk = jnp.broadcast_to(sink.astype(jnp.float32).reshape(1, g, r, 1, 1), s.shape[:-1] + (1,))
        s = jnp.concatenate([s, s_snk], axis=-1)
    p = jax.nn.softmax(s, axis=-1)[..., :n_ctx].astype(v.dtype)
    return jnp.einsum('bgrqk,bgkd->bgrqd', p, v).reshape(b, hq, n_ctx, d)


def merge_branches(ys, gate_logits, w_br, w_o):
    gates = jnp.split(gate_logits, N_BRANCH, axis=-1)
    merged = jax.nn.sigmoid(gates[0]) * (ys[0] @ w_br[0])
    for i in range(1, N_BRANCH):
        merged = merged + jax.nn.sigmoid(gates[i]) * (ys[i] @ w_br[i])
    return merged @ w_o


def token_mixers(hx, hc, w_in, b_in, ml_norm_g, win_sink, qn_g, kn_g, w_br, w_o, ang_row, ang_col, ctx_out):
    px = split_cols(hx @ w_in + b_in)
    pc = split_cols(hc @ w_in + b_in)
    ya_x, ya_c = mlstm_branch(px[0:6], pc[0:6], ml_norm_g, ctx_out)
    wq_x = rope_axial(split_heads(px[6], WIN_Q_HEADS), ang_row, ang_col)
    wk_x = rope_axial(split_heads(px[7], WIN_KV_HEADS), ang_row, ang_col)
    wv_x = split_heads(px[8], WIN_KV_HEADS)
    wk_c = split_heads(pc[7], WIN_KV_HEADS)
    wv_c = split_heads(pc[8], WIN_KV_HEADS)
    yb_x = merge_heads(window_attention(wq_x, wk_x, wv_x, wk_c, wv_c, win_sink))
    gq_x = rope_axial(rmsnorm(split_heads(px[9], GLB_Q_HEADS), qn_g), ang_row, ang_col)
    gk_x = rope_axial(rmsnorm(split_heads(px[10], GLB_KV_HEADS), kn_g), ang_row, ang_col)
    gk_c = rmsnorm(split_heads(pc[10], GLB_KV_HEADS), kn_g)
    gv_c = split_heads(pc[11], GLB_KV_HEADS)
    k_all = jnp.concatenate([gk_c, gk_x], axis=2)
    v_all = jnp.concatenate([gv_c, split_heads(px[11], GLB_KV_HEADS)], axis=2)
    yc_x = merge_heads(global_attention(gq_x, k_all, v_all))
    out_x = merge_branches((ya_x, yb_x, yc_x), px[12], w_br, w_o)
    if not ctx_out:
        return out_x, None
    yb_c = merge_heads(ctx_attention(split_heads(pc[6], WIN_Q_HEADS), wk_c, wv_c, win_sink))
    yc_c = merge_heads(ctx_attention(rmsnorm(split_heads(pc[9], GLB_Q_HEADS), qn_g), gk_c, gv_c, None))
    out_c = merge_branches((ya_c, yb_c, yc_c), pc[12], w_br, w_o)
    return out_x, out_c


def swiglu(h, w1, w3, w2):
    return (jax.nn.silu(h @ w1) * (h @ w3)) @ w2


def setup_inputs(seed: int = 0) -> dict:
    key = jax.random.key(seed)
    ks = jax.random.split(key, 24)
    f32 = jnp.float32

    def lin(k, shape, fan_in):
        return jax.random.normal(k, shape, f32) * fan_in ** -0.5

    def gain(k, shape):
        return 1.0 + 0.05 * jax.random.normal(k, shape, f32)

    b_in = 0.02 * jax.random.normal(ks[8], (DEPTH, D_IN), f32)
    f_bias = 3.0 + 3.0 * jax.random.uniform(ks[9], (DEPTH, 2 * ML_HEADS), f32)
    b_in = b_in.at[:, ML_F_OFF:ML_F_OFF + 2 * ML_HEADS].set(f_bias)
    return {
        'x': jax.random.normal(ks[0], (BATCH, SEQ, D_MODEL), f32),
        'c': jax.random.normal(ks[1], (BATCH, D_MODEL), f32),
        'ctx': jax.random.normal(ks[2], (BATCH, CTX_LEN, D_MODEL), f32),
        'c_ctx': jax.random.normal(ks[3], (D_MODEL,), f32),
        'w_mod': lin(ks[4], (DEPTH, D_MODEL, 6 * D_MODEL), D_MODEL),
        'b_mod': 0.02 * jax.random.normal(ks[5], (DEPTH, 6 * D_MODEL), f32),
        'norm1_g': gain(ks[6], (DEPTH, D_MODEL)),
        'w_in': lin(ks[7], (DEPTH, D_MODEL, D_IN), D_MODEL),
        'b_in': b_in,
        'ml_norm_g': gain(ks[10], (DEPTH, ML_HEADS, ML_DIM)),
        'win_sink': jax.random.normal(ks[11], (DEPTH, WIN_Q_HEADS), f32),
        'qn_g': gain(ks[12], (DEPTH, HEAD_DIM)),
        'kn_g': gain(ks[13], (DEPTH, HEAD_DIM)),
        'w_br': lin(ks[14], (DEPTH, N_BRANCH, BRANCH_WIDTH, D_MODEL), BRANCH_WIDTH),
        'w_o': lin(ks[15], (DEPTH, D_MODEL, D_MODEL), D_MODEL),
        'norm2_g': gain(ks[16], (DEPTH, D_MODEL)),
        'w_ff1': lin(ks[17], (DEPTH, D_MODEL, D_FF), D_MODEL),
        'w_ff3': lin(ks[18], (DEPTH, D_MODEL, D_FF), D_MODEL),
        'w_ff2': lin(ks[19], (DEPTH, D_FF, D_MODEL), D_FF),
        'final_g': gain(ks[20], (D_MODEL,)),
    }


def reference(x, c, ctx, c_ctx, w_mod, b_mod, norm1_g, w_in, b_in, ml_norm_g, win_sink, qn_g, kn_g,
              w_br, w_o, norm2_g, w_ff1, w_ff3, w_ff2, final_g):
    ang_row, ang_col = axial_angles(x.shape[1])
    xc = ctx
    for l in range(DEPTH):
        ctx_out = l < DEPTH - 1
        mod_x = jnp.split((jax.nn.silu(c) @ w_mod[l] + b_mod[l])[:, None, :], 6, axis=-1)
        mod_c = jnp.split(jax.nn.silu(c_ctx) @ w_mod[l] + b_mod[l], 6, axis=-1)
        hx = modulate(rmsnorm(x, norm1_g[l]), mod_x[0], mod_x[1])
        hc = modulate(rmsnorm(xc, norm1_g[l]), mod_c[0], mod_c[1])
        out_x, out_c = token_mixers(hx, hc, w_in[l], b_in[l], ml_norm_g[l], win_sink[l], qn_g[l], kn_g[l],
                                    w_br[l], w_o[l], ang_row, ang_col, ctx_out)
        x = x + mod_x[2] * out_x
        hx = modulate(rmsnorm(x, norm2_g[l]), mod_x[3], mod_x[4])
        x = x + mod_x[5] * swiglu(hx, w_ff1[l], w_ff3[l], w_ff2[l])
        if ctx_out:
            xc = xc + mod_c[2] * out_c
            hc = modulate(rmsnorm(xc, norm2_g[l]), mod_c[3], mod_c[4])
            xc = xc + mod_c[5] * swiglu(hc, w_ff1[l], w_ff3[l], w_ff2[l])
    return rmsnorm(x, final_g)
```

```python
import functools

import jax
import jax.numpy as jnp
import numpy as np
from jax import lax
from jax.experimental import pallas as pl
from jax.experimental.pallas import tpu as pltpu

F32 = jnp.float32
BF16 = jnp.bfloat16

D_MODEL = 1024
GRID_W = 64
CHUNK = 128
HEAD_DIM = 64
ROPE_THETA = 10000.0
EPS = 1e-6
ML_HEADS = 4
ML_DIM = 128
Q_HEADS = 8
KV_HEADS = 2
WIDTH = 512
N_GATE = 2 * ML_HEADS

LANES = 128
ROW_BLOCK = 256
ROW_TILE = 512
VMEM_LIMIT = 56 * 1024 * 1024

C_MLQ, C_MLK, C_MLV, C_MLO = 0, 512, 1024, 1536
C_WQ, C_WK, C_WV = 2048, 2560, 2816
C_GQ, C_GK, C_GV = 3072, 3584, 3840
C_GATE = 4096
N_PROJ = C_GATE + 3 * D_MODEL

NT_DIMS = (((1,), (1,)), ((), ()))


def _params(sem, vmem=VMEM_LIMIT):
    return pltpu.CompilerParams(dimension_semantics=sem, vmem_limit_bytes=vmem)


def _resident(shape):
    nd = len(shape)
    return pl.BlockSpec(shape, lambda *_: (0,) * nd, pipeline_mode=pl.Buffered(1))


def _sigmoid(x):
    return 1.0 / (1.0 + jnp.exp(-x))


def _log_sigmoid(x):
    return jnp.minimum(x, 0.0) - jnp.log(1.0 + jnp.exp(-jnp.abs(x)))


def _mod_kernel(c_ref, w_ref, b_ref, o_ref):
    c = c_ref[...]
    s = c * _sigmoid(c)
    o_ref[...] = jnp.dot(s, w_ref[...], preferred_element_type=F32) + b_ref[...]


def _mod_vectors(cvec, w_mod, b_mod):
    n_out = w_mod.shape[1]
    tn = 1536
    return pl.pallas_call(
        _mod_kernel,
        out_shape=jax.ShapeDtypeStruct((cvec.shape[0], n_out), F32),
        grid=(n_out // tn,),
        in_specs=[pl.BlockSpec(cvec.shape, lambda j: (0, 0)),
                  pl.BlockSpec((D_MODEL, tn), lambda j: (0, j)),
                  pl.BlockSpec((1, tn), lambda j: (0, j))],
        out_specs=pl.BlockSpec((cvec.shape[0], tn), lambda j: (0, j)),
        compiler_params=_params(("parallel",)),
        name="mod_vectors",
    )(cvec, w_mod, b_mod.reshape(1, n_out))


def _rope(acc, cos, sin, first_half):
    w = acc.shape[1]
    reps = w // LANES
    if reps > 1:
        cos = jnp.concatenate([cos] * reps, axis=1)
        sin = jnp.concatenate([sin] * reps, axis=1)
    ahead = pltpu.roll(acc, w - 16, axis=1)
    behind = pltpu.roll(acc, 16, axis=1)
    return acc * cos + jnp.where(first_half, ahead, behind) * sin


def _head_rms(acc, avg, gain):
    sq = acc * acc
    hi = sq.astype(BF16)
    lo = (sq - hi.astype(F32)).astype(BF16)
    outs = []
    for t in range(acc.shape[1] // LANES):
        sl = slice(t * LANES, (t + 1) * LANES)
        ms = (jnp.dot(hi[:, sl], avg, preferred_element_type=F32)
              + jnp.dot(lo[:, sl], avg, preferred_element_type=F32))
        outs.append(acc[:, sl] * lax.rsqrt(ms + EPS) * gain)
    return jnp.concatenate(outs, axis=1)


def _inproj_kernel(x_ref, mod_ref, g1_ref, w_ref, b_ref, wg_ref, bg_ref, wgt_ref, bgt_ref,
                   cos_ref, sin_ref, qg_ref, kg_ref, avg_ref,
                   p_ref, g_ref, gt_ref, h_ref):
    tm = x_ref.shape[0]
    for sb in range(tm // ROW_BLOCK):
        r0 = sb * ROW_BLOCK
        xs = x_ref[r0:r0 + ROW_BLOCK, :]
        ms = jnp.mean(xs * xs, axis=-1, keepdims=True)
        y = xs * lax.rsqrt(ms + EPS) * g1_ref[...]
        shift = mod_ref[sb, 0:1, :]
        scale = mod_ref[sb, 1:2, :]
        h_ref[r0:r0 + ROW_BLOCK, :] = (y * (1.0 + scale) + shift).astype(BF16)

    lane = lax.broadcasted_iota(jnp.int32, (1, WIDTH), 1)
    first_half = (lane % 32) < 16
    avg = avg_ref[...]
    qg = qg_ref[...]
    kg = kg_ref[...]
    q_scale = HEAD_DIM ** -0.5

    for sb in range(tm // ROW_BLOCK):
        r0 = sb * ROW_BLOCK
        rows = slice(r0, r0 + ROW_BLOCK)
        h = h_ref[rows, :]
        cos = cos_ref[rows, :]
        sin = sin_ref[rows, :]

        def proj(c0, width):
            return (jnp.dot(h, w_ref[:, c0:c0 + width], preferred_element_type=F32)
                    + b_ref[:, c0:c0 + width])

        for c0 in (C_MLQ, C_MLK, C_MLV, C_MLO):
            p_ref[rows, c0:c0 + WIDTH] = proj(c0, WIDTH).astype(BF16)
        wq = _rope(proj(C_WQ, WIDTH), cos, sin, first_half) * q_scale
        p_ref[rows, C_WQ:C_WQ + WIDTH] = wq.astype(BF16)
        wk = _rope(proj(C_WK, 256), cos, sin, first_half[:, :256])
        p_ref[rows, C_WK:C_WK + 256] = wk.astype(BF16)
        p_ref[rows, C_WV:C_WV + 256] = proj(C_WV, 256).astype(BF16)
        gq = _rope(_head_rms(proj(C_GQ, WIDTH), avg, qg), cos, sin, first_half) * q_scale
        p_ref[rows, C_GQ:C_GQ + WIDTH] = gq.astype(BF16)
        gk = _rope(_head_rms(proj(C_GK, 256), avg, kg), cos, sin, first_half[:, :256])
        p_ref[rows, C_GK:C_GK + 256] = gk.astype(BF16)
        p_ref[rows, C_GV:C_GV + 256] = proj(C_GV, 256).astype(BF16)
        for c0 in range(C_GATE, N_PROJ, WIDTH):
            p_ref[rows, c0:c0 + WIDTH] = proj(c0, WIDTH).astype(BF16)
        ga = jnp.dot(h, wg_ref[...], preferred_element_type=F32) + bg_ref[...]
        glane = lax.broadcasted_iota(jnp.int32, (1, LANES), 1)
        g_ref[rows, :] = jnp.where((glane >= N_GATE) & (glane < 2 * N_GATE), _log_sigmoid(ga), ga)

    gta = lax.dot_general(wgt_ref[...], h_ref[...], NT_DIMS, preferred_element_type=F32) + bgt_ref[...]
    grow = lax.broadcasted_iota(jnp.int32, (2 * N_GATE, 1), 0)
    gt_ref[...] = jnp.where(grow >= N_GATE, _log_sigmoid(gta), gta)


def _in_projection(x, modtab, g1, w, b, wg, bg, wgt, bgt, cos, sin, qg, kg, avg):
    rows = x.shape[0]
    tm = ROW_TILE
    nb = tm // ROW_BLOCK
    return pl.pallas_call(
        _inproj_kernel,
        out_shape=(jax.ShapeDtypeStruct((rows, N_PROJ), BF16),
                   jax.ShapeDtypeStruct((rows, LANES), F32),
                   jax.ShapeDtypeStruct((2 * N_GATE, rows), F32)),
        grid=(rows // tm,),
        in_specs=[pl.BlockSpec((tm, D_MODEL), lambda i: (i, 0)),
                  pl.BlockSpec((nb, 6, D_MODEL), lambda i: (i, 0, 0)),
                  _resident((1, D_MODEL)),
                  _resident((D_MODEL, N_PROJ)),
                  _resident((1, N_PROJ)),
                  _resident((D_MODEL, LANES)),
                  _resident((1, LANES)),
                  _resident((2 * N_GATE, D_MODEL)),
                  _resident((2 * N_GATE, 1)),
                  pl.BlockSpec((tm, LANES), lambda i: (i, 0)),
                  pl.BlockSpec((tm, LANES), lambda i: (i, 0)),
                  _resident((1, LANES)),
                  _resident((1, LANES)),
                  _resident((LANES, LANES))],
        out_specs=(pl.BlockSpec((tm, N_PROJ), lambda i: (i, 0)),
                   pl.BlockSpec((tm, LANES), lambda i: (i, 0)),
                   pl.BlockSpec((2 * N_GATE, tm), lambda i: (0, i))),
        scratch_shapes=[pltpu.VMEM((tm, D_MODEL), BF16)],
        compiler_params=_params(("parallel",)),
        name="in_projection",
    )(x, modtab, g1, w, b, wg, bg, wgt, bgt, cos, sin, qg, kg, avg)


def _mlstm_kernel(qkvf_ref, gf_ref, gtf_ref, qkvb_ref, gb_ref, gtb_ref, hf_ref, hb_ref,
                  c_ref, n_ref, m_ref):
    @pl.when(pl.program_id(1) == 0)
    def _():
        c_ref[...] = jnp.zeros_like(c_ref)
        n_ref[...] = jnp.zeros_like(n_ref)
        m_ref[...] = jnp.zeros_like(m_ref)

    L = CHUNK
    row = lax.broadcasted_iota(jnp.int32, (L, L), 0)
    col = lax.broadcasted_iota(jnp.int32, (L, L), 1)
    scale = ML_DIM ** -0.5
    dirs = ((qkvf_ref, gf_ref, gtf_ref, hf_ref, col <= row), (qkvb_ref, gb_ref, gtb_ref, hb_ref, col >= row))
    for d, (qkv_ref, g_ref, gt_ref, h_ref, seen) in enumerate(dirs):
        seen_t = (row <= col) if d == 0 else (row >= col)
        for hh in range(ML_HEADS):
            idx = d * ML_HEADS + hh
            q = qkv_ref[:, hh * ML_DIM:(hh + 1) * ML_DIM]
            k = qkv_ref[:, WIDTH + hh * ML_DIM:WIDTH + (hh + 1) * ML_DIM]
            v = qkv_ref[:, 2 * WIDTH + hh * ML_DIM:2 * WIDTH + (hh + 1) * ML_DIM]
            ic_col = g_ref[:, idx:idx + 1]
            lf_col = g_ref[:, N_GATE + idx:N_GATE + idx + 1]
            ic_row = gt_ref[idx:idx + 1, :]
            lf_row = gt_ref[N_GATE + idx:N_GATE + idx + 1, :]
            b_col = jnp.sum(jnp.where(seen, lf_row, 0.0), axis=1, keepdims=True)
            b_row = jnp.sum(jnp.where(seen_t, lf_col, 0.0), axis=0, keepdims=True)
            b_end = jnp.sum(lf_row, axis=1, keepdims=True)
            r_row = ic_row - b_row
            r_col = ic_col - b_col
            m_old = m_ref[idx]
            c_old = c_ref[idx]
            n_old = n_ref[idx]

            dlog = jnp.where(seen, b_col + r_row, -jnp.inf)
            inter = b_col + m_old
            m_t = jnp.maximum(inter, jnp.max(dlog, axis=1, keepdims=True))
            w = jnp.exp(dlog - m_t)
            a = jnp.exp(inter - m_t)
            qs = q.astype(F32) * scale
            qsb = qs.astype(BF16)
            s = lax.dot_general(qsb, k, NT_DIMS, preferred_element_type=F32) * w
            num = (a * jnp.dot(qsb, c_old.astype(BF16), preferred_element_type=F32)
                   + jnp.dot(s.astype(BF16), v, preferred_element_type=F32))
            den = (a * jnp.sum(qs * n_old, axis=1, keepdims=True)
                   + jnp.sum(s, axis=1, keepdims=True))
            hc = num / jnp.maximum(jnp.abs(den), jnp.exp(-m_t))
            h_ref[:, hh * ML_DIM:(hh + 1) * ML_DIM] = hc.astype(h_ref.dtype)

            wlog = b_end + r_col
            m_new = jnp.maximum(b_end + m_old, jnp.max(wlog, axis=0, keepdims=True))
            a_c = jnp.exp(b_end + m_old - m_new)
            kw = k.astype(F32) * jnp.exp(wlog - m_new)
            c_ref[idx] = a_c * c_old + jnp.dot(kw.T.astype(BF16), v, preferred_element_type=F32)
            n_ref[idx] = a_c * n_old + jnp.sum(kw, axis=0, keepdims=True)
            m_ref[idx] = m_new


def _mlstm(p, g, gt, batch, n_lat, n_ctx):
    rows = p.shape[0]
    cl, cc = n_lat // CHUNK, n_ctx // CHUNK
    lat_base, ctx_base = 0, batch * cl

    def fwd_chunk(b, i):
        return jnp.where(i < cc, ctx_base + b * cc + i, lat_base + b * cl + (i - cc))

    def bwd_chunk(b, i):
        return jnp.where(i < cc, ctx_base + b * cc + (cc - 1 - i), lat_base + b * cl + (cl - 1 - (i - cc)))

    def specs(chunk):
        return [pl.BlockSpec((CHUNK, 3 * WIDTH), lambda b, i: (chunk(b, i), 0)),
                pl.BlockSpec((CHUNK, LANES), lambda b, i: (chunk(b, i), 0)),
                pl.BlockSpec((2 * N_GATE, CHUNK), lambda b, i: (0, chunk(b, i)))]

    n_state = 2 * ML_HEADS
    return pl.pallas_call(
        _mlstm_kernel,
        out_shape=(jax.ShapeDtypeStruct((rows, WIDTH), BF16), jax.ShapeDtypeStruct((rows, WIDTH), BF16)),
        grid=(batch, cl + cc),
        in_specs=specs(fwd_chunk) + specs(bwd_chunk),
        out_specs=(pl.BlockSpec((CHUNK, WIDTH), lambda b, i: (fwd_chunk(b, i), 0)),
                   pl.BlockSpec((CHUNK, WIDTH), lambda b, i: (bwd_chunk(b, i), 0))),
        scratch_shapes=[pltpu.VMEM((n_state, ML_DIM, ML_DIM), F32),
                        pltpu.VMEM((n_state, 1, ML_DIM), F32),
                        pltpu.VMEM((n_state, 1, 1), F32)],
        compiler_params=_params(("parallel", "arbitrary")),
        name="mlstm_scan",
    )(p, g, gt, p, g, gt)


def _half_mask(e):
    lane = lax.broadcasted_iota(jnp.int32, (1, LANES), 1)
    return (lane < HEAD_DIM) if e == 0 else (lane >= HEAD_DIM)


def _win_kernel(sink_ref, q_ref, kvc_ref, kvp_ref, kvm_ref, kvn_ref, o_ref, *, n_lat_blocks):
    j = pl.program_id(1)
    L = CHUNK
    row = lax.broadcasted_iota(jnp.int32, (L, L), 0)
    col = lax.broadcasted_iota(jnp.int32, (L, L), 1)

    def attend(local):
        outs = []
        for t in range(Q_HEADS // 2):
            g = t // 2
            qt = q_ref[:, t * LANES:(t + 1) * LANES]
            srcs = [(kvc_ref, None)]
            if local:
                srcs += [(kvp_ref, (col >= row, j >= 1)),
                         (kvm_ref, None),
                         (kvn_ref, (col <= row, j <= n_lat_blocks - 2))]
            acc = None
            for e in range(2):
                half = _half_mask(e)
                qm = jnp.where(half, qt, jnp.zeros_like(qt))
                sink = sink_ref[2 * t + e]
                scores = []
                for ref, msk in srcs:
                    k = ref[:, g * LANES:(g + 1) * LANES]
                    s = lax.dot_general(qm, k, NT_DIMS, preferred_element_type=F32)
                    if msk is not None:
                        s = jnp.where(msk[0], s, -jnp.inf)
                        s = jnp.where(msk[1], s, -jnp.inf)
                    scores.append(s)
                m = jnp.maximum(functools.reduce(
                    jnp.maximum, [jnp.max(s, axis=1, keepdims=True) for s in scores]), sink)
                den = jnp.exp(sink - m)
                o = None
                for (ref, _), s in zip(srcs, scores):
                    pr = jnp.exp(s - m)
                    den = den + jnp.sum(pr, axis=1, keepdims=True)
                    v = ref[:, 2 * LANES + g * LANES:2 * LANES + (g + 1) * LANES]
                    vm = jnp.where(half, v, jnp.zeros_like(v))
                    pv = jnp.dot(pr.astype(BF16), vm, preferred_element_type=F32)
                    o = pv if o is None else o + pv
                o = o / den
                acc = o if acc is None else acc + o
            outs.append(acc)
        o_ref[...] = jnp.concatenate(outs, axis=1).astype(o_ref.dtype)

    @pl.when(j < n_lat_blocks)
    def _():
        attend(True)

    @pl.when(j >= n_lat_blocks)
    def _():
        attend(False)


def _window_attention(p, sink, batch, n_lat, n_ctx, ctx_queries):
    nl, nc = n_lat // CHUNK, n_ctx // CHUNK
    nq = nl + (nc if ctx_queries else 0)
    out_rows = batch * (n_lat + (n_ctx if ctx_queries else 0))
    ctx_base = batch * nl
    kv_col = C_WK // (2 * 256)

    def q_block(b, j):
        return jnp.where(j < nl, b * nl + j, ctx_base + b * nc + (j - nl))

    def near(off):
        return lambda b, j: (b * nl + jnp.clip(j + off, 0, nl - 1), kv_col)

    return pl.pallas_call(
        functools.partial(_win_kernel, n_lat_blocks=nl),
        out_shape=jax.ShapeDtypeStruct((out_rows, WIDTH), BF16),
        grid=(batch, nq),
        in_specs=[pl.BlockSpec(memory_space=pltpu.SMEM),
                  pl.BlockSpec((CHUNK, WIDTH), lambda b, j: (q_block(b, j), C_WQ // WIDTH)),
                  pl.BlockSpec((n_ctx, 2 * 256), lambda b, j: (batch * n_lat // n_ctx + b, kv_col)),
                  pl.BlockSpec((CHUNK, 2 * 256), near(-1)),
                  pl.BlockSpec((CHUNK, 2 * 256), near(0)),
                  pl.BlockSpec((CHUNK, 2 * 256), near(1))],
        out_specs=pl.BlockSpec((CHUNK, WIDTH), lambda b, j: (q_block(b, j), 0)),
        compiler_params=_params(("parallel", "parallel")),
        name="window_attention",
    )(sink, p, p, p, p, p)


def _glb_kernel(q_ref, kc_ref, vc_ref, kl_ref, vl_ref, o_ref, qs_ref, m_ref, l_ref, acc_ref,
                *, n_lat_tiles, tk):
    qi = pl.program_id(2)
    tq = q_ref.shape[0]
    halves = (_half_mask(0), _half_mask(1))
    for e in range(2):
        for t in range(2):
            qt = q_ref[:, t * LANES:(t + 1) * LANES]
            r0 = (2 * e + t) * tq
            qs_ref[r0:r0 + tq, :] = jnp.where(halves[e], qt, jnp.zeros_like(qt))
    m_ref[...] = jnp.full_like(m_ref, -jnp.inf)
    l_ref[...] = jnp.zeros_like(l_ref)
    acc_ref[...] = jnp.zeros_like(acc_ref)

    def step(k, v):
        s = lax.dot_general(qs_ref[...], k, NT_DIMS, preferred_element_type=F32)
        m_old = m_ref[...]
        m_new = jnp.maximum(m_old, jnp.max(s, axis=1, keepdims=True))
        alpha = jnp.exp(m_old - m_new)
        pr = jnp.exp(s - m_new)
        l_ref[...] = alpha * l_ref[...] + jnp.sum(pr, axis=1, keepdims=True)
        m_ref[...] = m_new
        prb = pr.astype(BF16)
        for e in range(2):
            rows = slice(2 * e * tq, 2 * (e + 1) * tq)
            vm = jnp.where(halves[e], v, jnp.zeros_like(v))
            acc_ref[rows, :] = (alpha[rows, :] * acc_ref[rows, :]
                                + jnp.dot(prb[rows, :], vm, preferred_element_type=F32))

    step(kc_ref[...], vc_ref[...])

    @pl.when(qi < n_lat_tiles)
    def _():
        def body(c, carry):
            off = pl.multiple_of(c * tk, tk)
            step(kl_ref[pl.ds(off, tk), :], vl_ref[pl.ds(off, tk), :])
            return carry
        lax.fori_loop(0, kl_ref.shape[0] // tk, body, 0)

    out = acc_ref[...] / l_ref[...]
    for t in range(2):
        o_ref[:, t * LANES:(t + 1) * LANES] = (
            out[t * tq:(t + 1) * tq, :] + out[(2 + t) * tq:(3 + t) * tq, :]).astype(o_ref.dtype)


def _global_attention(p, batch, n_lat, n_ctx, ctx_queries):
    tq, tk = 256, 512
    nl, nc = n_lat // tq, n_ctx // tq
    nq = nl + (nc if ctx_queries else 0)
    out_rows = batch * (n_lat + (n_ctx if ctx_queries else 0))
    ctx_base = batch * nl

    def q_block(b, qi):
        return jnp.where(qi < nl, b * nl + qi, ctx_base + b * nc + (qi - nl))

    ctx_blk = batch * n_lat // n_ctx
    return pl.pallas_call(
        functools.partial(_glb_kernel, n_lat_tiles=nl, tk=tk),
        out_shape=jax.ShapeDtypeStruct((out_rows, WIDTH), BF16),
        grid=(batch, KV_HEADS, nq),
        in_specs=[pl.BlockSpec((tq, 2 * LANES), lambda b, g, qi: (q_block(b, qi), C_GQ // 256 + g)),
                  pl.BlockSpec((n_ctx, LANES), lambda b, g, qi: (ctx_blk + b, C_GK // LANES + g)),
                  pl.BlockSpec((n_ctx, LANES), lambda b, g, qi: (ctx_blk + b, C_GV // LANES + g)),
                  pl.BlockSpec((n_lat, LANES), lambda b, g, qi: (b, C_GK // LANES + g)),
                  pl.BlockSpec((n_lat, LANES), lambda b, g, qi: (b, C_GV // LANES + g))],
        out_specs=pl.BlockSpec((tq, 2 * LANES), lambda b, g, qi: (q_block(b, qi), g)),
        scratch_shapes=[pltpu.VMEM((4 * tq, LANES), BF16),
                        pltpu.VMEM((4 * tq, 1), F32),
                        pltpu.VMEM((4 * tq, 1), F32),
                        pltpu.VMEM((4 * tq, LANES), F32)],
        compiler_params=_params(("parallel", "parallel", "arbitrary")),
        name="global_attention",
    )(p, p, p, p, p)


def _merge_kernel(x_ref, mod_ref, hf_ref, hb_ref, og_ref, yb_ref, yc_ref, ga_ref, gb_ref, gc_ref,
                  mlg_ref, wbr_ref, wo_ref, o_ref):
    tm = x_ref.shape[0]
    for sb in range(tm // ROW_BLOCK):
        rows = slice(sb * ROW_BLOCK, (sb + 1) * ROW_BLOCK)
        hs = hf_ref[rows, :].astype(F32) + hb_ref[rows, :].astype(F32)
        parts = []
        for t in range(ML_HEADS):
            ht = hs[:, t * ML_DIM:(t + 1) * ML_DIM]
            ms = jnp.mean(ht * ht, axis=-1, keepdims=True)
            parts.append(ht * lax.rsqrt(ms + EPS))
        ya = (jnp.concatenate(parts, axis=1) * mlg_ref[...]
              * _sigmoid(og_ref[rows, :].astype(F32))).astype(BF16)
        merged = None
        for y, gate_ref, i in ((ya, ga_ref, 0), (yb_ref[rows, :], gb_ref, 1), (yc_ref[rows, :], gc_ref, 2)):
            term = (_sigmoid(gate_ref[rows, :].astype(F32))
                    * jnp.dot(y, wbr_ref[i], preferred_element_type=F32))
            merged = term if merged is None else merged + term
        out = jnp.dot(merged.astype(BF16), wo_ref[...], preferred_element_type=F32)
        o_ref[rows, :] = x_ref[rows, :] + mod_ref[sb, 2:3, :] * out


def _merge(x, modtab, p, hf, hb, yb, yc, mlg, wbr, wo, n_rows):
    tm = ROW_TILE
    nb = tm // ROW_BLOCK
    row = lambda i: (i, 0)
    gate = lambda k: pl.BlockSpec((tm, D_MODEL), lambda i: (i, C_GATE // D_MODEL + k))
    return pl.pallas_call(
        _merge_kernel,
        out_shape=jax.ShapeDtypeStruct(x.shape, F32),
        grid=(n_rows // tm,),
        in_specs=[pl.BlockSpec((tm, D_MODEL), row),
                  pl.BlockSpec((nb, 6, D_MODEL), lambda i: (i, 0, 0)),
                  pl.BlockSpec((tm, WIDTH), row),
                  pl.BlockSpec((tm, WIDTH), row),
                  pl.BlockSpec((tm, WIDTH), lambda i: (i, C_MLO // WIDTH)),
                  pl.BlockSpec((tm, WIDTH), row),
                  pl.BlockSpec((tm, WIDTH), row),
                  gate(0), gate(1), gate(2),
                  _resident((1, WIDTH)),
                  _resident((3, WIDTH, D_MODEL)),
                  _resident((D_MODEL, D_MODEL))],
        out_specs=pl.BlockSpec((tm, D_MODEL), row),
        input_output_aliases={0: 0},
        compiler_params=_params(("parallel",)),
        name="merge",
    )(x, modtab, hf, hb, p, yb, yc, p, p, p, mlg, wbr, wo)


def _ffn_kernel(x_ref, mod_ref, g2_ref, w1_ref, w3_ref, w2_ref, gf_ref, o_ref, *, final):
    tm = x_ref.shape[0]
    d_ff = w1_ref.shape[1]
    half = d_ff // 2
    for sb in range(tm // ROW_BLOCK):
        rows = slice(sb * ROW_BLOCK, (sb + 1) * ROW_BLOCK)
        xs = x_ref[rows, :]
        ms = jnp.mean(xs * xs, axis=-1, keepdims=True)
        y = xs * lax.rsqrt(ms + EPS) * g2_ref[...]
        h = (y * (1.0 + mod_ref[sb, 4:5, :]) + mod_ref[sb, 3:4, :]).astype(BF16)
        out = None
        for c0 in (0, half):
            a = jnp.dot(h, w1_ref[:, c0:c0 + half], preferred_element_type=F32)
            b = jnp.dot(h, w3_ref[:, c0:c0 + half], preferred_element_type=F32)
            z = (a * _sigmoid(a) * b).astype(BF16)
            part = jnp.dot(z, w2_ref[c0:c0 + half, :], preferred_element_type=F32)
            out = part if out is None else out + part
        xn = xs + mod_ref[sb, 5:6, :] * out
        if final:
            ms = jnp.mean(xn * xn, axis=-1, keepdims=True)
            xn = xn * lax.rsqrt(ms + EPS) * gf_ref[...]
        o_ref[rows, :] = xn


def _ffn(x, modtab, g2, w1, w3, w2, gfin, n_rows, final):
    tm = ROW_TILE
    nb = tm // ROW_BLOCK
    d_ff = w1.shape[1]
    row = lambda i: (i, 0)
    out_rows = n_rows if final else x.shape[0]
    return pl.pallas_call(
        functools.partial(_ffn_kernel, final=final),
        out_shape=jax.ShapeDtypeStruct((out_rows, D_MODEL), F32),
        grid=(n_rows // tm,),
        in_specs=[pl.BlockSpec((tm, D_MODEL), row),
                  pl.BlockSpec((nb, 6, D_MODEL), lambda i: (i, 0, 0)),
                  _resident((1, D_MODEL)),
                  _resident((D_MODEL, d_ff)),
                  _resident((D_MODEL, d_ff)),
                  _resident((d_ff, D_MODEL)),
                  _resident((1, D_MODEL))],
        out_specs=pl.BlockSpec((tm, D_MODEL), row),
        input_output_aliases={} if final else {0: 0},
        compiler_params=_params(("parallel",)),
        name="ffn",
    )(x, modtab, g2, w1, w3, w2, gfin)


def _dup_halves(w, base):
    h0 = w[..., base:base + HEAD_DIM]
    h1 = w[..., base + HEAD_DIM:base + 2 * HEAD_DIM]
    return [h0, h0, h1, h1]


def _arrange_in_proj(w):
    o_wq = 4 * WIDTH + 2 * N_GATE
    o_wk, o_wv = o_wq + WIDTH, o_wq + WIDTH + 128
    o_gq = o_wv + 128
    o_gk, o_gv = o_gq + WIDTH, o_gq + WIDTH + 128
    o_gate = o_gv + 128
    main = jnp.concatenate(
        [w[..., :4 * WIDTH], w[..., o_wq:o_wq + WIDTH]]
        + _dup_halves(w, o_wk) + _dup_halves(w, o_wv)
        + [w[..., o_gq:o_gq + WIDTH]]
        + _dup_halves(w, o_gk) + _dup_halves(w, o_gv)
        + [w[..., o_gate:o_gate + 3 * D_MODEL]], axis=-1)
    gates = w[..., 4 * WIDTH:4 * WIDTH + 2 * N_GATE]
    return main, gates


def _rope_tables(batch, n_lat, n_ctx):
    t = jnp.arange(n_lat)
    quarter = HEAD_DIM // 4
    inv = ROPE_THETA ** (-jnp.arange(0, 2 * quarter, 2, dtype=F32) / (2 * quarter))
    ang_r = (t // GRID_W).astype(F32)[:, None] * inv
    ang_c = (t % GRID_W).astype(F32)[:, None] * inv
    cos = jnp.concatenate([jnp.cos(ang_r)] * 2 + [jnp.cos(ang_c)] * 2, axis=1)
    sin = jnp.concatenate([-jnp.sin(ang_r), jnp.sin(ang_r), -jnp.sin(ang_c), jnp.sin(ang_c)], axis=1)
    cos = jnp.tile(cos, (batch, LANES // HEAD_DIM))
    sin = jnp.tile(sin, (batch, LANES // HEAD_DIM))
    pad = batch * n_ctx
    return (jnp.concatenate([cos, jnp.ones((pad, LANES), F32)], axis=0),
            jnp.concatenate([sin, jnp.zeros((pad, LANES), F32)], axis=0))


def kernel(x, c, ctx, c_ctx, w_mod, b_mod, norm1_g, w_in, b_in, ml_norm_g, win_sink, qn_g, kn_g,
           w_br, w_o, norm2_g, w_ff1, w_ff3, w_ff2, final_g):
    batch, n_lat, d = x.shape
    n_ctx = ctx.shape[1]
    depth = w_mod.shape[0]
    assert d == D_MODEL and n_lat % ROW_TILE == 0 and n_lat % GRID_W == 0
    assert (batch * n_ctx) % ROW_TILE == 0 and n_ctx % ROW_BLOCK == 0 and (batch * n_lat) % n_ctx == 0
    lat_rows, ctx_rows = batch * n_lat, batch * n_ctx

    xs = jnp.concatenate([x.reshape(lat_rows, d), ctx.reshape(ctx_rows, d)], axis=0)
    cvec = jnp.concatenate([c, c_ctx[None, :], jnp.zeros((8 - batch - 1, d), F32)], axis=0)
    block_class = np.concatenate([np.repeat(np.arange(batch), n_lat // ROW_BLOCK),
                                  np.full(ctx_rows // ROW_BLOCK, batch)])
    cos, sin = _rope_tables(batch, n_lat, n_ctx)
    avg = jnp.asarray(np.kron(np.eye(LANES // HEAD_DIM), np.full((HEAD_DIM, HEAD_DIM), 1.0 / HEAD_DIM)), BF16)

    out = None
    for l in range(depth):
        last = l == depth - 1
        mod = _mod_vectors(cvec, w_mod[l], b_mod[l]).reshape(8, 6, d)
        modtab = mod[block_class]

        w_main, w_gate = _arrange_in_proj(w_in[l])
        b_main, b_gate = _arrange_in_proj(b_in[l][None, :])
        wg = jnp.pad(w_gate, ((0, 0), (0, LANES - 2 * N_GATE))).astype(BF16)
        bg = jnp.pad(b_gate, ((0, 0), (0, LANES - 2 * N_GATE)))
        p, g, gt = _in_projection(
            xs, modtab, norm1_g[l][None, :], w_main.astype(BF16), b_main, wg, bg,
            w_gate.T.astype(BF16), b_gate.T,
            cos, sin, jnp.tile(qn_g[l], 2)[None, :], jnp.tile(kn_g[l], 2)[None, :], avg)

        hf, hb = _mlstm(p, g, gt, batch, n_lat, n_ctx)
        yb = _window_attention(p, win_sink[l], batch, n_lat, n_ctx, ctx_queries=not last)
        yc = _global_attention(p, batch, n_lat, n_ctx, ctx_queries=not last)

        n_rows = lat_rows if last else lat_rows + ctx_rows
        xs = _merge(xs, modtab, p, hf, hb, yb, yc, ml_norm_g[l].reshape(1, WIDTH),
                    w_br[l].astype(BF16), w_o[l].astype(BF16), n_rows)
        out = _ffn(xs, modtab, norm2_g[l][None, :], w_ff1[l].astype(BF16), w_ff3[l].astype(BF16),
                   w_ff2[l].astype(BF16), final_g[None, :], n_rows, final=last)
        xs = out
    return out.reshape(batch, n_lat, d)
```

```python
import functools

import jax
import jax.numpy as jnp
import numpy as np
from jax import lax
from jax.experimental import pallas as pl
from jax.experimental.pallas import tpu as pltpu

F32 = jnp.float32
BF16 = jnp.bfloat16

D_MODEL = 1024
GRID_W = 64
CHUNK = 128
HEAD_DIM = 64
ROPE_THETA = 10000.0
EPS = 1e-6
ML_HEADS = 4
ML_DIM = 128
Q_HEADS = 8
KV_HEADS = 2
WIDTH = 512
N_GATE = 2 * ML_HEADS

LANES = 128
ROW_BLOCK = 256
ROW_TILE = 512
VMEM_LIMIT = 56 * 1024 * 1024

C_MLQ, C_MLK, C_MLV, C_MLO = 0, 512, 1024, 1536
C_WQ, C_WK, C_WV = 2048, 2560, 2816
C_GATE = 3072
C_GQ, C_GK = 6144, 6656
N_PROJ = C_GK + 256
LOG2E = 1.4426950408889634

NT_DIMS = (((1,), (1,)), ((), ()))


def _params(sem, vmem=VMEM_LIMIT):
    return pltpu.CompilerParams(dimension_semantics=sem, vmem_limit_bytes=vmem)


def _resident(shape):
    nd = len(shape)
    return pl.BlockSpec(shape, lambda *_: (0,) * nd, pipeline_mode=pl.Buffered(1))


def _sigmoid(x):
    return 1.0 / (1.0 + jnp.exp(-x))


def _log_sigmoid(x):
    return jnp.minimum(x, 0.0) - jnp.log(1.0 + jnp.exp(-jnp.abs(x)))


def _mod_kernel(c_ref, w_ref, b_ref, o_ref):
    c = c_ref[...]
    s = c * _sigmoid(c)
    o_ref[...] = jnp.dot(s, w_ref[...], preferred_element_type=F32) + b_ref[...]


def _mod_vectors(cvec, w_mod, b_mod):
    n_out = w_mod.shape[1]
    tn = 1536
    return pl.pallas_call(
        _mod_kernel,
        out_shape=jax.ShapeDtypeStruct((cvec.shape[0], n_out), F32),
        grid=(n_out // tn,),
        in_specs=[pl.BlockSpec(cvec.shape, lambda j: (0, 0)),
                  pl.BlockSpec((D_MODEL, tn), lambda j: (0, j)),
                  pl.BlockSpec((1, tn), lambda j: (0, j))],
        out_specs=pl.BlockSpec((cvec.shape[0], tn), lambda j: (0, j)),
        compiler_params=_params(("parallel",)),
        name="mod_vectors",
    )(cvec, w_mod, b_mod.reshape(1, n_out))


def _rope(acc, cos, sin, first_half):
    w = acc.shape[1]
    reps = w // LANES
    if reps > 1:
        cos = jnp.concatenate([cos] * reps, axis=1)
        sin = jnp.concatenate([sin] * reps, axis=1)
    ahead = pltpu.roll(acc, w - 16, axis=1)
    behind = pltpu.roll(acc, 16, axis=1)
    return acc * cos + jnp.where(first_half, ahead, behind) * sin


def _head_rms(acc, avg, gain):
    sq = acc * acc
    hi = sq.astype(BF16)
    lo = (sq - hi.astype(F32)).astype(BF16)
    outs = []
    for t in range(acc.shape[1] // LANES):
        sl = slice(t * LANES, (t + 1) * LANES)
        ms = (jnp.dot(hi[:, sl], avg, preferred_element_type=F32)
              + jnp.dot(lo[:, sl], avg, preferred_element_type=F32))
        outs.append(acc[:, sl] * lax.rsqrt(ms + EPS) * gain)
    return jnp.concatenate(outs, axis=1)


def _inproj_kernel(x_ref, mod_ref, g1_ref, w_ref, b_ref, wg_ref, bg_ref, wgt_ref, bgt_ref,
                   wvt_ref, bvt_ref, cos_ref, sin_ref, qg_ref, kg_ref, avg_ref,
                   p_ref, g_ref, gt_ref, vt_ref, h_ref):
    tm = x_ref.shape[0]
    for sb in range(tm // ROW_BLOCK):
        r0 = sb * ROW_BLOCK
        xs = x_ref[r0:r0 + ROW_BLOCK, :]
        ms = jnp.mean(xs * xs, axis=-1, keepdims=True)
        y = xs * lax.rsqrt(ms + EPS) * g1_ref[...]
        shift = mod_ref[sb, 0:1, :]
        scale = mod_ref[sb, 1:2, :]
        h_ref[r0:r0 + ROW_BLOCK, :] = (y * (1.0 + scale) + shift).astype(BF16)

    def first_half(width):
        return (lax.broadcasted_iota(jnp.int32, (1, width), 1) % 32) < 16

    avg = avg_ref[...]
    qg = qg_ref[...]
    kg = kg_ref[...]
    q_scale = HEAD_DIM ** -0.5

    for sb in range(tm // ROW_BLOCK):
        r0 = sb * ROW_BLOCK
        rows = slice(r0, r0 + ROW_BLOCK)
        h = h_ref[rows, :]
        cos = cos_ref[rows, :]
        sin = sin_ref[rows, :]

        def proj(c0, width):
            return (jnp.dot(h, w_ref[:, c0:c0 + width], preferred_element_type=F32)
                    + b_ref[:, c0:c0 + width])

        for c0 in (C_MLQ, C_MLK, C_MLV, C_MLO):
            p_ref[rows, c0:c0 + WIDTH] = proj(c0, WIDTH).astype(BF16)
        wq = _rope(proj(C_WQ, WIDTH), cos, sin, first_half(WIDTH)) * q_scale
        p_ref[rows, C_WQ:C_WQ + WIDTH] = wq.astype(BF16)
        wk = _rope(proj(C_WK, 256), cos, sin, first_half(256))
        p_ref[rows, C_WK:C_WK + 256] = wk.astype(BF16)
        p_ref[rows, C_WV:C_WV + 256] = proj(C_WV, 256).astype(BF16)
        gq = _rope(_head_rms(proj(C_GQ, WIDTH), avg, qg), cos, sin, first_half(WIDTH)) * (q_scale * LOG2E)
        p_ref[rows, C_GQ:C_GQ + WIDTH] = gq.astype(BF16)
        gk = _rope(_head_rms(proj(C_GK, 256), avg, kg), cos, sin, first_half(256))
        p_ref[rows, C_GK:C_GK + 256] = gk.astype(BF16)
        for c0 in range(C_GATE, C_GATE + 3 * D_MODEL, WIDTH):
            p_ref[rows, c0:c0 + WIDTH] = proj(c0, WIDTH).astype(BF16)
        ga = jnp.dot(h, wg_ref[...], preferred_element_type=F32) + bg_ref[...]
        glane = lax.broadcasted_iota(jnp.int32, (1, LANES), 1)
        g_ref[rows, :] = jnp.where((glane >= N_GATE) & (glane < 2 * N_GATE), _log_sigmoid(ga), ga)

    gta = lax.dot_general(wgt_ref[...], h_ref[...], NT_DIMS, preferred_element_type=F32) + bgt_ref[...]
    grow = lax.broadcasted_iota(jnp.int32, (2 * N_GATE, 1), 0)
    gt_ref[...] = jnp.where(grow >= N_GATE, _log_sigmoid(gta), gta)
    vt_ref[...] = (lax.dot_general(wvt_ref[...], h_ref[...], NT_DIMS, preferred_element_type=F32)
                   + bvt_ref[...]).astype(BF16)


def _in_projection(x, modtab, g1, w, b, wg, bg, wgt, bgt, wvt, bvt, cos, sin, qg, kg, avg):
    rows = x.shape[0]
    tm = ROW_TILE
    nb = tm // ROW_BLOCK
    return pl.pallas_call(
        _inproj_kernel,
        out_shape=(jax.ShapeDtypeStruct((rows, N_PROJ), BF16),
                   jax.ShapeDtypeStruct((rows, LANES), F32),
                   jax.ShapeDtypeStruct((2 * N_GATE, rows), F32),
                   jax.ShapeDtypeStruct((KV_HEADS * HEAD_DIM, rows), BF16)),
        grid=(rows // tm,),
        in_specs=[pl.BlockSpec((tm, D_MODEL), lambda i: (i, 0)),
                  pl.BlockSpec((nb, 6, D_MODEL), lambda i: (i, 0, 0)),
                  _resident((1, D_MODEL)),
                  _resident((D_MODEL, N_PROJ)),
                  _resident((1, N_PROJ)),
                  _resident((D_MODEL, LANES)),
                  _resident((1, LANES)),
                  _resident((2 * N_GATE, D_MODEL)),
                  _resident((2 * N_GATE, 1)),
                  _resident((KV_HEADS * HEAD_DIM, D_MODEL)),
                  _resident((KV_HEADS * HEAD_DIM, 1)),
                  pl.BlockSpec((tm, LANES), lambda i: (i, 0)),
                  pl.BlockSpec((tm, LANES), lambda i: (i, 0)),
                  _resident((1, LANES)),
                  _resident((1, LANES)),
                  _resident((LANES, LANES))],
        out_specs=(pl.BlockSpec((tm, N_PROJ), lambda i: (i, 0)),
                   pl.BlockSpec((tm, LANES), lambda i: (i, 0)),
                   pl.BlockSpec((2 * N_GATE, tm), lambda i: (0, i)),
                   pl.BlockSpec((KV_HEADS * HEAD_DIM, tm), lambda i: (0, i))),
        scratch_shapes=[pltpu.VMEM((tm, D_MODEL), BF16)],
        compiler_params=_params(("parallel",)),
        name="in_projection",
    )(x, modtab, g1, w, b, wg, bg, wgt, bgt, wvt, bvt, cos, sin, qg, kg, avg)


def _mlstm_kernel(qkvf_ref, gf_ref, gtf_ref, qkvb_ref, gb_ref, gtb_ref, hf_ref, hb_ref,
                  c_ref, n_ref, m_ref):
    @pl.when(pl.program_id(1) == 0)
    def _():
        c_ref[...] = jnp.zeros_like(c_ref)
        n_ref[...] = jnp.zeros_like(n_ref)
        m_ref[...] = jnp.zeros_like(m_ref)

    L = CHUNK
    row = lax.broadcasted_iota(jnp.int32, (L, L), 0)
    col = lax.broadcasted_iota(jnp.int32, (L, L), 1)
    scale = ML_DIM ** -0.5
    dirs = ((qkvf_ref, gf_ref, gtf_ref, hf_ref, col <= row), (qkvb_ref, gb_ref, gtb_ref, hb_ref, col >= row))
    for d, (qkv_ref, g_ref, gt_ref, h_ref, seen) in enumerate(dirs):
        seen_t = (row <= col) if d == 0 else (row >= col)
        for hh in range(ML_HEADS):
            idx = d * ML_HEADS + hh
            q = qkv_ref[:, hh * ML_DIM:(hh + 1) * ML_DIM]
            k = qkv_ref[:, WIDTH + hh * ML_DIM:WIDTH + (hh + 1) * ML_DIM]
            v = qkv_ref[:, 2 * WIDTH + hh * ML_DIM:2 * WIDTH + (hh + 1) * ML_DIM]
            ic_col = g_ref[:, idx:idx + 1]
            lf_col = g_ref[:, N_GATE + idx:N_GATE + idx + 1]
            ic_row = gt_ref[idx:idx + 1, :]
            lf_row = gt_ref[N_GATE + idx:N_GATE + idx + 1, :]
            b_col = jnp.sum(jnp.where(seen, lf_row, 0.0), axis=1, keepdims=True)
            b_row = jnp.sum(jnp.where(seen_t, lf_col, 0.0), axis=0, keepdims=True)
            b_end = jnp.sum(lf_row, axis=1, keepdims=True)
            r_row = ic_row - b_row
            r_col = ic_col - b_col
            m_old = m_ref[idx]
            c_old = c_ref[idx]
            n_old = n_ref[idx]

            dlog = jnp.where(seen, b_col + r_row, -jnp.inf)
            inter = b_col + m_old
            m_t = jnp.maximum(inter, jnp.max(dlog, axis=1, keepdims=True))
            w = jnp.exp(dlog - m_t)
            a = jnp.exp(inter - m_t)
            qs = q.astype(F32) * scale
            qsb = qs.astype(BF16)
            s = lax.dot_general(qsb, k, NT_DIMS, preferred_element_type=F32) * w
            num = (a * jnp.dot(qsb, c_old.astype(BF16), preferred_element_type=F32)
                   + jnp.dot(s.astype(BF16), v, preferred_element_type=F32))
            den = (a * jnp.sum(qs * n_old, axis=1, keepdims=True)
                   + jnp.sum(s, axis=1, keepdims=True))
            hc = num / jnp.maximum(jnp.abs(den), jnp.exp(-m_t))
            h_ref[:, hh * ML_DIM:(hh + 1) * ML_DIM] = hc.astype(h_ref.dtype)

            wlog = b_end + r_col
            m_new = jnp.maximum(b_end + m_old, jnp.max(wlog, axis=0, keepdims=True))
            a_c = jnp.exp(b_end + m_old - m_new)
            kw = k.astype(F32) * jnp.exp(wlog - m_new)
            c_ref[idx] = a_c * c_old + jnp.dot(kw.T.astype(BF16), v, preferred_element_type=F32)
            n_ref[idx] = a_c * n_old + jnp.sum(kw, axis=0, keepdims=True)
            m_ref[idx] = m_new


def _mlstm(p, g, gt, batch, n_lat, n_ctx):
    rows = p.shape[0]
    cl, cc = n_lat // CHUNK, n_ctx // CHUNK
    lat_base, ctx_base = 0, batch * cl

    def fwd_chunk(b, i):
        return jnp.where(i < cc, ctx_base + b * cc + i, lat_base + b * cl + (i - cc))

    def bwd_chunk(b, i):
        return jnp.where(i < cc, ctx_base + b * cc + (cc - 1 - i), lat_base + b * cl + (cl - 1 - (i - cc)))

    def specs(chunk):
        return [pl.BlockSpec((CHUNK, 3 * WIDTH), lambda b, i: (chunk(b, i), 0)),
                pl.BlockSpec((CHUNK, LANES), lambda b, i: (chunk(b, i), 0)),
                pl.BlockSpec((2 * N_GATE, CHUNK), lambda b, i: (0, chunk(b, i)))]

    n_state = 2 * ML_HEADS
    return pl.pallas_call(
        _mlstm_kernel,
        out_shape=(jax.ShapeDtypeStruct((rows, WIDTH), BF16), jax.ShapeDtypeStruct((rows, WIDTH), BF16)),
        grid=(batch, cl + cc),
        in_specs=specs(fwd_chunk) + specs(bwd_chunk),
        out_specs=(pl.BlockSpec((CHUNK, WIDTH), lambda b, i: (fwd_chunk(b, i), 0)),
                   pl.BlockSpec((CHUNK, WIDTH), lambda b, i: (bwd_chunk(b, i), 0))),
        scratch_shapes=[pltpu.VMEM((n_state, ML_DIM, ML_DIM), F32),
                        pltpu.VMEM((n_state, 1, ML_DIM), F32),
                        pltpu.VMEM((n_state, 1, 1), F32)],
        compiler_params=_params(("parallel", "arbitrary")),
        name="mlstm_scan",
    )(p, g, gt, p, g, gt)


def _half_mask(e):
    lane = lax.broadcasted_iota(jnp.int32, (1, LANES), 1)
    return (lane < HEAD_DIM) if e == 0 else (lane >= HEAD_DIM)


def _win_kernel(sink_ref, q_ref, kvc_ref, kvp_ref, kvm_ref, kvn_ref, o_ref, *, n_lat_blocks):
    j = pl.program_id(1)
    L = CHUNK
    row = lax.broadcasted_iota(jnp.int32, (L, L), 0)
    col = lax.broadcasted_iota(jnp.int32, (L, L), 1)

    def attend(local):
        outs = []
        for t in range(Q_HEADS // 2):
            g = t // 2
            qt = q_ref[:, t * LANES:(t + 1) * LANES]
            srcs = [(kvc_ref, None)]
            if local:
                srcs += [(kvp_ref, (col >= row, j >= 1)),
                         (kvm_ref, None),
                         (kvn_ref, (col <= row, j <= n_lat_blocks - 2))]
            acc = None
            for e in range(2):
                half = _half_mask(e)
                qm = jnp.where(half, qt, jnp.zeros_like(qt))
                sink = sink_ref[2 * t + e]
                scores = []
                for ref, msk in srcs:
                    k = ref[:, g * LANES:(g + 1) * LANES]
                    s = lax.dot_general(qm, k, NT_DIMS, preferred_element_type=F32)
                    if msk is not None:
                        s = jnp.where(msk[0], s, -jnp.inf)
                        s = jnp.where(msk[1], s, -jnp.inf)
                    scores.append(s)
                m = jnp.maximum(functools.reduce(
                    jnp.maximum, [jnp.max(s, axis=1, keepdims=True) for s in scores]), sink)
                den = jnp.exp(sink - m)
                o = None
                for (ref, _), s in zip(srcs, scores):
                    pr = jnp.exp(s - m)
                    den = den + jnp.sum(pr, axis=1, keepdims=True)
                    v = ref[:, 2 * LANES + g * LANES:2 * LANES + (g + 1) * LANES]
                    vm = jnp.where(half, v, jnp.zeros_like(v))
                    pv = jnp.dot(pr.astype(BF16), vm, preferred_element_type=F32)
                    o = pv if o is None else o + pv
                o = o / den
                acc = o if acc is None else acc + o
            outs.append(acc)
        o_ref[...] = jnp.concatenate(outs, axis=1).astype(o_ref.dtype)

    @pl.when(j < n_lat_blocks)
    def _():
        attend(True)

    @pl.when(j >= n_lat_blocks)
    def _():
        attend(False)


def _window_attention(p, sink, batch, n_lat, n_ctx, ctx_queries):
    nl, nc = n_lat // CHUNK, n_ctx // CHUNK
    nq = nl + (nc if ctx_queries else 0)
    out_rows = batch * (n_lat + (n_ctx if ctx_queries else 0))
    ctx_base = batch * nl
    kv_col = C_WK // (2 * 256)

    def q_block(b, j):
        return jnp.where(j < nl, b * nl + j, ctx_base + b * nc + (j - nl))

    def near(off):
        return lambda b, j: (b * nl + jnp.clip(j + off, 0, nl - 1), kv_col)

    return pl.pallas_call(
        functools.partial(_win_kernel, n_lat_blocks=nl),
        out_shape=jax.ShapeDtypeStruct((out_rows, WIDTH), BF16),
        grid=(batch, nq),
        in_specs=[pl.BlockSpec(memory_space=pltpu.SMEM),
                  pl.BlockSpec((CHUNK, WIDTH), lambda b, j: (q_block(b, j), C_WQ // WIDTH)),
                  pl.BlockSpec((n_ctx, 2 * 256), lambda b, j: (batch * n_lat // n_ctx + b, kv_col)),
                  pl.BlockSpec((CHUNK, 2 * 256), near(-1)),
                  pl.BlockSpec((CHUNK, 2 * 256), near(0)),
                  pl.BlockSpec((CHUNK, 2 * 256), near(1))],
        out_specs=pl.BlockSpec((CHUNK, WIDTH), lambda b, j: (q_block(b, j), 0)),
        compiler_params=_params(("parallel", "parallel")),
        name="window_attention",
    )(sink, p, p, p, p, p)


def _glb_kernel(q_ref, kc_ref, vtc_ref, kl_ref, vtl_ref, o_ref, qt_ref, m_ref, acc_ref,
                *, n_lat_tiles, tk):
    qi = pl.program_id(2)
    tq = q_ref.shape[0]
    heads = Q_HEADS // KV_HEADS
    for t in range(heads // 2):
        qf = q_ref[:, t * LANES:(t + 1) * LANES].astype(F32)
        for e in range(2):
            h = 2 * t + e
            qt_ref[:, h * tq:(h + 1) * tq] = jnp.where(_half_mask(e), qf, 0.0).T.astype(BF16)
    m_ref[...] = jnp.full_like(m_ref, -jnp.inf)
    acc_ref[...] = jnp.zeros_like(acc_ref)

    def step(k, vt):
        ones = jnp.ones((acc_ref.shape[1] - HEAD_DIM, vt.shape[1]), BF16)
        vta = jnp.concatenate([vt, ones], axis=0)
        for h in range(heads):
            cols = slice(h * tq, (h + 1) * tq)
            s = jnp.dot(k, qt_ref[:, cols], preferred_element_type=F32)
            m_old = m_ref[:, cols]
            m_new = jnp.maximum(m_old, jnp.max(s, axis=0, keepdims=True))
            alpha = jnp.exp2(m_old - m_new)
            pt = jnp.exp2(s - m_new).astype(BF16)
            acc_ref[h] = alpha * acc_ref[h] + jnp.dot(vta, pt, preferred_element_type=F32)
            m_ref[:, cols] = m_new

    step(kc_ref[...], vtc_ref[...])

    @pl.when(qi < n_lat_tiles)
    def _():
        def body(c, carry):
            off = pl.multiple_of(c * tk, tk)
            step(kl_ref[pl.ds(off, tk), :], vtl_ref[:, pl.ds(off, tk)])
            return carry
        lax.fori_loop(0, kl_ref.shape[0] // tk, body, 0)

    for t in range(heads // 2):
        pair = []
        for e in range(2):
            a = acc_ref[2 * t + e]
            pair.append(a[:HEAD_DIM, :] / a[HEAD_DIM:HEAD_DIM + 1, :])
        o_ref[:, t * LANES:(t + 1) * LANES] = jnp.concatenate(pair, axis=0).T.astype(o_ref.dtype)


def _global_attention(p, vt, batch, n_lat, n_ctx, ctx_queries):
    tq, tk = 256, 512
    nl, nc = n_lat // tq, n_ctx // tq
    nq = nl + (nc if ctx_queries else 0)
    out_rows = batch * (n_lat + (n_ctx if ctx_queries else 0))
    ctx_base = batch * nl
    heads = Q_HEADS // KV_HEADS
    acc_rows = HEAD_DIM + 16

    def q_block(b, qi):
        return jnp.where(qi < nl, b * nl + qi, ctx_base + b * nc + (qi - nl))

    ctx_blk = batch * n_lat // n_ctx
    return pl.pallas_call(
        functools.partial(_glb_kernel, n_lat_tiles=nl, tk=tk),
        out_shape=jax.ShapeDtypeStruct((out_rows, WIDTH), BF16),
        grid=(batch, KV_HEADS, nq),
        in_specs=[pl.BlockSpec((tq, 2 * LANES), lambda b, g, qi: (q_block(b, qi), C_GQ // 256 + g)),
                  pl.BlockSpec((n_ctx, LANES), lambda b, g, qi: (ctx_blk + b, C_GK // LANES + g)),
                  pl.BlockSpec((HEAD_DIM, n_ctx), lambda b, g, qi: (g, ctx_blk + b)),
                  pl.BlockSpec((n_lat, LANES), lambda b, g, qi: (b, C_GK // LANES + g)),
                  pl.BlockSpec((HEAD_DIM, n_lat), lambda b, g, qi: (g, b))],
        out_specs=pl.BlockSpec((tq, 2 * LANES), lambda b, g, qi: (q_block(b, qi), g)),
        scratch_shapes=[pltpu.VMEM((LANES, heads * tq), BF16),
                        pltpu.VMEM((1, heads * tq), F32),
                        pltpu.VMEM((heads, acc_rows, tq), F32)],
        compiler_params=_params(("parallel", "parallel", "arbitrary")),
        name="global_attention",
    )(p, p, vt, p, vt)


def _merge_kernel(x_ref, mod_ref, hf_ref, hb_ref, og_ref, yb_ref, yc_ref, ga_ref, gb_ref, gc_ref,
                  mlg_ref, wbr_ref, wo_ref, o_ref):
    tm = x_ref.shape[0]
    for sb in range(tm // ROW_BLOCK):
        rows = slice(sb * ROW_BLOCK, (sb + 1) * ROW_BLOCK)
        hs = hf_ref[rows, :].astype(F32) + hb_ref[rows, :].astype(F32)
        parts = []
        for t in range(ML_HEADS):
            ht = hs[:, t * ML_DIM:(t + 1) * ML_DIM]
            ms = jnp.mean(ht * ht, axis=-1, keepdims=True)
            parts.append(ht * lax.rsqrt(ms + EPS))
        ya = (jnp.concatenate(parts, axis=1) * mlg_ref[...]
              * _sigmoid(og_ref[rows, :].astype(F32))).astype(BF16)
        merged = None
        for y, gate_ref, i in ((ya, ga_ref, 0), (yb_ref[rows, :], gb_ref, 1), (yc_ref[rows, :], gc_ref, 2)):
            term = (_sigmoid(gate_ref[rows, :].astype(F32))
                    * jnp.dot(y, wbr_ref[i], preferred_element_type=F32))
            merged = term if merged is None else merged + term
        out = jnp.dot(merged.astype(BF16), wo_ref[...], preferred_element_type=F32)
        o_ref[rows, :] = x_ref[rows, :] + mod_ref[sb, 2:3, :] * out


def _merge(x, modtab, p, hf, hb, yb, yc, mlg, wbr, wo, n_rows):
    tm = ROW_TILE
    nb = tm // ROW_BLOCK
    row = lambda i: (i, 0)
    gate = lambda k: pl.BlockSpec((tm, D_MODEL), lambda i: (i, C_GATE // D_MODEL + k))
    return pl.pallas_call(
        _merge_kernel,
        out_shape=jax.ShapeDtypeStruct(x.shape, F32),
        grid=(n_rows // tm,),
        in_specs=[pl.BlockSpec((tm, D_MODEL), row),
                  pl.BlockSpec((nb, 6, D_MODEL), lambda i: (i, 0, 0)),
                  pl.BlockSpec((tm, WIDTH), row),
                  pl.BlockSpec((tm, WIDTH), row),
                  pl.BlockSpec((tm, WIDTH), lambda i: (i, C_MLO // WIDTH)),
                  pl.BlockSpec((tm, WIDTH), row),
                  pl.BlockSpec((tm, WIDTH), row),
                  gate(0), gate(1), gate(2),
                  _resident((1, WIDTH)),
                  _resident((3, WIDTH, D_MODEL)),
                  _resident((D_MODEL, D_MODEL))],
        out_specs=pl.BlockSpec((tm, D_MODEL), row),
        input_output_aliases={0: 0},
        compiler_params=_params(("parallel",)),
        name="merge",
    )(x, modtab, hf, hb, p, yb, yc, p, p, p, mlg, wbr, wo)


def _ffn_kernel(x_ref, mod_ref, g2_ref, w1_ref, w3_ref, w2_ref, gf_ref, o_ref, *, final):
    tm = x_ref.shape[0]
    d_ff = w1_ref.shape[1]
    half = d_ff // 2
    for sb in range(tm // ROW_BLOCK):
        rows = slice(sb * ROW_BLOCK, (sb + 1) * ROW_BLOCK)
        xs = x_ref[rows, :]
        ms = jnp.mean(xs * xs, axis=-1, keepdims=True)
        y = xs * lax.rsqrt(ms + EPS) * g2_ref[...]
        h = (y * (1.0 + mod_ref[sb, 4:5, :]) + mod_ref[sb, 3:4, :]).astype(BF16)
        out = None
        for c0 in (0, half):
            a = jnp.dot(h, w1_ref[:, c0:c0 + half], preferred_element_type=F32)
            b = jnp.dot(h, w3_ref[:, c0:c0 + half], preferred_element_type=F32)
            z = (a * _sigmoid(a) * b).astype(BF16)
            part = jnp.dot(z, w2_ref[c0:c0 + half, :], preferred_element_type=F32)
            out = part if out is None else out + part
        xn = xs + mod_ref[sb, 5:6, :] * out
        if final:
            ms = jnp.mean(xn * xn, axis=-1, keepdims=True)
            xn = xn * lax.rsqrt(ms + EPS) * gf_ref[...]
        o_ref[rows, :] = xn


def _ffn(x, modtab, g2, w1, w3, w2, gfin, n_rows, final):
    tm = ROW_TILE
    nb = tm // ROW_BLOCK
    d_ff = w1.shape[1]
    row = lambda i: (i, 0)
    out_rows = n_rows if final else x.shape[0]
    return pl.pallas_call(
        functools.partial(_ffn_kernel, final=final),
        out_shape=jax.ShapeDtypeStruct((out_rows, D_MODEL), F32),
        grid=(n_rows // tm,),
        in_specs=[pl.BlockSpec((tm, D_MODEL), row),
                  pl.BlockSpec((nb, 6, D_MODEL), lambda i: (i, 0, 0)),
                  _resident((1, D_MODEL)),
                  _resident((D_MODEL, d_ff)),
                  _resident((D_MODEL, d_ff)),
                  _resident((d_ff, D_MODEL)),
                  _resident((1, D_MODEL))],
        out_specs=pl.BlockSpec((tm, D_MODEL), row),
        input_output_aliases={} if final else {0: 0},
        compiler_params=_params(("parallel",)),
        name="ffn",
    )(x, modtab, g2, w1, w3, w2, gfin)


def _dup_halves(w, base):
    h0 = w[..., base:base + HEAD_DIM]
    h1 = w[..., base + HEAD_DIM:base + 2 * HEAD_DIM]
    return [h0, h0, h1, h1]


def _arrange_in_proj(w):
    o_wq = 4 * WIDTH + 2 * N_GATE
    o_wk, o_wv = o_wq + WIDTH, o_wq + WIDTH + 128
    o_gq = o_wv + 128
    o_gk, o_gv = o_gq + WIDTH, o_gq + WIDTH + 128
    o_gate = o_gv + 128
    main = jnp.concatenate(
        [w[..., :4 * WIDTH], w[..., o_wq:o_wq + WIDTH]]
        + _dup_halves(w, o_wk) + _dup_halves(w, o_wv)
        + [w[..., o_gate:o_gate + 3 * D_MODEL], w[..., o_gq:o_gq + WIDTH]]
        + _dup_halves(w, o_gk), axis=-1)
    gates = w[..., 4 * WIDTH:4 * WIDTH + 2 * N_GATE]
    glb_v = w[..., o_gv:o_gv + KV_HEADS * HEAD_DIM]
    return main, gates, glb_v


def _rope_tables(batch, n_lat, n_ctx):
    t = jnp.arange(n_lat)
    quarter = HEAD_DIM // 4
    inv = ROPE_THETA ** (-jnp.arange(0, 2 * quarter, 2, dtype=F32) / (2 * quarter))
    ang_r = (t // GRID_W).astype(F32)[:, None] * inv
    ang_c = (t % GRID_W).astype(F32)[:, None] * inv
    cos = jnp.concatenate([jnp.cos(ang_r)] * 2 + [jnp.cos(ang_c)] * 2, axis=1)
    sin = jnp.concatenate([-jnp.sin(ang_r), jnp.sin(ang_r), -jnp.sin(ang_c), jnp.sin(ang_c)], axis=1)
    cos = jnp.tile(cos, (batch, LANES // HEAD_DIM))
    sin = jnp.tile(sin, (batch, LANES // HEAD_DIM))
    pad = batch * n_ctx
    return (jnp.concatenate([cos, jnp.ones((pad, LANES), F32)], axis=0),
            jnp.concatenate([sin, jnp.zeros((pad, LANES), F32)], axis=0))


def kernel(x, c, ctx, c_ctx, w_mod, b_mod, norm1_g, w_in, b_in, ml_norm_g, win_sink, qn_g, kn_g,
           w_br, w_o, norm2_g, w_ff1, w_ff3, w_ff2, final_g):
    batch, n_lat, d = x.shape
    n_ctx = ctx.shape[1]
    depth = w_mod.shape[0]
    assert d == D_MODEL and n_lat % ROW_TILE == 0 and n_lat % GRID_W == 0
    assert (batch * n_ctx) % ROW_TILE == 0 and n_ctx % ROW_BLOCK == 0 and (batch * n_lat) % n_ctx == 0
    lat_rows, ctx_rows = batch * n_lat, batch * n_ctx

    xs = jnp.concatenate([x.reshape(lat_rows, d), ctx.reshape(ctx_rows, d)], axis=0)
    cvec = jnp.concatenate([c, c_ctx[None, :], jnp.zeros((8 - batch - 1, d), F32)], axis=0)
    block_class = np.concatenate([np.repeat(np.arange(batch), n_lat // ROW_BLOCK),
                                  np.full(ctx_rows // ROW_BLOCK, batch)])
    cos, sin = _rope_tables(batch, n_lat, n_ctx)
    avg = jnp.asarray(np.kron(np.eye(LANES // HEAD_DIM), np.full((HEAD_DIM, HEAD_DIM), 1.0 / HEAD_DIM)), BF16)

    out = None
    for l in range(depth):
        last = l == depth - 1
        mod = _mod_vectors(cvec, w_mod[l], b_mod[l]).reshape(8, 6, d)
        modtab = mod[block_class]

        w_main, w_gate, w_gv = _arrange_in_proj(w_in[l])
        b_main, b_gate, b_gv = _arrange_in_proj(b_in[l][None, :])
        wg = jnp.pad(w_gate, ((0, 0), (0, LANES - 2 * N_GATE))).astype(BF16)
        bg = jnp.pad(b_gate, ((0, 0), (0, LANES - 2 * N_GATE)))
        p, g, gt, vt = _in_projection(
            xs, modtab, norm1_g[l][None, :], w_main.astype(BF16), b_main, wg, bg,
            w_gate.T.astype(BF16), b_gate.T, w_gv.T.astype(BF16), b_gv.T,
            cos, sin, jnp.tile(qn_g[l], 2)[None, :], jnp.tile(kn_g[l], 2)[None, :], avg)

        hf, hb = _mlstm(p, g, gt, batch, n_lat, n_ctx)
        yb = _window_attention(p, win_sink[l], batch, n_lat, n_ctx, ctx_queries=not last)
        yc = _global_attention(p, vt, batch, n_lat, n_ctx, ctx_queries=not last)

        n_rows = lat_rows if last else lat_rows + ctx_rows
        xs = _merge(xs, modtab, p, hf, hb, yb, yc, ml_norm_g[l].reshape(1, WIDTH),
                    w_br[l].astype(BF16), w_o[l].astype(BF16), n_rows)
        out = _ffn(xs, modtab, norm2_g[l][None, :], w_ff1[l].astype(BF16), w_ff3[l].astype(BF16),
                   w_ff2[l].astype(BF16), final_g[None, :], n_rows, final=last)
        xs = out
    return out.reshape(batch, n_lat, d)
```

```python
import functools

import jax
import jax.numpy as jnp
import numpy as np
from jax import lax
from jax.experimental import pallas as pl
from jax.experimental.pallas import tpu as pltpu

F32 = jnp.float32
BF16 = jnp.bfloat16

D_MODEL = 1024
GRID_W = 64
CHUNK = 128
HEAD_DIM = 64
ROPE_THETA = 10000.0
EPS = 1e-6
ML_HEADS = 4
ML_DIM = 128
Q_HEADS = 8
KV_HEADS = 2
WIDTH = 512
N_GATE = 2 * ML_HEADS

LANES = 128
ROW_BLOCK = 256
ROW_TILE = 512
VMEM_LIMIT = 56 * 1024 * 1024

C_MLQ, C_MLK, C_MLV, C_MLO = 0, 512, 1024, 1536
C_WQ, C_WK, C_WV = 2048, 2560, 2816
C_GATE = 3072
C_GQ, C_GK = 6144, 6656
N_PROJ = C_GK + 256
LOG2E = 1.4426950408889634
N_SCORE_BUFS = 4

NT_DIMS = (((1,), (1,)), ((), ()))


def _params(sem, vmem=VMEM_LIMIT):
    return pltpu.CompilerParams(dimension_semantics=sem, vmem_limit_bytes=vmem)


def _resident(shape):
    nd = len(shape)
    return pl.BlockSpec(shape, lambda *_: (0,) * nd, pipeline_mode=pl.Buffered(1))


def _sigmoid(x):
    return 1.0 / (1.0 + jnp.exp(-x))


def _log_sigmoid(x):
    return jnp.minimum(x, 0.0) - jnp.log(1.0 + jnp.exp(-jnp.abs(x)))


def _mod_kernel(c_ref, w_ref, b_ref, o_ref):
    c = c_ref[...]
    s = c * _sigmoid(c)
    o_ref[...] = jnp.dot(s, w_ref[...], preferred_element_type=F32) + b_ref[...]


def _mod_vectors(cvec, w_mod, b_mod):
    n_out = w_mod.shape[1]
    tn = 1536
    return pl.pallas_call(
        _mod_kernel,
        out_shape=jax.ShapeDtypeStruct((cvec.shape[0], n_out), F32),
        grid=(n_out // tn,),
        in_specs=[pl.BlockSpec(cvec.shape, lambda j: (0, 0)),
                  pl.BlockSpec((D_MODEL, tn), lambda j: (0, j)),
                  pl.BlockSpec((1, tn), lambda j: (0, j))],
        out_specs=pl.BlockSpec((cvec.shape[0], tn), lambda j: (0, j)),
        compiler_params=_params(("parallel",)),
        name="mod_vectors",
    )(cvec, w_mod, b_mod.reshape(1, n_out))


def _rope(acc, cos, sin, first_half):
    w = acc.shape[1]
    reps = w // LANES
    if reps > 1:
        cos = jnp.concatenate([cos] * reps, axis=1)
        sin = jnp.concatenate([sin] * reps, axis=1)
    ahead = pltpu.roll(acc, w - 16, axis=1)
    behind = pltpu.roll(acc, 16, axis=1)
    return acc * cos + jnp.where(first_half, ahead, behind) * sin


def _head_rms(acc, avg, gain):
    sq = acc * acc
    hi = sq.astype(BF16)
    lo = (sq - hi.astype(F32)).astype(BF16)
    outs = []
    for t in range(acc.shape[1] // LANES):
        sl = slice(t * LANES, (t + 1) * LANES)
        ms = (jnp.dot(hi[:, sl], avg, preferred_element_type=F32)
              + jnp.dot(lo[:, sl], avg, preferred_element_type=F32))
        outs.append(acc[:, sl] * lax.rsqrt(ms + EPS) * gain)
    return jnp.concatenate(outs, axis=1)


def _inproj_kernel(x_ref, mod_ref, g1_ref, w_ref, b_ref, wg_ref, bg_ref, wgt_ref, bgt_ref,
                   wvt_ref, bvt_ref, cos_ref, sin_ref, qg_ref, kg_ref, avg_ref,
                   p_ref, g_ref, gt_ref, vt_ref, h_ref):
    tm = x_ref.shape[0]
    for sb in range(tm // ROW_BLOCK):
        r0 = sb * ROW_BLOCK
        xs = x_ref[r0:r0 + ROW_BLOCK, :]
        ms = jnp.mean(xs * xs, axis=-1, keepdims=True)
        y = xs * lax.rsqrt(ms + EPS) * g1_ref[...]
        shift = mod_ref[sb, 0:1, :]
        scale = mod_ref[sb, 1:2, :]
        h_ref[r0:r0 + ROW_BLOCK, :] = (y * (1.0 + scale) + shift).astype(BF16)

    def first_half(width):
        return (lax.broadcasted_iota(jnp.int32, (1, width), 1) % 32) < 16

    avg = avg_ref[...]
    qg = qg_ref[...]
    kg = kg_ref[...]
    q_scale = HEAD_DIM ** -0.5

    for sb in range(tm // ROW_BLOCK):
        r0 = sb * ROW_BLOCK
        rows = slice(r0, r0 + ROW_BLOCK)
        h = h_ref[rows, :]
        cos = cos_ref[rows, :]
        sin = sin_ref[rows, :]

        def proj(c0, width):
            return (jnp.dot(h, w_ref[:, c0:c0 + width], preferred_element_type=F32)
                    + b_ref[:, c0:c0 + width])

        for c0 in (C_MLQ, C_MLK, C_MLV, C_MLO):
            p_ref[rows, c0:c0 + WIDTH] = proj(c0, WIDTH).astype(BF16)
        wq = _rope(proj(C_WQ, WIDTH), cos, sin, first_half(WIDTH)) * q_scale
        p_ref[rows, C_WQ:C_WQ + WIDTH] = wq.astype(BF16)
        wk = _rope(proj(C_WK, 256), cos, sin, first_half(256))
        p_ref[rows, C_WK:C_WK + 256] = wk.astype(BF16)
        p_ref[rows, C_WV:C_WV + 256] = proj(C_WV, 256).astype(BF16)
        gq = _rope(_head_rms(proj(C_GQ, WIDTH), avg, qg), cos, sin, first_half(WIDTH)) * (q_scale * LOG2E)
        p_ref[rows, C_GQ:C_GQ + WIDTH] = gq.astype(BF16)
        gk = _rope(_head_rms(proj(C_GK, 256), avg, kg), cos, sin, first_half(256))
        p_ref[rows, C_GK:C_GK + 256] = gk.astype(BF16)
        for c0 in range(C_GATE, C_GATE + 3 * D_MODEL, WIDTH):
            p_ref[rows, c0:c0 + WIDTH] = proj(c0, WIDTH).astype(BF16)
        ga = jnp.dot(h, wg_ref[...], preferred_element_type=F32) + bg_ref[...]
        glane = lax.broadcasted_iota(jnp.int32, (1, LANES), 1)
        g_ref[rows, :] = jnp.where((glane >= N_GATE) & (glane < 2 * N_GATE), _log_sigmoid(ga), ga)

    gta = lax.dot_general(wgt_ref[...], h_ref[...], NT_DIMS, preferred_element_type=F32) + bgt_ref[...]
    grow = lax.broadcasted_iota(jnp.int32, (2 * N_GATE, 1), 0)
    gt_ref[...] = jnp.where(grow >= N_GATE, _log_sigmoid(gta), gta)
    vt_ref[...] = (lax.dot_general(wvt_ref[...], h_ref[...], NT_DIMS, preferred_element_type=F32)
                   + bvt_ref[...]).astype(BF16)


def _in_projection(x, modtab, g1, w, b, wg, bg, wgt, bgt, wvt, bvt, cos, sin, qg, kg, avg):
    rows = x.shape[0]
    tm = ROW_TILE
    nb = tm // ROW_BLOCK
    return pl.pallas_call(
        _inproj_kernel,
        out_shape=(jax.ShapeDtypeStruct((rows, N_PROJ), BF16),
                   jax.ShapeDtypeStruct((rows, LANES), F32),
                   jax.ShapeDtypeStruct((2 * N_GATE, rows), F32),
                   jax.ShapeDtypeStruct((KV_HEADS * HEAD_DIM, rows), BF16)),
        grid=(rows // tm,),
        in_specs=[pl.BlockSpec((tm, D_MODEL), lambda i: (i, 0)),
                  pl.BlockSpec((nb, 6, D_MODEL), lambda i: (i, 0, 0)),
                  _resident((1, D_MODEL)),
                  _resident((D_MODEL, N_PROJ)),
                  _resident((1, N_PROJ)),
                  _resident((D_MODEL, LANES)),
                  _resident((1, LANES)),
                  _resident((2 * N_GATE, D_MODEL)),
                  _resident((2 * N_GATE, 1)),
                  _resident((KV_HEADS * HEAD_DIM, D_MODEL)),
                  _resident((KV_HEADS * HEAD_DIM, 1)),
                  pl.BlockSpec((tm, LANES), lambda i: (i, 0)),
                  pl.BlockSpec((tm, LANES), lambda i: (i, 0)),
                  _resident((1, LANES)),
                  _resident((1, LANES)),
                  _resident((LANES, LANES))],
        out_specs=(pl.BlockSpec((tm, N_PROJ), lambda i: (i, 0)),
                   pl.BlockSpec((tm, LANES), lambda i: (i, 0)),
                   pl.BlockSpec((2 * N_GATE, tm), lambda i: (0, i)),
                   pl.BlockSpec((KV_HEADS * HEAD_DIM, tm), lambda i: (0, i))),
        scratch_shapes=[pltpu.VMEM((tm, D_MODEL), BF16)],
        compiler_params=_params(("parallel",)),
        name="in_projection",
    )(x, modtab, g1, w, b, wg, bg, wgt, bgt, wvt, bvt, cos, sin, qg, kg, avg)


def _mlstm_kernel(qkvf_ref, gf_ref, gtf_ref, qkvb_ref, gb_ref, gtb_ref, hf_ref, hb_ref,
                  c_ref, n_ref, m_ref):
    @pl.when(pl.program_id(1) == 0)
    def _():
        c_ref[...] = jnp.zeros_like(c_ref)
        n_ref[...] = jnp.zeros_like(n_ref)
        m_ref[...] = jnp.zeros_like(m_ref)

    L = CHUNK
    row = lax.broadcasted_iota(jnp.int32, (L, L), 0)
    col = lax.broadcasted_iota(jnp.int32, (L, L), 1)
    scale = ML_DIM ** -0.5
    dirs = ((qkvf_ref, gf_ref, gtf_ref, hf_ref, col <= row), (qkvb_ref, gb_ref, gtb_ref, hb_ref, col >= row))
    for d, (qkv_ref, g_ref, gt_ref, h_ref, seen) in enumerate(dirs):
        seen_t = (row <= col) if d == 0 else (row >= col)
        for hh in range(ML_HEADS):
            idx = d * ML_HEADS + hh
            q = qkv_ref[:, hh * ML_DIM:(hh + 1) * ML_DIM]
            k = qkv_ref[:, WIDTH + hh * ML_DIM:WIDTH + (hh + 1) * ML_DIM]
            v = qkv_ref[:, 2 * WIDTH + hh * ML_DIM:2 * WIDTH + (hh + 1) * ML_DIM]
            ic_col = g_ref[:, idx:idx + 1]
            lf_col = g_ref[:, N_GATE + idx:N_GATE + idx + 1]
            ic_row = gt_ref[idx:idx + 1, :]
            lf_row = gt_ref[N_GATE + idx:N_GATE + idx + 1, :]
            b_col = jnp.sum(jnp.where(seen, lf_row, 0.0), axis=1, keepdims=True)
            b_row = jnp.sum(jnp.where(seen_t, lf_col, 0.0), axis=0, keepdims=True)
            b_end = jnp.sum(lf_row, axis=1, keepdims=True)
            r_row = ic_row - b_row
            r_col = ic_col - b_col
            m_old = m_ref[idx]
            c_old = c_ref[idx]
            n_old = n_ref[idx]

            dlog = jnp.where(seen, b_col + r_row, -jnp.inf)
            inter = b_col + m_old
            m_t = jnp.maximum(inter, jnp.max(dlog, axis=1, keepdims=True))
            w = jnp.exp(dlog - m_t)
            a = jnp.exp(inter - m_t)
            qs = q.astype(F32) * scale
            qsb = qs.astype(BF16)
            s = lax.dot_general(qsb, k, NT_DIMS, preferred_element_type=F32) * w
            num = (a * jnp.dot(qsb, c_old.astype(BF16), preferred_element_type=F32)
                   + jnp.dot(s.astype(BF16), v, preferred_element_type=F32))
            den = (a * jnp.sum(qs * n_old, axis=1, keepdims=True)
                   + jnp.sum(s, axis=1, keepdims=True))
            hc = num / jnp.maximum(jnp.abs(den), jnp.exp(-m_t))
            h_ref[:, hh * ML_DIM:(hh + 1) * ML_DIM] = hc.astype(h_ref.dtype)

            wlog = b_end + r_col
            m_new = jnp.maximum(b_end + m_old, jnp.max(wlog, axis=0, keepdims=True))
            a_c = jnp.exp(b_end + m_old - m_new)
            kw = k.astype(F32) * jnp.exp(wlog - m_new)
            c_ref[idx] = a_c * c_old + jnp.dot(kw.T.astype(BF16), v, preferred_element_type=F32)
            n_ref[idx] = a_c * n_old + jnp.sum(kw, axis=0, keepdims=True)
            m_ref[idx] = m_new


def _mlstm(p, g, gt, batch, n_lat, n_ctx):
    rows = p.shape[0]
    cl, cc = n_lat // CHUNK, n_ctx // CHUNK
    lat_base, ctx_base = 0, batch * cl

    def fwd_chunk(b, i):
        return jnp.where(i < cc, ctx_base + b * cc + i, lat_base + b * cl + (i - cc))

    def bwd_chunk(b, i):
        return jnp.where(i < cc, ctx_base + b * cc + (cc - 1 - i), lat_base + b * cl + (cl - 1 - (i - cc)))

    def specs(chunk):
        return [pl.BlockSpec((CHUNK, 3 * WIDTH), lambda b, i: (chunk(b, i), 0)),
                pl.BlockSpec((CHUNK, LANES), lambda b, i: (chunk(b, i), 0)),
                pl.BlockSpec((2 * N_GATE, CHUNK), lambda b, i: (0, chunk(b, i)))]

    n_state = 2 * ML_HEADS
    return pl.pallas_call(
        _mlstm_kernel,
        out_shape=(jax.ShapeDtypeStruct((rows, WIDTH), BF16), jax.ShapeDtypeStruct((rows, WIDTH), BF16)),
        grid=(batch, cl + cc),
        in_specs=specs(fwd_chunk) + specs(bwd_chunk),
        out_specs=(pl.BlockSpec((CHUNK, WIDTH), lambda b, i: (fwd_chunk(b, i), 0)),
                   pl.BlockSpec((CHUNK, WIDTH), lambda b, i: (bwd_chunk(b, i), 0))),
        scratch_shapes=[pltpu.VMEM((n_state, ML_DIM, ML_DIM), F32),
                        pltpu.VMEM((n_state, 1, ML_DIM), F32),
                        pltpu.VMEM((n_state, 1, 1), F32)],
        compiler_params=_params(("parallel", "arbitrary")),
        name="mlstm_scan",
    )(p, g, gt, p, g, gt)


def _half_mask(e):
    lane = lax.broadcasted_iota(jnp.int32, (1, LANES), 1)
    return (lane < HEAD_DIM) if e == 0 else (lane >= HEAD_DIM)


def _win_kernel(sink_ref, q_ref, kvc_ref, kvp_ref, kvm_ref, kvn_ref, o_ref, *, n_lat_blocks):
    j = pl.program_id(1)
    L = CHUNK
    row = lax.broadcasted_iota(jnp.int32, (L, L), 0)
    col = lax.broadcasted_iota(jnp.int32, (L, L), 1)

    def attend(local):
        outs = []
        for t in range(Q_HEADS // 2):
            g = t // 2
            qt = q_ref[:, t * LANES:(t + 1) * LANES]
            srcs = [(kvc_ref, None)]
            if local:
                srcs += [(kvp_ref, (col >= row, j >= 1)),
                         (kvm_ref, None),
                         (kvn_ref, (col <= row, j <= n_lat_blocks - 2))]
            acc = None
            for e in range(2):
                half = _half_mask(e)
                qm = jnp.where(half, qt, jnp.zeros_like(qt))
                sink = sink_ref[2 * t + e]
                scores = []
                for ref, msk in srcs:
                    k = ref[:, g * LANES:(g + 1) * LANES]
                    s = lax.dot_general(qm, k, NT_DIMS, preferred_element_type=F32)
                    if msk is not None:
                        s = jnp.where(msk[0], s, -jnp.inf)
                        s = jnp.where(msk[1], s, -jnp.inf)
                    scores.append(s)
                m = jnp.maximum(functools.reduce(
                    jnp.maximum, [jnp.max(s, axis=1, keepdims=True) for s in scores]), sink)
                den = jnp.exp(sink - m)
                o = None
                for (ref, _), s in zip(srcs, scores):
                    pr = jnp.exp(s - m)
                    den = den + jnp.sum(pr, axis=1, keepdims=True)
                    v = ref[:, 2 * LANES + g * LANES:2 * LANES + (g + 1) * LANES]
                    vm = jnp.where(half, v, jnp.zeros_like(v))
                    pv = jnp.dot(pr.astype(BF16), vm, preferred_element_type=F32)
                    o = pv if o is None else o + pv
                o = o / den
                acc = o if acc is None else acc + o
            outs.append(acc)
        o_ref[...] = jnp.concatenate(outs, axis=1).astype(o_ref.dtype)

    @pl.when(j < n_lat_blocks)
    def _():
        attend(True)

    @pl.when(j >= n_lat_blocks)
    def _():
        attend(False)


def _window_attention(p, sink, batch, n_lat, n_ctx, ctx_queries):
    nl, nc = n_lat // CHUNK, n_ctx // CHUNK
    nq = nl + (nc if ctx_queries else 0)
    out_rows = batch * (n_lat + (n_ctx if ctx_queries else 0))
    ctx_base = batch * nl
    kv_col = C_WK // (2 * 256)

    def q_block(b, j):
        return jnp.where(j < nl, b * nl + j, ctx_base + b * nc + (j - nl))

    def near(off):
        return lambda b, j: (b * nl + jnp.clip(j + off, 0, nl - 1), kv_col)

    return pl.pallas_call(
        functools.partial(_win_kernel, n_lat_blocks=nl),
        out_shape=jax.ShapeDtypeStruct((out_rows, WIDTH), BF16),
        grid=(batch, nq),
        in_specs=[pl.BlockSpec(memory_space=pltpu.SMEM),
                  pl.BlockSpec((CHUNK, WIDTH), lambda b, j: (q_block(b, j), C_WQ // WIDTH)),
                  pl.BlockSpec((n_ctx, 2 * 256), lambda b, j: (batch * n_lat // n_ctx + b, kv_col)),
                  pl.BlockSpec((CHUNK, 2 * 256), near(-1)),
                  pl.BlockSpec((CHUNK, 2 * 256), near(0)),
                  pl.BlockSpec((CHUNK, 2 * 256), near(1))],
        out_specs=pl.BlockSpec((CHUNK, WIDTH), lambda b, j: (q_block(b, j), 0)),
        compiler_params=_params(("parallel", "parallel")),
        name="window_attention",
    )(sink, p, p, p, p, p)


def _glb_kernel(q_ref, kc_ref, vtc_ref, kl_ref, vtl_ref, o_ref, qt_ref, *scratch, n_lat_tiles, tk):
    heads = Q_HEADS // KV_HEADS
    acc_refs = scratch[0:heads]
    sc_refs = scratch[heads:2 * heads]
    ring = [scratch[(2 + i) * heads:(3 + i) * heads] for i in range(N_SCORE_BUFS)]
    qi = pl.program_id(2)
    tq = q_ref.shape[0]
    n_chunks = kl_ref.shape[0] // tk
    for t in range(heads // 2):
        qf = q_ref[:, t * LANES:(t + 1) * LANES].astype(F32)
        for e in range(2):
            h = 2 * t + e
            qt_ref[:, h * tq:(h + 1) * tq] = jnp.where(_half_mask(e), qf, 0.0).T.astype(BF16)
    for h in range(heads):
        acc_refs[h][...] = jnp.zeros_like(acc_refs[h])
    neg_inf = (jnp.full((1, tq), -jnp.inf, F32),) * heads

    def fill(s_refs, k, m_run):
        out = []
        for h in range(heads):
            s = jnp.dot(k, qt_ref[:, h * tq:(h + 1) * tq], preferred_element_type=F32)
            s_refs[h][...] = s
            out.append(jnp.maximum(m_run[h], jnp.max(s, axis=0, keepdims=True)))
        return tuple(out)

    def drain(s_refs, vt, m_run, m_acc):
        ones = jnp.ones((acc_refs[0].shape[0] - HEAD_DIM, vt.shape[1]), BF16)
        vta = jnp.concatenate([vt, ones], axis=0)
        for h in range(heads):
            alpha = jnp.exp2(m_acc[h] - m_run[h])
            pt = jnp.exp2(s_refs[h][...] - m_run[h]).astype(BF16)
            acc_refs[h][...] = alpha * acc_refs[h][...] + jnp.dot(vta, pt, preferred_element_type=F32)
        return m_run

    def k_chunk(c):
        off = pl.multiple_of(jnp.minimum(c, n_chunks - 1) * tk, tk)
        return kl_ref[pl.ds(off, tk), :]

    def vt_chunk(c):
        return vtl_ref[:, pl.ds(pl.multiple_of(c * tk, tk), tk)]

    m_ctx = fill(sc_refs, kc_ref[...], neg_inf)

    @pl.when(qi >= n_lat_tiles)
    def _():
        drain(sc_refs, vtc_ref[...], m_ctx, neg_inf)

    @pl.when(qi < n_lat_tiles)
    def _():
        m_one = fill(ring[0], k_chunk(0), m_ctx)
        m_two = fill(ring[1], k_chunk(1), m_one)
        m_acc = drain(sc_refs, vtc_ref[...], m_one, neg_inf)

        def body(j, carry):
            m_run, m_acc = carry
            for i in range(N_SCORE_BUFS):
                c = N_SCORE_BUFS * j + i
                m_next = fill(ring[(i + 2) % N_SCORE_BUFS], k_chunk(c + 2), m_run)
                m_acc = drain(ring[i], vt_chunk(c), m_run, m_acc)
                m_run = m_next
            return m_run, m_acc
        lax.fori_loop(0, n_chunks // N_SCORE_BUFS, body, (m_two, m_acc))

    for t in range(heads // 2):
        pair = []
        for e in range(2):
            a = acc_refs[2 * t + e][...]
            pair.append(a[:HEAD_DIM, :] / a[HEAD_DIM:HEAD_DIM + 1, :])
        o_ref[:, t * LANES:(t + 1) * LANES] = jnp.concatenate(pair, axis=0).T.astype(o_ref.dtype)


def _global_attention(p, vt, batch, n_lat, n_ctx, ctx_queries):
    tq = 256
    tk = min(512, n_lat // N_SCORE_BUFS)
    assert n_lat % (tk * N_SCORE_BUFS) == 0 and tk % LANES == 0
    nl, nc = n_lat // tq, n_ctx // tq
    nq = nl + (nc if ctx_queries else 0)
    out_rows = batch * (n_lat + (n_ctx if ctx_queries else 0))
    ctx_base = batch * nl
    heads = Q_HEADS // KV_HEADS
    acc_rows = HEAD_DIM + 16

    def q_block(b, qi):
        return jnp.where(qi < nl, b * nl + qi, ctx_base + b * nc + (qi - nl))

    ctx_blk = batch * n_lat // n_ctx
    return pl.pallas_call(
        functools.partial(_glb_kernel, n_lat_tiles=nl, tk=tk),
        out_shape=jax.ShapeDtypeStruct((out_rows, WIDTH), BF16),
        grid=(batch, KV_HEADS, nq),
        in_specs=[pl.BlockSpec((tq, 2 * LANES), lambda b, g, qi: (q_block(b, qi), C_GQ // 256 + g)),
                  pl.BlockSpec((n_ctx, LANES), lambda b, g, qi: (ctx_blk + b, C_GK // LANES + g)),
                  pl.BlockSpec((HEAD_DIM, n_ctx), lambda b, g, qi: (g, ctx_blk + b)),
                  pl.BlockSpec((n_lat, LANES), lambda b, g, qi: (b, C_GK // LANES + g)),
                  pl.BlockSpec((HEAD_DIM, n_lat), lambda b, g, qi: (g, b))],
        out_specs=pl.BlockSpec((tq, 2 * LANES), lambda b, g, qi: (q_block(b, qi), g)),
        scratch_shapes=([pltpu.VMEM((LANES, heads * tq), BF16)]
                        + [pltpu.VMEM((acc_rows, tq), F32)] * heads
                        + [pltpu.VMEM((n_ctx, tq), F32)] * heads
                        + [pltpu.VMEM((tk, tq), F32)] * (N_SCORE_BUFS * heads)),
        compiler_params=_params(("parallel", "parallel", "arbitrary")),
        name="global_attention",
    )(p, p, vt, p, vt)


def _merge_kernel(x_ref, mod_ref, hf_ref, hb_ref, og_ref, yb_ref, yc_ref, ga_ref, gb_ref, gc_ref,
                  mlg_ref, wbr_ref, wo_ref, o_ref):
    tm = x_ref.shape[0]
    for sb in range(tm // ROW_BLOCK):
        rows = slice(sb * ROW_BLOCK, (sb + 1) * ROW_BLOCK)
        hs = hf_ref[rows, :].astype(F32) + hb_ref[rows, :].astype(F32)
        parts = []
        for t in range(ML_HEADS):
            ht = hs[:, t * ML_DIM:(t + 1) * ML_DIM]
            ms = jnp.mean(ht * ht, axis=-1, keepdims=True)
            parts.append(ht * lax.rsqrt(ms + EPS))
        ya = (jnp.concatenate(parts, axis=1) * mlg_ref[...]
              * _sigmoid(og_ref[rows, :].astype(F32))).astype(BF16)
        merged = None
        for y, gate_ref, i in ((ya, ga_ref, 0), (yb_ref[rows, :], gb_ref, 1), (yc_ref[rows, :], gc_ref, 2)):
            term = (_sigmoid(gate_ref[rows, :].astype(F32))
                    * jnp.dot(y, wbr_ref[i], preferred_element_type=F32))
            merged = term if merged is None else merged + term
        out = jnp.dot(merged.astype(BF16), wo_ref[...], preferred_element_type=F32)
        o_ref[rows, :] = x_ref[rows, :] + mod_ref[sb, 2:3, :] * out


def _merge(x, modtab, p, hf, hb, yb, yc, mlg, wbr, wo, n_rows):
    tm = ROW_TILE
    nb = tm // ROW_BLOCK
    row = lambda i: (i, 0)
    gate = lambda k: pl.BlockSpec((tm, D_MODEL), lambda i: (i, C_GATE // D_MODEL + k))
    return pl.pallas_call(
        _merge_kernel,
        out_shape=jax.ShapeDtypeStruct(x.shape, F32),
        grid=(n_rows // tm,),
        in_specs=[pl.BlockSpec((tm, D_MODEL), row),
                  pl.BlockSpec((nb, 6, D_MODEL), lambda i: (i, 0, 0)),
                  pl.BlockSpec((tm, WIDTH), row),
                  pl.BlockSpec((tm, WIDTH), row),
                  pl.BlockSpec((tm, WIDTH), lambda i: (i, C_MLO // WIDTH)),
                  pl.BlockSpec((tm, WIDTH), row),
                  pl.BlockSpec((tm, WIDTH), row),
                  gate(0), gate(1), gate(2),
                  _resident((1, WIDTH)),
                  _resident((3, WIDTH, D_MODEL)),
                  _resident((D_MODEL, D_MODEL))],
        out_specs=pl.BlockSpec((tm, D_MODEL), row),
        input_output_aliases={0: 0},
        compiler_params=_params(("parallel",)),
        name="merge",
    )(x, modtab, hf, hb, p, yb, yc, p, p, p, mlg, wbr, wo)


def _ffn_kernel(x_ref, mod_ref, g2_ref, w1_ref, w3_ref, w2_ref, gf_ref, o_ref, *, final):
    tm = x_ref.shape[0]
    d_ff = w1_ref.shape[1]
    half = d_ff // 2
    for sb in range(tm // ROW_BLOCK):
        rows = slice(sb * ROW_BLOCK, (sb + 1) * ROW_BLOCK)
        xs = x_ref[rows, :]
        ms = jnp.mean(xs * xs, axis=-1, keepdims=True)
        y = xs * lax.rsqrt(ms + EPS) * g2_ref[...]
        h = (y * (1.0 + mod_ref[sb, 4:5, :]) + mod_ref[sb, 3:4, :]).astype(BF16)
        out = None
        for c0 in (0, half):
            a = jnp.dot(h, w1_ref[:, c0:c0 + half], preferred_element_type=F32)
            b = jnp.dot(h, w3_ref[:, c0:c0 + half], preferred_element_type=F32)
            z = (a * _sigmoid(a) * b).astype(BF16)
            part = jnp.dot(z, w2_ref[c0:c0 + half, :], preferred_element_type=F32)
            out = part if out is None else out + part
        xn = xs + mod_ref[sb, 5:6, :] * out
        if final:
            ms = jnp.mean(xn * xn, axis=-1, keepdims=True)
            xn = xn * lax.rsqrt(ms + EPS) * gf_ref[...]
        o_ref[rows, :] = xn


def _ffn(x, modtab, g2, w1, w3, w2, gfin, n_rows, final):
    tm = ROW_TILE
    nb = tm // ROW_BLOCK
    d_ff = w1.shape[1]
    row = lambda i: (i, 0)
    out_rows = n_rows if final else x.shape[0]
    return pl.pallas_call(
        functools.partial(_ffn_kernel, final=final),
        out_shape=jax.ShapeDtypeStruct((out_rows, D_MODEL), F32),
        grid=(n_rows // tm,),
        in_specs=[pl.BlockSpec((tm, D_MODEL), row),
                  pl.BlockSpec((nb, 6, D_MODEL), lambda i: (i, 0, 0)),
                  _resident((1, D_MODEL)),
                  _resident((D_MODEL, d_ff)),
                  _resident((D_MODEL, d_ff)),
                  _resident((d_ff, D_MODEL)),
                  _resident((1, D_MODEL))],
        out_specs=pl.BlockSpec((tm, D_MODEL), row),
        input_output_aliases={} if final else {0: 0},
        compiler_params=_params(("parallel",)),
        name="ffn",
    )(x, modtab, g2, w1, w3, w2, gfin)


def _dup_halves(w, base):
    h0 = w[..., base:base + HEAD_DIM]
    h1 = w[..., base + HEAD_DIM:base + 2 * HEAD_DIM]
    return [h0, h0, h1, h1]


def _arrange_in_proj(w):
    o_wq = 4 * WIDTH + 2 * N_GATE
    o_wk, o_wv = o_wq + WIDTH, o_wq + WIDTH + 128
    o_gq = o_wv + 128
    o_gk, o_gv = o_gq + WIDTH, o_gq + WIDTH + 128
    o_gate = o_gv + 128
    main = jnp.concatenate(
        [w[..., :4 * WIDTH], w[..., o_wq:o_wq + WIDTH]]
        + _dup_halves(w, o_wk) + _dup_halves(w, o_wv)
        + [w[..., o_gate:o_gate + 3 * D_MODEL], w[..., o_gq:o_gq + WIDTH]]
        + _dup_halves(w, o_gk), axis=-1)
    gates = w[..., 4 * WIDTH:4 * WIDTH + 2 * N_GATE]
    glb_v = w[..., o_gv:o_gv + KV_HEADS * HEAD_DIM]
    return main, gates, glb_v


def _rope_tables(batch, n_lat, n_ctx):
    t = jnp.arange(n_lat)
    quarter = HEAD_DIM // 4
    inv = ROPE_THETA ** (-jnp.arange(0, 2 * quarter, 2, dtype=F32) / (2 * quarter))
    ang_r = (t // GRID_W).astype(F32)[:, None] * inv
    ang_c = (t % GRID_W).astype(F32)[:, None] * inv
    cos = jnp.concatenate([jnp.cos(ang_r)] * 2 + [jnp.cos(ang_c)] * 2, axis=1)
    sin = jnp.concatenate([-jnp.sin(ang_r), jnp.sin(ang_r), -jnp.sin(ang_c), jnp.sin(ang_c)], axis=1)
    cos = jnp.tile(cos, (batch, LANES // HEAD_DIM))
    sin = jnp.tile(sin, (batch, LANES // HEAD_DIM))
    pad = batch * n_ctx
    return (jnp.concatenate([cos, jnp.ones((pad, LANES), F32)], axis=0),
            jnp.concatenate([sin, jnp.zeros((pad, LANES), F32)], axis=0))


def kernel(x, c, ctx, c_ctx, w_mod, b_mod, norm1_g, w_in, b_in, ml_norm_g, win_sink, qn_g, kn_g,
           w_br, w_o, norm2_g, w_ff1, w_ff3, w_ff2, final_g):
    batch, n_lat, d = x.shape
    n_ctx = ctx.shape[1]
    depth = w_mod.shape[0]
    assert d == D_MODEL and n_lat % ROW_TILE == 0 and n_lat % GRID_W == 0
    assert (batch * n_ctx) % ROW_TILE == 0 and n_ctx % ROW_BLOCK == 0 and (batch * n_lat) % n_ctx == 0
    lat_rows, ctx_rows = batch * n_lat, batch * n_ctx

    xs = jnp.concatenate([x.reshape(lat_rows, d), ctx.reshape(ctx_rows, d)], axis=0)
    cvec = jnp.concatenate([c, c_ctx[None, :], jnp.zeros((8 - batch - 1, d), F32)], axis=0)
    block_class = np.concatenate([np.repeat(np.arange(batch), n_lat // ROW_BLOCK),
                                  np.full(ctx_rows // ROW_BLOCK, batch)])
    cos, sin = _rope_tables(batch, n_lat, n_ctx)
    avg = jnp.asarray(np.kron(np.eye(LANES // HEAD_DIM), np.full((HEAD_DIM, HEAD_DIM), 1.0 / HEAD_DIM)), BF16)

    out = None
    for l in range(depth):
        last = l == depth - 1
        mod = _mod_vectors(cvec, w_mod[l], b_mod[l]).reshape(8, 6, d)
        modtab = mod[block_class]

        w_main, w_gate, w_gv = _arrange_in_proj(w_in[l])
        b_main, b_gate, b_gv = _arrange_in_proj(b_in[l][None, :])
        wg = jnp.pad(w_gate, ((0, 0), (0, LANES - 2 * N_GATE))).astype(BF16)
        bg = jnp.pad(b_gate, ((0, 0), (0, LANES - 2 * N_GATE)))
        p, g, gt, vt = _in_projection(
            xs, modtab, norm1_g[l][None, :], w_main.astype(BF16), b_main, wg, bg,
            w_gate.T.astype(BF16), b_gate.T, w_gv.T.astype(BF16), b_gv.T,
            cos, sin, jnp.tile(qn_g[l], 2)[None, :], jnp.tile(kn_g[l], 2)[None, :], avg)

        hf, hb = _mlstm(p, g, gt, batch, n_lat, n_ctx)
        yb = _window_attention(p, win_sink[l], batch, n_lat, n_ctx, ctx_queries=not last)
        yc = _global_attention(p, vt, batch, n_lat, n_ctx, ctx_queries=not last)

        n_rows = lat_rows if last else lat_rows + ctx_rows
        xs = _merge(xs, modtab, p, hf, hb, yb, yc, ml_norm_g[l].reshape(1, WIDTH),
                    w_br[l].astype(BF16), w_o[l].astype(BF16), n_rows)
        out = _ffn(xs, modtab, norm2_g[l][None, :], w_ff1[l].astype(BF16), w_ff3[l].astype(BF16),
                   w_ff2[l].astype(BF16), final_g[None, :], n_rows, final=last)
        xs = out
    return out.reshape(batch, n_lat, d)
```

```python
import functools

import jax
import jax.numpy as jnp
import numpy as np
from jax import lax
from jax.experimental import pallas as pl
from jax.experimental.pallas import tpu as pltpu

F32 = jnp.float32
BF16 = jnp.bfloat16

D_MODEL = 1024
GRID_W = 64
CHUNK = 128
HEAD_DIM = 64
ROPE_THETA = 10000.0
EPS = 1e-6
ML_HEADS = 4
ML_DIM = 128
Q_HEADS = 8
KV_HEADS = 2
WIDTH = 512
N_GATE = 2 * ML_HEADS

LANES = 128
ROW_BLOCK = 256
ROW_TILE = 512
VMEM_LIMIT = 56 * 1024 * 1024

C_GATE = 0
C_MLQ, C_MLV, C_MLO = 3072, 3584, 4096
C_WQ, C_WK = 4608, 5120
C_GQ, C_GK = 5376, 5888
N_PROJ = C_GK + 256
VT_GLOBAL, VT_WINDOW = 0, 1
LOG2E = 1.4426950408889634
N_SCORE_BUFS = 4

NT_DIMS = (((1,), (1,)), ((), ()))


def _params(sem, vmem=VMEM_LIMIT):
    return pltpu.CompilerParams(dimension_semantics=sem, vmem_limit_bytes=vmem)


def _resident(shape):
    nd = len(shape)
    return pl.BlockSpec(shape, lambda *_: (0,) * nd, pipeline_mode=pl.Buffered(1))


def _sigmoid(x):
    return 1.0 / (1.0 + jnp.exp(-x))


def _log_sigmoid(x):
    return jnp.minimum(x, 0.0) - jnp.log(1.0 + jnp.exp(-jnp.abs(x)))


def _mod_kernel(c_ref, w_ref, b_ref, o_ref):
    c = c_ref[...]
    s = c * _sigmoid(c)
    o_ref[...] = jnp.dot(s, w_ref[...], preferred_element_type=F32) + b_ref[...]


def _mod_vectors(cvec, w_mod, b_mod):
    n_out = w_mod.shape[1]
    tn = 1536
    return pl.pallas_call(
        _mod_kernel,
        out_shape=jax.ShapeDtypeStruct((cvec.shape[0], n_out), F32),
        grid=(n_out // tn,),
        in_specs=[pl.BlockSpec(cvec.shape, lambda j: (0, 0)),
                  pl.BlockSpec((D_MODEL, tn), lambda j: (0, j)),
                  pl.BlockSpec((1, tn), lambda j: (0, j))],
        out_specs=pl.BlockSpec((cvec.shape[0], tn), lambda j: (0, j)),
        compiler_params=_params(("parallel",)),
        name="mod_vectors",
    )(cvec, w_mod, b_mod.reshape(1, n_out))


def _rope(acc, cos, sin, first_half):
    w = acc.shape[1]
    reps = w // LANES
    if reps > 1:
        cos = jnp.concatenate([cos] * reps, axis=1)
        sin = jnp.concatenate([sin] * reps, axis=1)
    ahead = pltpu.roll(acc, w - 16, axis=1)
    behind = pltpu.roll(acc, 16, axis=1)
    return acc * cos + jnp.where(first_half, ahead, behind) * sin


def _head_rms(acc, avg, gain):
    sq = acc * acc
    hi = sq.astype(BF16)
    lo = (sq - hi.astype(F32)).astype(BF16)
    outs = []
    for t in range(acc.shape[1] // LANES):
        sl = slice(t * LANES, (t + 1) * LANES)
        ms = (jnp.dot(hi[:, sl], avg, preferred_element_type=F32)
              + jnp.dot(lo[:, sl], avg, preferred_element_type=F32))
        outs.append(acc[:, sl] * lax.rsqrt(ms + EPS) * gain)
    return jnp.concatenate(outs, axis=1)


def _inproj_kernel(x_ref, mod_ref, g1_ref, w_ref, b_ref, wgt_ref, bgt_ref, wkt_ref, bkt_ref,
                   wvt_ref, bvt_ref, cos_ref, sin_ref, qg_ref, kg_ref, avg_ref,
                   p_ref, gt_ref, kt_ref, vt_ref, h_ref):
    tm = x_ref.shape[0]
    for sb in range(tm // ROW_BLOCK):
        r0 = sb * ROW_BLOCK
        xs = x_ref[r0:r0 + ROW_BLOCK, :]
        ms = jnp.mean(xs * xs, axis=-1, keepdims=True)
        y = xs * lax.rsqrt(ms + EPS) * g1_ref[...]
        shift = mod_ref[sb, 0:1, :]
        scale = mod_ref[sb, 1:2, :]
        h_ref[r0:r0 + ROW_BLOCK, :] = (y * (1.0 + scale) + shift).astype(BF16)

    def first_half(width):
        return (lax.broadcasted_iota(jnp.int32, (1, width), 1) % 32) < 16

    avg = avg_ref[...]
    qg = qg_ref[...]
    kg = kg_ref[...]
    q_scale = HEAD_DIM ** -0.5

    for sb in range(tm // ROW_BLOCK):
        r0 = sb * ROW_BLOCK
        rows = slice(r0, r0 + ROW_BLOCK)
        h = h_ref[rows, :]
        cos = cos_ref[rows, :]
        sin = sin_ref[rows, :]

        def proj(c0, width):
            return (jnp.dot(h, w_ref[:, c0:c0 + width], preferred_element_type=F32)
                    + b_ref[:, c0:c0 + width])

        for c0 in list(range(C_GATE, C_GATE + 3 * D_MODEL, WIDTH)) + [C_MLQ, C_MLV, C_MLO]:
            p_ref[rows, c0:c0 + WIDTH] = proj(c0, WIDTH).astype(BF16)
        wq = _rope(proj(C_WQ, WIDTH), cos, sin, first_half(WIDTH)) * q_scale
        p_ref[rows, C_WQ:C_WQ + WIDTH] = wq.astype(BF16)
        wk = _rope(proj(C_WK, 256), cos, sin, first_half(256))
        p_ref[rows, C_WK:C_WK + 256] = wk.astype(BF16)
        gq = _rope(_head_rms(proj(C_GQ, WIDTH), avg, qg), cos, sin, first_half(WIDTH)) * (q_scale * LOG2E)
        p_ref[rows, C_GQ:C_GQ + WIDTH] = gq.astype(BF16)
        gk = _rope(_head_rms(proj(C_GK, 256), avg, kg), cos, sin, first_half(256))
        p_ref[rows, C_GK:C_GK + 256] = gk.astype(BF16)

    def proj_t(wt_ref, bt_ref):
        return lax.dot_general(wt_ref[...], h_ref[...], NT_DIMS, preferred_element_type=F32) + bt_ref[...]

    gta = proj_t(wgt_ref, bgt_ref)
    grow = lax.broadcasted_iota(jnp.int32, (2 * N_GATE, 1), 0)
    gt_ref[...] = jnp.where(grow >= N_GATE, _log_sigmoid(gta), gta)
    kt_ref[...] = proj_t(wkt_ref, bkt_ref).astype(BF16)
    vt_ref[...] = proj_t(wvt_ref, bvt_ref).astype(BF16)


def _in_projection(x, modtab, g1, w, b, wgt, bgt, wkt, bkt, wvt, bvt, cos, sin, qg, kg, avg):
    rows = x.shape[0]
    tm = ROW_TILE
    nb = tm // ROW_BLOCK
    kv = 2 * KV_HEADS * HEAD_DIM
    return pl.pallas_call(
        _inproj_kernel,
        out_shape=(jax.ShapeDtypeStruct((rows, N_PROJ), BF16),
                   jax.ShapeDtypeStruct((2 * N_GATE, rows), F32),
                   jax.ShapeDtypeStruct((WIDTH, rows), BF16),
                   jax.ShapeDtypeStruct((kv, rows), BF16)),
        grid=(rows // tm,),
        in_specs=[pl.BlockSpec((tm, D_MODEL), lambda i: (i, 0)),
                  pl.BlockSpec((nb, 6, D_MODEL), lambda i: (i, 0, 0)),
                  _resident((1, D_MODEL)),
                  _resident((D_MODEL, N_PROJ)),
                  _resident((1, N_PROJ)),
                  _resident((2 * N_GATE, D_MODEL)),
                  _resident((2 * N_GATE, 1)),
                  _resident((WIDTH, D_MODEL)),
                  _resident((WIDTH, 1)),
                  _resident((kv, D_MODEL)),
                  _resident((kv, 1)),
                  pl.BlockSpec((tm, LANES), lambda i: (i, 0)),
                  pl.BlockSpec((tm, LANES), lambda i: (i, 0)),
                  _resident((1, LANES)),
                  _resident((1, LANES)),
                  _resident((LANES, LANES))],
        out_specs=(pl.BlockSpec((tm, N_PROJ), lambda i: (i, 0)),
                   pl.BlockSpec((2 * N_GATE, tm), lambda i: (0, i)),
                   pl.BlockSpec((WIDTH, tm), lambda i: (0, i)),
                   pl.BlockSpec((kv, tm), lambda i: (0, i))),
        scratch_shapes=[pltpu.VMEM((tm, D_MODEL), BF16)],
        compiler_params=_params(("parallel",)),
        name="in_projection",
    )(x, modtab, g1, w, b, wgt, bgt, wkt, bkt, wvt, bvt, cos, sin, qg, kg, avg)


def _scan_lanes(x, op, fill, reverse):
    n = x.shape[1]
    lane = lax.broadcasted_iota(jnp.int32, x.shape, 1)
    sh = 1
    while sh < n:
        if reverse:
            moved = jnp.where(lane < n - sh, pltpu.roll(x, n - sh, axis=1), fill)
        else:
            moved = jnp.where(lane >= sh, pltpu.roll(x, sh, axis=1), fill)
        x = op(x, moved)
        sh *= 2
    return x


def _mlstm_kernel(qvf_ref, ktf_ref, gtf_ref, qvb_ref, ktb_ref, gtb_ref, hf_ref, hb_ref, cn_ref, m_ref):
    @pl.when(pl.program_id(1) == 0)
    def _():
        cn_ref[...] = jnp.zeros_like(cn_ref)
        m_ref[...] = jnp.zeros_like(m_ref)

    L = CHUNK
    row = lax.broadcasted_iota(jnp.int32, (L, L), 0)
    col = lax.broadcasted_iota(jnp.int32, (L, L), 1)
    scale = ML_DIM ** -0.5
    ones_v = jnp.ones((L, ML_DIM), BF16)
    pending = []
    for d, (qv_ref, kt_ref, gt_ref, h_ref) in enumerate(((qvf_ref, ktf_ref, gtf_ref, hf_ref),
                                                          (qvb_ref, ktb_ref, gtb_ref, hb_ref))):
        seen = (col <= row) if d == 0 else (col >= row)
        g0 = d * ML_HEADS
        ic = gt_ref[g0:g0 + ML_HEADS, :]
        lf = gt_ref[N_GATE + g0:N_GATE + g0 + ML_HEADS, :]
        r = ic - _scan_lanes(lf, jnp.add, 0.0, d == 1)
        r_max = jnp.max(r, axis=1, keepdims=True)
        b_end = jnp.sum(lf, axis=1, keepdims=True)
        for hh in range(ML_HEADS):
            idx = g0 + hh
            lanes = slice(hh * ML_DIM, (hh + 1) * ML_DIM)
            r_row = r[hh:hh + 1, :]
            m_old = m_ref[idx]
            m_end = jnp.maximum(m_old, r_max[hh:hh + 1, :])
            q = qv_ref[:, lanes]
            vo = jnp.concatenate([qv_ref[:, WIDTH + hh * ML_DIM:WIDTH + (hh + 1) * ML_DIM], ones_v], axis=1)
            kt = kt_ref[lanes, :]
            qsb = (q.astype(F32) * scale).astype(BF16)
            cn = cn_ref[idx]
            s_raw = jnp.dot(qsb, kt, preferred_element_type=F32)
            q_cn = jnp.dot(qsb, cn.astype(BF16), preferred_element_type=F32)
            kwt = (kt.astype(F32) * jnp.exp(r_row - m_end)).astype(BF16)
            cn_ref[idx] = jnp.exp(m_old - m_end) * cn + jnp.dot(kwt, vo, preferred_element_type=F32)
            m_ref[idx] = b_end[hh:hh + 1, :] + m_end
            pending.append((h_ref, lanes, seen, r_row, lf[hh:hh + 1, :], m_old, s_raw, q_cn, vo))

    for h_ref, lanes, seen, r_row, lf_row, m_old, s_raw, q_cn, vo in pending:
        b_col = jnp.sum(jnp.where(seen, lf_row, 0.0), axis=1, keepdims=True)
        m_col = jnp.maximum(m_old, jnp.max(jnp.where(seen, r_row, -jnp.inf), axis=1, keepdims=True))
        w = jnp.exp(jnp.where(seen, r_row - m_col, -jnp.inf))
        a = jnp.exp(m_old - m_col)
        s_vo = jnp.dot((s_raw * w).astype(BF16), vo, preferred_element_type=F32)
        num = a * q_cn[:, :ML_DIM] + s_vo[:, :ML_DIM]
        den = a * q_cn[:, ML_DIM:] + s_vo[:, ML_DIM:]
        hc = num / jnp.maximum(jnp.abs(den), jnp.exp(-(b_col + m_col)))
        h_ref[:, lanes] = hc.astype(h_ref.dtype)


def _mlstm(p, kt, gt, batch, n_lat, n_ctx):
    rows = p.shape[0]
    cl, cc = n_lat // CHUNK, n_ctx // CHUNK
    lat_base, ctx_base = 0, batch * cl

    def fwd_chunk(b, i):
        return jnp.where(i < cc, ctx_base + b * cc + i, lat_base + b * cl + (i - cc))

    def bwd_chunk(b, i):
        return jnp.where(i < cc, ctx_base + b * cc + (cc - 1 - i), lat_base + b * cl + (cl - 1 - (i - cc)))

    def specs(chunk):
        return [pl.BlockSpec((CHUNK, 2 * WIDTH), lambda b, i: (chunk(b, i), C_MLQ // (2 * WIDTH))),
                pl.BlockSpec((WIDTH, CHUNK), lambda b, i: (0, chunk(b, i))),
                pl.BlockSpec((2 * N_GATE, CHUNK), lambda b, i: (0, chunk(b, i)))]

    n_state = 2 * ML_HEADS
    return pl.pallas_call(
        _mlstm_kernel,
        out_shape=(jax.ShapeDtypeStruct((rows, WIDTH), BF16), jax.ShapeDtypeStruct((rows, WIDTH), BF16)),
        grid=(batch, cl + cc),
        in_specs=specs(fwd_chunk) + specs(bwd_chunk),
        out_specs=(pl.BlockSpec((CHUNK, WIDTH), lambda b, i: (fwd_chunk(b, i), 0)),
                   pl.BlockSpec((CHUNK, WIDTH), lambda b, i: (bwd_chunk(b, i), 0))),
        scratch_shapes=[pltpu.VMEM((n_state, ML_DIM, 2 * ML_DIM), F32),
                        pltpu.VMEM((n_state, 1, 1), F32)],
        compiler_params=_params(("parallel", "arbitrary")),
        name="mlstm_scan",
    )(p, kt, gt, p, kt, gt)


def _half_mask(e):
    lane = lax.broadcasted_iota(jnp.int32, (1, LANES), 1)
    return (lane < HEAD_DIM) if e == 0 else (lane >= HEAD_DIM)


def _win_kernel(sink_ref, q_ref, kc_ref, vtc_ref, k0_ref, k1_ref, k2_ref, k3_ref,
                vt0_ref, vt1_ref, vt2_ref, vt3_ref, o_ref, *, n_lat_tiles, n_lat_blocks):
    t = pl.program_id(1)
    tq = q_ref.shape[0]
    n_ctx = kc_ref.shape[0]
    is_lat = t < n_lat_tiles
    kk = lax.broadcasted_iota(jnp.int32, (CHUNK, tq), 0)
    qq = lax.broadcasted_iota(jnp.int32, (CHUNK, tq), 1)
    band = (kk >= qq, qq <= kk + CHUNK, kk <= qq, kk + CHUNK <= qq)
    first = 2 * t - 1
    present = [jnp.logical_and(is_lat, jnp.logical_and(first + i >= 0, first + i < n_lat_blocks))
               for i in range(4)]
    k_refs = (k0_ref, k1_ref, k2_ref, k3_ref)
    vt_refs = (vt0_ref, vt1_ref, vt2_ref, vt3_ref)
    ones = jnp.ones((16, n_ctx + 4 * CHUNK), BF16)

    for g in range(KV_HEADS):
        lanes = slice(g * LANES, (g + 1) * LANES)
        k_all = jnp.concatenate([kc_ref[:, lanes]] + [r[:, lanes] for r in k_refs], axis=0)
        vrows = slice(g * HEAD_DIM, (g + 1) * HEAD_DIM)
        vta = jnp.concatenate(
            [jnp.concatenate([vtc_ref[vrows, :]] + [r[vrows, :] for r in vt_refs], axis=1), ones], axis=0)
        scores = []
        for tt in range(2 * g, 2 * g + 2):
            qf = q_ref[:, tt * LANES:(tt + 1) * LANES].astype(F32)
            for e in range(2):
                qt = jnp.where(_half_mask(e), qf, 0.0).T.astype(BF16)
                scores.append(jnp.dot(k_all, qt, preferred_element_type=F32))
        outs = []
        for j, s in enumerate(scores):
            sink = sink_ref[4 * g + j]
            parts = [s[:n_ctx, :]]
            for i in range(4):
                blk = s[n_ctx + i * CHUNK:n_ctx + (i + 1) * CHUNK, :]
                blk = jnp.where(band[i], blk, -jnp.inf)
                parts.append(jnp.where(present[i], blk, -jnp.inf))
            s = jnp.concatenate(parts, axis=0)
            m = jnp.maximum(jnp.max(s, axis=0, keepdims=True), sink)
            pt = jnp.exp(s - m).astype(BF16)
            ol = jnp.dot(vta, pt, preferred_element_type=F32)
            den = ol[HEAD_DIM:HEAD_DIM + 1, :] + jnp.exp(sink - m)
            outs.append(ol[:HEAD_DIM, :] / den)
        for j in range(2):
            tt = 2 * g + j
            o_ref[:, tt * LANES:(tt + 1) * LANES] = (
                jnp.concatenate(outs[2 * j:2 * j + 2], axis=0).T.astype(o_ref.dtype))


def _window_attention(p, vt, sink, batch, n_lat, n_ctx, ctx_queries):
    tq = 256
    nlt, nct = n_lat // tq, n_ctx // tq
    nlb = n_lat // CHUNK
    nq = nlt + (nct if ctx_queries else 0)
    out_rows = batch * (n_lat + (n_ctx if ctx_queries else 0))
    ctx_blk = batch * n_lat // n_ctx
    k_col = C_WK // 256

    def q_block(b, t):
        return jnp.where(t < nlt, b * nlt + t, batch * nlt + b * nct + (t - nlt))

    def near(i):
        return lambda b, t: b * nlb + jnp.clip(2 * t - 1 + i, 0, nlb - 1)

    k_specs = [pl.BlockSpec((CHUNK, 256), (lambda f: lambda b, t: (f(b, t), k_col))(near(i))) for i in range(4)]
    vt_specs = [pl.BlockSpec((LANES, CHUNK), (lambda f: lambda b, t: (VT_WINDOW, f(b, t)))(near(i)))
                for i in range(4)]
    return pl.pallas_call(
        functools.partial(_win_kernel, n_lat_tiles=nlt, n_lat_blocks=nlb),
        out_shape=jax.ShapeDtypeStruct((out_rows, WIDTH), BF16),
        grid=(batch, nq),
        in_specs=[pl.BlockSpec(memory_space=pltpu.SMEM),
                  pl.BlockSpec((tq, WIDTH), lambda b, t: (q_block(b, t), C_WQ // WIDTH)),
                  pl.BlockSpec((n_ctx, 256), lambda b, t: (ctx_blk + b, k_col)),
                  pl.BlockSpec((LANES, n_ctx), lambda b, t: (VT_WINDOW, ctx_blk + b))]
                 + k_specs + vt_specs,
        out_specs=pl.BlockSpec((tq, WIDTH), lambda b, t: (q_block(b, t), 0)),
        compiler_params=_params(("parallel", "parallel")),
        name="window_attention",
    )(sink, p, p, vt, p, p, p, p, vt, vt, vt, vt)


def _glb_kernel(q_ref, kc_ref, vtc_ref, kl_ref, vtl_ref, o_ref, qt_ref, *scratch, n_lat_tiles, tk):
    heads = Q_HEADS // KV_HEADS
    acc_refs = scratch[0:heads]
    sc_refs = scratch[heads:2 * heads]
    ring = [scratch[(2 + i) * heads:(3 + i) * heads] for i in range(N_SCORE_BUFS)]
    qi = pl.program_id(2)
    tq = q_ref.shape[0]
    n_chunks = kl_ref.shape[0] // tk
    for t in range(heads // 2):
        qf = q_ref[:, t * LANES:(t + 1) * LANES].astype(F32)
        for e in range(2):
            h = 2 * t + e
            qt_ref[:, h * tq:(h + 1) * tq] = jnp.where(_half_mask(e), qf, 0.0).T.astype(BF16)
    for h in range(heads):
        acc_refs[h][...] = jnp.zeros_like(acc_refs[h])
    neg_inf = (jnp.full((1, tq), -jnp.inf, F32),) * heads

    def fill(s_refs, k, m_run):
        out = []
        for h in range(heads):
            s = jnp.dot(k, qt_ref[:, h * tq:(h + 1) * tq], preferred_element_type=F32)
            s_refs[h][...] = s
            out.append(jnp.maximum(m_run[h], jnp.max(s, axis=0, keepdims=True)))
        return tuple(out)

    def drain(s_refs, vt, m_run, m_acc):
        ones = jnp.ones((acc_refs[0].shape[0] - HEAD_DIM, vt.shape[1]), BF16)
        vta = jnp.concatenate([vt, ones], axis=0)
        for h in range(heads):
            alpha = jnp.exp2(m_acc[h] - m_run[h])
            pt = jnp.exp2(s_refs[h][...] - m_run[h]).astype(BF16)
            acc_refs[h][...] = alpha * acc_refs[h][...] + jnp.dot(vta, pt, preferred_element_type=F32)
        return m_run

    def k_chunk(c):
        off = pl.multiple_of(jnp.minimum(c, n_chunks - 1) * tk, tk)
        return kl_ref[pl.ds(off, tk), :]

    def vt_chunk(c):
        return vtl_ref[:, pl.ds(pl.multiple_of(c * tk, tk), tk)]

    m_ctx = fill(sc_refs, kc_ref[...], neg_inf)

    @pl.when(qi >= n_lat_tiles)
    def _():
        drain(sc_refs, vtc_ref[...], m_ctx, neg_inf)

    @pl.when(qi < n_lat_tiles)
    def _():
        m_one = fill(ring[0], k_chunk(0), m_ctx)
        m_two = fill(ring[1], k_chunk(1), m_one)
        m_acc = drain(sc_refs, vtc_ref[...], m_one, neg_inf)

        def body(j, carry):
            m_run, m_acc = carry
            for i in range(N_SCORE_BUFS):
                c = N_SCORE_BUFS * j + i
                m_next = fill(ring[(i + 2) % N_SCORE_BUFS], k_chunk(c + 2), m_run)
                m_acc = drain(ring[i], vt_chunk(c), m_run, m_acc)
                m_run = m_next
            return m_run, m_acc
        lax.fori_loop(0, n_chunks // N_SCORE_BUFS, body, (m_two, m_acc))

    for t in range(heads // 2):
        pair = []
        for e in range(2):
            a = acc_refs[2 * t + e][...]
            pair.append(a[:HEAD_DIM, :] / a[HEAD_DIM:HEAD_DIM + 1, :])
        o_ref[:, t * LANES:(t + 1) * LANES] = jnp.concatenate(pair, axis=0).T.astype(o_ref.dtype)


def _global_attention(p, vt, batch, n_lat, n_ctx, ctx_queries):
    tq = 256
    tk = min(512, n_lat // N_SCORE_BUFS)
    assert n_lat % (tk * N_SCORE_BUFS) == 0 and tk % LANES == 0
    nl, nc = n_lat // tq, n_ctx // tq
    nq = nl + (nc if ctx_queries else 0)
    out_rows = batch * (n_lat + (n_ctx if ctx_queries else 0))
    ctx_base = batch * nl
    heads = Q_HEADS // KV_HEADS
    acc_rows = HEAD_DIM + 16

    def q_block(b, qi):
        return jnp.where(qi < nl, b * nl + qi, ctx_base + b * nc + (qi - nl))

    ctx_blk = batch * n_lat // n_ctx
    return pl.pallas_call(
        functools.partial(_glb_kernel, n_lat_tiles=nl, tk=tk),
        out_shape=jax.ShapeDtypeStruct((out_rows, WIDTH), BF16),
        grid=(batch, KV_HEADS, nq),
        in_specs=[pl.BlockSpec((tq, 2 * LANES), lambda b, g, qi: (q_block(b, qi), C_GQ // 256 + g)),
                  pl.BlockSpec((n_ctx, LANES), lambda b, g, qi: (ctx_blk + b, C_GK // LANES + g)),
                  pl.BlockSpec((HEAD_DIM, n_ctx), lambda b, g, qi: (KV_HEADS * VT_GLOBAL + g, ctx_blk + b)),
                  pl.BlockSpec((n_lat, LANES), lambda b, g, qi: (b, C_GK // LANES + g)),
                  pl.BlockSpec((HEAD_DIM, n_lat), lambda b, g, qi: (KV_HEADS * VT_GLOBAL + g, b))],
        out_specs=pl.BlockSpec((tq, 2 * LANES), lambda b, g, qi: (q_block(b, qi), g)),
        scratch_shapes=([pltpu.VMEM((LANES, heads * tq), BF16)]
                        + [pltpu.VMEM((acc_rows, tq), F32)] * heads
                        + [pltpu.VMEM((n_ctx, tq), F32)] * heads
                        + [pltpu.VMEM((tk, tq), F32)] * (N_SCORE_BUFS * heads)),
        compiler_params=_params(("parallel", "parallel", "arbitrary")),
        name="global_attention",
    )(p, p, vt, p, vt)


def _merge_kernel(x_ref, mod_ref, hf_ref, hb_ref, og_ref, yb_ref, yc_ref, ga_ref, gb_ref, gc_ref,
                  mlg_ref, wbr_ref, wo_ref, o_ref):
    tm = x_ref.shape[0]
    for sb in range(tm // ROW_BLOCK):
        rows = slice(sb * ROW_BLOCK, (sb + 1) * ROW_BLOCK)
        hs = hf_ref[rows, :].astype(F32) + hb_ref[rows, :].astype(F32)
        parts = []
        for t in range(ML_HEADS):
            ht = hs[:, t * ML_DIM:(t + 1) * ML_DIM]
            ms = jnp.mean(ht * ht, axis=-1, keepdims=True)
            parts.append(ht * lax.rsqrt(ms + EPS))
        ya = (jnp.concatenate(parts, axis=1) * mlg_ref[...]
              * _sigmoid(og_ref[rows, :].astype(F32))).astype(BF16)
        merged = None
        for y, gate_ref, i in ((ya, ga_ref, 0), (yb_ref[rows, :], gb_ref, 1), (yc_ref[rows, :], gc_ref, 2)):
            term = (_sigmoid(gate_ref[rows, :].astype(F32))
                    * jnp.dot(y, wbr_ref[i], preferred_element_type=F32))
            merged = term if merged is None else merged + term
        out = jnp.dot(merged.astype(BF16), wo_ref[...], preferred_element_type=F32)
        o_ref[rows, :] = x_ref[rows, :] + mod_ref[sb, 2:3, :] * out


def _merge(x, modtab, p, hf, hb, yb, yc, mlg, wbr, wo, n_rows):
    tm = ROW_TILE
    nb = tm // ROW_BLOCK
    row = lambda i: (i, 0)
    gate = lambda k: pl.BlockSpec((tm, D_MODEL), lambda i: (i, C_GATE // D_MODEL + k))
    return pl.pallas_call(
        _merge_kernel,
        out_shape=jax.ShapeDtypeStruct(x.shape, F32),
        grid=(n_rows // tm,),
        in_specs=[pl.BlockSpec((tm, D_MODEL), row),
                  pl.BlockSpec((nb, 6, D_MODEL), lambda i: (i, 0, 0)),
                  pl.BlockSpec((tm, WIDTH), row),
                  pl.BlockSpec((tm, WIDTH), row),
                  pl.BlockSpec((tm, WIDTH), lambda i: (i, C_MLO // WIDTH)),
                  pl.BlockSpec((tm, WIDTH), row),
                  pl.BlockSpec((tm, WIDTH), row),
                  gate(0), gate(1), gate(2),
                  _resident((1, WIDTH)),
                  _resident((3, WIDTH, D_MODEL)),
                  _resident((D_MODEL, D_MODEL))],
        out_specs=pl.BlockSpec((tm, D_MODEL), row),
        input_output_aliases={0: 0},
        compiler_params=_params(("parallel",)),
        name="merge",
    )(x, modtab, hf, hb, p, yb, yc, p, p, p, mlg, wbr, wo)


def _ffn_kernel(x_ref, mod_ref, g2_ref, w1_ref, w3_ref, w2_ref, gf_ref, o_ref, *, final):
    tm = x_ref.shape[0]
    d_ff = w1_ref.shape[1]
    half = d_ff // 2
    for sb in range(tm // ROW_BLOCK):
        rows = slice(sb * ROW_BLOCK, (sb + 1) * ROW_BLOCK)
        xs = x_ref[rows, :]
        ms = jnp.mean(xs * xs, axis=-1, keepdims=True)
        y = xs * lax.rsqrt(ms + EPS) * g2_ref[...]
        h = (y * (1.0 + mod_ref[sb, 4:5, :]) + mod_ref[sb, 3:4, :]).astype(BF16)
        out = None
        for c0 in (0, half):
            a = jnp.dot(h, w1_ref[:, c0:c0 + half], preferred_element_type=F32)
            b = jnp.dot(h, w3_ref[:, c0:c0 + half], preferred_element_type=F32)
            z = (a * _sigmoid(a) * b).astype(BF16)
            part = jnp.dot(z, w2_ref[c0:c0 + half, :], preferred_element_type=F32)
            out = part if out is None else out + part
        xn = xs + mod_ref[sb, 5:6, :] * out
        if final:
            ms = jnp.mean(xn * xn, axis=-1, keepdims=True)
            xn = xn * lax.rsqrt(ms + EPS) * gf_ref[...]
        o_ref[rows, :] = xn


def _ffn(x, modtab, g2, w1, w3, w2, gfin, n_rows, final):
    tm = ROW_TILE
    nb = tm // ROW_BLOCK
    d_ff = w1.shape[1]
    row = lambda i: (i, 0)
    out_rows = n_rows if final else x.shape[0]
    return pl.pallas_call(
        functools.partial(_ffn_kernel, final=final),
        out_shape=jax.ShapeDtypeStruct((out_rows, D_MODEL), F32),
        grid=(n_rows // tm,),
        in_specs=[pl.BlockSpec((tm, D_MODEL), row),
                  pl.BlockSpec((nb, 6, D_MODEL), lambda i: (i, 0, 0)),
                  _resident((1, D_MODEL)),
                  _resident((D_MODEL, d_ff)),
                  _resident((D_MODEL, d_ff)),
                  _resident((d_ff, D_MODEL)),
                  _resident((1, D_MODEL))],
        out_specs=pl.BlockSpec((tm, D_MODEL), row),
        input_output_aliases={} if final else {0: 0},
        compiler_params=_params(("parallel",)),
        name="ffn",
    )(x, modtab, g2, w1, w3, w2, gfin)


def _dup_halves(w, base):
    h0 = w[..., base:base + HEAD_DIM]
    h1 = w[..., base + HEAD_DIM:base + 2 * HEAD_DIM]
    return [h0, h0, h1, h1]


def _arrange_in_proj(w):
    o_gate_ml = 4 * WIDTH
    o_wq = o_gate_ml + 2 * N_GATE
    o_wk, o_wv = o_wq + WIDTH, o_wq + WIDTH + 128
    o_gq = o_wv + 128
    o_gk, o_gv = o_gq + WIDTH, o_gq + WIDTH + 128
    o_gate = o_gv + 128
    main = jnp.concatenate(
        [w[..., o_gate:o_gate + 3 * D_MODEL], w[..., 0:WIDTH], w[..., 2 * WIDTH:4 * WIDTH],
         w[..., o_wq:o_wq + WIDTH]]
        + _dup_halves(w, o_wk) + [w[..., o_gq:o_gq + WIDTH]] + _dup_halves(w, o_gk), axis=-1)
    values = jnp.concatenate([w[..., o_gv:o_gv + 128], w[..., o_wv:o_wv + 128]], axis=-1)
    return main, w[..., o_gate_ml:o_gate_ml + 2 * N_GATE], w[..., WIDTH:2 * WIDTH], values


def _rope_tables(batch, n_lat, n_ctx):
    t = jnp.arange(n_lat)
    quarter = HEAD_DIM // 4
    inv = ROPE_THETA ** (-jnp.arange(0, 2 * quarter, 2, dtype=F32) / (2 * quarter))
    ang_r = (t // GRID_W).astype(F32)[:, None] * inv
    ang_c = (t % GRID_W).astype(F32)[:, None] * inv
    cos = jnp.concatenate([jnp.cos(ang_r)] * 2 + [jnp.cos(ang_c)] * 2, axis=1)
    sin = jnp.concatenate([-jnp.sin(ang_r), jnp.sin(ang_r), -jnp.sin(ang_c), jnp.sin(ang_c)], axis=1)
    cos = jnp.tile(cos, (batch, LANES // HEAD_DIM))
    sin = jnp.tile(sin, (batch, LANES // HEAD_DIM))
    pad = batch * n_ctx
    return (jnp.concatenate([cos, jnp.ones((pad, LANES), F32)], axis=0),
            jnp.concatenate([sin, jnp.zeros((pad, LANES), F32)], axis=0))


def kernel(x, c, ctx, c_ctx, w_mod, b_mod, norm1_g, w_in, b_in, ml_norm_g, win_sink, qn_g, kn_g,
           w_br, w_o, norm2_g, w_ff1, w_ff3, w_ff2, final_g):
    batch, n_lat, d = x.shape
    n_ctx = ctx.shape[1]
    depth = w_mod.shape[0]
    assert d == D_MODEL and n_lat % ROW_TILE == 0 and n_lat % GRID_W == 0
    assert (batch * n_ctx) % ROW_TILE == 0 and n_ctx % ROW_BLOCK == 0 and (batch * n_lat) % n_ctx == 0
    lat_rows, ctx_rows = batch * n_lat, batch * n_ctx

    xs = jnp.concatenate([x.reshape(lat_rows, d), ctx.reshape(ctx_rows, d)], axis=0)
    cvec = jnp.concatenate([c, c_ctx[None, :], jnp.zeros((8 - batch - 1, d), F32)], axis=0)
    block_class = np.concatenate([np.repeat(np.arange(batch), n_lat // ROW_BLOCK),
                                  np.full(ctx_rows // ROW_BLOCK, batch)])
    cos, sin = _rope_tables(batch, n_lat, n_ctx)
    avg = jnp.asarray(np.kron(np.eye(LANES // HEAD_DIM), np.full((HEAD_DIM, HEAD_DIM), 1.0 / HEAD_DIM)), BF16)

    out = None
    for l in range(depth):
        last = l == depth - 1
        mod = _mod_vectors(cvec, w_mod[l], b_mod[l]).reshape(8, 6, d)
        modtab = mod[block_class]

        w_main, w_gate, w_mlk, w_val = _arrange_in_proj(w_in[l])
        b_main, b_gate, b_mlk, b_val = _arrange_in_proj(b_in[l][None, :])
        p, gt, kt, vt = _in_projection(
            xs, modtab, norm1_g[l][None, :], w_main.astype(BF16), b_main,
            w_gate.T.astype(BF16), b_gate.T, w_mlk.T.astype(BF16), b_mlk.T, w_val.T.astype(BF16), b_val.T,
            cos, sin, jnp.tile(qn_g[l], 2)[None, :], jnp.tile(kn_g[l], 2)[None, :], avg)

        hf, hb = _mlstm(p, kt, gt, batch, n_lat, n_ctx)
        yb = _window_attention(p, vt, win_sink[l], batch, n_lat, n_ctx, ctx_queries=not last)
        yc = _global_attention(p, vt, batch, n_lat, n_ctx, ctx_queries=not last)

        n_rows = lat_rows if last else lat_rows + ctx_rows
        xs = _merge(xs, modtab, p, hf, hb, yb, yc, ml_norm_g[l].reshape(1, WIDTH),
                    w_br[l].astype(BF16), w_o[l].astype(BF16), n_rows)
        out = _ffn(xs, modtab, norm2_g[l][None, :], w_ff1[l].astype(BF16), w_ff3[l].astype(BF16),
                   w_ff2[l].astype(BF16), final_g[None, :], n_rows, final=last)
        xs = out
    return out.reshape(batch, n_lat, d)
```

```python
import functools

import jax
import jax.numpy as jnp
import numpy as np
from jax import lax
from jax.experimental import pallas as pl
from jax.experimental.pallas import tpu as pltpu

F32 = jnp.float32
BF16 = jnp.bfloat16

D_MODEL = 1024
GRID_W = 64
CHUNK = 128
HEAD_DIM = 64
ROPE_THETA = 10000.0
EPS = 1e-6
ML_HEADS = 4
ML_DIM = 128
Q_HEADS = 8
KV_HEADS = 2
WIDTH = 512
N_GATE = 2 * ML_HEADS

LANES = 128
ROW_BLOCK = 256
ROW_TILE = 512
VMEM_LIMIT = 56 * 1024 * 1024

C_GATE = 0
C_MLQ, C_MLV, C_MLO = 3072, 3584, 4096
C_WQ, C_WK = 4608, 5120
C_GQ, C_GK = 5376, 5888
N_PROJ = C_GK + 256
VT_GLOBAL, VT_WINDOW = 0, 1
LOG2E = 1.4426950408889634
N_SCORE_BUFS = 4
SAFE_LOG2_BOUND = 50.0

NT_DIMS = (((1,), (1,)), ((), ()))


def _params(sem, vmem=VMEM_LIMIT):
    return pltpu.CompilerParams(dimension_semantics=sem, vmem_limit_bytes=vmem)


def _resident(shape):
    nd = len(shape)
    return pl.BlockSpec(shape, lambda *_: (0,) * nd, pipeline_mode=pl.Buffered(1))


def _sigmoid(x):
    return 1.0 / (1.0 + jnp.exp(-x))


def _log_sigmoid(x):
    return jnp.minimum(x, 0.0) - jnp.log(1.0 + jnp.exp(-jnp.abs(x)))


def _mod_kernel(c_ref, w_ref, b_ref, o_ref):
    c = c_ref[...]
    s = c * _sigmoid(c)
    o_ref[...] = jnp.dot(s, w_ref[...], preferred_element_type=F32) + b_ref[...]


def _mod_vectors(cvec, w_mod, b_mod):
    n_out = w_mod.shape[1]
    tn = 1536
    return pl.pallas_call(
        _mod_kernel,
        out_shape=jax.ShapeDtypeStruct((cvec.shape[0], n_out), F32),
        grid=(n_out // tn,),
        in_specs=[pl.BlockSpec(cvec.shape, lambda j: (0, 0)),
                  pl.BlockSpec((D_MODEL, tn), lambda j: (0, j)),
                  pl.BlockSpec((1, tn), lambda j: (0, j))],
        out_specs=pl.BlockSpec((cvec.shape[0], tn), lambda j: (0, j)),
        compiler_params=_params(("parallel",)),
        name="mod_vectors",
    )(cvec, w_mod, b_mod.reshape(1, n_out))


def _rope(acc, cos, sin, first_half):
    w = acc.shape[1]
    reps = w // LANES
    if reps > 1:
        cos = jnp.concatenate([cos] * reps, axis=1)
        sin = jnp.concatenate([sin] * reps, axis=1)
    ahead = pltpu.roll(acc, w - 16, axis=1)
    behind = pltpu.roll(acc, 16, axis=1)
    return acc * cos + jnp.where(first_half, ahead, behind) * sin


def _head_rms(acc, avg, gain):
    sq = acc * acc
    hi = sq.astype(BF16)
    lo = (sq - hi.astype(F32)).astype(BF16)
    outs = []
    for t in range(acc.shape[1] // LANES):
        sl = slice(t * LANES, (t + 1) * LANES)
        ms = (jnp.dot(hi[:, sl], avg, preferred_element_type=F32)
              + jnp.dot(lo[:, sl], avg, preferred_element_type=F32))
        outs.append(acc[:, sl] * lax.rsqrt(ms + EPS) * gain)
    return jnp.concatenate(outs, axis=1)


def _inproj_kernel(x_ref, mod_ref, g1_ref, w_ref, b_ref, wgt_ref, bgt_ref, wkt_ref, bkt_ref,
                   wvt_ref, bvt_ref, cos_ref, sin_ref, qg_ref, kg_ref, avg_ref,
                   p_ref, gt_ref, kt_ref, vt_ref, h_ref):
    tm = x_ref.shape[0]
    for sb in range(tm // ROW_BLOCK):
        r0 = sb * ROW_BLOCK
        xs = x_ref[r0:r0 + ROW_BLOCK, :]
        ms = jnp.mean(xs * xs, axis=-1, keepdims=True)
        y = xs * lax.rsqrt(ms + EPS) * g1_ref[...]
        shift = mod_ref[sb, 0:1, :]
        scale = mod_ref[sb, 1:2, :]
        h_ref[r0:r0 + ROW_BLOCK, :] = (y * (1.0 + scale) + shift).astype(BF16)

    def first_half(width):
        return (lax.broadcasted_iota(jnp.int32, (1, width), 1) % 32) < 16

    avg = avg_ref[...]
    qg = qg_ref[...]
    kg = kg_ref[...]
    q_scale = HEAD_DIM ** -0.5

    for sb in range(tm // ROW_BLOCK):
        r0 = sb * ROW_BLOCK
        rows = slice(r0, r0 + ROW_BLOCK)
        h = h_ref[rows, :]
        cos = cos_ref[rows, :]
        sin = sin_ref[rows, :]

        def proj(c0, width):
            return (jnp.dot(h, w_ref[:, c0:c0 + width], preferred_element_type=F32)
                    + b_ref[:, c0:c0 + width])

        for c0 in list(range(C_GATE, C_GATE + 3 * D_MODEL, WIDTH)) + [C_MLQ, C_MLV, C_MLO]:
            p_ref[rows, c0:c0 + WIDTH] = proj(c0, WIDTH).astype(BF16)
        wq = _rope(proj(C_WQ, WIDTH), cos, sin, first_half(WIDTH)) * q_scale
        p_ref[rows, C_WQ:C_WQ + WIDTH] = wq.astype(BF16)
        wk = _rope(proj(C_WK, 256), cos, sin, first_half(256))
        p_ref[rows, C_WK:C_WK + 256] = wk.astype(BF16)
        gq = _rope(_head_rms(proj(C_GQ, WIDTH), avg, qg), cos, sin, first_half(WIDTH)) * (q_scale * LOG2E)
        p_ref[rows, C_GQ:C_GQ + WIDTH] = gq.astype(BF16)
        gk = _rope(_head_rms(proj(C_GK, 256), avg, kg), cos, sin, first_half(256))
        p_ref[rows, C_GK:C_GK + 256] = gk.astype(BF16)

    def proj_t(wt_ref, bt_ref):
        return lax.dot_general(wt_ref[...], h_ref[...], NT_DIMS, preferred_element_type=F32) + bt_ref[...]

    gta = proj_t(wgt_ref, bgt_ref)
    grow = lax.broadcasted_iota(jnp.int32, (2 * N_GATE, 1), 0)
    gt_ref[...] = jnp.where(grow >= N_GATE, _log_sigmoid(gta), gta)
    kt_ref[...] = proj_t(wkt_ref, bkt_ref).astype(BF16)
    vt_ref[...] = proj_t(wvt_ref, bvt_ref).astype(BF16)


def _in_projection(x, modtab, g1, w, b, wgt, bgt, wkt, bkt, wvt, bvt, cos, sin, qg, kg, avg):
    rows = x.shape[0]
    tm = ROW_TILE
    nb = tm // ROW_BLOCK
    kv = 2 * KV_HEADS * HEAD_DIM
    return pl.pallas_call(
        _inproj_kernel,
        out_shape=(jax.ShapeDtypeStruct((rows, N_PROJ), BF16),
                   jax.ShapeDtypeStruct((2 * N_GATE, rows), F32),
                   jax.ShapeDtypeStruct((WIDTH, rows), BF16),
                   jax.ShapeDtypeStruct((kv, rows), BF16)),
        grid=(rows // tm,),
        in_specs=[pl.BlockSpec((tm, D_MODEL), lambda i: (i, 0)),
                  pl.BlockSpec((nb, 6, D_MODEL), lambda i: (i, 0, 0)),
                  _resident((1, D_MODEL)),
                  _resident((D_MODEL, N_PROJ)),
                  _resident((1, N_PROJ)),
                  _resident((2 * N_GATE, D_MODEL)),
                  _resident((2 * N_GATE, 1)),
                  _resident((WIDTH, D_MODEL)),
                  _resident((WIDTH, 1)),
                  _resident((kv, D_MODEL)),
                  _resident((kv, 1)),
                  pl.BlockSpec((tm, LANES), lambda i: (i, 0)),
                  pl.BlockSpec((tm, LANES), lambda i: (i, 0)),
                  _resident((1, LANES)),
                  _resident((1, LANES)),
                  _resident((LANES, LANES))],
        out_specs=(pl.BlockSpec((tm, N_PROJ), lambda i: (i, 0)),
                   pl.BlockSpec((2 * N_GATE, tm), lambda i: (0, i)),
                   pl.BlockSpec((WIDTH, tm), lambda i: (0, i)),
                   pl.BlockSpec((kv, tm), lambda i: (0, i))),
        scratch_shapes=[pltpu.VMEM((tm, D_MODEL), BF16)],
        compiler_params=_params(("parallel",)),
        name="in_projection",
    )(x, modtab, g1, w, b, wgt, bgt, wkt, bkt, wvt, bvt, cos, sin, qg, kg, avg)


def _scan_lanes(x, op, fill, reverse):
    n = x.shape[1]
    lane = lax.broadcasted_iota(jnp.int32, x.shape, 1)
    sh = 1
    while sh < n:
        if reverse:
            moved = jnp.where(lane < n - sh, pltpu.roll(x, n - sh, axis=1), fill)
        else:
            moved = jnp.where(lane >= sh, pltpu.roll(x, sh, axis=1), fill)
        x = op(x, moved)
        sh *= 2
    return x


def _mlstm_kernel(qvf_ref, ktf_ref, gtf_ref, qvb_ref, ktb_ref, gtb_ref, hf_ref, hb_ref, cn_ref, m_ref):
    @pl.when(pl.program_id(1) == 0)
    def _():
        cn_ref[...] = jnp.zeros_like(cn_ref)
        m_ref[...] = jnp.zeros_like(m_ref)

    L = CHUNK
    row = lax.broadcasted_iota(jnp.int32, (L, L), 0)
    col = lax.broadcasted_iota(jnp.int32, (L, L), 1)
    scale = ML_DIM ** -0.5
    ones_v = jnp.ones((L, ML_DIM), BF16)
    pending = []
    for d, (qv_ref, kt_ref, gt_ref, h_ref) in enumerate(((qvf_ref, ktf_ref, gtf_ref, hf_ref),
                                                          (qvb_ref, ktb_ref, gtb_ref, hb_ref))):
        seen = (col <= row) if d == 0 else (col >= row)
        g0 = d * ML_HEADS
        ic = gt_ref[g0:g0 + ML_HEADS, :]
        lf = gt_ref[N_GATE + g0:N_GATE + g0 + ML_HEADS, :]
        r = ic - _scan_lanes(lf, jnp.add, 0.0, d == 1)
        r_max = jnp.max(r, axis=1, keepdims=True)
        b_end = jnp.sum(lf, axis=1, keepdims=True)
        for hh in range(ML_HEADS):
            idx = g0 + hh
            lanes = slice(hh * ML_DIM, (hh + 1) * ML_DIM)
            r_row = r[hh:hh + 1, :]
            m_old = m_ref[idx]
            m_end = jnp.maximum(m_old, r_max[hh:hh + 1, :])
            q = qv_ref[:, lanes]
            vo = jnp.concatenate([qv_ref[:, WIDTH + hh * ML_DIM:WIDTH + (hh + 1) * ML_DIM], ones_v], axis=1)
            kt = kt_ref[lanes, :]
            qsb = (q.astype(F32) * scale).astype(BF16)
            cn = cn_ref[idx]
            s_raw = jnp.dot(qsb, kt, preferred_element_type=F32)
            q_cn = jnp.dot(qsb, cn.astype(BF16), preferred_element_type=F32)
            kwt = (kt.astype(F32) * jnp.exp(r_row - m_end)).astype(BF16)
            cn_ref[idx] = jnp.exp(m_old - m_end) * cn + jnp.dot(kwt, vo, preferred_element_type=F32)
            m_ref[idx] = b_end[hh:hh + 1, :] + m_end
            pending.append((h_ref, lanes, seen, r_row, lf[hh:hh + 1, :], m_old, s_raw, q_cn, vo))

    for h_ref, lanes, seen, r_row, lf_row, m_old, s_raw, q_cn, vo in pending:
        b_col = jnp.sum(jnp.where(seen, lf_row, 0.0), axis=1, keepdims=True)
        m_col = jnp.maximum(m_old, jnp.max(jnp.where(seen, r_row, -jnp.inf), axis=1, keepdims=True))
        w = jnp.exp(jnp.where(seen, r_row - m_col, -jnp.inf))
        a = jnp.exp(m_old - m_col)
        s_vo = jnp.dot((s_raw * w).astype(BF16), vo, preferred_element_type=F32)
        num = a * q_cn[:, :ML_DIM] + s_vo[:, :ML_DIM]
        den = a * q_cn[:, ML_DIM:] + s_vo[:, ML_DIM:]
        hc = num / jnp.maximum(jnp.abs(den), jnp.exp(-(b_col + m_col)))
        h_ref[:, lanes] = hc.astype(h_ref.dtype)


def _mlstm(p, kt, gt, batch, n_lat, n_ctx):
    rows = p.shape[0]
    cl, cc = n_lat // CHUNK, n_ctx // CHUNK
    lat_base, ctx_base = 0, batch * cl

    def fwd_chunk(b, i):
        return jnp.where(i < cc, ctx_base + b * cc + i, lat_base + b * cl + (i - cc))

    def bwd_chunk(b, i):
        return jnp.where(i < cc, ctx_base + b * cc + (cc - 1 - i), lat_base + b * cl + (cl - 1 - (i - cc)))

    def specs(chunk):
        return [pl.BlockSpec((CHUNK, 2 * WIDTH), lambda b, i: (chunk(b, i), C_MLQ // (2 * WIDTH))),
                pl.BlockSpec((WIDTH, CHUNK), lambda b, i: (0, chunk(b, i))),
                pl.BlockSpec((2 * N_GATE, CHUNK), lambda b, i: (0, chunk(b, i)))]

    n_state = 2 * ML_HEADS
    return pl.pallas_call(
        _mlstm_kernel,
        out_shape=(jax.ShapeDtypeStruct((rows, WIDTH), BF16), jax.ShapeDtypeStruct((rows, WIDTH), BF16)),
        grid=(batch, cl + cc),
        in_specs=specs(fwd_chunk) + specs(bwd_chunk),
        out_specs=(pl.BlockSpec((CHUNK, WIDTH), lambda b, i: (fwd_chunk(b, i), 0)),
                   pl.BlockSpec((CHUNK, WIDTH), lambda b, i: (bwd_chunk(b, i), 0))),
        scratch_shapes=[pltpu.VMEM((n_state, ML_DIM, 2 * ML_DIM), F32),
                        pltpu.VMEM((n_state, 1, 1), F32)],
        compiler_params=_params(("parallel", "arbitrary")),
        name="mlstm_scan",
    )(p, kt, gt, p, kt, gt)


def _half_mask(e):
    lane = lax.broadcasted_iota(jnp.int32, (1, LANES), 1)
    return (lane < HEAD_DIM) if e == 0 else (lane >= HEAD_DIM)


def _win_kernel(sink_ref, q_ref, kc_ref, vtc_ref, k0_ref, k1_ref, k2_ref, k3_ref,
                vt0_ref, vt1_ref, vt2_ref, vt3_ref, o_ref, *, n_lat_tiles, n_lat_blocks):
    t = pl.program_id(1)
    tq = q_ref.shape[0]
    n_ctx = kc_ref.shape[0]
    is_lat = t < n_lat_tiles
    kk = lax.broadcasted_iota(jnp.int32, (CHUNK, tq), 0)
    qq = lax.broadcasted_iota(jnp.int32, (CHUNK, tq), 1)
    band = (kk >= qq, qq <= kk + CHUNK, kk <= qq, kk + CHUNK <= qq)
    first = 2 * t - 1
    present = [jnp.logical_and(is_lat, jnp.logical_and(first + i >= 0, first + i < n_lat_blocks))
               for i in range(4)]
    k_refs = (k0_ref, k1_ref, k2_ref, k3_ref)
    vt_refs = (vt0_ref, vt1_ref, vt2_ref, vt3_ref)
    ones = jnp.ones((16, n_ctx + 4 * CHUNK), BF16)

    for g in range(KV_HEADS):
        lanes = slice(g * LANES, (g + 1) * LANES)
        k_all = jnp.concatenate([kc_ref[:, lanes]] + [r[:, lanes] for r in k_refs], axis=0)
        vrows = slice(g * HEAD_DIM, (g + 1) * HEAD_DIM)
        vta = jnp.concatenate(
            [jnp.concatenate([vtc_ref[vrows, :]] + [r[vrows, :] for r in vt_refs], axis=1), ones], axis=0)
        scores = []
        for tt in range(2 * g, 2 * g + 2):
            qf = q_ref[:, tt * LANES:(tt + 1) * LANES].astype(F32)
            for e in range(2):
                qt = jnp.where(_half_mask(e), qf, 0.0).T.astype(BF16)
                scores.append(jnp.dot(k_all, qt, preferred_element_type=F32))
        outs = []
        for j, s in enumerate(scores):
            sink = sink_ref[4 * g + j]
            parts = [s[:n_ctx, :]]
            for i in range(4):
                blk = s[n_ctx + i * CHUNK:n_ctx + (i + 1) * CHUNK, :]
                blk = jnp.where(band[i], blk, -jnp.inf)
                parts.append(jnp.where(present[i], blk, -jnp.inf))
            s = jnp.concatenate(parts, axis=0)
            m = jnp.maximum(jnp.max(s, axis=0, keepdims=True), sink)
            pt = jnp.exp(s - m).astype(BF16)
            ol = jnp.dot(vta, pt, preferred_element_type=F32)
            den = ol[HEAD_DIM:HEAD_DIM + 1, :] + jnp.exp(sink - m)
            outs.append(ol[:HEAD_DIM, :] / den)
        for j in range(2):
            tt = 2 * g + j
            o_ref[:, tt * LANES:(tt + 1) * LANES] = (
                jnp.concatenate(outs[2 * j:2 * j + 2], axis=0).T.astype(o_ref.dtype))


def _window_attention(p, vt, sink, batch, n_lat, n_ctx, ctx_queries):
    tq = 256
    nlt, nct = n_lat // tq, n_ctx // tq
    nlb = n_lat // CHUNK
    nq = nlt + (nct if ctx_queries else 0)
    out_rows = batch * (n_lat + (n_ctx if ctx_queries else 0))
    ctx_blk = batch * n_lat // n_ctx
    k_col = C_WK // 256

    def q_block(b, t):
        return jnp.where(t < nlt, b * nlt + t, batch * nlt + b * nct + (t - nlt))

    def near(i):
        return lambda b, t: b * nlb + jnp.clip(2 * t - 1 + i, 0, nlb - 1)

    k_specs = [pl.BlockSpec((CHUNK, 256), (lambda f: lambda b, t: (f(b, t), k_col))(near(i))) for i in range(4)]
    vt_specs = [pl.BlockSpec((LANES, CHUNK), (lambda f: lambda b, t: (VT_WINDOW, f(b, t)))(near(i)))
                for i in range(4)]
    return pl.pallas_call(
        functools.partial(_win_kernel, n_lat_tiles=nlt, n_lat_blocks=nlb),
        out_shape=jax.ShapeDtypeStruct((out_rows, WIDTH), BF16),
        grid=(batch, nq),
        in_specs=[pl.BlockSpec(memory_space=pltpu.SMEM),
                  pl.BlockSpec((tq, WIDTH), lambda b, t: (q_block(b, t), C_WQ // WIDTH)),
                  pl.BlockSpec((n_ctx, 256), lambda b, t: (ctx_blk + b, k_col)),
                  pl.BlockSpec((LANES, n_ctx), lambda b, t: (VT_WINDOW, ctx_blk + b))]
                 + k_specs + vt_specs,
        out_specs=pl.BlockSpec((tq, WIDTH), lambda b, t: (q_block(b, t), 0)),
        compiler_params=_params(("parallel", "parallel")),
        name="window_attention",
    )(sink, p, p, vt, p, p, p, p, vt, vt, vt, vt)


def _glb_kernel(q_ref, kc_ref, vtc_ref, kl_ref, vtl_ref, o_ref, qt_ref, knorm_ref, *scratch, n_lat_tiles, tk):
    heads = Q_HEADS // KV_HEADS
    acc_refs = scratch[0:heads]
    sc_refs = scratch[heads:2 * heads]
    ring = [scratch[(2 + i) * heads:(3 + i) * heads] for i in range(N_SCORE_BUFS)]
    base = (2 + N_SCORE_BUFS) * heads
    pc_refs = scratch[base:base + heads]
    p_ring = [scratch[base + (1 + i) * heads:base + (2 + i) * heads] for i in range(2)]
    qi = pl.program_id(2)
    tq = q_ref.shape[0]
    n_chunks = kl_ref.shape[0] // tk

    def k_chunk(c):
        off = pl.multiple_of(jnp.minimum(c, n_chunks - 1) * tk, tk)
        return kl_ref[pl.ds(off, tk), :]

    def vt_chunk(c):
        return vtl_ref[:, pl.ds(pl.multiple_of(c * tk, tk), tk)]

    def with_ones(vt):
        ones = jnp.ones((acc_refs[0].shape[0] - HEAD_DIM, vt.shape[1]), BF16)
        return jnp.concatenate([vt, ones], axis=0)

    @pl.when(qi == 0)
    def _():
        def sq_norm(k):
            kf = k.astype(F32)
            return jnp.max(jnp.sum(kf * kf, axis=1, keepdims=True), axis=0, keepdims=True)

        def body(c, best):
            return jnp.maximum(best, sq_norm(k_chunk(c)))
        best = lax.fori_loop(0, n_chunks, body, sq_norm(kc_ref[...]))
        knorm_ref[...] = jnp.sqrt(0.5 * best)

    for t in range(heads // 2):
        qf = q_ref[:, t * LANES:(t + 1) * LANES].astype(F32)
        for e in range(2):
            h = 2 * t + e
            qt_ref[:, h * tq:(h + 1) * tq] = jnp.where(_half_mask(e), qf, 0.0).T.astype(BF16)
    for h in range(heads):
        acc_refs[h][...] = jnp.zeros_like(acc_refs[h])

    def q_t(h):
        return qt_ref[:, h * tq:(h + 1) * tq]

    bound = [jnp.sqrt(jnp.sum(jnp.square(q_t(h).astype(F32)), axis=0, keepdims=True)) * knorm_ref[...]
             for h in range(heads)]
    bounded = jnp.max(functools.reduce(jnp.maximum, bound)) <= SAFE_LOG2_BOUND

    def produce(p_refs, k):
        for h in range(heads):
            s = jnp.dot(k, q_t(h), preferred_element_type=F32)
            p_refs[h][...] = jnp.exp2(s - bound[h]).astype(BF16)

    def consume(p_refs, vt):
        vta = with_ones(vt)
        for h in range(heads):
            acc_refs[h][...] += jnp.dot(vta, p_refs[h][...], preferred_element_type=F32)

    @pl.when(bounded)
    def _():
        produce(pc_refs, kc_ref[...])

        @pl.when(qi >= n_lat_tiles)
        def _():
            consume(pc_refs, vtc_ref[...])

        @pl.when(qi < n_lat_tiles)
        def _():
            produce(p_ring[0], k_chunk(0))
            consume(pc_refs, vtc_ref[...])

            def body(j, carry):
                for i in range(N_SCORE_BUFS):
                    c = N_SCORE_BUFS * j + i
                    produce(p_ring[(i + 1) % 2], k_chunk(c + 1))
                    consume(p_ring[i % 2], vt_chunk(c))
                return carry
            lax.fori_loop(0, n_chunks // N_SCORE_BUFS, body, 0)

    neg_inf = (jnp.full((1, tq), -jnp.inf, F32),) * heads

    def fill(s_refs, k, m_run):
        out = []
        for h in range(heads):
            s = jnp.dot(k, q_t(h), preferred_element_type=F32)
            s_refs[h][...] = s
            out.append(jnp.maximum(m_run[h], jnp.max(s, axis=0, keepdims=True)))
        return tuple(out)

    def drain(s_refs, vt, m_run, m_acc):
        vta = with_ones(vt)
        for h in range(heads):
            alpha = jnp.exp2(m_acc[h] - m_run[h])
            pt = jnp.exp2(s_refs[h][...] - m_run[h]).astype(BF16)
            acc_refs[h][...] = alpha * acc_refs[h][...] + jnp.dot(vta, pt, preferred_element_type=F32)
        return m_run

    @pl.when(jnp.logical_not(bounded))
    def _():
        m_ctx = fill(sc_refs, kc_ref[...], neg_inf)

        @pl.when(qi >= n_lat_tiles)
        def _():
            drain(sc_refs, vtc_ref[...], m_ctx, neg_inf)

        @pl.when(qi < n_lat_tiles)
        def _():
            m_one = fill(ring[0], k_chunk(0), m_ctx)
            m_two = fill(ring[1], k_chunk(1), m_one)
            m_acc = drain(sc_refs, vtc_ref[...], m_one, neg_inf)

            def body(j, carry):
                m_run, m_acc = carry
                for i in range(N_SCORE_BUFS):
                    c = N_SCORE_BUFS * j + i
                    m_next = fill(ring[(i + 2) % N_SCORE_BUFS], k_chunk(c + 2), m_run)
                    m_acc = drain(ring[i], vt_chunk(c), m_run, m_acc)
                    m_run = m_next
                return m_run, m_acc
            lax.fori_loop(0, n_chunks // N_SCORE_BUFS, body, (m_two, m_acc))

    for t in range(heads // 2):
        pair = []
        for e in range(2):
            a = acc_refs[2 * t + e][...]
            pair.append(a[:HEAD_DIM, :] / a[HEAD_DIM:HEAD_DIM + 1, :])
        o_ref[:, t * LANES:(t + 1) * LANES] = jnp.concatenate(pair, axis=0).T.astype(o_ref.dtype)


def _global_attention(p, vt, batch, n_lat, n_ctx, ctx_queries):
    tq = 256
    tk = min(512, n_lat // N_SCORE_BUFS)
    assert n_lat % (tk * N_SCORE_BUFS) == 0 and tk % LANES == 0
    nl, nc = n_lat // tq, n_ctx // tq
    nq = nl + (nc if ctx_queries else 0)
    out_rows = batch * (n_lat + (n_ctx if ctx_queries else 0))
    ctx_base = batch * nl
    heads = Q_HEADS // KV_HEADS
    acc_rows = HEAD_DIM + 16

    def q_block(b, qi):
        return jnp.where(qi < nl, b * nl + qi, ctx_base + b * nc + (qi - nl))

    ctx_blk = batch * n_lat // n_ctx
    return pl.pallas_call(
        functools.partial(_glb_kernel, n_lat_tiles=nl, tk=tk),
        out_shape=jax.ShapeDtypeStruct((out_rows, WIDTH), BF16),
        grid=(batch, KV_HEADS, nq),
        in_specs=[pl.BlockSpec((tq, 2 * LANES), lambda b, g, qi: (q_block(b, qi), C_GQ // 256 + g)),
                  pl.BlockSpec((n_ctx, LANES), lambda b, g, qi: (ctx_blk + b, C_GK // LANES + g)),
                  pl.BlockSpec((HEAD_DIM, n_ctx), lambda b, g, qi: (KV_HEADS * VT_GLOBAL + g, ctx_blk + b)),
                  pl.BlockSpec((n_lat, LANES), lambda b, g, qi: (b, C_GK // LANES + g)),
                  pl.BlockSpec((HEAD_DIM, n_lat), lambda b, g, qi: (KV_HEADS * VT_GLOBAL + g, b))],
        out_specs=pl.BlockSpec((tq, 2 * LANES), lambda b, g, qi: (q_block(b, qi), g)),
        scratch_shapes=([pltpu.VMEM((LANES, heads * tq), BF16), pltpu.VMEM((1, 1), F32)]
                        + [pltpu.VMEM((acc_rows, tq), F32)] * heads
                        + [pltpu.VMEM((n_ctx, tq), F32)] * heads
                        + [pltpu.VMEM((tk, tq), F32)] * (N_SCORE_BUFS * heads)
                        + [pltpu.VMEM((n_ctx, tq), BF16)] * heads
                        + [pltpu.VMEM((tk, tq), BF16)] * (2 * heads)),
        compiler_params=_params(("parallel", "parallel", "arbitrary")),
        name="global_attention",
    )(p, p, vt, p, vt)


def _merge_kernel(x_ref, mod_ref, hf_ref, hb_ref, og_ref, yb_ref, yc_ref, ga_ref, gb_ref, gc_ref,
                  mlg_ref, wbr_ref, wo_ref, o_ref):
    tm = x_ref.shape[0]
    for sb in range(tm // ROW_BLOCK):
        rows = slice(sb * ROW_BLOCK, (sb + 1) * ROW_BLOCK)
        hs = hf_ref[rows, :].astype(F32) + hb_ref[rows, :].astype(F32)
        parts = []
        for t in range(ML_HEADS):
            ht = hs[:, t * ML_DIM:(t + 1) * ML_DIM]
            ms = jnp.mean(ht * ht, axis=-1, keepdims=True)
            parts.append(ht * lax.rsqrt(ms + EPS))
        ya = (jnp.concatenate(parts, axis=1) * mlg_ref[...]
              * _sigmoid(og_ref[rows, :].astype(F32))).astype(BF16)
        merged = None
        for y, gate_ref, i in ((ya, ga_ref, 0), (yb_ref[rows, :], gb_ref, 1), (yc_ref[rows, :], gc_ref, 2)):
            term = (_sigmoid(gate_ref[rows, :].astype(F32))
                    * jnp.dot(y, wbr_ref[i], preferred_element_type=F32))
            merged = term if merged is None else merged + term
        out = jnp.dot(merged.astype(BF16), wo_ref[...], preferred_element_type=F32)
        o_ref[rows, :] = x_ref[rows, :] + mod_ref[sb, 2:3, :] * out


def _merge(x, modtab, p, hf, hb, yb, yc, mlg, wbr, wo, n_rows):
    tm = ROW_TILE
    nb = tm // ROW_BLOCK
    row = lambda i: (i, 0)
    gate = lambda k: pl.BlockSpec((tm, D_MODEL), lambda i: (i, C_GATE // D_MODEL + k))
    return pl.pallas_call(
        _merge_kernel,
        out_shape=jax.ShapeDtypeStruct(x.shape, F32),
        grid=(n_rows // tm,),
        in_specs=[pl.BlockSpec((tm, D_MODEL), row),
                  pl.BlockSpec((nb, 6, D_MODEL), lambda i: (i, 0, 0)),
                  pl.BlockSpec((tm, WIDTH), row),
                  pl.BlockSpec((tm, WIDTH), row),
                  pl.BlockSpec((tm, WIDTH), lambda i: (i, C_MLO // WIDTH)),
                  pl.BlockSpec((tm, WIDTH), row),
                  pl.BlockSpec((tm, WIDTH), row),
                  gate(0), gate(1), gate(2),
                  _resident((1, WIDTH)),
                  _resident((3, WIDTH, D_MODEL)),
                  _resident((D_MODEL, D_MODEL))],
        out_specs=pl.BlockSpec((tm, D_MODEL), row),
        input_output_aliases={0: 0},
        compiler_params=_params(("parallel",)),
        name="merge",
    )(x, modtab, hf, hb, p, yb, yc, p, p, p, mlg, wbr, wo)


def _ffn_kernel(x_ref, mod_ref, g2_ref, w1_ref, w3_ref, w2_ref, gf_ref, o_ref, *, final):
    tm = x_ref.shape[0]
    d_ff = w1_ref.shape[1]
    half = d_ff // 2
    for sb in range(tm // ROW_BLOCK):
        rows = slice(sb * ROW_BLOCK, (sb + 1) * ROW_BLOCK)
        xs = x_ref[rows, :]
        ms = jnp.mean(xs * xs, axis=-1, keepdims=True)
        y = xs * lax.rsqrt(ms + EPS) * g2_ref[...]
        h = (y * (1.0 + mod_ref[sb, 4:5, :]) + mod_ref[sb, 3:4, :]).astype(BF16)
        out = None
        for c0 in (0, half):
            a = jnp.dot(h, w1_ref[:, c0:c0 + half], preferred_element_type=F32)
            b = jnp.dot(h, w3_ref[:, c0:c0 + half], preferred_element_type=F32)
            z = (a * _sigmoid(a) * b).astype(BF16)
            part = jnp.dot(z, w2_ref[c0:c0 + half, :], preferred_element_type=F32)
            out = part if out is None else out + part
        xn = xs + mod_ref[sb, 5:6, :] * out
        if final:
            ms = jnp.mean(xn * xn, axis=-1, keepdims=True)
            xn = xn * lax.rsqrt(ms + EPS) * gf_ref[...]
        o_ref[rows, :] = xn


def _ffn(x, modtab, g2, w1, w3, w2, gfin, n_rows, final):
    tm = ROW_TILE
    nb = tm // ROW_BLOCK
    d_ff = w1.shape[1]
    row = lambda i: (i, 0)
    out_rows = n_rows if final else x.shape[0]
    return pl.pallas_call(
        functools.partial(_ffn_kernel, final=final),
        out_shape=jax.ShapeDtypeStruct((out_rows, D_MODEL), F32),
        grid=(n_rows // tm,),
        in_specs=[pl.BlockSpec((tm, D_MODEL), row),
                  pl.BlockSpec((nb, 6, D_MODEL), lambda i: (i, 0, 0)),
                  _resident((1, D_MODEL)),
                  _resident((D_MODEL, d_ff)),
                  _resident((D_MODEL, d_ff)),
                  _resident((d_ff, D_MODEL)),
                  _resident((1, D_MODEL))],
        out_specs=pl.BlockSpec((tm, D_MODEL), row),
        input_output_aliases={} if final else {0: 0},
        compiler_params=_params(("parallel",)),
        name="ffn",
    )(x, modtab, g2, w1, w3, w2, gfin)


def _dup_halves(w, base):
    h0 = w[..., base:base + HEAD_DIM]
    h1 = w[..., base + HEAD_DIM:base + 2 * HEAD_DIM]
    return [h0, h0, h1, h1]


def _arrange_in_proj(w):
    o_gate_ml = 4 * WIDTH
    o_wq = o_gate_ml + 2 * N_GATE
    o_wk, o_wv = o_wq + WIDTH, o_wq + WIDTH + 128
    o_gq = o_wv + 128
    o_gk, o_gv = o_gq + WIDTH, o_gq + WIDTH + 128
    o_gate = o_gv + 128
    main = jnp.concatenate(
        [w[..., o_gate:o_gate + 3 * D_MODEL], w[..., 0:WIDTH], w[..., 2 * WIDTH:4 * WIDTH],
         w[..., o_wq:o_wq + WIDTH]]
        + _dup_halves(w, o_wk) + [w[..., o_gq:o_gq + WIDTH]] + _dup_halves(w, o_gk), axis=-1)
    values = jnp.concatenate([w[..., o_gv:o_gv + 128], w[..., o_wv:o_wv + 128]], axis=-1)
    return main, w[..., o_gate_ml:o_gate_ml + 2 * N_GATE], w[..., WIDTH:2 * WIDTH], values


def _rope_tables(batch, n_lat, n_ctx):
    t = jnp.arange(n_lat)
    quarter = HEAD_DIM // 4
    inv = ROPE_THETA ** (-jnp.arange(0, 2 * quarter, 2, dtype=F32) / (2 * quarter))
    ang_r = (t // GRID_W).astype(F32)[:, None] * inv
    ang_c = (t % GRID_W).astype(F32)[:, None] * inv
    cos = jnp.concatenate([jnp.cos(ang_r)] * 2 + [jnp.cos(ang_c)] * 2, axis=1)
    sin = jnp.concatenate([-jnp.sin(ang_r), jnp.sin(ang_r), -jnp.sin(ang_c), jnp.sin(ang_c)], axis=1)
    cos = jnp.tile(cos, (batch, LANES // HEAD_DIM))
    sin = jnp.tile(sin, (batch, LANES // HEAD_DIM))
    pad = batch * n_ctx
    return (jnp.concatenate([cos, jnp.ones((pad, LANES), F32)], axis=0),
            jnp.concatenate([sin, jnp.zeros((pad, LANES), F32)], axis=0))


def kernel(x, c, ctx, c_ctx, w_mod, b_mod, norm1_g, w_in, b_in, ml_norm_g, win_sink, qn_g, kn_g,
           w_br, w_o, norm2_g, w_ff1, w_ff3, w_ff2, final_g):
    batch, n_lat, d = x.shape
    n_ctx = ctx.shape[1]
    depth = w_mod.shape[0]
    assert d == D_MODEL and n_lat % ROW_TILE == 0 and n_lat % GRID_W == 0
    assert (batch * n_ctx) % ROW_TILE == 0 and n_ctx % ROW_BLOCK == 0 and (batch * n_lat) % n_ctx == 0
    lat_rows, ctx_rows = batch * n_lat, batch * n_ctx

    xs = jnp.concatenate([x.reshape(lat_rows, d), ctx.reshape(ctx_rows, d)], axis=0)
    cvec = jnp.concatenate([c, c_ctx[None, :], jnp.zeros((8 - batch - 1, d), F32)], axis=0)
    block_class = np.concatenate([np.repeat(np.arange(batch), n_lat // ROW_BLOCK),
                                  np.full(ctx_rows // ROW_BLOCK, batch)])
    cos, sin = _rope_tables(batch, n_lat, n_ctx)
    avg = jnp.asarray(np.kron(np.eye(LANES // HEAD_DIM), np.full((HEAD_DIM, HEAD_DIM), 1.0 / HEAD_DIM)), BF16)

    out = None
    for l in range(depth):
        last = l == depth - 1
        mod = _mod_vectors(cvec, w_mod[l], b_mod[l]).reshape(8, 6, d)
        modtab = mod[block_class]

        w_main, w_gate, w_mlk, w_val = _arrange_in_proj(w_in[l])
        b_main, b_gate, b_mlk, b_val = _arrange_in_proj(b_in[l][None, :])
        p, gt, kt, vt = _in_projection(
            xs, modtab, norm1_g[l][None, :], w_main.astype(BF16), b_main,
            w_gate.T.astype(BF16), b_gate.T, w_mlk.T.astype(BF16), b_mlk.T, w_val.T.astype(BF16), b_val.T,
            cos, sin, jnp.tile(qn_g[l], 2)[None, :], jnp.tile(kn_g[l], 2)[None, :], avg)

        hf, hb = _mlstm(p, kt, gt, batch, n_lat, n_ctx)
        yb = _window_attention(p, vt, win_sink[l], batch, n_lat, n_ctx, ctx_queries=not last)
        yc = _global_attention(p, vt, batch, n_lat, n_ctx, ctx_queries=not last)

        n_rows = lat_rows if last else lat_rows + ctx_rows
        xs = _merge(xs, modtab, p, hf, hb, yb, yc, ml_norm_g[l].reshape(1, WIDTH),
                    w_br[l].astype(BF16), w_o[l].astype(BF16), n_rows)
        out = _ffn(xs, modtab, norm2_g[l][None, :], w_ff1[l].astype(BF16), w_ff3[l].astype(BF16),
                   w_ff2[l].astype(BF16), final_g[None, :], n_rows, final=last)
        xs = out
    return out.reshape(batch, n_lat, d)
```

```python
import functools

import jax
import jax.numpy as jnp
import numpy as np
from jax import lax
from jax.experimental import pallas as pl
from jax.experimental.pallas import tpu as pltpu

F32 = jnp.float32
BF16 = jnp.bfloat16

D_MODEL = 1024
GRID_W = 64
CHUNK = 128
HEAD_DIM = 64
ROPE_THETA = 10000.0
EPS = 1e-6
ML_HEADS = 4
ML_DIM = 128
Q_HEADS = 8
KV_HEADS = 2
WIDTH = 512
N_GATE = 2 * ML_HEADS

LANES = 128
MXU_TILE = 256
ROW_BLOCK = 256
ROW_TILE = 512
VMEM_LIMIT = 56 * 1024 * 1024

C_GATE = 0
C_MLQ, C_MLV, C_MLO = 3072, 3584, 4096
C_WQ, C_WK = 4608, 5120
C_GQ, C_GK = 5376, 5888
N_PROJ = C_GK + 256
VT_GLOBAL, VT_WINDOW = 0, 1
LOG2E = 1.4426950408889634
N_SCORE_BUFS = 4
BOUNDED_UNROLL = 8
SAFE_LOG2_BOUND = 50.0

NT_DIMS = (((1,), (1,)), ((), ()))


def _params(sem, vmem=VMEM_LIMIT):
    return pltpu.CompilerParams(dimension_semantics=sem, vmem_limit_bytes=vmem)


def _resident(shape):
    nd = len(shape)
    return pl.BlockSpec(shape, lambda *_: (0,) * nd, pipeline_mode=pl.Buffered(1))


def _sigmoid(x):
    return 1.0 / (1.0 + jnp.exp(-x))


def _log_sigmoid(x):
    return jnp.minimum(x, 0.0) - jnp.log(1.0 + jnp.exp(-jnp.abs(x)))


def _mod_kernel(c_ref, w_ref, b_ref, o_ref):
    c = c_ref[...]
    s = c * _sigmoid(c)
    o_ref[...] = jnp.dot(s, w_ref[...], preferred_element_type=F32) + b_ref[...]


def _mod_vectors(cvec, w_mod, b_mod):
    n_out = w_mod.shape[1]
    tn = 1536
    return pl.pallas_call(
        _mod_kernel,
        out_shape=jax.ShapeDtypeStruct((cvec.shape[0], n_out), F32),
        grid=(n_out // tn,),
        in_specs=[pl.BlockSpec(cvec.shape, lambda j: (0, 0)),
                  pl.BlockSpec((D_MODEL, tn), lambda j: (0, j)),
                  pl.BlockSpec((1, tn), lambda j: (0, j))],
        out_specs=pl.BlockSpec((cvec.shape[0], tn), lambda j: (0, j)),
        compiler_params=_params(("parallel",)),
        name="mod_vectors",
    )(cvec, w_mod, b_mod.reshape(1, n_out))


def _rope(acc, cos, sin, first_half):
    w = acc.shape[1]
    reps = w // LANES
    if reps > 1:
        cos = jnp.concatenate([cos] * reps, axis=1)
        sin = jnp.concatenate([sin] * reps, axis=1)
    ahead = pltpu.roll(acc, w - 16, axis=1)
    behind = pltpu.roll(acc, 16, axis=1)
    return acc * cos + jnp.where(first_half, ahead, behind) * sin


def _head_rms(acc, avg, gain):
    sq = (acc * acc).astype(BF16)
    outs = []
    for t in range(acc.shape[1] // LANES):
        sl = slice(t * LANES, (t + 1) * LANES)
        ms = jnp.dot(sq[:, sl], avg, preferred_element_type=F32)
        outs.append(acc[:, sl] * lax.rsqrt(ms + EPS) * gain)
    return jnp.concatenate(outs, axis=1)


def _inproj_kernel(x_ref, mod_ref, g1_ref, w_ref, b_ref, wgt_ref, bgt_ref, wkt_ref, bkt_ref,
                   wvt_ref, bvt_ref, cos_ref, sin_ref, qg_ref, kg_ref, avg_ref,
                   p_ref, gt_ref, kt_ref, vt_ref, h_ref):
    tm = x_ref.shape[0]
    for sb in range(tm // ROW_BLOCK):
        r0 = sb * ROW_BLOCK
        xs = x_ref[r0:r0 + ROW_BLOCK, :]
        ms = jnp.mean(xs * xs, axis=-1, keepdims=True)
        y = xs * lax.rsqrt(ms + EPS) * g1_ref[...]
        shift = mod_ref[sb, 0:1, :]
        scale = mod_ref[sb, 1:2, :]
        h_ref[r0:r0 + ROW_BLOCK, :] = (y * (1.0 + scale) + shift).astype(BF16)

    def first_half(width):
        return (lax.broadcasted_iota(jnp.int32, (1, width), 1) % 32) < 16

    avg = avg_ref[...]
    qg = qg_ref[...]
    kg = kg_ref[...]
    q_scale = HEAD_DIM ** -0.5

    h = h_ref[...]
    cos = cos_ref[...]
    sin = sin_ref[...]

    def proj(c0, width):
        return (jnp.dot(h, w_ref[:, c0:c0 + width], preferred_element_type=F32)
                + b_ref[:, c0:c0 + width])

    def store_plain(cols):
        for c0 in cols:
            p_ref[:, c0:c0 + WIDTH] = proj(c0, WIDTH).astype(BF16)

    plain = list(range(C_GATE, C_GATE + 3 * D_MODEL, WIDTH)) + [C_MLQ, C_MLV, C_MLO]
    raw_gq, raw_gk = proj(C_GQ, WIDTH), proj(C_GK, 256)
    raw_wq, raw_wk = proj(C_WQ, WIDTH), proj(C_WK, 256)
    store_plain(plain[:3])
    gq = _rope(_head_rms(raw_gq, avg, qg), cos, sin, first_half(WIDTH)) * (q_scale * LOG2E)
    p_ref[:, C_GQ:C_GQ + WIDTH] = gq.astype(BF16)
    gk = _rope(_head_rms(raw_gk, avg, kg), cos, sin, first_half(256))
    p_ref[:, C_GK:C_GK + 256] = gk.astype(BF16)
    p_ref[:, C_WQ:C_WQ + WIDTH] = (_rope(raw_wq, cos, sin, first_half(WIDTH)) * q_scale).astype(BF16)
    p_ref[:, C_WK:C_WK + 256] = _rope(raw_wk, cos, sin, first_half(256)).astype(BF16)
    store_plain(plain[3:])

    def proj_t(wt_ref, bt_ref):
        return lax.dot_general(wt_ref[...], h_ref[...], NT_DIMS, preferred_element_type=F32) + bt_ref[...]

    gta = proj_t(wgt_ref, bgt_ref)
    grow = lax.broadcasted_iota(jnp.int32, (2 * N_GATE, 1), 0)
    gt_ref[...] = jnp.where(grow >= N_GATE, _log_sigmoid(gta), gta)
    kt_ref[...] = proj_t(wkt_ref, bkt_ref).astype(BF16)
    vt_ref[...] = proj_t(wvt_ref, bvt_ref).astype(BF16)


def _in_projection(x, modtab, g1, w, b, wgt, bgt, wkt, bkt, wvt, bvt, cos, sin, qg, kg, avg):
    rows = x.shape[0]
    tm = ROW_TILE
    nb = tm // ROW_BLOCK
    kv = 2 * KV_HEADS * HEAD_DIM
    return pl.pallas_call(
        _inproj_kernel,
        out_shape=(jax.ShapeDtypeStruct((rows, N_PROJ), BF16),
                   jax.ShapeDtypeStruct((2 * N_GATE, rows), F32),
                   jax.ShapeDtypeStruct((WIDTH, rows), BF16),
                   jax.ShapeDtypeStruct((kv, rows), BF16)),
        grid=(rows // tm,),
        in_specs=[pl.BlockSpec((tm, D_MODEL), lambda i: (i, 0)),
                  pl.BlockSpec((nb, 6, D_MODEL), lambda i: (i, 0, 0)),
                  _resident((1, D_MODEL)),
                  _resident((D_MODEL, N_PROJ)),
                  _resident((1, N_PROJ)),
                  _resident((2 * N_GATE, D_MODEL)),
                  _resident((2 * N_GATE, 1)),
                  _resident((WIDTH, D_MODEL)),
                  _resident((WIDTH, 1)),
                  _resident((kv, D_MODEL)),
                  _resident((kv, 1)),
                  pl.BlockSpec((tm, LANES), lambda i: (i, 0)),
                  pl.BlockSpec((tm, LANES), lambda i: (i, 0)),
                  _resident((1, LANES)),
                  _resident((1, LANES)),
                  _resident((LANES, LANES))],
        out_specs=(pl.BlockSpec((tm, N_PROJ), lambda i: (i, 0)),
                   pl.BlockSpec((2 * N_GATE, tm), lambda i: (0, i)),
                   pl.BlockSpec((WIDTH, tm), lambda i: (0, i)),
                   pl.BlockSpec((kv, tm), lambda i: (0, i))),
        scratch_shapes=[pltpu.VMEM((tm, D_MODEL), BF16)],
        compiler_params=_params(("parallel",)),
        name="in_projection",
    )(x, modtab, g1, w, b, wgt, bgt, wkt, bkt, wvt, bvt, cos, sin, qg, kg, avg)


def _scan_lanes(x, op, fill, reverse):
    n = x.shape[1]
    lane = lax.broadcasted_iota(jnp.int32, x.shape, 1)
    sh = 1
    while sh < n:
        if reverse:
            moved = jnp.where(lane < n - sh, pltpu.roll(x, n - sh, axis=1), fill)
        else:
            moved = jnp.where(lane >= sh, pltpu.roll(x, sh, axis=1), fill)
        x = op(x, moved)
        sh *= 2
    return x


def _mlstm_kernel(qvf_ref, ktf_ref, gtf_ref, qvb_ref, ktb_ref, gtb_ref, hf_ref, hb_ref, cn_ref, m_ref):
    @pl.when(pl.program_id(1) == 0)
    def _():
        cn_ref[...] = jnp.zeros_like(cn_ref)
        m_ref[...] = jnp.zeros_like(m_ref)

    L = CHUNK
    row = lax.broadcasted_iota(jnp.int32, (L, L), 0)
    col = lax.broadcasted_iota(jnp.int32, (L, L), 1)
    scale = ML_DIM ** -0.5
    ones_v = jnp.ones((L, ML_DIM), BF16)
    pending = []
    for d, (qv_ref, kt_ref, gt_ref, h_ref) in enumerate(((qvf_ref, ktf_ref, gtf_ref, hf_ref),
                                                          (qvb_ref, ktb_ref, gtb_ref, hb_ref))):
        seen = (col <= row) if d == 0 else (col >= row)
        g0 = d * ML_HEADS
        ic = gt_ref[g0:g0 + ML_HEADS, :]
        lf = gt_ref[N_GATE + g0:N_GATE + g0 + ML_HEADS, :]
        r = ic - _scan_lanes(lf, jnp.add, 0.0, d == 1)
        r_max = jnp.max(r, axis=1, keepdims=True)
        b_end = jnp.sum(lf, axis=1, keepdims=True)
        for hh in range(ML_HEADS):
            idx = g0 + hh
            lanes = slice(hh * ML_DIM, (hh + 1) * ML_DIM)
            r_row = r[hh:hh + 1, :]
            m_old = m_ref[idx]
            m_end = jnp.maximum(m_old, r_max[hh:hh + 1, :])
            q = qv_ref[:, lanes]
            vo = jnp.concatenate([qv_ref[:, WIDTH + hh * ML_DIM:WIDTH + (hh + 1) * ML_DIM], ones_v], axis=1)
            kt = kt_ref[lanes, :]
            qsb = (q.astype(F32) * scale).astype(BF16)
            cn = cn_ref[idx]
            s_raw = jnp.dot(qsb, kt, preferred_element_type=F32)
            q_cn = jnp.dot(qsb, cn.astype(BF16), preferred_element_type=F32)
            kwt = (kt.astype(F32) * jnp.exp(r_row - m_end)).astype(BF16)
            cn_ref[idx] = jnp.exp(m_old - m_end) * cn + jnp.dot(kwt, vo, preferred_element_type=F32)
            m_ref[idx] = b_end[hh:hh + 1, :] + m_end
            pending.append((h_ref, lanes, seen, r_row, lf[hh:hh + 1, :], m_old, s_raw, q_cn, vo))

    for h_ref, lanes, seen, r_row, lf_row, m_old, s_raw, q_cn, vo in pending:
        b_col = jnp.sum(jnp.where(seen, lf_row, 0.0), axis=1, keepdims=True)
        m_col = jnp.maximum(m_old, jnp.max(jnp.where(seen, r_row, -jnp.inf), axis=1, keepdims=True))
        w = jnp.exp(jnp.where(seen, r_row - m_col, -jnp.inf))
        a = jnp.exp(m_old - m_col)
        s_vo = jnp.dot((s_raw * w).astype(BF16), vo, preferred_element_type=F32)
        num = a * q_cn[:, :ML_DIM] + s_vo[:, :ML_DIM]
        den = a * q_cn[:, ML_DIM:] + s_vo[:, ML_DIM:]
        hc = num / jnp.maximum(jnp.abs(den), jnp.exp(-(b_col + m_col)))
        h_ref[:, lanes] = hc.astype(h_ref.dtype)


def _mlstm(p, kt, gt, batch, n_lat, n_ctx):
    rows = p.shape[0]
    cl, cc = n_lat // CHUNK, n_ctx // CHUNK
    lat_base, ctx_base = 0, batch * cl

    def fwd_chunk(b, i):
        return jnp.where(i < cc, ctx_base + b * cc + i, lat_base + b * cl + (i - cc))

    def bwd_chunk(b, i):
        return jnp.where(i < cc, ctx_base + b * cc + (cc - 1 - i), lat_base + b * cl + (cl - 1 - (i - cc)))

    def specs(chunk):
        return [pl.BlockSpec((CHUNK, 2 * WIDTH), lambda b, i: (chunk(b, i), C_MLQ // (2 * WIDTH))),
                pl.BlockSpec((WIDTH, CHUNK), lambda b, i: (0, chunk(b, i))),
                pl.BlockSpec((2 * N_GATE, CHUNK), lambda b, i: (0, chunk(b, i)))]

    n_state = 2 * ML_HEADS
    return pl.pallas_call(
        _mlstm_kernel,
        out_shape=(jax.ShapeDtypeStruct((rows, WIDTH), BF16), jax.ShapeDtypeStruct((rows, WIDTH), BF16)),
        grid=(batch, cl + cc),
        in_specs=specs(fwd_chunk) + specs(bwd_chunk),
        out_specs=(pl.BlockSpec((CHUNK, WIDTH), lambda b, i: (fwd_chunk(b, i), 0)),
                   pl.BlockSpec((CHUNK, WIDTH), lambda b, i: (bwd_chunk(b, i), 0))),
        scratch_shapes=[pltpu.VMEM((n_state, ML_DIM, 2 * ML_DIM), F32),
                        pltpu.VMEM((n_state, 1, 1), F32)],
        compiler_params=_params(("parallel", "arbitrary")),
        name="mlstm_scan",
    )(p, kt, gt, p, kt, gt)


def _half_mask(e):
    lane = lax.broadcasted_iota(jnp.int32, (1, LANES), 1)
    return (lane < HEAD_DIM) if e == 0 else (lane >= HEAD_DIM)


def _win_kernel(sink_ref, q_ref, kc_ref, vtc_ref, k0_ref, k1_ref, k2_ref, k3_ref,
                vt0_ref, vt1_ref, vt2_ref, vt3_ref, o_ref, *, n_lat_tiles, n_lat_blocks):
    t = pl.program_id(1)
    tq = q_ref.shape[0]
    n_ctx = kc_ref.shape[0]
    is_lat = t < n_lat_tiles
    kk = lax.broadcasted_iota(jnp.int32, (CHUNK, tq), 0)
    qq = lax.broadcasted_iota(jnp.int32, (CHUNK, tq), 1)
    band = (kk >= qq, qq <= kk + CHUNK, kk <= qq, kk + CHUNK <= qq)
    first = 2 * t - 1
    present = [jnp.logical_and(is_lat, jnp.logical_and(first + i >= 0, first + i < n_lat_blocks))
               for i in range(4)]
    k_refs = (k0_ref, k1_ref, k2_ref, k3_ref)
    vt_refs = (vt0_ref, vt1_ref, vt2_ref, vt3_ref)
    ones = jnp.ones((16, n_ctx + 4 * CHUNK), BF16)

    for g in range(KV_HEADS):
        lanes = slice(g * LANES, (g + 1) * LANES)
        k_all = jnp.concatenate([kc_ref[:, lanes]] + [r[:, lanes] for r in k_refs], axis=0)
        vrows = slice(g * HEAD_DIM, (g + 1) * HEAD_DIM)
        vta = jnp.concatenate(
            [jnp.concatenate([vtc_ref[vrows, :]] + [r[vrows, :] for r in vt_refs], axis=1), ones], axis=0)
        scores = []
        for tt in range(2 * g, 2 * g + 2):
            qf = q_ref[:, tt * LANES:(tt + 1) * LANES].astype(F32)
            for e in range(2):
                qt = jnp.where(_half_mask(e), qf, 0.0).T.astype(BF16)
                scores.append(jnp.dot(k_all, qt, preferred_element_type=F32))
        outs = []
        for j, s in enumerate(scores):
            sink = sink_ref[4 * g + j]
            parts = [s[:n_ctx, :]]
            for i in range(4):
                blk = s[n_ctx + i * CHUNK:n_ctx + (i + 1) * CHUNK, :]
                blk = jnp.where(band[i], blk, -jnp.inf)
                parts.append(jnp.where(present[i], blk, -jnp.inf))
            s = jnp.concatenate(parts, axis=0)
            m = jnp.maximum(jnp.max(s, axis=0, keepdims=True), sink)
            pt = jnp.exp(s - m).astype(BF16)
            ol = jnp.dot(vta, pt, preferred_element_type=F32)
            den = ol[HEAD_DIM:HEAD_DIM + 1, :] + jnp.exp(sink - m)
            outs.append(ol[:HEAD_DIM, :] / den)
        for j in range(2):
            tt = 2 * g + j
            o_ref[:, tt * LANES:(tt + 1) * LANES] = (
                jnp.concatenate(outs[2 * j:2 * j + 2], axis=0).T.astype(o_ref.dtype))


def _window_attention(p, vt, sink, batch, n_lat, n_ctx, ctx_queries):
    tq = 256
    nlt, nct = n_lat // tq, n_ctx // tq
    nlb = n_lat // CHUNK
    nq = nlt + (nct if ctx_queries else 0)
    out_rows = batch * (n_lat + (n_ctx if ctx_queries else 0))
    ctx_blk = batch * n_lat // n_ctx
    k_col = C_WK // 256

    def q_block(b, t):
        return jnp.where(t < nlt, b * nlt + t, batch * nlt + b * nct + (t - nlt))

    def near(i):
        return lambda b, t: b * nlb + jnp.clip(2 * t - 1 + i, 0, nlb - 1)

    k_specs = [pl.BlockSpec((CHUNK, 256), (lambda f: lambda b, t: (f(b, t), k_col))(near(i))) for i in range(4)]
    vt_specs = [pl.BlockSpec((LANES, CHUNK), (lambda f: lambda b, t: (VT_WINDOW, f(b, t)))(near(i)))
                for i in range(4)]
    return pl.pallas_call(
        functools.partial(_win_kernel, n_lat_tiles=nlt, n_lat_blocks=nlb),
        out_shape=jax.ShapeDtypeStruct((out_rows, WIDTH), BF16),
        grid=(batch, nq),
        in_specs=[pl.BlockSpec(memory_space=pltpu.SMEM),
                  pl.BlockSpec((tq, WIDTH), lambda b, t: (q_block(b, t), C_WQ // WIDTH)),
                  pl.BlockSpec((n_ctx, 256), lambda b, t: (ctx_blk + b, k_col)),
                  pl.BlockSpec((LANES, n_ctx), lambda b, t: (VT_WINDOW, ctx_blk + b))]
                 + k_specs + vt_specs,
        out_specs=pl.BlockSpec((tq, WIDTH), lambda b, t: (q_block(b, t), 0)),
        compiler_params=_params(("parallel", "parallel")),
        name="window_attention",
    )(sink, p, p, vt, p, p, p, p, vt, vt, vt, vt)


def _glb_kernel(q_ref, kc_ref, vtc_ref, kl_ref, vtl_ref, o_ref, qt_ref, knorm_ref, *scratch, n_lat_tiles, tk):
    heads = Q_HEADS // KV_HEADS
    acc_refs = scratch[0:heads]
    sc_refs = scratch[heads:2 * heads]
    ring = [scratch[(2 + i) * heads:(3 + i) * heads] for i in range(N_SCORE_BUFS)]
    base = (2 + N_SCORE_BUFS) * heads
    pc_refs = scratch[base:base + heads]
    p_ring = [scratch[base + (1 + i) * heads:base + (2 + i) * heads] for i in range(2)]
    qi = pl.program_id(2)
    tq = q_ref.shape[0]
    n_chunks = kl_ref.shape[0] // tk

    def k_chunk(c):
        off = pl.multiple_of(jnp.minimum(c, n_chunks - 1) * tk, tk)
        return kl_ref[pl.ds(off, tk), :]

    def vt_chunk(c):
        return vtl_ref[:, pl.ds(pl.multiple_of(c * tk, tk), tk)]

    def with_ones(vt):
        ones = jnp.ones((acc_refs[0].shape[0] - HEAD_DIM, vt.shape[1]), BF16)
        return jnp.concatenate([vt, ones], axis=0)

    @pl.when(qi == 0)
    def _():
        def sq_norm(k):
            kf = k.astype(F32)
            return jnp.max(jnp.sum(kf * kf, axis=1, keepdims=True), axis=0, keepdims=True)

        def body(c, best):
            return jnp.maximum(best, sq_norm(k_chunk(c)))
        best = lax.fori_loop(0, n_chunks, body, sq_norm(kc_ref[...]))
        knorm_ref[...] = jnp.sqrt(0.5 * best)

    for t in range(heads // 2):
        qf = q_ref[:, t * LANES:(t + 1) * LANES].astype(F32)
        for e in range(2):
            h = 2 * t + e
            qt_ref[:, h * tq:(h + 1) * tq] = jnp.where(_half_mask(e), qf, 0.0).T.astype(BF16)
    for h in range(heads):
        acc_refs[h][...] = jnp.zeros_like(acc_refs[h])

    def q_t(h):
        return qt_ref[:, h * tq:(h + 1) * tq]

    bound = [jnp.sqrt(jnp.sum(jnp.square(q_t(h).astype(F32)), axis=0, keepdims=True)) * knorm_ref[...]
             for h in range(heads)]
    bounded = jnp.max(functools.reduce(jnp.maximum, bound)) <= SAFE_LOG2_BOUND

    def produce(p_refs, k):
        for h in range(heads):
            s = jnp.dot(k, q_t(h), preferred_element_type=F32)
            p_refs[h][...] = jnp.exp2(s - bound[h]).astype(BF16)

    def consume(p_refs, vt):
        vta = with_ones(vt)
        for h in range(heads):
            acc_refs[h][...] += jnp.dot(vta, p_refs[h][...], preferred_element_type=F32)

    @pl.when(bounded)
    def _():
        produce(pc_refs, kc_ref[...])

        @pl.when(qi >= n_lat_tiles)
        def _():
            consume(pc_refs, vtc_ref[...])

        @pl.when(qi < n_lat_tiles)
        def _():
            produce(p_ring[0], k_chunk(0))
            consume(pc_refs, vtc_ref[...])

            def body(j, carry):
                for i in range(unroll):
                    c = unroll * j + i
                    produce(p_ring[(i + 1) % 2], k_chunk(c + 1))
                    consume(p_ring[i % 2], vt_chunk(c))
                return carry
            unroll = BOUNDED_UNROLL if n_chunks % BOUNDED_UNROLL == 0 else N_SCORE_BUFS
            lax.fori_loop(0, n_chunks // unroll, body, 0)

    neg_inf = (jnp.full((1, tq), -jnp.inf, F32),) * heads

    def fill(s_refs, k, m_run):
        out = []
        for h in range(heads):
            s = jnp.dot(k, q_t(h), preferred_element_type=F32)
            s_refs[h][...] = s
            out.append(jnp.maximum(m_run[h], jnp.max(s, axis=0, keepdims=True)))
        return tuple(out)

    def drain(s_refs, vt, m_run, m_acc):
        vta = with_ones(vt)
        for h in range(heads):
            alpha = jnp.exp2(m_acc[h] - m_run[h])
            pt = jnp.exp2(s_refs[h][...] - m_run[h]).astype(BF16)
            acc_refs[h][...] = alpha * acc_refs[h][...] + jnp.dot(vta, pt, preferred_element_type=F32)
        return m_run

    @pl.when(jnp.logical_not(bounded))
    def _():
        m_ctx = fill(sc_refs, kc_ref[...], neg_inf)

        @pl.when(qi >= n_lat_tiles)
        def _():
            drain(sc_refs, vtc_ref[...], m_ctx, neg_inf)

        @pl.when(qi < n_lat_tiles)
        def _():
            m_one = fill(ring[0], k_chunk(0), m_ctx)
            m_two = fill(ring[1], k_chunk(1), m_one)
            m_acc = drain(sc_refs, vtc_ref[...], m_one, neg_inf)

            def body(j, carry):
                m_run, m_acc = carry
                for i in range(N_SCORE_BUFS):
                    c = N_SCORE_BUFS * j + i
                    m_next = fill(ring[(i + 2) % N_SCORE_BUFS], k_chunk(c + 2), m_run)
                    m_acc = drain(ring[i], vt_chunk(c), m_run, m_acc)
                    m_run = m_next
                return m_run, m_acc
            lax.fori_loop(0, n_chunks // N_SCORE_BUFS, body, (m_two, m_acc))

    for t in range(heads // 2):
        pair = []
        for e in range(2):
            a = acc_refs[2 * t + e][...]
            pair.append(a[:HEAD_DIM, :] / a[HEAD_DIM:HEAD_DIM + 1, :])
        o_ref[:, t * LANES:(t + 1) * LANES] = jnp.concatenate(pair, axis=0).T.astype(o_ref.dtype)


def _global_attention(p, vt, batch, n_lat, n_ctx, ctx_queries):
    tq = 256
    tk = min(512, n_lat // N_SCORE_BUFS)
    assert n_lat % (tk * N_SCORE_BUFS) == 0 and tk % LANES == 0
    nl, nc = n_lat // tq, n_ctx // tq
    nq = nl + (nc if ctx_queries else 0)
    out_rows = batch * (n_lat + (n_ctx if ctx_queries else 0))
    ctx_base = batch * nl
    heads = Q_HEADS // KV_HEADS
    acc_rows = HEAD_DIM + 16

    def q_block(b, qi):
        return jnp.where(qi < nl, b * nl + qi, ctx_base + b * nc + (qi - nl))

    ctx_blk = batch * n_lat // n_ctx
    return pl.pallas_call(
        functools.partial(_glb_kernel, n_lat_tiles=nl, tk=tk),
        out_shape=jax.ShapeDtypeStruct((out_rows, WIDTH), BF16),
        grid=(batch, KV_HEADS, nq),
        in_specs=[pl.BlockSpec((tq, 2 * LANES), lambda b, g, qi: (q_block(b, qi), C_GQ // 256 + g)),
                  pl.BlockSpec((n_ctx, LANES), lambda b, g, qi: (ctx_blk + b, C_GK // LANES + g)),
                  pl.BlockSpec((HEAD_DIM, n_ctx), lambda b, g, qi: (KV_HEADS * VT_GLOBAL + g, ctx_blk + b)),
                  pl.BlockSpec((n_lat, LANES), lambda b, g, qi: (b, C_GK // LANES + g)),
                  pl.BlockSpec((HEAD_DIM, n_lat), lambda b, g, qi: (KV_HEADS * VT_GLOBAL + g, b))],
        out_specs=pl.BlockSpec((tq, 2 * LANES), lambda b, g, qi: (q_block(b, qi), g)),
        scratch_shapes=([pltpu.VMEM((LANES, heads * tq), BF16), pltpu.VMEM((1, 1), F32)]
                        + [pltpu.VMEM((acc_rows, tq), F32)] * heads
                        + [pltpu.VMEM((n_ctx, tq), F32)] * heads
                        + [pltpu.VMEM((tk, tq), F32)] * (N_SCORE_BUFS * heads)
                        + [pltpu.VMEM((n_ctx, tq), BF16)] * heads
                        + [pltpu.VMEM((tk, tq), BF16)] * (2 * heads)),
        compiler_params=_params(("parallel", "parallel", "arbitrary")),
        name="global_attention",
    )(p, p, vt, p, vt)


def _merge_kernel(x_ref, mod_ref, hf_ref, hb_ref, og_ref, yb_ref, yc_ref, ga_ref, gb_ref, gc_ref,
                  mlg_ref, wbr_ref, wo_ref, o_ref):
    tm = x_ref.shape[0]
    for sb in range(tm // ROW_BLOCK):
        rows = slice(sb * ROW_BLOCK, (sb + 1) * ROW_BLOCK)
        hs = hf_ref[rows, :].astype(F32) + hb_ref[rows, :].astype(F32)
        parts = []
        for t in range(ML_HEADS):
            ht = hs[:, t * ML_DIM:(t + 1) * ML_DIM]
            ms = jnp.mean(ht * ht, axis=-1, keepdims=True)
            parts.append(ht * lax.rsqrt(ms + EPS))
        ya = (jnp.concatenate(parts, axis=1) * mlg_ref[...]
              * _sigmoid(og_ref[rows, :].astype(F32))).astype(BF16)
        merged = None
        for y, gate_ref, i in ((ya, ga_ref, 0), (yb_ref[rows, :], gb_ref, 1), (yc_ref[rows, :], gc_ref, 2)):
            term = (_sigmoid(gate_ref[rows, :].astype(F32))
                    * jnp.dot(y, wbr_ref[i], preferred_element_type=F32))
            merged = term if merged is None else merged + term
        out = jnp.dot(merged.astype(BF16), wo_ref[...], preferred_element_type=F32)
        o_ref[rows, :] = x_ref[rows, :] + mod_ref[sb, 2:3, :] * out


def _merge(x, modtab, p, hf, hb, yb, yc, mlg, wbr, wo, n_rows):
    tm = ROW_TILE
    nb = tm // ROW_BLOCK
    row = lambda i: (i, 0)
    gate = lambda k: pl.BlockSpec((tm, D_MODEL), lambda i: (i, C_GATE // D_MODEL + k))
    return pl.pallas_call(
        _merge_kernel,
        out_shape=jax.ShapeDtypeStruct(x.shape, F32),
        grid=(n_rows // tm,),
        in_specs=[pl.BlockSpec((tm, D_MODEL), row),
                  pl.BlockSpec((nb, 6, D_MODEL), lambda i: (i, 0, 0)),
                  pl.BlockSpec((tm, WIDTH), row),
                  pl.BlockSpec((tm, WIDTH), row),
                  pl.BlockSpec((tm, WIDTH), lambda i: (i, C_MLO // WIDTH)),
                  pl.BlockSpec((tm, WIDTH), row),
                  pl.BlockSpec((tm, WIDTH), row),
                  gate(0), gate(1), gate(2),
                  _resident((1, WIDTH)),
                  _resident((3, WIDTH, D_MODEL)),
                  _resident((D_MODEL, D_MODEL))],
        out_specs=pl.BlockSpec((tm, D_MODEL), row),
        input_output_aliases={0: 0},
        compiler_params=_params(("parallel",)),
        name="merge",
    )(x, modtab, hf, hb, p, yb, yc, p, p, p, mlg, wbr, wo)


def _ffn_kernel(x_ref, mod_ref, g2_ref, w1_ref, w3_ref, w2_ref, gf_ref, o_ref, *, final):
    tm = x_ref.shape[0]
    d_ff = w1_ref.shape[1]
    split = pl.cdiv(d_ff // MXU_TILE, 2) * MXU_TILE
    for sb in range(tm // ROW_BLOCK):
        rows = slice(sb * ROW_BLOCK, (sb + 1) * ROW_BLOCK)
        xs = x_ref[rows, :]
        ms = jnp.mean(xs * xs, axis=-1, keepdims=True)
        y = xs * lax.rsqrt(ms + EPS) * g2_ref[...]
        h = (y * (1.0 + mod_ref[sb, 4:5, :]) + mod_ref[sb, 3:4, :]).astype(BF16)
        out = None
        for c0, c1 in ((0, split), (split, d_ff)):
            a = jnp.dot(h, w1_ref[:, c0:c1], preferred_element_type=F32)
            b = jnp.dot(h, w3_ref[:, c0:c1], preferred_element_type=F32)
            z = (a * _sigmoid(a) * b).astype(BF16)
            part = jnp.dot(z, w2_ref[c0:c1, :], preferred_element_type=F32)
            out = part if out is None else out + part
        xn = xs + mod_ref[sb, 5:6, :] * out
        if final:
            ms = jnp.mean(xn * xn, axis=-1, keepdims=True)
            xn = xn * lax.rsqrt(ms + EPS) * gf_ref[...]
        o_ref[rows, :] = xn


def _ffn(x, modtab, g2, w1, w3, w2, gfin, n_rows, final):
    tm = ROW_TILE
    nb = tm // ROW_BLOCK
    d_ff = w1.shape[1]
    row = lambda i: (i, 0)
    out_rows = n_rows if final else x.shape[0]
    return pl.pallas_call(
        functools.partial(_ffn_kernel, final=final),
        out_shape=jax.ShapeDtypeStruct((out_rows, D_MODEL), F32),
        grid=(n_rows // tm,),
        in_specs=[pl.BlockSpec((tm, D_MODEL), row),
                  pl.BlockSpec((nb, 6, D_MODEL), lambda i: (i, 0, 0)),
                  _resident((1, D_MODEL)),
                  _resident((D_MODEL, d_ff)),
                  _resident((D_MODEL, d_ff)),
                  _resident((d_ff, D_MODEL)),
                  _resident((1, D_MODEL))],
        out_specs=pl.BlockSpec((tm, D_MODEL), row),
        input_output_aliases={} if final else {0: 0},
        compiler_params=_params(("parallel",)),
        name="ffn",
    )(x, modtab, g2, w1, w3, w2, gfin)


def _dup_halves(w, base):
    h0 = w[..., base:base + HEAD_DIM]
    h1 = w[..., base + HEAD_DIM:base + 2 * HEAD_DIM]
    return [h0, h0, h1, h1]


def _arrange_in_proj(w):
    o_gate_ml = 4 * WIDTH
    o_wq = o_gate_ml + 2 * N_GATE
    o_wk, o_wv = o_wq + WIDTH, o_wq + WIDTH + 128
    o_gq = o_wv + 128
    o_gk, o_gv = o_gq + WIDTH, o_gq + WIDTH + 128
    o_gate = o_gv + 128
    main = jnp.concatenate(
        [w[..., o_gate:o_gate + 3 * D_MODEL], w[..., 0:WIDTH], w[..., 2 * WIDTH:4 * WIDTH],
         w[..., o_wq:o_wq + WIDTH]]
        + _dup_halves(w, o_wk) + [w[..., o_gq:o_gq + WIDTH]] + _dup_halves(w, o_gk), axis=-1)
    values = jnp.concatenate([w[..., o_gv:o_gv + 128], w[..., o_wv:o_wv + 128]], axis=-1)
    return main, w[..., o_gate_ml:o_gate_ml + 2 * N_GATE], w[..., WIDTH:2 * WIDTH], values


def _rope_tables(batch, n_lat, n_ctx):
    t = jnp.arange(n_lat)
    quarter = HEAD_DIM // 4
    inv = ROPE_THETA ** (-jnp.arange(0, 2 * quarter, 2, dtype=F32) / (2 * quarter))
    ang_r = (t // GRID_W).astype(F32)[:, None] * inv
    ang_c = (t % GRID_W).astype(F32)[:, None] * inv
    cos = jnp.concatenate([jnp.cos(ang_r)] * 2 + [jnp.cos(ang_c)] * 2, axis=1)
    sin = jnp.concatenate([-jnp.sin(ang_r), jnp.sin(ang_r), -jnp.sin(ang_c), jnp.sin(ang_c)], axis=1)
    cos = jnp.tile(cos, (batch, LANES // HEAD_DIM))
    sin = jnp.tile(sin, (batch, LANES // HEAD_DIM))
    pad = batch * n_ctx
    return (jnp.concatenate([cos, jnp.ones((pad, LANES), F32)], axis=0),
            jnp.concatenate([sin, jnp.zeros((pad, LANES), F32)], axis=0))


def kernel(x, c, ctx, c_ctx, w_mod, b_mod, norm1_g, w_in, b_in, ml_norm_g, win_sink, qn_g, kn_g,
           w_br, w_o, norm2_g, w_ff1, w_ff3, w_ff2, final_g):
    batch, n_lat, d = x.shape
    n_ctx = ctx.shape[1]
    depth = w_mod.shape[0]
    assert d == D_MODEL and n_lat % ROW_TILE == 0 and n_lat % GRID_W == 0
    assert (batch * n_ctx) % ROW_TILE == 0 and n_ctx % ROW_BLOCK == 0 and (batch * n_lat) % n_ctx == 0
    lat_rows, ctx_rows = batch * n_lat, batch * n_ctx

    xs = jnp.concatenate([x.reshape(lat_rows, d), ctx.reshape(ctx_rows, d)], axis=0)
    cvec = jnp.concatenate([c, c_ctx[None, :], jnp.zeros((8 - batch - 1, d), F32)], axis=0)
    block_class = np.concatenate([np.repeat(np.arange(batch), n_lat // ROW_BLOCK),
                                  np.full(ctx_rows // ROW_BLOCK, batch)])
    cos, sin = _rope_tables(batch, n_lat, n_ctx)
    avg = jnp.asarray(np.kron(np.eye(LANES // HEAD_DIM), np.full((HEAD_DIM, HEAD_DIM), 1.0 / HEAD_DIM)), BF16)

    out = None
    for l in range(depth):
        last = l == depth - 1
        mod = _mod_vectors(cvec, w_mod[l], b_mod[l]).reshape(8, 6, d)
        modtab = mod[block_class]

        w_main, w_gate, w_mlk, w_val = _arrange_in_proj(w_in[l])
        b_main, b_gate, b_mlk, b_val = _arrange_in_proj(b_in[l][None, :])
        p, gt, kt, vt = _in_projection(
            xs, modtab, norm1_g[l][None, :], w_main.astype(BF16), b_main,
            w_gate.T.astype(BF16), b_gate.T, w_mlk.T.astype(BF16), b_mlk.T, w_val.T.astype(BF16), b_val.T,
            cos, sin, jnp.tile(qn_g[l], 2)[None, :], jnp.tile(kn_g[l], 2)[None, :], avg)

        hf, hb = _mlstm(p, kt, gt, batch, n_lat, n_ctx)
        yb = _window_attention(p, vt, win_sink[l], batch, n_lat, n_ctx, ctx_queries=not last)
        yc = _global_attention(p, vt, batch, n_lat, n_ctx, ctx_queries=not last)

        n_rows = lat_rows if last else lat_rows + ctx_rows
        xs = _merge(xs, modtab, p, hf, hb, yb, yc, ml_norm_g[l].reshape(1, WIDTH),
                    w_br[l].astype(BF16), w_o[l].astype(BF16), n_rows)
        out = _ffn(xs, modtab, norm2_g[l][None, :], w_ff1[l].astype(BF16), w_ff3[l].astype(BF16),
                   w_ff2[l].astype(BF16), final_g[None, :], n_rows, final=last)
        xs = out
    return out.reshape(batch, n_lat, d)
```

```python
import functools

import jax
import jax.numpy as jnp
import numpy as np
from jax import lax
from jax.experimental import pallas as pl
from jax.experimental.pallas import tpu as pltpu

F32 = jnp.float32
BF16 = jnp.bfloat16

D_MODEL = 1024
GRID_W = 64
CHUNK = 128
HEAD_DIM = 64
ROPE_THETA = 10000.0
EPS = 1e-6
ML_HEADS = 4
ML_DIM = 128
Q_HEADS = 8
KV_HEADS = 2
WIDTH = 512
N_GATE = 2 * ML_HEADS

LANES = 128
MXU_TILE = 256
ROW_BLOCK = 256
ROW_TILE = 512
VMEM_LIMIT = 56 * 1024 * 1024

C_GATE = 0
C_MLQ, C_MLV, C_MLO = 3072, 3584, 4096
C_WQ, C_WK = 4608, 5120
C_GQ, C_GK = 5376, 5888
N_PROJ = C_GK + 256
VT_GLOBAL, VT_WINDOW = 0, 1
LOG2E = 1.4426950408889634
N_SCORE_BUFS = 4
BOUNDED_UNROLL = 8
SAFE_LOG2_BOUND = 50.0

NT_DIMS = (((1,), (1,)), ((), ()))


def _params(sem, vmem=VMEM_LIMIT):
    return pltpu.CompilerParams(dimension_semantics=sem, vmem_limit_bytes=vmem)


def _resident(shape):
    nd = len(shape)
    return pl.BlockSpec(shape, lambda *_: (0,) * nd, pipeline_mode=pl.Buffered(1))


def _sigmoid(x):
    return 1.0 / (1.0 + jnp.exp(-x))


def _log_sigmoid(x):
    return jnp.minimum(x, 0.0) - jnp.log(1.0 + jnp.exp(-jnp.abs(x)))


def _mod_kernel(c_ref, w_ref, b_ref, o_ref):
    c = c_ref[...]
    s = c * _sigmoid(c)
    o_ref[...] = jnp.dot(s, w_ref[...], preferred_element_type=F32) + b_ref[...]


def _mod_vectors(cvec, w_mod, b_mod):
    n_out = w_mod.shape[1]
    tn = 1536
    return pl.pallas_call(
        _mod_kernel,
        out_shape=jax.ShapeDtypeStruct((cvec.shape[0], n_out), F32),
        grid=(n_out // tn,),
        in_specs=[pl.BlockSpec(cvec.shape, lambda j: (0, 0)),
                  pl.BlockSpec((D_MODEL, tn), lambda j: (0, j)),
                  pl.BlockSpec((1, tn), lambda j: (0, j))],
        out_specs=pl.BlockSpec((cvec.shape[0], tn), lambda j: (0, j)),
        compiler_params=_params(("parallel",)),
        name="mod_vectors",
    )(cvec, w_mod, b_mod.reshape(1, n_out))


def _rope(acc, cos, sin, first_half):
    w = acc.shape[1]
    reps = w // LANES
    if reps > 1:
        cos = jnp.concatenate([cos] * reps, axis=1)
        sin = jnp.concatenate([sin] * reps, axis=1)
    ahead = pltpu.roll(acc, w - 16, axis=1)
    behind = pltpu.roll(acc, 16, axis=1)
    return acc * cos + jnp.where(first_half, ahead, behind) * sin


def _head_rms(acc, avg, gain):
    sq = (acc * acc).astype(BF16)
    outs = []
    for t in range(acc.shape[1] // LANES):
        sl = slice(t * LANES, (t + 1) * LANES)
        ms = jnp.dot(sq[:, sl], avg, preferred_element_type=F32)
        outs.append(acc[:, sl] * lax.rsqrt(ms + EPS) * gain)
    return jnp.concatenate(outs, axis=1)


def _inproj_kernel(x_ref, mod_ref, g1_ref, w_ref, b_ref, wgt_ref, bgt_ref, wkt_ref, bkt_ref,
                   wvt_ref, bvt_ref, cos_ref, sin_ref, qg_ref, kg_ref, avg_ref,
                   p_ref, gt_ref, kt_ref, vt_ref, h_ref):
    tm = x_ref.shape[0]
    for sb in range(tm // ROW_BLOCK):
        r0 = sb * ROW_BLOCK
        xs = x_ref[r0:r0 + ROW_BLOCK, :]
        ms = jnp.mean(xs * xs, axis=-1, keepdims=True)
        y = xs * lax.rsqrt(ms + EPS) * g1_ref[...]
        shift = mod_ref[sb, 0:1, :]
        scale = mod_ref[sb, 1:2, :]
        h_ref[r0:r0 + ROW_BLOCK, :] = (y * (1.0 + scale) + shift).astype(BF16)

    def first_half(width):
        return (lax.broadcasted_iota(jnp.int32, (1, width), 1) % 32) < 16

    avg = avg_ref[...]
    qg = qg_ref[...]
    kg = kg_ref[...]
    q_scale = HEAD_DIM ** -0.5

    h = h_ref[...]
    cos = cos_ref[...]
    sin = sin_ref[...]

    def proj(c0, width):
        return (jnp.dot(h, w_ref[:, c0:c0 + width], preferred_element_type=F32)
                + b_ref[:, c0:c0 + width])

    def store_plain(cols):
        for c0 in cols:
            p_ref[:, c0:c0 + WIDTH] = proj(c0, WIDTH).astype(BF16)

    plain = list(range(C_GATE, C_GATE + 3 * D_MODEL, WIDTH)) + [C_MLQ, C_MLV, C_MLO]
    raw_gq, raw_gk = proj(C_GQ, WIDTH), proj(C_GK, 256)
    raw_wq, raw_wk = proj(C_WQ, WIDTH), proj(C_WK, 256)
    store_plain(plain[:3])
    gq = _rope(_head_rms(raw_gq, avg, qg), cos, sin, first_half(WIDTH)) * (q_scale * LOG2E)
    p_ref[:, C_GQ:C_GQ + WIDTH] = gq.astype(BF16)
    gk = _rope(_head_rms(raw_gk, avg, kg), cos, sin, first_half(256))
    p_ref[:, C_GK:C_GK + 256] = gk.astype(BF16)
    p_ref[:, C_WQ:C_WQ + WIDTH] = (_rope(raw_wq, cos, sin, first_half(WIDTH)) * (q_scale * LOG2E)).astype(BF16)
    p_ref[:, C_WK:C_WK + 256] = _rope(raw_wk, cos, sin, first_half(256)).astype(BF16)
    store_plain(plain[3:])

    def proj_t(wt_ref, bt_ref):
        return lax.dot_general(wt_ref[...], h_ref[...], NT_DIMS, preferred_element_type=F32) + bt_ref[...]

    gta = proj_t(wgt_ref, bgt_ref)
    grow = lax.broadcasted_iota(jnp.int32, (2 * N_GATE, 1), 0)
    gt_ref[...] = jnp.where(grow >= N_GATE, _log_sigmoid(gta), gta)
    kt_ref[...] = proj_t(wkt_ref, bkt_ref).astype(BF16)
    vt_ref[...] = proj_t(wvt_ref, bvt_ref).astype(BF16)


def _in_projection(x, modtab, g1, w, b, wgt, bgt, wkt, bkt, wvt, bvt, cos, sin, qg, kg, avg):
    rows = x.shape[0]
    tm = ROW_TILE
    nb = tm // ROW_BLOCK
    kv = 2 * KV_HEADS * HEAD_DIM
    return pl.pallas_call(
        _inproj_kernel,
        out_shape=(jax.ShapeDtypeStruct((rows, N_PROJ), BF16),
                   jax.ShapeDtypeStruct((2 * N_GATE, rows), F32),
                   jax.ShapeDtypeStruct((WIDTH, rows), BF16),
                   jax.ShapeDtypeStruct((kv, rows), BF16)),
        grid=(rows // tm,),
        in_specs=[pl.BlockSpec((tm, D_MODEL), lambda i: (i, 0)),
                  pl.BlockSpec((nb, 6, D_MODEL), lambda i: (i, 0, 0)),
                  _resident((1, D_MODEL)),
                  _resident((D_MODEL, N_PROJ)),
                  _resident((1, N_PROJ)),
                  _resident((2 * N_GATE, D_MODEL)),
                  _resident((2 * N_GATE, 1)),
                  _resident((WIDTH, D_MODEL)),
                  _resident((WIDTH, 1)),
                  _resident((kv, D_MODEL)),
                  _resident((kv, 1)),
                  pl.BlockSpec((tm, LANES), lambda i: (i, 0)),
                  pl.BlockSpec((tm, LANES), lambda i: (i, 0)),
                  _resident((1, LANES)),
                  _resident((1, LANES)),
                  _resident((LANES, LANES))],
        out_specs=(pl.BlockSpec((tm, N_PROJ), lambda i: (i, 0)),
                   pl.BlockSpec((2 * N_GATE, tm), lambda i: (0, i)),
                   pl.BlockSpec((WIDTH, tm), lambda i: (0, i)),
                   pl.BlockSpec((kv, tm), lambda i: (0, i))),
        scratch_shapes=[pltpu.VMEM((tm, D_MODEL), BF16)],
        compiler_params=_params(("parallel",)),
        name="in_projection",
    )(x, modtab, g1, w, b, wgt, bgt, wkt, bkt, wvt, bvt, cos, sin, qg, kg, avg)


def _scan_lanes(x, op, fill, reverse):
    n = x.shape[1]
    lane = lax.broadcasted_iota(jnp.int32, x.shape, 1)
    sh = 1
    while sh < n:
        if reverse:
            moved = jnp.where(lane < n - sh, pltpu.roll(x, n - sh, axis=1), fill)
        else:
            moved = jnp.where(lane >= sh, pltpu.roll(x, sh, axis=1), fill)
        x = op(x, moved)
        sh *= 2
    return x


def _mlstm_kernel(qvf_ref, ktf_ref, gtf_ref, qvb_ref, ktb_ref, gtb_ref, hf_ref, hb_ref, cn_ref, m_ref):
    @pl.when(pl.program_id(1) == 0)
    def _():
        cn_ref[...] = jnp.zeros_like(cn_ref)
        m_ref[...] = jnp.zeros_like(m_ref)

    L = CHUNK
    row = lax.broadcasted_iota(jnp.int32, (L, L), 0)
    col = lax.broadcasted_iota(jnp.int32, (L, L), 1)
    scale = ML_DIM ** -0.5
    ones_v = jnp.ones((L, ML_DIM), BF16)
    pending = []
    for d, (qv_ref, kt_ref, gt_ref, h_ref) in enumerate(((qvf_ref, ktf_ref, gtf_ref, hf_ref),
                                                          (qvb_ref, ktb_ref, gtb_ref, hb_ref))):
        seen = (col <= row) if d == 0 else (col >= row)
        g0 = d * ML_HEADS
        ic = gt_ref[g0:g0 + ML_HEADS, :]
        lf = gt_ref[N_GATE + g0:N_GATE + g0 + ML_HEADS, :]
        r = ic - _scan_lanes(lf, jnp.add, 0.0, d == 1)
        r_max = jnp.max(r, axis=1, keepdims=True)
        b_end = jnp.sum(lf, axis=1, keepdims=True)
        for hh in range(ML_HEADS):
            idx = g0 + hh
            lanes = slice(hh * ML_DIM, (hh + 1) * ML_DIM)
            r_row = r[hh:hh + 1, :]
            m_old = m_ref[idx]
            m_end = jnp.maximum(m_old, r_max[hh:hh + 1, :])
            q = qv_ref[:, lanes]
            vo = jnp.concatenate([qv_ref[:, WIDTH + hh * ML_DIM:WIDTH + (hh + 1) * ML_DIM], ones_v], axis=1)
            kt = kt_ref[lanes, :]
            qsb = (q.astype(F32) * scale).astype(BF16)
            cn = cn_ref[idx]
            s_raw = jnp.dot(qsb, kt, preferred_element_type=F32)
            q_cn = jnp.dot(qsb, cn.astype(BF16), preferred_element_type=F32)
            kwt = (kt.astype(F32) * jnp.exp(r_row - m_end)).astype(BF16)
            cn_ref[idx] = jnp.exp(m_old - m_end) * cn + jnp.dot(kwt, vo, preferred_element_type=F32)
            m_ref[idx] = b_end[hh:hh + 1, :] + m_end
            pending.append((h_ref, lanes, seen, r_row, lf[hh:hh + 1, :], m_old, s_raw, q_cn, vo))

    for h_ref, lanes, seen, r_row, lf_row, m_old, s_raw, q_cn, vo in pending:
        b_col = jnp.sum(jnp.where(seen, lf_row, 0.0), axis=1, keepdims=True)
        m_col = jnp.maximum(m_old, jnp.max(jnp.where(seen, r_row, -jnp.inf), axis=1, keepdims=True))
        w = jnp.exp(jnp.where(seen, r_row - m_col, -jnp.inf))
        a = jnp.exp(m_old - m_col)
        s_vo = jnp.dot((s_raw * w).astype(BF16), vo, preferred_element_type=F32)
        num = a * q_cn[:, :ML_DIM] + s_vo[:, :ML_DIM]
        den = a * q_cn[:, ML_DIM:] + s_vo[:, ML_DIM:]
        hc = num / jnp.maximum(jnp.abs(den), jnp.exp(-(b_col + m_col)))
        h_ref[:, lanes] = hc.astype(h_ref.dtype)


def _mlstm(p, kt, gt, batch, n_lat, n_ctx):
    rows = p.shape[0]
    cl, cc = n_lat // CHUNK, n_ctx // CHUNK
    lat_base, ctx_base = 0, batch * cl

    def fwd_chunk(b, i):
        return jnp.where(i < cc, ctx_base + b * cc + i, lat_base + b * cl + (i - cc))

    def bwd_chunk(b, i):
        return jnp.where(i < cc, ctx_base + b * cc + (cc - 1 - i), lat_base + b * cl + (cl - 1 - (i - cc)))

    def specs(chunk):
        return [pl.BlockSpec((CHUNK, 2 * WIDTH), lambda b, i: (chunk(b, i), C_MLQ // (2 * WIDTH))),
                pl.BlockSpec((WIDTH, CHUNK), lambda b, i: (0, chunk(b, i))),
                pl.BlockSpec((2 * N_GATE, CHUNK), lambda b, i: (0, chunk(b, i)))]

    n_state = 2 * ML_HEADS
    return pl.pallas_call(
        _mlstm_kernel,
        out_shape=(jax.ShapeDtypeStruct((rows, WIDTH), BF16), jax.ShapeDtypeStruct((rows, WIDTH), BF16)),
        grid=(batch, cl + cc),
        in_specs=specs(fwd_chunk) + specs(bwd_chunk),
        out_specs=(pl.BlockSpec((CHUNK, WIDTH), lambda b, i: (fwd_chunk(b, i), 0)),
                   pl.BlockSpec((CHUNK, WIDTH), lambda b, i: (bwd_chunk(b, i), 0))),
        scratch_shapes=[pltpu.VMEM((n_state, ML_DIM, 2 * ML_DIM), F32),
                        pltpu.VMEM((n_state, 1, 1), F32)],
        compiler_params=_params(("parallel", "arbitrary")),
        name="mlstm_scan",
    )(p, kt, gt, p, kt, gt)


def _half_mask(e):
    lane = lax.broadcasted_iota(jnp.int32, (1, LANES), 1)
    return (lane < HEAD_DIM) if e == 0 else (lane >= HEAD_DIM)


def _win_kernel(sink_ref, q_ref, kc_ref, vtc_ref, k0_ref, k1_ref, k2_ref, k3_ref,
                vt0_ref, vt1_ref, vt2_ref, vt3_ref, o_ref, *, n_lat_tiles, n_lat_blocks):
    t = pl.program_id(1)
    tq = q_ref.shape[0]
    n_ctx = kc_ref.shape[0]
    is_lat = t < n_lat_tiles
    kk = lax.broadcasted_iota(jnp.int32, (CHUNK, tq), 0)
    qq = lax.broadcasted_iota(jnp.int32, (CHUNK, tq), 1)
    band = (kk >= qq, qq <= kk + CHUNK, kk <= qq, kk + CHUNK <= qq)
    first = 2 * t - 1
    present = [jnp.logical_and(is_lat, jnp.logical_and(first + i >= 0, first + i < n_lat_blocks))
               for i in range(4)]
    visible = [jnp.logical_and(band[i], present[i]) for i in range(4)]
    k_refs = (k0_ref, k1_ref, k2_ref, k3_ref)
    vt_refs = (vt0_ref, vt1_ref, vt2_ref, vt3_ref)
    ones = jnp.ones((16, n_ctx + 4 * CHUNK), BF16)

    scores, vtas = [], []
    for g in range(KV_HEADS):
        lanes = slice(g * LANES, (g + 1) * LANES)
        k_all = jnp.concatenate([kc_ref[:, lanes]] + [r[:, lanes] for r in k_refs], axis=0)
        vrows = slice(g * HEAD_DIM, (g + 1) * HEAD_DIM)
        vtas.append(jnp.concatenate(
            [jnp.concatenate([vtc_ref[vrows, :]] + [r[vrows, :] for r in vt_refs], axis=1), ones], axis=0))
        for tt in range(2 * g, 2 * g + 2):
            qf = q_ref[:, tt * LANES:(tt + 1) * LANES].astype(F32)
            for e in range(2):
                qt = jnp.where(_half_mask(e), qf, 0.0).T.astype(BF16)
                scores.append(jnp.dot(k_all, qt, preferred_element_type=F32))
    outs = []
    for h, s in enumerate(scores):
        sink = sink_ref[h] * LOG2E
        parts = [s[:n_ctx, :]]
        for i in range(4):
            blk = s[n_ctx + i * CHUNK:n_ctx + (i + 1) * CHUNK, :]
            parts.append(jnp.where(visible[i], blk, -jnp.inf))
        s = jnp.concatenate(parts, axis=0)
        m = jnp.maximum(jnp.max(s, axis=0, keepdims=True), sink)
        pt = jnp.exp2(s - m).astype(BF16)
        ol = jnp.dot(vtas[h // (Q_HEADS // KV_HEADS)], pt, preferred_element_type=F32)
        den = ol[HEAD_DIM:HEAD_DIM + 1, :] + jnp.exp2(sink - m)
        outs.append(ol[:HEAD_DIM, :] / den)
    for tt in range(Q_HEADS // 2):
        o_ref[:, tt * LANES:(tt + 1) * LANES] = (
            jnp.concatenate(outs[2 * tt:2 * tt + 2], axis=0).T.astype(o_ref.dtype))


def _window_attention(p, vt, sink, batch, n_lat, n_ctx, ctx_queries):
    tq = 256
    nlt, nct = n_lat // tq, n_ctx // tq
    nlb = n_lat // CHUNK
    nq = nlt + (nct if ctx_queries else 0)
    out_rows = batch * (n_lat + (n_ctx if ctx_queries else 0))
    ctx_blk = batch * n_lat // n_ctx
    k_col = C_WK // 256

    def q_block(b, t):
        return jnp.where(t < nlt, b * nlt + t, batch * nlt + b * nct + (t - nlt))

    def near(i):
        return lambda b, t: b * nlb + jnp.clip(2 * t - 1 + i, 0, nlb - 1)

    k_specs = [pl.BlockSpec((CHUNK, 256), (lambda f: lambda b, t: (f(b, t), k_col))(near(i))) for i in range(4)]
    vt_specs = [pl.BlockSpec((LANES, CHUNK), (lambda f: lambda b, t: (VT_WINDOW, f(b, t)))(near(i)))
                for i in range(4)]
    return pl.pallas_call(
        functools.partial(_win_kernel, n_lat_tiles=nlt, n_lat_blocks=nlb),
        out_shape=jax.ShapeDtypeStruct((out_rows, WIDTH), BF16),
        grid=(batch, nq),
        in_specs=[pl.BlockSpec(memory_space=pltpu.SMEM),
                  pl.BlockSpec((tq, WIDTH), lambda b, t: (q_block(b, t), C_WQ // WIDTH)),
                  pl.BlockSpec((n_ctx, 256), lambda b, t: (ctx_blk + b, k_col)),
                  pl.BlockSpec((LANES, n_ctx), lambda b, t: (VT_WINDOW, ctx_blk + b))]
                 + k_specs + vt_specs,
        out_specs=pl.BlockSpec((tq, WIDTH), lambda b, t: (q_block(b, t), 0)),
        compiler_params=_params(("parallel", "parallel")),
        name="window_attention",
    )(sink, p, p, vt, p, p, p, p, vt, vt, vt, vt)


def _glb_kernel(q_ref, kc_ref, vtc_ref, kl_ref, vtl_ref, o_ref, qt_ref, knorm_ref, *scratch, n_lat_tiles, tk):
    heads = Q_HEADS // KV_HEADS
    acc_refs = scratch[0:heads]
    sc_refs = scratch[heads:2 * heads]
    ring = [scratch[(2 + i) * heads:(3 + i) * heads] for i in range(N_SCORE_BUFS)]
    base = (2 + N_SCORE_BUFS) * heads
    pc_refs = scratch[base:base + heads]
    p_ring = [scratch[base + (1 + i) * heads:base + (2 + i) * heads] for i in range(2)]
    qi = pl.program_id(2)
    tq = q_ref.shape[0]
    n_chunks = kl_ref.shape[0] // tk

    def k_chunk(c):
        off = pl.multiple_of(jnp.minimum(c, n_chunks - 1) * tk, tk)
        return kl_ref[pl.ds(off, tk), :]

    def vt_chunk(c):
        return vtl_ref[:, pl.ds(pl.multiple_of(c * tk, tk), tk)]

    def with_ones(vt):
        ones = jnp.ones((acc_refs[0].shape[0] - HEAD_DIM, vt.shape[1]), BF16)
        return jnp.concatenate([vt, ones], axis=0)

    @pl.when(qi == 0)
    def _():
        def sq_norm(k):
            kf = k.astype(F32)
            return jnp.max(jnp.sum(kf * kf, axis=1, keepdims=True), axis=0, keepdims=True)

        def body(c, best):
            return jnp.maximum(best, sq_norm(k_chunk(c)))
        best = lax.fori_loop(0, n_chunks, body, sq_norm(kc_ref[...]))
        knorm_ref[...] = jnp.sqrt(0.5 * best)

    for t in range(heads // 2):
        qf = q_ref[:, t * LANES:(t + 1) * LANES].astype(F32)
        for e in range(2):
            h = 2 * t + e
            qt_ref[:, h * tq:(h + 1) * tq] = jnp.where(_half_mask(e), qf, 0.0).T.astype(BF16)
    for h in range(heads):
        acc_refs[h][...] = jnp.zeros_like(acc_refs[h])

    def q_t(h):
        return qt_ref[:, h * tq:(h + 1) * tq]

    bound = [jnp.sqrt(jnp.sum(jnp.square(q_t(h).astype(F32)), axis=0, keepdims=True)) * knorm_ref[...]
             for h in range(heads)]
    bounded = jnp.max(functools.reduce(jnp.maximum, bound)) <= SAFE_LOG2_BOUND

    def produce(p_refs, k):
        for h in range(heads):
            s = jnp.dot(k, q_t(h), preferred_element_type=F32)
            p_refs[h][...] = jnp.exp2(s - bound[h]).astype(BF16)

    def consume(p_refs, vt):
        vta = with_ones(vt)
        for h in range(heads):
            acc_refs[h][...] += jnp.dot(vta, p_refs[h][...], preferred_element_type=F32)

    @pl.when(bounded)
    def _():
        produce(pc_refs, kc_ref[...])

        @pl.when(qi >= n_lat_tiles)
        def _():
            consume(pc_refs, vtc_ref[...])

        @pl.when(qi < n_lat_tiles)
        def _():
            produce(p_ring[0], k_chunk(0))
            consume(pc_refs, vtc_ref[...])

            def body(j, carry):
                for i in range(unroll):
                    c = unroll * j + i
                    produce(p_ring[(i + 1) % 2], k_chunk(c + 1))
                    consume(p_ring[i % 2], vt_chunk(c))
                return carry
            unroll = BOUNDED_UNROLL if n_chunks % BOUNDED_UNROLL == 0 else N_SCORE_BUFS
            lax.fori_loop(0, n_chunks // unroll, body, 0)

    neg_inf = (jnp.full((1, tq), -jnp.inf, F32),) * heads

    def fill(s_refs, k, m_run):
        out = []
        for h in range(heads):
            s = jnp.dot(k, q_t(h), preferred_element_type=F32)
            s_refs[h][...] = s
            out.append(jnp.maximum(m_run[h], jnp.max(s, axis=0, keepdims=True)))
        return tuple(out)

    def drain(s_refs, vt, m_run, m_acc):
        vta = with_ones(vt)
        for h in range(heads):
            alpha = jnp.exp2(m_acc[h] - m_run[h])
            pt = jnp.exp2(s_refs[h][...] - m_run[h]).astype(BF16)
            acc_refs[h][...] = alpha * acc_refs[h][...] + jnp.dot(vta, pt, preferred_element_type=F32)
        return m_run

    @pl.when(jnp.logical_not(bounded))
    def _():
        m_ctx = fill(sc_refs, kc_ref[...], neg_inf)

        @pl.when(qi >= n_lat_tiles)
        def _():
            drain(sc_refs, vtc_ref[...], m_ctx, neg_inf)

        @pl.when(qi < n_lat_tiles)
        def _():
            m_one = fill(ring[0], k_chunk(0), m_ctx)
            m_two = fill(ring[1], k_chunk(1), m_one)
            m_acc = drain(sc_refs, vtc_ref[...], m_one, neg_inf)

            def body(j, carry):
                m_run, m_acc = carry
                for i in range(N_SCORE_BUFS):
                    c = N_SCORE_BUFS * j + i
                    m_next = fill(ring[(i + 2) % N_SCORE_BUFS], k_chunk(c + 2), m_run)
                    m_acc = drain(ring[i], vt_chunk(c), m_run, m_acc)
                    m_run = m_next
                return m_run, m_acc
            lax.fori_loop(0, n_chunks // N_SCORE_BUFS, body, (m_two, m_acc))

    for t in range(heads // 2):
        pair = []
        for e in range(2):
            a = acc_refs[2 * t + e][...]
            pair.append(a[:HEAD_DIM, :] / a[HEAD_DIM:HEAD_DIM + 1, :])
        o_ref[:, t * LANES:(t + 1) * LANES] = jnp.concatenate(pair, axis=0).T.astype(o_ref.dtype)


def _global_attention(p, vt, batch, n_lat, n_ctx, ctx_queries):
    tq = 256
    tk = min(512, n_lat // N_SCORE_BUFS)
    assert n_lat % (tk * N_SCORE_BUFS) == 0 and tk % LANES == 0
    nl, nc = n_lat // tq, n_ctx // tq
    nq = nl + (nc if ctx_queries else 0)
    out_rows = batch * (n_lat + (n_ctx if ctx_queries else 0))
    ctx_base = batch * nl
    heads = Q_HEADS // KV_HEADS
    acc_rows = HEAD_DIM + 16

    def q_block(b, qi):
        return jnp.where(qi < nl, b * nl + qi, ctx_base + b * nc + (qi - nl))

    ctx_blk = batch * n_lat // n_ctx
    return pl.pallas_call(
        functools.partial(_glb_kernel, n_lat_tiles=nl, tk=tk),
        out_shape=jax.ShapeDtypeStruct((out_rows, WIDTH), BF16),
        grid=(batch, KV_HEADS, nq),
        in_specs=[pl.BlockSpec((tq, 2 * LANES), lambda b, g, qi: (q_block(b, qi), C_GQ // 256 + g)),
                  pl.BlockSpec((n_ctx, LANES), lambda b, g, qi: (ctx_blk + b, C_GK // LANES + g)),
                  pl.BlockSpec((HEAD_DIM, n_ctx), lambda b, g, qi: (KV_HEADS * VT_GLOBAL + g, ctx_blk + b)),
                  pl.BlockSpec((n_lat, LANES), lambda b, g, qi: (b, C_GK // LANES + g)),
                  pl.BlockSpec((HEAD_DIM, n_lat), lambda b, g, qi: (KV_HEADS * VT_GLOBAL + g, b))],
        out_specs=pl.BlockSpec((tq, 2 * LANES), lambda b, g, qi: (q_block(b, qi), g)),
        scratch_shapes=([pltpu.VMEM((LANES, heads * tq), BF16), pltpu.VMEM((1, 1), F32)]
                        + [pltpu.VMEM((acc_rows, tq), F32)] * heads
                        + [pltpu.VMEM((n_ctx, tq), F32)] * heads
                        + [pltpu.VMEM((tk, tq), F32)] * (N_SCORE_BUFS * heads)
                        + [pltpu.VMEM((n_ctx, tq), BF16)] * heads
                        + [pltpu.VMEM((tk, tq), BF16)] * (2 * heads)),
        compiler_params=_params(("parallel", "parallel", "arbitrary")),
        name="global_attention",
    )(p, p, vt, p, vt)


def _merge_kernel(x_ref, mod_ref, hf_ref, hb_ref, og_ref, yb_ref, yc_ref, ga_ref, gb_ref, gc_ref,
                  mlg_ref, wbr_ref, wo_ref, o_ref):
    tm = x_ref.shape[0]
    proj_b = jnp.dot(yb_ref[...], wbr_ref[1], preferred_element_type=F32)
    proj_c = jnp.dot(yc_ref[...], wbr_ref[2], preferred_element_type=F32)
    hs = hf_ref[...].astype(F32) + hb_ref[...].astype(F32)
    parts = []
    for t in range(ML_HEADS):
        ht = hs[:, t * ML_DIM:(t + 1) * ML_DIM]
        ms = jnp.mean(ht * ht, axis=-1, keepdims=True)
        parts.append(ht * lax.rsqrt(ms + EPS))
    ya = (jnp.concatenate(parts, axis=1) * mlg_ref[...] * _sigmoid(og_ref[...].astype(F32))).astype(BF16)
    proj_a = jnp.dot(ya, wbr_ref[0], preferred_element_type=F32)
    merged = (_sigmoid(gb_ref[...].astype(F32)) * proj_b + _sigmoid(gc_ref[...].astype(F32)) * proj_c
              + _sigmoid(ga_ref[...].astype(F32)) * proj_a)
    out = jnp.dot(merged.astype(BF16), wo_ref[...], preferred_element_type=F32)
    for sb in range(tm // ROW_BLOCK):
        rows = slice(sb * ROW_BLOCK, (sb + 1) * ROW_BLOCK)
        o_ref[rows, :] = x_ref[rows, :] + mod_ref[sb, 2:3, :] * out[rows, :]


def _merge(x, modtab, p, hf, hb, yb, yc, mlg, wbr, wo, n_rows):
    tm = ROW_TILE
    nb = tm // ROW_BLOCK
    row = lambda i: (i, 0)
    gate = lambda k: pl.BlockSpec((tm, D_MODEL), lambda i: (i, C_GATE // D_MODEL + k))
    return pl.pallas_call(
        _merge_kernel,
        out_shape=jax.ShapeDtypeStruct(x.shape, F32),
        grid=(n_rows // tm,),
        in_specs=[pl.BlockSpec((tm, D_MODEL), row),
                  pl.BlockSpec((nb, 6, D_MODEL), lambda i: (i, 0, 0)),
                  pl.BlockSpec((tm, WIDTH), row),
                  pl.BlockSpec((tm, WIDTH), row),
                  pl.BlockSpec((tm, WIDTH), lambda i: (i, C_MLO // WIDTH)),
                  pl.BlockSpec((tm, WIDTH), row),
                  pl.BlockSpec((tm, WIDTH), row),
                  gate(0), gate(1), gate(2),
                  _resident((1, WIDTH)),
                  _resident((3, WIDTH, D_MODEL)),
                  _resident((D_MODEL, D_MODEL))],
        out_specs=pl.BlockSpec((tm, D_MODEL), row),
        input_output_aliases={0: 0},
        compiler_params=_params(("parallel",)),
        name="merge",
    )(x, modtab, hf, hb, p, yb, yc, p, p, p, mlg, wbr, wo)


def _ffn_kernel(x_ref, mod_ref, g2_ref, w1_ref, w3_ref, w2_ref, gf_ref, o_ref, *, final):
    tm = x_ref.shape[0]
    d_ff = w1_ref.shape[1]
    split = pl.cdiv(d_ff // MXU_TILE, 2) * MXU_TILE
    for sb in range(tm // ROW_BLOCK):
        rows = slice(sb * ROW_BLOCK, (sb + 1) * ROW_BLOCK)
        xs = x_ref[rows, :]
        ms = jnp.mean(xs * xs, axis=-1, keepdims=True)
        y = xs * lax.rsqrt(ms + EPS) * g2_ref[...]
        h = (y * (1.0 + mod_ref[sb, 4:5, :]) + mod_ref[sb, 3:4, :]).astype(BF16)
        out = None
        for c0, c1 in ((0, split), (split, d_ff)):
            a = jnp.dot(h, w1_ref[:, c0:c1], preferred_element_type=F32)
            b = jnp.dot(h, w3_ref[:, c0:c1], preferred_element_type=F32)
            z = (a * _sigmoid(a) * b).astype(BF16)
            part = jnp.dot(z, w2_ref[c0:c1, :], preferred_element_type=F32)
            out = part if out is None else out + part
        xn = xs + mod_ref[sb, 5:6, :] * out
        if final:
            ms = jnp.mean(xn * xn, axis=-1, keepdims=True)
            xn = xn * lax.rsqrt(ms + EPS) * gf_ref[...]
        o_ref[rows, :] = xn


def _ffn(x, modtab, g2, w1, w3, w2, gfin, n_rows, final):
    tm = ROW_TILE
    nb = tm // ROW_BLOCK
    d_ff = w1.shape[1]
    row = lambda i: (i, 0)
    out_rows = n_rows if final else x.shape[0]
    return pl.pallas_call(
        functools.partial(_ffn_kernel, final=final),
        out_shape=jax.ShapeDtypeStruct((out_rows, D_MODEL), F32),
        grid=(n_rows // tm,),
        in_specs=[pl.BlockSpec((tm, D_MODEL), row),
                  pl.BlockSpec((nb, 6, D_MODEL), lambda i: (i, 0, 0)),
                  _resident((1, D_MODEL)),
                  _resident((D_MODEL, d_ff)),
                  _resident((D_MODEL, d_ff)),
                  _resident((d_ff, D_MODEL)),
                  _resident((1, D_MODEL))],
        out_specs=pl.BlockSpec((tm, D_MODEL), row),
        input_output_aliases={} if final else {0: 0},
        compiler_params=_params(("parallel",)),
        name="ffn",
    )(x, modtab, g2, w1, w3, w2, gfin)


def _dup_halves(w, base):
    h0 = w[..., base:base + HEAD_DIM]
    h1 = w[..., base + HEAD_DIM:base + 2 * HEAD_DIM]
    return [h0, h0, h1, h1]


def _arrange_in_proj(w):
    o_gate_ml = 4 * WIDTH
    o_wq = o_gate_ml + 2 * N_GATE
    o_wk, o_wv = o_wq + WIDTH, o_wq + WIDTH + 128
    o_gq = o_wv + 128
    o_gk, o_gv = o_gq + WIDTH, o_gq + WIDTH + 128
    o_gate = o_gv + 128
    main = jnp.concatenate(
        [w[..., o_gate:o_gate + 3 * D_MODEL], w[..., 0:WIDTH], w[..., 2 * WIDTH:4 * WIDTH],
         w[..., o_wq:o_wq + WIDTH]]
        + _dup_halves(w, o_wk) + [w[..., o_gq:o_gq + WIDTH]] + _dup_halves(w, o_gk), axis=-1)
    values = jnp.concatenate([w[..., o_gv:o_gv + 128], w[..., o_wv:o_wv + 128]], axis=-1)
    return main, w[..., o_gate_ml:o_gate_ml + 2 * N_GATE], w[..., WIDTH:2 * WIDTH], values


def _rope_tables(batch, n_lat, n_ctx):
    t = jnp.arange(n_lat)
    quarter = HEAD_DIM // 4
    inv = ROPE_THETA ** (-jnp.arange(0, 2 * quarter, 2, dtype=F32) / (2 * quarter))
    ang_r = (t // GRID_W).astype(F32)[:, None] * inv
    ang_c = (t % GRID_W).astype(F32)[:, None] * inv
    cos = jnp.concatenate([jnp.cos(ang_r)] * 2 + [jnp.cos(ang_c)] * 2, axis=1)
    sin = jnp.concatenate([-jnp.sin(ang_r), jnp.sin(ang_r), -jnp.sin(ang_c), jnp.sin(ang_c)], axis=1)
    cos = jnp.tile(cos, (batch, LANES // HEAD_DIM))
    sin = jnp.tile(sin, (batch, LANES // HEAD_DIM))
    pad = batch * n_ctx
    return (jnp.concatenate([cos, jnp.ones((pad, LANES), F32)], axis=0),
            jnp.concatenate([sin, jnp.zeros((pad, LANES), F32)], axis=0))


def kernel(x, c, ctx, c_ctx, w_mod, b_mod, norm1_g, w_in, b_in, ml_norm_g, win_sink, qn_g, kn_g,
           w_br, w_o, norm2_g, w_ff1, w_ff3, w_ff2, final_g):
    batch, n_lat, d = x.shape
    n_ctx = ctx.shape[1]
    depth = w_mod.shape[0]
    assert d == D_MODEL and n_lat % ROW_TILE == 0 and n_lat % GRID_W == 0
    assert (batch * n_ctx) % ROW_TILE == 0 and n_ctx % ROW_BLOCK == 0 and (batch * n_lat) % n_ctx == 0
    lat_rows, ctx_rows = batch * n_lat, batch * n_ctx

    xs = jnp.concatenate([x.reshape(lat_rows, d), ctx.reshape(ctx_rows, d)], axis=0)
    cvec = jnp.concatenate([c, c_ctx[None, :], jnp.zeros((8 - batch - 1, d), F32)], axis=0)
    block_class = np.concatenate([np.repeat(np.arange(batch), n_lat // ROW_BLOCK),
                                  np.full(ctx_rows // ROW_BLOCK, batch)])
    cos, sin = _rope_tables(batch, n_lat, n_ctx)
    avg = jnp.asarray(np.kron(np.eye(LANES // HEAD_DIM), np.full((HEAD_DIM, HEAD_DIM), 1.0 / HEAD_DIM)), BF16)

    out = None
    for l in range(depth):
        last = l == depth - 1
        mod = _mod_vectors(cvec, w_mod[l], b_mod[l]).reshape(8, 6, d)
        modtab = mod[block_class]

        w_main, w_gate, w_mlk, w_val = _arrange_in_proj(w_in[l])
        b_main, b_gate, b_mlk, b_val = _arrange_in_proj(b_in[l][None, :])
        p, gt, kt, vt = _in_projection(
            xs, modtab, norm1_g[l][None, :], w_main.astype(BF16), b_main,
            w_gate.T.astype(BF16), b_gate.T, w_mlk.T.astype(BF16), b_mlk.T, w_val.T.astype(BF16), b_val.T,
            cos, sin, jnp.tile(qn_g[l], 2)[None, :], jnp.tile(kn_g[l], 2)[None, :], avg)

        hf, hb = _mlstm(p, kt, gt, batch, n_lat, n_ctx)
        yb = _window_attention(p, vt, win_sink[l], batch, n_lat, n_ctx, ctx_queries=not last)
        yc = _global_attention(p, vt, batch, n_lat, n_ctx, ctx_queries=not last)

        n_rows = lat_rows if last else lat_rows + ctx_rows
        xs = _merge(xs, modtab, p, hf, hb, yb, yc, ml_norm_g[l].reshape(1, WIDTH),
                    w_br[l].astype(BF16), w_o[l].astype(BF16), n_rows)
        out = _ffn(xs, modtab, norm2_g[l][None, :], w_ff1[l].astype(BF16), w_ff3[l].astype(BF16),
                   w_ff2[l].astype(BF16), final_g[None, :], n_rows, final=last)
        xs = out
    return out.reshape(batch, n_lat, d)
```

```python
import functools

import jax
import jax.numpy as jnp
import numpy as np
from jax import lax
from jax.experimental import pallas as pl
from jax.experimental.pallas import tpu as pltpu

F32 = jnp.float32
BF16 = jnp.bfloat16

D_MODEL = 1024
GRID_W = 64
CHUNK = 128
HEAD_DIM = 64
ROPE_THETA = 10000.0
EPS = 1e-6
ML_HEADS = 4
ML_DIM = 128
Q_HEADS = 8
KV_HEADS = 2
WIDTH = 512
N_GATE = 2 * ML_HEADS

LANES = 128
MXU_TILE = 256
ROW_BLOCK = 256
ROW_TILE = 512
VMEM_LIMIT = 56 * 1024 * 1024

C_GATE = 0
C_MLQ, C_MLV, C_MLO = 3072, 3584, 4096
C_WQ, C_WK = 4608, 5120
C_GQ, C_GK = 5376, 5888
N_PROJ = C_GK + 256
VT_GLOBAL, VT_WINDOW = 0, 1
LOG2E = 1.4426950408889634
N_SCORE_BUFS = 4
SPAN_CHUNKS = 4
SAFE_LOG2_BOUND = 50.0

NT_DIMS = (((1,), (1,)), ((), ()))


def _params(sem, vmem=VMEM_LIMIT):
    return pltpu.CompilerParams(dimension_semantics=sem, vmem_limit_bytes=vmem)


def _resident(shape):
    nd = len(shape)
    return pl.BlockSpec(shape, lambda *_: (0,) * nd, pipeline_mode=pl.Buffered(1))


def _sigmoid(x):
    return 1.0 / (1.0 + jnp.exp(-x))


def _log_sigmoid(x):
    return jnp.minimum(x, 0.0) - jnp.log(1.0 + jnp.exp(-jnp.abs(x)))


def _mod_kernel(c_ref, w_ref, b_ref, o_ref):
    c = c_ref[...]
    s = c * _sigmoid(c)
    o_ref[...] = jnp.dot(s, w_ref[...], preferred_element_type=F32) + b_ref[...]


def _mod_vectors(cvec, w_mod, b_mod):
    n_out = w_mod.shape[1]
    tn = 1536
    return pl.pallas_call(
        _mod_kernel,
        out_shape=jax.ShapeDtypeStruct((cvec.shape[0], n_out), F32),
        grid=(n_out // tn,),
        in_specs=[pl.BlockSpec(cvec.shape, lambda j: (0, 0)),
                  pl.BlockSpec((D_MODEL, tn), lambda j: (0, j)),
                  pl.BlockSpec((1, tn), lambda j: (0, j))],
        out_specs=pl.BlockSpec((cvec.shape[0], tn), lambda j: (0, j)),
        compiler_params=_params(("parallel",)),
        name="mod_vectors",
    )(cvec, w_mod, b_mod.reshape(1, n_out))


def _rope(acc, cos, sin, first_half):
    w = acc.shape[1]
    reps = w // LANES
    if reps > 1:
        cos = jnp.concatenate([cos] * reps, axis=1)
        sin = jnp.concatenate([sin] * reps, axis=1)
    ahead = pltpu.roll(acc, w - 16, axis=1)
    behind = pltpu.roll(acc, 16, axis=1)
    return acc * cos + jnp.where(first_half, ahead, behind) * sin


def _head_rms(acc, avg, gain):
    sq = (acc * acc).astype(BF16)
    outs = []
    for t in range(acc.shape[1] // LANES):
        sl = slice(t * LANES, (t + 1) * LANES)
        ms = jnp.dot(sq[:, sl], avg, preferred_element_type=F32)
        outs.append(acc[:, sl] * lax.rsqrt(ms + EPS) * gain)
    return jnp.concatenate(outs, axis=1)


def _inproj_kernel(x_ref, mod_ref, g1_ref, w_ref, b_ref, wgt_ref, bgt_ref, wkt_ref, bkt_ref,
                   wvt_ref, bvt_ref, cos_ref, sin_ref, qg_ref, kg_ref, avg_ref,
                   p_ref, gt_ref, kt_ref, vt_ref, h_ref):
    tm = x_ref.shape[0]
    for sb in range(tm // ROW_BLOCK):
        r0 = sb * ROW_BLOCK
        xs = x_ref[r0:r0 + ROW_BLOCK, :]
        ms = jnp.mean(xs * xs, axis=-1, keepdims=True)
        y = xs * lax.rsqrt(ms + EPS) * g1_ref[...]
        shift = mod_ref[sb, 0:1, :]
        scale = mod_ref[sb, 1:2, :]
        h_ref[r0:r0 + ROW_BLOCK, :] = (y * (1.0 + scale) + shift).astype(BF16)

    def first_half(width):
        return (lax.broadcasted_iota(jnp.int32, (1, width), 1) % 32) < 16

    avg = avg_ref[...]
    qg = qg_ref[...]
    kg = kg_ref[...]
    q_scale = HEAD_DIM ** -0.5

    h = h_ref[...]
    cos = cos_ref[...]
    sin = sin_ref[...]

    def proj(c0, width):
        return (jnp.dot(h, w_ref[:, c0:c0 + width], preferred_element_type=F32)
                + b_ref[:, c0:c0 + width])

    def store_plain(cols):
        for c0 in cols:
            p_ref[:, c0:c0 + WIDTH] = proj(c0, WIDTH).astype(BF16)

    plain = list(range(C_GATE, C_GATE + 3 * D_MODEL, WIDTH)) + [C_MLQ, C_MLV, C_MLO]
    raw_gq, raw_gk = proj(C_GQ, WIDTH), proj(C_GK, 256)
    raw_wq, raw_wk = proj(C_WQ, WIDTH), proj(C_WK, 256)
    store_plain(plain[:3])
    gq = _rope(_head_rms(raw_gq, avg, qg), cos, sin, first_half(WIDTH)) * (q_scale * LOG2E)
    p_ref[:, C_GQ:C_GQ + WIDTH] = gq.astype(BF16)
    gk = _rope(_head_rms(raw_gk, avg, kg), cos, sin, first_half(256))
    p_ref[:, C_GK:C_GK + 256] = gk.astype(BF16)
    p_ref[:, C_WQ:C_WQ + WIDTH] = (_rope(raw_wq, cos, sin, first_half(WIDTH)) * (q_scale * LOG2E)).astype(BF16)
    p_ref[:, C_WK:C_WK + 256] = _rope(raw_wk, cos, sin, first_half(256)).astype(BF16)
    store_plain(plain[3:])

    def proj_t(wt_ref, bt_ref):
        return lax.dot_general(wt_ref[...], h_ref[...], NT_DIMS, preferred_element_type=F32) + bt_ref[...]

    gta = proj_t(wgt_ref, bgt_ref)
    grow = lax.broadcasted_iota(jnp.int32, (2 * N_GATE, 1), 0)
    gt_ref[...] = jnp.where(grow >= N_GATE, _log_sigmoid(gta), gta)
    kt_ref[...] = proj_t(wkt_ref, bkt_ref).astype(BF16)
    vt_ref[...] = proj_t(wvt_ref, bvt_ref).astype(BF16)


def _in_projection(x, modtab, g1, w, b, wgt, bgt, wkt, bkt, wvt, bvt, cos, sin, qg, kg, avg):
    rows = x.shape[0]
    tm = ROW_TILE
    nb = tm // ROW_BLOCK
    kv = 2 * KV_HEADS * HEAD_DIM
    return pl.pallas_call(
        _inproj_kernel,
        out_shape=(jax.ShapeDtypeStruct((rows, N_PROJ), BF16),
                   jax.ShapeDtypeStruct((2 * N_GATE, rows), F32),
                   jax.ShapeDtypeStruct((WIDTH, rows), BF16),
                   jax.ShapeDtypeStruct((kv, rows), BF16)),
        grid=(rows // tm,),
        in_specs=[pl.BlockSpec((tm, D_MODEL), lambda i: (i, 0)),
                  pl.BlockSpec((nb, 6, D_MODEL), lambda i: (i, 0, 0)),
                  _resident((1, D_MODEL)),
                  _resident((D_MODEL, N_PROJ)),
                  _resident((1, N_PROJ)),
                  _resident((2 * N_GATE, D_MODEL)),
                  _resident((2 * N_GATE, 1)),
                  _resident((WIDTH, D_MODEL)),
                  _resident((WIDTH, 1)),
                  _resident((kv, D_MODEL)),
                  _resident((kv, 1)),
                  pl.BlockSpec((tm, LANES), lambda i: (i, 0)),
                  pl.BlockSpec((tm, LANES), lambda i: (i, 0)),
                  _resident((1, LANES)),
                  _resident((1, LANES)),
                  _resident((LANES, LANES))],
        out_specs=(pl.BlockSpec((tm, N_PROJ), lambda i: (i, 0)),
                   pl.BlockSpec((2 * N_GATE, tm), lambda i: (0, i)),
                   pl.BlockSpec((WIDTH, tm), lambda i: (0, i)),
                   pl.BlockSpec((kv, tm), lambda i: (0, i))),
        scratch_shapes=[pltpu.VMEM((tm, D_MODEL), BF16)],
        compiler_params=_params(("parallel",)),
        name="in_projection",
    )(x, modtab, g1, w, b, wgt, bgt, wkt, bkt, wvt, bvt, cos, sin, qg, kg, avg)


def _scan_lanes(x, op, fill, reverse):
    n = x.shape[1]
    lane = lax.broadcasted_iota(jnp.int32, x.shape, 1)
    sh = 1
    while sh < n:
        if reverse:
            moved = jnp.where(lane < n - sh, pltpu.roll(x, n - sh, axis=1), fill)
        else:
            moved = jnp.where(lane >= sh, pltpu.roll(x, sh, axis=1), fill)
        x = op(x, moved)
        sh *= 2
    return x


def _mlstm_kernel(qvf_ref, ktf_ref, gtf_ref, qvb_ref, ktb_ref, gtb_ref, hf_ref, hb_ref, cn_ref, m_ref):
    @pl.when(pl.program_id(1) == 0)
    def _():
        cn_ref[...] = jnp.zeros_like(cn_ref)
        m_ref[...] = jnp.zeros_like(m_ref)

    L = CHUNK
    row = lax.broadcasted_iota(jnp.int32, (L, L), 0)
    col = lax.broadcasted_iota(jnp.int32, (L, L), 1)
    scale = ML_DIM ** -0.5
    ones_v = jnp.ones((L, ML_DIM), BF16)
    pending = []
    for d, (qv_ref, kt_ref, gt_ref, h_ref) in enumerate(((qvf_ref, ktf_ref, gtf_ref, hf_ref),
                                                          (qvb_ref, ktb_ref, gtb_ref, hb_ref))):
        seen = (col <= row) if d == 0 else (col >= row)
        g0 = d * ML_HEADS
        ic = gt_ref[g0:g0 + ML_HEADS, :]
        lf = gt_ref[N_GATE + g0:N_GATE + g0 + ML_HEADS, :]
        r = ic - _scan_lanes(lf, jnp.add, 0.0, d == 1)
        r_max = jnp.max(r, axis=1, keepdims=True)
        b_end = jnp.sum(lf, axis=1, keepdims=True)
        for hh in range(ML_HEADS):
            idx = g0 + hh
            lanes = slice(hh * ML_DIM, (hh + 1) * ML_DIM)
            r_row = r[hh:hh + 1, :]
            m_old = m_ref[idx]
            m_end = jnp.maximum(m_old, r_max[hh:hh + 1, :])
            q = qv_ref[:, lanes]
            vo = jnp.concatenate([qv_ref[:, WIDTH + hh * ML_DIM:WIDTH + (hh + 1) * ML_DIM], ones_v], axis=1)
            kt = kt_ref[lanes, :]
            qsb = (q.astype(F32) * scale).astype(BF16)
            cn = cn_ref[idx]
            s_raw = jnp.dot(qsb, kt, preferred_element_type=F32)
            q_cn = jnp.dot(qsb, cn.astype(BF16), preferred_element_type=F32)
            kwt = (kt.astype(F32) * jnp.exp(r_row - m_end)).astype(BF16)
            cn_ref[idx] = jnp.exp(m_old - m_end) * cn + jnp.dot(kwt, vo, preferred_element_type=F32)
            m_ref[idx] = b_end[hh:hh + 1, :] + m_end
            pending.append((h_ref, lanes, seen, r_row, lf[hh:hh + 1, :], m_old, s_raw, q_cn, vo))

    for h_ref, lanes, seen, r_row, lf_row, m_old, s_raw, q_cn, vo in pending:
        b_col = jnp.sum(jnp.where(seen, lf_row, 0.0), axis=1, keepdims=True)
        m_col = jnp.maximum(m_old, jnp.max(jnp.where(seen, r_row, -jnp.inf), axis=1, keepdims=True))
        w = jnp.exp(jnp.where(seen, r_row - m_col, -jnp.inf))
        a = jnp.exp(m_old - m_col)
        s_vo = jnp.dot((s_raw * w).astype(BF16), vo, preferred_element_type=F32)
        num = a * q_cn[:, :ML_DIM] + s_vo[:, :ML_DIM]
        den = a * q_cn[:, ML_DIM:] + s_vo[:, ML_DIM:]
        hc = num / jnp.maximum(jnp.abs(den), jnp.exp(-(b_col + m_col)))
        h_ref[:, lanes] = hc.astype(h_ref.dtype)


def _mlstm(p, kt, gt, batch, n_lat, n_ctx):
    rows = p.shape[0]
    cl, cc = n_lat // CHUNK, n_ctx // CHUNK
    lat_base, ctx_base = 0, batch * cl

    def fwd_chunk(b, i):
        return jnp.where(i < cc, ctx_base + b * cc + i, lat_base + b * cl + (i - cc))

    def bwd_chunk(b, i):
        return jnp.where(i < cc, ctx_base + b * cc + (cc - 1 - i), lat_base + b * cl + (cl - 1 - (i - cc)))

    def specs(chunk):
        return [pl.BlockSpec((CHUNK, 2 * WIDTH), lambda b, i: (chunk(b, i), C_MLQ // (2 * WIDTH))),
                pl.BlockSpec((WIDTH, CHUNK), lambda b, i: (0, chunk(b, i))),
                pl.BlockSpec((2 * N_GATE, CHUNK), lambda b, i: (0, chunk(b, i)))]

    n_state = 2 * ML_HEADS
    return pl.pallas_call(
        _mlstm_kernel,
        out_shape=(jax.ShapeDtypeStruct((rows, WIDTH), BF16), jax.ShapeDtypeStruct((rows, WIDTH), BF16)),
        grid=(batch, cl + cc),
        in_specs=specs(fwd_chunk) + specs(bwd_chunk),
        out_specs=(pl.BlockSpec((CHUNK, WIDTH), lambda b, i: (fwd_chunk(b, i), 0)),
                   pl.BlockSpec((CHUNK, WIDTH), lambda b, i: (bwd_chunk(b, i), 0))),
        scratch_shapes=[pltpu.VMEM((n_state, ML_DIM, 2 * ML_DIM), F32),
                        pltpu.VMEM((n_state, 1, 1), F32)],
        compiler_params=_params(("parallel", "arbitrary")),
        name="mlstm_scan",
    )(p, kt, gt, p, kt, gt)


def _half_mask(e):
    lane = lax.broadcasted_iota(jnp.int32, (1, LANES), 1)
    return (lane < HEAD_DIM) if e == 0 else (lane >= HEAD_DIM)


def _win_kernel(sink_ref, q_ref, kc_ref, vtc_ref, k0_ref, k1_ref, k2_ref, k3_ref,
                vt0_ref, vt1_ref, vt2_ref, vt3_ref, o_ref, *, n_lat_tiles, n_lat_blocks):
    t = pl.program_id(1)
    tq = q_ref.shape[0]
    n_ctx = kc_ref.shape[0]
    is_lat = t < n_lat_tiles
    kk = lax.broadcasted_iota(jnp.int32, (CHUNK, tq), 0)
    qq = lax.broadcasted_iota(jnp.int32, (CHUNK, tq), 1)
    band = (kk >= qq, qq <= kk + CHUNK, kk <= qq, kk + CHUNK <= qq)
    first = 2 * t - 1
    present = [jnp.logical_and(is_lat, jnp.logical_and(first + i >= 0, first + i < n_lat_blocks))
               for i in range(4)]
    visible = [jnp.logical_and(band[i], present[i]) for i in range(4)]
    k_refs = (k0_ref, k1_ref, k2_ref, k3_ref)
    vt_refs = (vt0_ref, vt1_ref, vt2_ref, vt3_ref)
    ones = jnp.ones((16, n_ctx + 4 * CHUNK), BF16)

    scores, vtas = [], []
    for g in range(KV_HEADS):
        lanes = slice(g * LANES, (g + 1) * LANES)
        k_all = jnp.concatenate([kc_ref[:, lanes]] + [r[:, lanes] for r in k_refs], axis=0)
        vrows = slice(g * HEAD_DIM, (g + 1) * HEAD_DIM)
        vtas.append(jnp.concatenate(
            [jnp.concatenate([vtc_ref[vrows, :]] + [r[vrows, :] for r in vt_refs], axis=1), ones], axis=0))
        for tt in range(2 * g, 2 * g + 2):
            qf = q_ref[:, tt * LANES:(tt + 1) * LANES].astype(F32)
            for e in range(2):
                qt = jnp.where(_half_mask(e), qf, 0.0).T.astype(BF16)
                scores.append(jnp.dot(k_all, qt, preferred_element_type=F32))
    outs = []
    for h, s in enumerate(scores):
        sink = sink_ref[h] * LOG2E
        parts = [s[:n_ctx, :]]
        for i in range(4):
            blk = s[n_ctx + i * CHUNK:n_ctx + (i + 1) * CHUNK, :]
            parts.append(jnp.where(visible[i], blk, -jnp.inf))
        s = jnp.concatenate(parts, axis=0)
        m = jnp.maximum(jnp.max(s, axis=0, keepdims=True), sink)
        pt = jnp.exp2(s - m).astype(BF16)
        ol = jnp.dot(vtas[h // (Q_HEADS // KV_HEADS)], pt, preferred_element_type=F32)
        den = ol[HEAD_DIM:HEAD_DIM + 1, :] + jnp.exp2(sink - m)
        outs.append(ol[:HEAD_DIM, :] / den)
    for tt in range(Q_HEADS // 2):
        o_ref[:, tt * LANES:(tt + 1) * LANES] = (
            jnp.concatenate(outs[2 * tt:2 * tt + 2], axis=0).T.astype(o_ref.dtype))


def _window_attention(p, vt, sink, batch, n_lat, n_ctx, ctx_queries):
    tq = 256
    nlt, nct = n_lat // tq, n_ctx // tq
    nlb = n_lat // CHUNK
    nq = nlt + (nct if ctx_queries else 0)
    out_rows = batch * (n_lat + (n_ctx if ctx_queries else 0))
    ctx_blk = batch * n_lat // n_ctx
    k_col = C_WK // 256

    def q_block(b, t):
        return jnp.where(t < nlt, b * nlt + t, batch * nlt + b * nct + (t - nlt))

    def near(i):
        return lambda b, t: b * nlb + jnp.clip(2 * t - 1 + i, 0, nlb - 1)

    k_specs = [pl.BlockSpec((CHUNK, 256), (lambda f: lambda b, t: (f(b, t), k_col))(near(i))) for i in range(4)]
    vt_specs = [pl.BlockSpec((LANES, CHUNK), (lambda f: lambda b, t: (VT_WINDOW, f(b, t)))(near(i)))
                for i in range(4)]
    return pl.pallas_call(
        functools.partial(_win_kernel, n_lat_tiles=nlt, n_lat_blocks=nlb),
        out_shape=jax.ShapeDtypeStruct((out_rows, WIDTH), BF16),
        grid=(batch, nq),
        in_specs=[pl.BlockSpec(memory_space=pltpu.SMEM),
                  pl.BlockSpec((tq, WIDTH), lambda b, t: (q_block(b, t), C_WQ // WIDTH)),
                  pl.BlockSpec((n_ctx, 256), lambda b, t: (ctx_blk + b, k_col)),
                  pl.BlockSpec((LANES, n_ctx), lambda b, t: (VT_WINDOW, ctx_blk + b))]
                 + k_specs + vt_specs,
        out_specs=pl.BlockSpec((tq, WIDTH), lambda b, t: (q_block(b, t), 0)),
        compiler_params=_params(("parallel", "parallel")),
        name="window_attention",
    )(sink, p, p, vt, p, p, p, p, vt, vt, vt, vt)


def _glb_kernel(q_ref, kc_ref, vtc_ref, kl_ref, vtl_ref, o_ref, qt_ref, knorm_ref, *scratch, n_lat_tiles, tk):
    heads = Q_HEADS // KV_HEADS
    acc_refs = scratch[0:heads]
    sc_refs = scratch[heads:2 * heads]
    ring = [scratch[(2 + i) * heads:(3 + i) * heads] for i in range(N_SCORE_BUFS)]
    base = (2 + N_SCORE_BUFS) * heads
    pc_refs = scratch[base:base + heads]
    p_ring = scratch[base + heads:base + heads + 2]
    qi = pl.program_id(2)
    tq = q_ref.shape[0]
    n_chunks = kl_ref.shape[0] // tk

    def k_chunk(c):
        off = pl.multiple_of(jnp.minimum(c, n_chunks - 1) * tk, tk)
        return kl_ref[pl.ds(off, tk), :]

    def vt_chunk(c):
        return vtl_ref[:, pl.ds(pl.multiple_of(c * tk, tk), tk)]

    def with_ones(vt):
        ones = jnp.ones((acc_refs[0].shape[0] - HEAD_DIM, vt.shape[1]), BF16)
        return jnp.concatenate([vt, ones], axis=0)

    @pl.when(qi == 0)
    def _():
        def sq_norm(k):
            kf = k.astype(F32)
            return jnp.max(jnp.sum(kf * kf, axis=1, keepdims=True), axis=0, keepdims=True)

        def body(c, best):
            return jnp.maximum(best, sq_norm(k_chunk(c)))
        best = lax.fori_loop(0, n_chunks, body, sq_norm(kc_ref[...]))
        knorm_ref[...] = jnp.sqrt(0.5 * best)

    for t in range(heads // 2):
        qf = q_ref[:, t * LANES:(t + 1) * LANES].astype(F32)
        for e in range(2):
            h = 2 * t + e
            qt_ref[:, h * tq:(h + 1) * tq] = jnp.where(_half_mask(e), qf, 0.0).T.astype(BF16)
    for h in range(heads):
        acc_refs[h][...] = jnp.zeros_like(acc_refs[h])

    def q_t(h):
        return qt_ref[:, h * tq:(h + 1) * tq]

    bound = [jnp.sqrt(jnp.sum(jnp.square(q_t(h).astype(F32)), axis=0, keepdims=True)) * knorm_ref[...]
             for h in range(heads)]
    bounded = jnp.max(functools.reduce(jnp.maximum, bound)) <= SAFE_LOG2_BOUND

    span = SPAN_CHUNKS * tk
    n_spans = n_chunks // SPAN_CHUNKS
    n_steps = n_spans * heads

    def span_of(step):
        return jnp.minimum(step // heads, n_spans - 1)

    def produce_ctx():
        for h in range(heads):
            s = jnp.dot(kc_ref[...], q_t(h), preferred_element_type=F32)
            pc_refs[h][...] = jnp.exp2(s - bound[h]).astype(BF16)

    def consume_ctx():
        vta = with_ones(vtc_ref[...])
        for h in range(heads):
            acc_refs[h][...] += jnp.dot(vta, pc_refs[h][...], preferred_element_type=F32)

    def produce(p_ref, sp, h):
        k = kl_ref[pl.ds(pl.multiple_of(sp * span, span), span), :]
        s = jnp.dot(k, q_t(h), preferred_element_type=F32)
        p_ref[...] = jnp.exp2(s - bound[h]).astype(BF16)

    def consume(p_ref, sp, h):
        vta = with_ones(vtl_ref[:, pl.ds(pl.multiple_of(sp * span, span), span)])
        acc_refs[h][...] += jnp.dot(vta, p_ref[...], preferred_element_type=F32)

    @pl.when(bounded)
    def _():
        produce_ctx()

        @pl.when(qi >= n_lat_tiles)
        def _():
            consume_ctx()

        @pl.when(qi < n_lat_tiles)
        def _():
            produce(p_ring[0], 0, 0)
            consume_ctx()

            unroll = heads * (2 if n_spans % 2 == 0 else 1)

            def body(j, carry):
                for i in range(unroll):
                    step = unroll * j + i
                    produce(p_ring[(i + 1) % 2], span_of(step + 1), (i + 1) % heads)
                    consume(p_ring[i % 2], span_of(step), i % heads)
                return carry
            lax.fori_loop(0, n_steps // unroll, body, 0)

    neg_inf = (jnp.full((1, tq), -jnp.inf, F32),) * heads

    def fill(s_refs, k, m_run):
        out = []
        for h in range(heads):
            s = jnp.dot(k, q_t(h), preferred_element_type=F32)
            s_refs[h][...] = s
            out.append(jnp.maximum(m_run[h], jnp.max(s, axis=0, keepdims=True)))
        return tuple(out)

    def drain(s_refs, vt, m_run, m_acc):
        vta = with_ones(vt)
        for h in range(heads):
            alpha = jnp.exp2(m_acc[h] - m_run[h])
            pt = jnp.exp2(s_refs[h][...] - m_run[h]).astype(BF16)
            acc_refs[h][...] = alpha * acc_refs[h][...] + jnp.dot(vta, pt, preferred_element_type=F32)
        return m_run

    @pl.when(jnp.logical_not(bounded))
    def _():
        m_ctx = fill(sc_refs, kc_ref[...], neg_inf)

        @pl.when(qi >= n_lat_tiles)
        def _():
            drain(sc_refs, vtc_ref[...], m_ctx, neg_inf)

        @pl.when(qi < n_lat_tiles)
        def _():
            m_one = fill(ring[0], k_chunk(0), m_ctx)
            m_two = fill(ring[1], k_chunk(1), m_one)
            m_acc = drain(sc_refs, vtc_ref[...], m_one, neg_inf)

            def body(j, carry):
                m_run, m_acc = carry
                for i in range(N_SCORE_BUFS):
                    c = N_SCORE_BUFS * j + i
                    m_next = fill(ring[(i + 2) % N_SCORE_BUFS], k_chunk(c + 2), m_run)
                    m_acc = drain(ring[i], vt_chunk(c), m_run, m_acc)
                    m_run = m_next
                return m_run, m_acc
            lax.fori_loop(0, n_chunks // N_SCORE_BUFS, body, (m_two, m_acc))

    for t in range(heads // 2):
        pair = []
        for e in range(2):
            a = acc_refs[2 * t + e][...]
            pair.append(a[:HEAD_DIM, :] / a[HEAD_DIM:HEAD_DIM + 1, :])
        o_ref[:, t * LANES:(t + 1) * LANES] = jnp.concatenate(pair, axis=0).T.astype(o_ref.dtype)


def _global_attention(p, vt, batch, n_lat, n_ctx, ctx_queries):
    tq = 256
    tk = min(512, n_lat // N_SCORE_BUFS)
    assert n_lat % (tk * N_SCORE_BUFS) == 0 and tk % LANES == 0 and N_SCORE_BUFS % SPAN_CHUNKS == 0
    nl, nc = n_lat // tq, n_ctx // tq
    nq = nl + (nc if ctx_queries else 0)
    out_rows = batch * (n_lat + (n_ctx if ctx_queries else 0))
    ctx_base = batch * nl
    heads = Q_HEADS // KV_HEADS
    acc_rows = HEAD_DIM + 16

    def q_block(b, qi):
        return jnp.where(qi < nl, b * nl + qi, ctx_base + b * nc + (qi - nl))

    ctx_blk = batch * n_lat // n_ctx
    return pl.pallas_call(
        functools.partial(_glb_kernel, n_lat_tiles=nl, tk=tk),
        out_shape=jax.ShapeDtypeStruct((out_rows, WIDTH), BF16),
        grid=(batch, KV_HEADS, nq),
        in_specs=[pl.BlockSpec((tq, 2 * LANES), lambda b, g, qi: (q_block(b, qi), C_GQ // 256 + g)),
                  pl.BlockSpec((n_ctx, LANES), lambda b, g, qi: (ctx_blk + b, C_GK // LANES + g)),
                  pl.BlockSpec((HEAD_DIM, n_ctx), lambda b, g, qi: (KV_HEADS * VT_GLOBAL + g, ctx_blk + b)),
                  pl.BlockSpec((n_lat, LANES), lambda b, g, qi: (b, C_GK // LANES + g)),
                  pl.BlockSpec((HEAD_DIM, n_lat), lambda b, g, qi: (KV_HEADS * VT_GLOBAL + g, b))],
        out_specs=pl.BlockSpec((tq, 2 * LANES), lambda b, g, qi: (q_block(b, qi), g)),
        scratch_shapes=([pltpu.VMEM((LANES, heads * tq), BF16), pltpu.VMEM((1, 1), F32)]
                        + [pltpu.VMEM((acc_rows, tq), F32)] * heads
                        + [pltpu.VMEM((n_ctx, tq), F32)] * heads
                        + [pltpu.VMEM((tk, tq), F32)] * (N_SCORE_BUFS * heads)
                        + [pltpu.VMEM((n_ctx, tq), BF16)] * heads
                        + [pltpu.VMEM((SPAN_CHUNKS * tk, tq), BF16)] * 2),
        compiler_params=_params(("parallel", "parallel", "arbitrary")),
        name="global_attention",
    )(p, p, vt, p, vt)


def _merge_kernel(x_ref, mod_ref, hf_ref, hb_ref, og_ref, yb_ref, yc_ref, ga_ref, gb_ref, gc_ref,
                  mlg_ref, wbr_ref, wo_ref, o_ref):
    tm = x_ref.shape[0]
    proj_b = jnp.dot(yb_ref[...], wbr_ref[1], preferred_element_type=F32)
    proj_c = jnp.dot(yc_ref[...], wbr_ref[2], preferred_element_type=F32)
    hs = hf_ref[...].astype(F32) + hb_ref[...].astype(F32)
    parts = []
    for t in range(ML_HEADS):
        ht = hs[:, t * ML_DIM:(t + 1) * ML_DIM]
        ms = jnp.mean(ht * ht, axis=-1, keepdims=True)
        parts.append(ht * lax.rsqrt(ms + EPS))
    ya = (jnp.concatenate(parts, axis=1) * mlg_ref[...] * _sigmoid(og_ref[...].astype(F32))).astype(BF16)
    proj_a = jnp.dot(ya, wbr_ref[0], preferred_element_type=F32)
    merged = (_sigmoid(gb_ref[...].astype(F32)) * proj_b + _sigmoid(gc_ref[...].astype(F32)) * proj_c
              + _sigmoid(ga_ref[...].astype(F32)) * proj_a)
    out = jnp.dot(merged.astype(BF16), wo_ref[...], preferred_element_type=F32)
    for sb in range(tm // ROW_BLOCK):
        rows = slice(sb * ROW_BLOCK, (sb + 1) * ROW_BLOCK)
        o_ref[rows, :] = x_ref[rows, :] + mod_ref[sb, 2:3, :] * out[rows, :]


def _merge(x, modtab, p, hf, hb, yb, yc, mlg, wbr, wo, n_rows):
    tm = ROW_TILE
    nb = tm // ROW_BLOCK
    row = lambda i: (i, 0)
    gate = lambda k: pl.BlockSpec((tm, D_MODEL), lambda i: (i, C_GATE // D_MODEL + k))
    return pl.pallas_call(
        _merge_kernel,
        out_shape=jax.ShapeDtypeStruct(x.shape, F32),
        grid=(n_rows // tm,),
        in_specs=[pl.BlockSpec((tm, D_MODEL), row),
                  pl.BlockSpec((nb, 6, D_MODEL), lambda i: (i, 0, 0)),
                  pl.BlockSpec((tm, WIDTH), row),
                  pl.BlockSpec((tm, WIDTH), row),
                  pl.BlockSpec((tm, WIDTH), lambda i: (i, C_MLO // WIDTH)),
                  pl.BlockSpec((tm, WIDTH), row),
                  pl.BlockSpec((tm, WIDTH), row),
                  gate(0), gate(1), gate(2),
                  _resident((1, WIDTH)),
                  _resident((3, WIDTH, D_MODEL)),
                  _resident((D_MODEL, D_MODEL))],
        out_specs=pl.BlockSpec((tm, D_MODEL), row),
        input_output_aliases={0: 0},
        compiler_params=_params(("parallel",)),
        name="merge",
    )(x, modtab, hf, hb, p, yb, yc, p, p, p, mlg, wbr, wo)


def _ffn_kernel(x_ref, mod_ref, g2_ref, w1_ref, w3_ref, w2_ref, gf_ref, o_ref, *, final):
    tm = x_ref.shape[0]
    d_ff = w1_ref.shape[1]
    split = pl.cdiv(d_ff // MXU_TILE, 2) * MXU_TILE
    for sb in range(tm // ROW_BLOCK):
        rows = slice(sb * ROW_BLOCK, (sb + 1) * ROW_BLOCK)
        xs = x_ref[rows, :]
        ms = jnp.mean(xs * xs, axis=-1, keepdims=True)
        y = xs * lax.rsqrt(ms + EPS) * g2_ref[...]
        h = (y * (1.0 + mod_ref[sb, 4:5, :]) + mod_ref[sb, 3:4, :]).astype(BF16)
        out = None
        for c0, c1 in ((0, split), (split, d_ff)):
            a = jnp.dot(h, w1_ref[:, c0:c1], preferred_element_type=F32)
            b = jnp.dot(h, w3_ref[:, c0:c1], preferred_element_type=F32)
            z = (a * _sigmoid(a) * b).astype(BF16)
            part = jnp.dot(z, w2_ref[c0:c1, :], preferred_element_type=F32)
            out = part if out is None else out + part
        xn = xs + mod_ref[sb, 5:6, :] * out
        if final:
            ms = jnp.mean(xn * xn, axis=-1, keepdims=True)
            xn = xn * lax.rsqrt(ms + EPS) * gf_ref[...]
        o_ref[rows, :] = xn


def _ffn(x, modtab, g2, w1, w3, w2, gfin, n_rows, final):
    tm = ROW_TILE
    nb = tm // ROW_BLOCK
    d_ff = w1.shape[1]
    row = lambda i: (i, 0)
    out_rows = n_rows if final else x.shape[0]
    return pl.pallas_call(
        functools.partial(_ffn_kernel, final=final),
        out_shape=jax.ShapeDtypeStruct((out_rows, D_MODEL), F32),
        grid=(n_rows // tm,),
        in_specs=[pl.BlockSpec((tm, D_MODEL), row),
                  pl.BlockSpec((nb, 6, D_MODEL), lambda i: (i, 0, 0)),
                  _resident((1, D_MODEL)),
                  _resident((D_MODEL, d_ff)),
                  _resident((D_MODEL, d_ff)),
                  _resident((d_ff, D_MODEL)),
                  _resident((1, D_MODEL))],
        out_specs=pl.BlockSpec((tm, D_MODEL), row),
        input_output_aliases={} if final else {0: 0},
        compiler_params=_params(("parallel",)),
        name="ffn",
    )(x, modtab, g2, w1, w3, w2, gfin)


def _dup_halves(w, base):
    h0 = w[..., base:base + HEAD_DIM]
    h1 = w[..., base + HEAD_DIM:base + 2 * HEAD_DIM]
    return [h0, h0, h1, h1]


def _arrange_in_proj(w):
    o_gate_ml = 4 * WIDTH
    o_wq = o_gate_ml + 2 * N_GATE
    o_wk, o_wv = o_wq + WIDTH, o_wq + WIDTH + 128
    o_gq = o_wv + 128
    o_gk, o_gv = o_gq + WIDTH, o_gq + WIDTH + 128
    o_gate = o_gv + 128
    main = jnp.concatenate(
        [w[..., o_gate:o_gate + 3 * D_MODEL], w[..., 0:WIDTH], w[..., 2 * WIDTH:4 * WIDTH],
         w[..., o_wq:o_wq + WIDTH]]
        + _dup_halves(w, o_wk) + [w[..., o_gq:o_gq + WIDTH]] + _dup_halves(w, o_gk), axis=-1)
    values = jnp.concatenate([w[..., o_gv:o_gv + 128], w[..., o_wv:o_wv + 128]], axis=-1)
    return main, w[..., o_gate_ml:o_gate_ml + 2 * N_GATE], w[..., WIDTH:2 * WIDTH], values


def _rope_tables(batch, n_lat, n_ctx):
    t = jnp.arange(n_lat)
    quarter = HEAD_DIM // 4
    inv = ROPE_THETA ** (-jnp.arange(0, 2 * quarter, 2, dtype=F32) / (2 * quarter))
    ang_r = (t // GRID_W).astype(F32)[:, None] * inv
    ang_c = (t % GRID_W).astype(F32)[:, None] * inv
    cos = jnp.concatenate([jnp.cos(ang_r)] * 2 + [jnp.cos(ang_c)] * 2, axis=1)
    sin = jnp.concatenate([-jnp.sin(ang_r), jnp.sin(ang_r), -jnp.sin(ang_c), jnp.sin(ang_c)], axis=1)
    cos = jnp.tile(cos, (batch, LANES // HEAD_DIM))
    sin = jnp.tile(sin, (batch, LANES // HEAD_DIM))
    pad = batch * n_ctx
    return (jnp.concatenate([cos, jnp.ones((pad, LANES), F32)], axis=0),
            jnp.concatenate([sin, jnp.zeros((pad, LANES), F32)], axis=0))


def kernel(x, c, ctx, c_ctx, w_mod, b_mod, norm1_g, w_in, b_in, ml_norm_g, win_sink, qn_g, kn_g,
           w_br, w_o, norm2_g, w_ff1, w_ff3, w_ff2, final_g):
    batch, n_lat, d = x.shape
    n_ctx = ctx.shape[1]
    depth = w_mod.shape[0]
    assert d == D_MODEL and n_lat % ROW_TILE == 0 and n_lat % GRID_W == 0
    assert (batch * n_ctx) % ROW_TILE == 0 and n_ctx % ROW_BLOCK == 0 and (batch * n_lat) % n_ctx == 0
    lat_rows, ctx_rows = batch * n_lat, batch * n_ctx

    xs = jnp.concatenate([x.reshape(lat_rows, d), ctx.reshape(ctx_rows, d)], axis=0)
    cvec = jnp.concatenate([c, c_ctx[None, :], jnp.zeros((8 - batch - 1, d), F32)], axis=0)
    block_class = np.concatenate([np.repeat(np.arange(batch), n_lat // ROW_BLOCK),
                                  np.full(ctx_rows // ROW_BLOCK, batch)])
    cos, sin = _rope_tables(batch, n_lat, n_ctx)
    avg = jnp.asarray(np.kron(np.eye(LANES // HEAD_DIM), np.full((HEAD_DIM, HEAD_DIM), 1.0 / HEAD_DIM)), BF16)

    out = None
    for l in range(depth):
        last = l == depth - 1
        mod = _mod_vectors(cvec, w_mod[l], b_mod[l]).reshape(8, 6, d)
        modtab = mod[block_class]

        w_main, w_gate, w_mlk, w_val = _arrange_in_proj(w_in[l])
        b_main, b_gate, b_mlk, b_val = _arrange_in_proj(b_in[l][None, :])
        p, gt, kt, vt = _in_projection(
            xs, modtab, norm1_g[l][None, :], w_main.astype(BF16), b_main,
            w_gate.T.astype(BF16), b_gate.T, w_mlk.T.astype(BF16), b_mlk.T, w_val.T.astype(BF16), b_val.T,
            cos, sin, jnp.tile(qn_g[l], 2)[None, :], jnp.tile(kn_g[l], 2)[None, :], avg)

        hf, hb = _mlstm(p, kt, gt, batch, n_lat, n_ctx)
        yb = _window_attention(p, vt, win_sink[l], batch, n_lat, n_ctx, ctx_queries=not last)
        yc = _global_attention(p, vt, batch, n_lat, n_ctx, ctx_queries=not last)

        n_rows = lat_rows if last else lat_rows + ctx_rows
        xs = _merge(xs, modtab, p, hf, hb, yb, yc, ml_norm_g[l].reshape(1, WIDTH),
                    w_br[l].astype(BF16), w_o[l].astype(BF16), n_rows)
        out = _ffn(xs, modtab, norm2_g[l][None, :], w_ff1[l].astype(BF16), w_ff3[l].astype(BF16),
                   w_ff2[l].astype(BF16), final_g[None, :], n_rows, final=last)
        xs = out
    return out.reshape(batch, n_lat, d)
```

```python
import functools
import math

import jax
import jax.numpy as jnp
import numpy as np
from jax import lax
from jax.experimental import pallas as pl
from jax.experimental.pallas import tpu as pltpu

F32 = jnp.float32
BF16 = jnp.bfloat16

D_MODEL = 1024
GRID_W = 64
CHUNK = 128
HEAD_DIM = 64
ROPE_THETA = 10000.0
EPS = 1e-6
ML_HEADS = 4
ML_DIM = 128
Q_HEADS = 8
KV_HEADS = 2
WIDTH = 512
N_GATE = 2 * ML_HEADS

LANES = 128
MXU_TILE = 256
ROW_BLOCK = 256
ROW_TILE = 512
VMEM_LIMIT = 56 * 1024 * 1024

C_GATE = 0
C_MLQ, C_MLV, C_MLO = 3072, 3584, 4096
C_WQ, C_WK = 4608, 5120
C_GQ, C_GK = 5376, 5888
N_PROJ = C_GK + 256
VT_GLOBAL, VT_WINDOW = 0, 1
LOG2E = 1.4426950408889634
N_SCORE_BUFS = 4
SPAN_CHUNKS = 8
SAFE_LOG2_BOUND = 50.0

NT_DIMS = (((1,), (1,)), ((), ()))


def _params(sem, vmem=VMEM_LIMIT):
    return pltpu.CompilerParams(dimension_semantics=sem, vmem_limit_bytes=vmem)


def _resident(shape):
    nd = len(shape)
    return pl.BlockSpec(shape, lambda *_: (0,) * nd, pipeline_mode=pl.Buffered(1))


def _sigmoid(x):
    return 1.0 / (1.0 + jnp.exp(-x))


def _log_sigmoid(x):
    return jnp.minimum(x, 0.0) - jnp.log(1.0 + jnp.exp(-jnp.abs(x)))


def _mod_kernel(c_ref, w_ref, b_ref, o_ref):
    c = c_ref[...]
    s = c * _sigmoid(c)
    o_ref[...] = jnp.dot(s, w_ref[...], preferred_element_type=F32) + b_ref[...]


def _mod_vectors(cvec, w_mod, b_mod):
    n_out = w_mod.shape[1]
    tn = 1536
    return pl.pallas_call(
        _mod_kernel,
        out_shape=jax.ShapeDtypeStruct((cvec.shape[0], n_out), F32),
        grid=(n_out // tn,),
        in_specs=[pl.BlockSpec(cvec.shape, lambda j: (0, 0)),
                  pl.BlockSpec((D_MODEL, tn), lambda j: (0, j)),
                  pl.BlockSpec((1, tn), lambda j: (0, j))],
        out_specs=pl.BlockSpec((cvec.shape[0], tn), lambda j: (0, j)),
        compiler_params=_params(("parallel",)),
        name="mod_vectors",
    )(cvec, w_mod, b_mod.reshape(1, n_out))


def _rope(acc, cos, sin, first_half):
    w = acc.shape[1]
    reps = w // LANES
    if reps > 1:
        cos = jnp.concatenate([cos] * reps, axis=1)
        sin = jnp.concatenate([sin] * reps, axis=1)
    ahead = pltpu.roll(acc, w - 16, axis=1)
    behind = pltpu.roll(acc, 16, axis=1)
    return acc * cos + jnp.where(first_half, ahead, behind) * sin


def _head_rms(acc, avg, gain):
    sq = (acc * acc).astype(BF16)
    outs = []
    for t in range(acc.shape[1] // LANES):
        sl = slice(t * LANES, (t + 1) * LANES)
        ms = jnp.dot(sq[:, sl], avg, preferred_element_type=F32)
        outs.append(acc[:, sl] * lax.rsqrt(ms + EPS) * gain)
    return jnp.concatenate(outs, axis=1)


def _inproj_kernel(x_ref, mod_ref, g1_ref, w_ref, b_ref, wgt_ref, bgt_ref, wkt_ref, bkt_ref,
                   wvt_ref, bvt_ref, cos_ref, sin_ref, qg_ref, kg_ref, avg_ref,
                   p_ref, gt_ref, kt_ref, vt_ref, h_ref):
    tm = x_ref.shape[0]
    for sb in range(tm // ROW_BLOCK):
        r0 = sb * ROW_BLOCK
        xs = x_ref[r0:r0 + ROW_BLOCK, :]
        ms = jnp.mean(xs * xs, axis=-1, keepdims=True)
        y = xs * lax.rsqrt(ms + EPS) * g1_ref[...]
        shift = mod_ref[sb, 0:1, :]
        scale = mod_ref[sb, 1:2, :]
        h_ref[r0:r0 + ROW_BLOCK, :] = (y * (1.0 + scale) + shift).astype(BF16)

    def first_half(width):
        return (lax.broadcasted_iota(jnp.int32, (1, width), 1) % 32) < 16

    avg = avg_ref[...]
    qg = qg_ref[...]
    kg = kg_ref[...]
    q_scale = HEAD_DIM ** -0.5

    h = h_ref[...]
    cos = cos_ref[...]
    sin = sin_ref[...]

    def proj(c0, width):
        return (jnp.dot(h, w_ref[:, c0:c0 + width], preferred_element_type=F32)
                + b_ref[:, c0:c0 + width])

    def store_plain(cols):
        for c0 in cols:
            p_ref[:, c0:c0 + WIDTH] = proj(c0, WIDTH).astype(BF16)

    plain = list(range(C_GATE, C_GATE + 3 * D_MODEL, WIDTH)) + [C_MLQ, C_MLV, C_MLO]
    raw_gq, raw_gk = proj(C_GQ, WIDTH), proj(C_GK, 256)
    raw_wq, raw_wk = proj(C_WQ, WIDTH), proj(C_WK, 256)
    store_plain(plain[:3])
    gq = _rope(_head_rms(raw_gq, avg, qg), cos, sin, first_half(WIDTH)) * (q_scale * LOG2E)
    p_ref[:, C_GQ:C_GQ + WIDTH] = gq.astype(BF16)
    gk = _rope(_head_rms(raw_gk, avg, kg), cos, sin, first_half(256))
    p_ref[:, C_GK:C_GK + 256] = gk.astype(BF16)
    p_ref[:, C_WQ:C_WQ + WIDTH] = (_rope(raw_wq, cos, sin, first_half(WIDTH)) * (q_scale * LOG2E)).astype(BF16)
    p_ref[:, C_WK:C_WK + 256] = _rope(raw_wk, cos, sin, first_half(256)).astype(BF16)
    store_plain(plain[3:])

    def proj_t(wt_ref, bt_ref):
        return lax.dot_general(wt_ref[...], h_ref[...], NT_DIMS, preferred_element_type=F32) + bt_ref[...]

    gta = proj_t(wgt_ref, bgt_ref)
    grow = lax.broadcasted_iota(jnp.int32, (2 * N_GATE, 1), 0)
    gt_ref[...] = jnp.where(grow >= N_GATE, _log_sigmoid(gta), gta)
    kt_ref[...] = proj_t(wkt_ref, bkt_ref).astype(BF16)
    vt_ref[...] = proj_t(wvt_ref, bvt_ref).astype(BF16)


def _in_projection(x, modtab, g1, w, b, wgt, bgt, wkt, bkt, wvt, bvt, cos, sin, qg, kg, avg):
    rows = x.shape[0]
    tm = ROW_TILE
    nb = tm // ROW_BLOCK
    kv = 2 * KV_HEADS * HEAD_DIM
    return pl.pallas_call(
        _inproj_kernel,
        out_shape=(jax.ShapeDtypeStruct((rows, N_PROJ), BF16),
                   jax.ShapeDtypeStruct((2 * N_GATE, rows), F32),
                   jax.ShapeDtypeStruct((WIDTH, rows), BF16),
                   jax.ShapeDtypeStruct((kv, rows), BF16)),
        grid=(rows // tm,),
        in_specs=[pl.BlockSpec((tm, D_MODEL), lambda i: (i, 0)),
                  pl.BlockSpec((nb, 6, D_MODEL), lambda i: (i, 0, 0)),
                  _resident((1, D_MODEL)),
                  _resident((D_MODEL, N_PROJ)),
                  _resident((1, N_PROJ)),
                  _resident((2 * N_GATE, D_MODEL)),
                  _resident((2 * N_GATE, 1)),
                  _resident((WIDTH, D_MODEL)),
                  _resident((WIDTH, 1)),
                  _resident((kv, D_MODEL)),
                  _resident((kv, 1)),
                  pl.BlockSpec((tm, LANES), lambda i: (i, 0)),
                  pl.BlockSpec((tm, LANES), lambda i: (i, 0)),
                  _resident((1, LANES)),
                  _resident((1, LANES)),
                  _resident((LANES, LANES))],
        out_specs=(pl.BlockSpec((tm, N_PROJ), lambda i: (i, 0)),
                   pl.BlockSpec((2 * N_GATE, tm), lambda i: (0, i)),
                   pl.BlockSpec((WIDTH, tm), lambda i: (0, i)),
                   pl.BlockSpec((kv, tm), lambda i: (0, i))),
        scratch_shapes=[pltpu.VMEM((tm, D_MODEL), BF16)],
        compiler_params=_params(("parallel",)),
        name="in_projection",
    )(x, modtab, g1, w, b, wgt, bgt, wkt, bkt, wvt, bvt, cos, sin, qg, kg, avg)


def _scan_lanes(x, op, fill, reverse):
    n = x.shape[1]
    lane = lax.broadcasted_iota(jnp.int32, x.shape, 1)
    sh = 1
    while sh < n:
        if reverse:
            moved = jnp.where(lane < n - sh, pltpu.roll(x, n - sh, axis=1), fill)
        else:
            moved = jnp.where(lane >= sh, pltpu.roll(x, sh, axis=1), fill)
        x = op(x, moved)
        sh *= 2
    return x


def _mlstm_kernel(qvf_ref, ktf_ref, gtf_ref, qvb_ref, ktb_ref, gtb_ref, hf_ref, hb_ref, cn_ref, m_ref):
    @pl.when(pl.program_id(1) == 0)
    def _():
        cn_ref[...] = jnp.zeros_like(cn_ref)
        m_ref[...] = jnp.zeros_like(m_ref)

    L = CHUNK
    row = lax.broadcasted_iota(jnp.int32, (L, L), 0)
    col = lax.broadcasted_iota(jnp.int32, (L, L), 1)
    scale = ML_DIM ** -0.5
    ones_v = jnp.ones((L, ML_DIM), BF16)
    pending = []
    for d, (qv_ref, kt_ref, gt_ref, h_ref) in enumerate(((qvf_ref, ktf_ref, gtf_ref, hf_ref),
                                                          (qvb_ref, ktb_ref, gtb_ref, hb_ref))):
        seen = (col <= row) if d == 0 else (col >= row)
        g0 = d * ML_HEADS
        ic = gt_ref[g0:g0 + ML_HEADS, :]
        lf = gt_ref[N_GATE + g0:N_GATE + g0 + ML_HEADS, :]
        r = ic - _scan_lanes(lf, jnp.add, 0.0, d == 1)
        r_max = jnp.max(r, axis=1, keepdims=True)
        b_end = jnp.sum(lf, axis=1, keepdims=True)
        for hh in range(ML_HEADS):
            idx = g0 + hh
            lanes = slice(hh * ML_DIM, (hh + 1) * ML_DIM)
            r_row = r[hh:hh + 1, :]
            m_old = m_ref[idx]
            m_end = jnp.maximum(m_old, r_max[hh:hh + 1, :])
            q = qv_ref[:, lanes]
            vo = jnp.concatenate([qv_ref[:, WIDTH + hh * ML_DIM:WIDTH + (hh + 1) * ML_DIM], ones_v], axis=1)
            kt = kt_ref[lanes, :]
            qsb = (q.astype(F32) * scale).astype(BF16)
            cn = cn_ref[idx]
            s_raw = jnp.dot(qsb, kt, preferred_element_type=F32)
            q_cn = jnp.dot(qsb, cn.astype(BF16), preferred_element_type=F32)
            kwt = (kt.astype(F32) * jnp.exp(r_row - m_end)).astype(BF16)
            cn_ref[idx] = jnp.exp(m_old - m_end) * cn + jnp.dot(kwt, vo, preferred_element_type=F32)
            m_ref[idx] = b_end[hh:hh + 1, :] + m_end
            pending.append((h_ref, lanes, seen, r_row, lf[hh:hh + 1, :], m_old, s_raw, q_cn, vo))

    for h_ref, lanes, seen, r_row, lf_row, m_old, s_raw, q_cn, vo in pending:
        b_col = jnp.sum(jnp.where(seen, lf_row, 0.0), axis=1, keepdims=True)
        m_col = jnp.maximum(m_old, jnp.max(jnp.where(seen, r_row, -jnp.inf), axis=1, keepdims=True))
        w = jnp.exp(jnp.where(seen, r_row - m_col, -jnp.inf))
        a = jnp.exp(m_old - m_col)
        s_vo = jnp.dot((s_raw * w).astype(BF16), vo, preferred_element_type=F32)
        num = a * q_cn[:, :ML_DIM] + s_vo[:, :ML_DIM]
        den = a * q_cn[:, ML_DIM:] + s_vo[:, ML_DIM:]
        hc = num / jnp.maximum(jnp.abs(den), jnp.exp(-(b_col + m_col)))
        h_ref[:, lanes] = hc.astype(h_ref.dtype)


def _mlstm(p, kt, gt, batch, n_lat, n_ctx):
    rows = p.shape[0]
    cl, cc = n_lat // CHUNK, n_ctx // CHUNK
    lat_base, ctx_base = 0, batch * cl

    def fwd_chunk(b, i):
        return jnp.where(i < cc, ctx_base + b * cc + i, lat_base + b * cl + (i - cc))

    def bwd_chunk(b, i):
        return jnp.where(i < cc, ctx_base + b * cc + (cc - 1 - i), lat_base + b * cl + (cl - 1 - (i - cc)))

    def specs(chunk):
        return [pl.BlockSpec((CHUNK, 2 * WIDTH), lambda b, i: (chunk(b, i), C_MLQ // (2 * WIDTH))),
                pl.BlockSpec((WIDTH, CHUNK), lambda b, i: (0, chunk(b, i))),
                pl.BlockSpec((2 * N_GATE, CHUNK), lambda b, i: (0, chunk(b, i)))]

    n_state = 2 * ML_HEADS
    return pl.pallas_call(
        _mlstm_kernel,
        out_shape=(jax.ShapeDtypeStruct((rows, WIDTH), BF16), jax.ShapeDtypeStruct((rows, WIDTH), BF16)),
        grid=(batch, cl + cc),
        in_specs=specs(fwd_chunk) + specs(bwd_chunk),
        out_specs=(pl.BlockSpec((CHUNK, WIDTH), lambda b, i: (fwd_chunk(b, i), 0)),
                   pl.BlockSpec((CHUNK, WIDTH), lambda b, i: (bwd_chunk(b, i), 0))),
        scratch_shapes=[pltpu.VMEM((n_state, ML_DIM, 2 * ML_DIM), F32),
                        pltpu.VMEM((n_state, 1, 1), F32)],
        compiler_params=_params(("parallel", "arbitrary")),
        name="mlstm_scan",
    )(p, kt, gt, p, kt, gt)


def _half_mask(e):
    lane = lax.broadcasted_iota(jnp.int32, (1, LANES), 1)
    return (lane < HEAD_DIM) if e == 0 else (lane >= HEAD_DIM)


def _win_kernel(sink_ref, q_ref, kc_ref, vtc_ref, k0_ref, k1_ref, k2_ref, k3_ref,
                vt0_ref, vt1_ref, vt2_ref, vt3_ref, o_ref, *, n_lat_tiles, n_lat_blocks):
    t = pl.program_id(1)
    tq = q_ref.shape[0]
    n_ctx = kc_ref.shape[0]
    is_lat = t < n_lat_tiles
    kk = lax.broadcasted_iota(jnp.int32, (CHUNK, tq), 0)
    qq = lax.broadcasted_iota(jnp.int32, (CHUNK, tq), 1)
    band = (kk >= qq, qq <= kk + CHUNK, kk <= qq, kk + CHUNK <= qq)
    first = 2 * t - 1
    present = [jnp.logical_and(is_lat, jnp.logical_and(first + i >= 0, first + i < n_lat_blocks))
               for i in range(4)]
    visible = [jnp.logical_and(band[i], present[i]) for i in range(4)]
    k_refs = (k0_ref, k1_ref, k2_ref, k3_ref)
    vt_refs = (vt0_ref, vt1_ref, vt2_ref, vt3_ref)
    ones = jnp.ones((16, n_ctx + 4 * CHUNK), BF16)

    scores, vtas = [], []
    for g in range(KV_HEADS):
        lanes = slice(g * LANES, (g + 1) * LANES)
        k_all = jnp.concatenate([kc_ref[:, lanes]] + [r[:, lanes] for r in k_refs], axis=0)
        vrows = slice(g * HEAD_DIM, (g + 1) * HEAD_DIM)
        vtas.append(jnp.concatenate(
            [jnp.concatenate([vtc_ref[vrows, :]] + [r[vrows, :] for r in vt_refs], axis=1), ones], axis=0))
        for tt in range(2 * g, 2 * g + 2):
            qf = q_ref[:, tt * LANES:(tt + 1) * LANES].astype(F32)
            for e in range(2):
                qt = jnp.where(_half_mask(e), qf, 0.0).T.astype(BF16)
                scores.append(jnp.dot(k_all, qt, preferred_element_type=F32))
    outs = []
    for h, s in enumerate(scores):
        sink = sink_ref[h] * LOG2E
        parts = [s[:n_ctx, :]]
        for i in range(4):
            blk = s[n_ctx + i * CHUNK:n_ctx + (i + 1) * CHUNK, :]
            parts.append(jnp.where(visible[i], blk, -jnp.inf))
        s = jnp.concatenate(parts, axis=0)
        m = jnp.maximum(jnp.max(s, axis=0, keepdims=True), sink)
        pt = jnp.exp2(s - m).astype(BF16)
        ol = jnp.dot(vtas[h // (Q_HEADS // KV_HEADS)], pt, preferred_element_type=F32)
        den = ol[HEAD_DIM:HEAD_DIM + 1, :] + jnp.exp2(sink - m)
        outs.append(ol[:HEAD_DIM, :] / den)
    for tt in range(Q_HEADS // 2):
        o_ref[:, tt * LANES:(tt + 1) * LANES] = (
            jnp.concatenate(outs[2 * tt:2 * tt + 2], axis=0).T.astype(o_ref.dtype))


def _window_attention(p, vt, sink, batch, n_lat, n_ctx, ctx_queries):
    tq = 256
    nlt, nct = n_lat // tq, n_ctx // tq
    nlb = n_lat // CHUNK
    nq = nlt + (nct if ctx_queries else 0)
    out_rows = batch * (n_lat + (n_ctx if ctx_queries else 0))
    ctx_blk = batch * n_lat // n_ctx
    k_col = C_WK // 256

    def q_block(b, t):
        return jnp.where(t < nlt, b * nlt + t, batch * nlt + b * nct + (t - nlt))

    def near(i):
        return lambda b, t: b * nlb + jnp.clip(2 * t - 1 + i, 0, nlb - 1)

    k_specs = [pl.BlockSpec((CHUNK, 256), (lambda f: lambda b, t: (f(b, t), k_col))(near(i))) for i in range(4)]
    vt_specs = [pl.BlockSpec((LANES, CHUNK), (lambda f: lambda b, t: (VT_WINDOW, f(b, t)))(near(i)))
                for i in range(4)]
    return pl.pallas_call(
        functools.partial(_win_kernel, n_lat_tiles=nlt, n_lat_blocks=nlb),
        out_shape=jax.ShapeDtypeStruct((out_rows, WIDTH), BF16),
        grid=(batch, nq),
        in_specs=[pl.BlockSpec(memory_space=pltpu.SMEM),
                  pl.BlockSpec((tq, WIDTH), lambda b, t: (q_block(b, t), C_WQ // WIDTH)),
                  pl.BlockSpec((n_ctx, 256), lambda b, t: (ctx_blk + b, k_col)),
                  pl.BlockSpec((LANES, n_ctx), lambda b, t: (VT_WINDOW, ctx_blk + b))]
                 + k_specs + vt_specs,
        out_specs=pl.BlockSpec((tq, WIDTH), lambda b, t: (q_block(b, t), 0)),
        compiler_params=_params(("parallel", "parallel")),
        name="window_attention",
    )(sink, p, p, vt, p, p, p, p, vt, vt, vt, vt)


def _glb_kernel(q_ref, kc_ref, vtc_ref, kl_ref, vtl_ref, o_ref, qt_ref, knorm_ref, *scratch, n_lat_tiles, tk, span_chunks):
    heads = Q_HEADS // KV_HEADS
    acc_refs = scratch[0:heads]
    sc_refs = scratch[heads:2 * heads]
    ring = [scratch[(2 + i) * heads:(3 + i) * heads] for i in range(N_SCORE_BUFS)]
    base = (2 + N_SCORE_BUFS) * heads
    pc_refs = scratch[base:base + heads]
    p_ring = scratch[base + heads:base + heads + 2]
    qi = pl.program_id(2)
    tq = q_ref.shape[0]
    n_chunks = kl_ref.shape[0] // tk

    def k_chunk(c):
        off = pl.multiple_of(jnp.minimum(c, n_chunks - 1) * tk, tk)
        return kl_ref[pl.ds(off, tk), :]

    def vt_chunk(c):
        return vtl_ref[:, pl.ds(pl.multiple_of(c * tk, tk), tk)]

    def with_ones(vt):
        ones = jnp.ones((acc_refs[0].shape[0] - HEAD_DIM, vt.shape[1]), BF16)
        return jnp.concatenate([vt, ones], axis=0)

    @pl.when(qi == 0)
    def _():
        def sq_norm(k):
            kf = k.astype(F32)
            return jnp.max(jnp.sum(kf * kf, axis=1, keepdims=True), axis=0, keepdims=True)

        def body(c, best):
            return jnp.maximum(best, sq_norm(k_chunk(c)))
        best = lax.fori_loop(0, n_chunks, body, sq_norm(kc_ref[...]))
        knorm_ref[...] = jnp.sqrt(0.5 * best)

    for t in range(heads // 2):
        qf = q_ref[:, t * LANES:(t + 1) * LANES].astype(F32)
        for e in range(2):
            h = 2 * t + e
            qt_ref[:, h * tq:(h + 1) * tq] = jnp.where(_half_mask(e), qf, 0.0).T.astype(BF16)
    for h in range(heads):
        acc_refs[h][...] = jnp.zeros_like(acc_refs[h])

    def q_t(h):
        return qt_ref[:, h * tq:(h + 1) * tq]

    bound = [jnp.sqrt(jnp.sum(jnp.square(q_t(h).astype(F32)), axis=0, keepdims=True)) * knorm_ref[...]
             for h in range(heads)]
    bounded = jnp.max(functools.reduce(jnp.maximum, bound)) <= SAFE_LOG2_BOUND

    span = span_chunks * tk
    n_spans = n_chunks // span_chunks
    n_steps = n_spans * heads

    def span_of(step):
        return jnp.minimum(step // heads, n_spans - 1)

    def produce_ctx():
        for h in range(heads):
            s = jnp.dot(kc_ref[...], q_t(h), preferred_element_type=F32)
            pc_refs[h][...] = jnp.exp2(s - bound[h]).astype(BF16)

    def consume_ctx():
        vta = with_ones(vtc_ref[...])
        for h in range(heads):
            acc_refs[h][...] += jnp.dot(vta, pc_refs[h][...], preferred_element_type=F32)

    def produce(p_ref, sp, h):
        k = kl_ref[pl.ds(pl.multiple_of(sp * span, span), span), :]
        s = jnp.dot(k, q_t(h), preferred_element_type=F32)
        p_ref[...] = jnp.exp2(s - bound[h]).astype(BF16)

    def consume(p_ref, sp, h):
        vta = with_ones(vtl_ref[:, pl.ds(pl.multiple_of(sp * span, span), span)])
        acc_refs[h][...] += jnp.dot(vta, p_ref[...], preferred_element_type=F32)

    @pl.when(bounded)
    def _():
        produce_ctx()

        @pl.when(qi >= n_lat_tiles)
        def _():
            consume_ctx()

        @pl.when(qi < n_lat_tiles)
        def _():
            produce(p_ring[0], 0, 0)
            consume_ctx()

            unroll = heads * (2 if n_spans % 2 == 0 else 1)

            def body(j, carry):
                for i in range(unroll):
                    step = unroll * j + i
                    produce(p_ring[(i + 1) % 2], span_of(step + 1), (i + 1) % heads)
                    consume(p_ring[i % 2], span_of(step), i % heads)
                return carry
            lax.fori_loop(0, n_steps // unroll, body, 0)

    neg_inf = (jnp.full((1, tq), -jnp.inf, F32),) * heads

    def fill(s_refs, k, m_run):
        out = []
        for h in range(heads):
            s = jnp.dot(k, q_t(h), preferred_element_type=F32)
            s_refs[h][...] = s
            out.append(jnp.maximum(m_run[h], jnp.max(s, axis=0, keepdims=True)))
        return tuple(out)

    def drain(s_refs, vt, m_run, m_acc):
        vta = with_ones(vt)
        for h in range(heads):
            alpha = jnp.exp2(m_acc[h] - m_run[h])
            pt = jnp.exp2(s_refs[h][...] - m_run[h]).astype(BF16)
            acc_refs[h][...] = alpha * acc_refs[h][...] + jnp.dot(vta, pt, preferred_element_type=F32)
        return m_run

    @pl.when(jnp.logical_not(bounded))
    def _():
        m_ctx = fill(sc_refs, kc_ref[...], neg_inf)

        @pl.when(qi >= n_lat_tiles)
        def _():
            drain(sc_refs, vtc_ref[...], m_ctx, neg_inf)

        @pl.when(qi < n_lat_tiles)
        def _():
            m_one = fill(ring[0], k_chunk(0), m_ctx)
            m_two = fill(ring[1], k_chunk(1), m_one)
            m_acc = drain(sc_refs, vtc_ref[...], m_one, neg_inf)

            def body(j, carry):
                m_run, m_acc = carry
                for i in range(N_SCORE_BUFS):
                    c = N_SCORE_BUFS * j + i
                    m_next = fill(ring[(i + 2) % N_SCORE_BUFS], k_chunk(c + 2), m_run)
                    m_acc = drain(ring[i], vt_chunk(c), m_run, m_acc)
                    m_run = m_next
                return m_run, m_acc
            lax.fori_loop(0, n_chunks // N_SCORE_BUFS, body, (m_two, m_acc))

    for t in range(heads // 2):
        pair = []
        for e in range(2):
            a = acc_refs[2 * t + e][...]
            pair.append(a[:HEAD_DIM, :] / a[HEAD_DIM:HEAD_DIM + 1, :])
        o_ref[:, t * LANES:(t + 1) * LANES] = jnp.concatenate(pair, axis=0).T.astype(o_ref.dtype)


def _global_attention(p, vt, batch, n_lat, n_ctx, ctx_queries):
    tq = 256
    tk = min(512, n_lat // N_SCORE_BUFS)
    assert n_lat % (tk * N_SCORE_BUFS) == 0 and tk % LANES == 0
    span_chunks = math.gcd(SPAN_CHUNKS, n_lat // tk)
    nl, nc = n_lat // tq, n_ctx // tq
    nq = nl + (nc if ctx_queries else 0)
    out_rows = batch * (n_lat + (n_ctx if ctx_queries else 0))
    ctx_base = batch * nl
    heads = Q_HEADS // KV_HEADS
    acc_rows = HEAD_DIM + 16

    def q_block(b, qi):
        return jnp.where(qi < nl, b * nl + qi, ctx_base + b * nc + (qi - nl))

    ctx_blk = batch * n_lat // n_ctx
    return pl.pallas_call(
        functools.partial(_glb_kernel, n_lat_tiles=nl, tk=tk, span_chunks=span_chunks),
        out_shape=jax.ShapeDtypeStruct((out_rows, WIDTH), BF16),
        grid=(batch, KV_HEADS, nq),
        in_specs=[pl.BlockSpec((tq, 2 * LANES), lambda b, g, qi: (q_block(b, qi), C_GQ // 256 + g)),
                  pl.BlockSpec((n_ctx, LANES), lambda b, g, qi: (ctx_blk + b, C_GK // LANES + g)),
                  pl.BlockSpec((HEAD_DIM, n_ctx), lambda b, g, qi: (KV_HEADS * VT_GLOBAL + g, ctx_blk + b)),
                  pl.BlockSpec((n_lat, LANES), lambda b, g, qi: (b, C_GK // LANES + g)),
                  pl.BlockSpec((HEAD_DIM, n_lat), lambda b, g, qi: (KV_HEADS * VT_GLOBAL + g, b))],
        out_specs=pl.BlockSpec((tq, 2 * LANES), lambda b, g, qi: (q_block(b, qi), g)),
        scratch_shapes=([pltpu.VMEM((LANES, heads * tq), BF16), pltpu.VMEM((1, 1), F32)]
                        + [pltpu.VMEM((acc_rows, tq), F32)] * heads
                        + [pltpu.VMEM((n_ctx, tq), F32)] * heads
                        + [pltpu.VMEM((tk, tq), F32)] * (N_SCORE_BUFS * heads)
                        + [pltpu.VMEM((n_ctx, tq), BF16)] * heads
                        + [pltpu.VMEM((span_chunks * tk, tq), BF16)] * 2),
        compiler_params=_params(("parallel", "parallel", "arbitrary")),
        name="global_attention",
    )(p, p, vt, p, vt)


def _merge_kernel(x_ref, mod_ref, hf_ref, hb_ref, og_ref, yb_ref, yc_ref, ga_ref, gb_ref, gc_ref,
                  mlg_ref, wbr_ref, wo_ref, o_ref):
    tm = x_ref.shape[0]
    proj_b = jnp.dot(yb_ref[...], wbr_ref[1], preferred_element_type=F32)
    proj_c = jnp.dot(yc_ref[...], wbr_ref[2], preferred_element_type=F32)
    hs = hf_ref[...].astype(F32) + hb_ref[...].astype(F32)
    parts = []
    for t in range(ML_HEADS):
        ht = hs[:, t * ML_DIM:(t + 1) * ML_DIM]
        ms = jnp.mean(ht * ht, axis=-1, keepdims=True)
        parts.append(ht * lax.rsqrt(ms + EPS))
    ya = (jnp.concatenate(parts, axis=1) * mlg_ref[...] * _sigmoid(og_ref[...].astype(F32))).astype(BF16)
    proj_a = jnp.dot(ya, wbr_ref[0], preferred_element_type=F32)
    merged = (_sigmoid(gb_ref[...].astype(F32)) * proj_b + _sigmoid(gc_ref[...].astype(F32)) * proj_c
              + _sigmoid(ga_ref[...].astype(F32)) * proj_a)
    out = jnp.dot(merged.astype(BF16), wo_ref[...], preferred_element_type=F32)
    for sb in range(tm // ROW_BLOCK):
        rows = slice(sb * ROW_BLOCK, (sb + 1) * ROW_BLOCK)
        o_ref[rows, :] = x_ref[rows, :] + mod_ref[sb, 2:3, :] * out[rows, :]


def _merge(x, modtab, p, hf, hb, yb, yc, mlg, wbr, wo, n_rows):
    tm = ROW_TILE
    nb = tm // ROW_BLOCK
    row = lambda i: (i, 0)
    gate = lambda k: pl.BlockSpec((tm, D_MODEL), lambda i: (i, C_GATE // D_MODEL + k))
    return pl.pallas_call(
        _merge_kernel,
        out_shape=jax.ShapeDtypeStruct(x.shape, F32),
        grid=(n_rows // tm,),
        in_specs=[pl.BlockSpec((tm, D_MODEL), row),
                  pl.BlockSpec((nb, 6, D_MODEL), lambda i: (i, 0, 0)),
                  pl.BlockSpec((tm, WIDTH), row),
                  pl.BlockSpec((tm, WIDTH), row),
                  pl.BlockSpec((tm, WIDTH), lambda i: (i, C_MLO // WIDTH)),
                  pl.BlockSpec((tm, WIDTH), row),
                  pl.BlockSpec((tm, WIDTH), row),
                  gate(0), gate(1), gate(2),
                  _resident((1, WIDTH)),
                  _resident((3, WIDTH, D_MODEL)),
                  _resident((D_MODEL, D_MODEL))],
        out_specs=pl.BlockSpec((tm, D_MODEL), row),
        input_output_aliases={0: 0},
        compiler_params=_params(("parallel",)),
        name="merge",
    )(x, modtab, hf, hb, p, yb, yc, p, p, p, mlg, wbr, wo)


def _ffn_kernel(x_ref, mod_ref, g2_ref, w1_ref, w3_ref, w2_ref, gf_ref, o_ref, *, final):
    tm = x_ref.shape[0]
    d_ff = w1_ref.shape[1]
    split = pl.cdiv(d_ff // MXU_TILE, 2) * MXU_TILE
    for sb in range(tm // ROW_BLOCK):
        rows = slice(sb * ROW_BLOCK, (sb + 1) * ROW_BLOCK)
        xs = x_ref[rows, :]
        ms = jnp.mean(xs * xs, axis=-1, keepdims=True)
        y = xs * lax.rsqrt(ms + EPS) * g2_ref[...]
        h = (y * (1.0 + mod_ref[sb, 4:5, :]) + mod_ref[sb, 3:4, :]).astype(BF16)
        out = None
        for c0, c1 in ((0, split), (split, d_ff)):
            a = jnp.dot(h, w1_ref[:, c0:c1], preferred_element_type=F32)
            b = jnp.dot(h, w3_ref[:, c0:c1], preferred_element_type=F32)
            z = (a * _sigmoid(a) * b).astype(BF16)
            part = jnp.dot(z, w2_ref[c0:c1, :], preferred_element_type=F32)
            out = part if out is None else out + part
        xn = xs + mod_ref[sb, 5:6, :] * out
        if final:
            ms = jnp.mean(xn * xn, axis=-1, keepdims=True)
            xn = xn * lax.rsqrt(ms + EPS) * gf_ref[...]
        o_ref[rows, :] = xn


def _ffn(x, modtab, g2, w1, w3, w2, gfin, n_rows, final):
    tm = ROW_TILE
    nb = tm // ROW_BLOCK
    d_ff = w1.shape[1]
    row = lambda i: (i, 0)
    out_rows = n_rows if final else x.shape[0]
    return pl.pallas_call(
        functools.partial(_ffn_kernel, final=final),
        out_shape=jax.ShapeDtypeStruct((out_rows, D_MODEL), F32),
        grid=(n_rows // tm,),
        in_specs=[pl.BlockSpec((tm, D_MODEL), row),
                  pl.BlockSpec((nb, 6, D_MODEL), lambda i: (i, 0, 0)),
                  _resident((1, D_MODEL)),
                  _resident((D_MODEL, d_ff)),
                  _resident((D_MODEL, d_ff)),
                  _resident((d_ff, D_MODEL)),
                  _resident((1, D_MODEL))],
        out_specs=pl.BlockSpec((tm, D_MODEL), row),
        input_output_aliases={} if final else {0: 0},
        compiler_params=_params(("parallel",)),
        name="ffn",
    )(x, modtab, g2, w1, w3, w2, gfin)


def _dup_halves(w, base):
    h0 = w[..., base:base + HEAD_DIM]
    h1 = w[..., base + HEAD_DIM:base + 2 * HEAD_DIM]
    return [h0, h0, h1, h1]


def _arrange_in_proj(w):
    o_gate_ml = 4 * WIDTH
    o_wq = o_gate_ml + 2 * N_GATE
    o_wk, o_wv = o_wq + WIDTH, o_wq + WIDTH + 128
    o_gq = o_wv + 128
    o_gk, o_gv = o_gq + WIDTH, o_gq + WIDTH + 128
    o_gate = o_gv + 128
    main = jnp.concatenate(
        [w[..., o_gate:o_gate + 3 * D_MODEL], w[..., 0:WIDTH], w[..., 2 * WIDTH:4 * WIDTH],
         w[..., o_wq:o_wq + WIDTH]]
        + _dup_halves(w, o_wk) + [w[..., o_gq:o_gq + WIDTH]] + _dup_halves(w, o_gk), axis=-1)
    values = jnp.concatenate([w[..., o_gv:o_gv + 128], w[..., o_wv:o_wv + 128]], axis=-1)
    return main, w[..., o_gate_ml:o_gate_ml + 2 * N_GATE], w[..., WIDTH:2 * WIDTH], values


def _rope_tables(batch, n_lat, n_ctx):
    t = jnp.arange(n_lat)
    quarter = HEAD_DIM // 4
    inv = ROPE_THETA ** (-jnp.arange(0, 2 * quarter, 2, dtype=F32) / (2 * quarter))
    ang_r = (t // GRID_W).astype(F32)[:, None] * inv
    ang_c = (t % GRID_W).astype(F32)[:, None] * inv
    cos = jnp.concatenate([jnp.cos(ang_r)] * 2 + [jnp.cos(ang_c)] * 2, axis=1)
    sin = jnp.concatenate([-jnp.sin(ang_r), jnp.sin(ang_r), -jnp.sin(ang_c), jnp.sin(ang_c)], axis=1)
    cos = jnp.tile(cos, (batch, LANES // HEAD_DIM))
    sin = jnp.tile(sin, (batch, LANES // HEAD_DIM))
    pad = batch * n_ctx
    return (jnp.concatenate([cos, jnp.ones((pad, LANES), F32)], axis=0),
            jnp.concatenate([sin, jnp.zeros((pad, LANES), F32)], axis=0))


def kernel(x, c, ctx, c_ctx, w_mod, b_mod, norm1_g, w_in, b_in, ml_norm_g, win_sink, qn_g, kn_g,
           w_br, w_o, norm2_g, w_ff1, w_ff3, w_ff2, final_g):
    batch, n_lat, d = x.shape
    n_ctx = ctx.shape[1]
    depth = w_mod.shape[0]
    assert d == D_MODEL and n_lat % ROW_TILE == 0 and n_lat % GRID_W == 0
    assert (batch * n_ctx) % ROW_TILE == 0 and n_ctx % ROW_BLOCK == 0 and (batch * n_lat) % n_ctx == 0
    lat_rows, ctx_rows = batch * n_lat, batch * n_ctx

    xs = jnp.concatenate([x.reshape(lat_rows, d), ctx.reshape(ctx_rows, d)], axis=0)
    cvec = jnp.concatenate([c, c_ctx[None, :], jnp.zeros((8 - batch - 1, d), F32)], axis=0)
    block_class = np.concatenate([np.repeat(np.arange(batch), n_lat // ROW_BLOCK),
                                  np.full(ctx_rows // ROW_BLOCK, batch)])
    cos, sin = _rope_tables(batch, n_lat, n_ctx)
    avg = jnp.asarray(np.kron(np.eye(LANES // HEAD_DIM), np.full((HEAD_DIM, HEAD_DIM), 1.0 / HEAD_DIM)), BF16)

    out = None
    for l in range(depth):
        last = l == depth - 1
        mod = _mod_vectors(cvec, w_mod[l], b_mod[l]).reshape(8, 6, d)
        modtab = mod[block_class]

        w_main, w_gate, w_mlk, w_val = _arrange_in_proj(w_in[l])
        b_main, b_gate, b_mlk, b_val = _arrange_in_proj(b_in[l][None, :])
        p, gt, kt, vt = _in_projection(
            xs, modtab, norm1_g[l][None, :], w_main.astype(BF16), b_main,
            w_gate.T.astype(BF16), b_gate.T, w_mlk.T.astype(BF16), b_mlk.T, w_val.T.astype(BF16), b_val.T,
            cos, sin, jnp.tile(qn_g[l], 2)[None, :], jnp.tile(kn_g[l], 2)[None, :], avg)

        hf, hb = _mlstm(p, kt, gt, batch, n_lat, n_ctx)
        yb = _window_attention(p, vt, win_sink[l], batch, n_lat, n_ctx, ctx_queries=not last)
        yc = _global_attention(p, vt, batch, n_lat, n_ctx, ctx_queries=not last)

        n_rows = lat_rows if last else lat_rows + ctx_rows
        xs = _merge(xs, modtab, p, hf, hb, yb, yc, ml_norm_g[l].reshape(1, WIDTH),
                    w_br[l].astype(BF16), w_o[l].astype(BF16), n_rows)
        out = _ffn(xs, modtab, norm2_g[l][None, :], w_ff1[l].astype(BF16), w_ff3[l].astype(BF16),
                   w_ff2[l].astype(BF16), final_g[None, :], n_rows, final=last)
        xs = out
    return out.reshape(batch, n_lat, d)
```

```python
import functools
import math

import jax
import jax.numpy as jnp
import numpy as np
from jax import lax
from jax.experimental import pallas as pl
from jax.experimental.pallas import tpu as pltpu

F32 = jnp.float32
BF16 = jnp.bfloat16

D_MODEL = 1024
GRID_W = 64
CHUNK = 128
HEAD_DIM = 64
ROPE_THETA = 10000.0
EPS = 1e-6
ML_HEADS = 4
ML_DIM = 128
Q_HEADS = 8
KV_HEADS = 2
WIDTH = 512
N_GATE = 2 * ML_HEADS

LANES = 128
MXU_TILE = 256
ROW_BLOCK = 256
ROW_TILE = 512
VMEM_LIMIT = 56 * 1024 * 1024

C_GATE = 0
C_MLQ, C_MLV, C_MLO = 3072, 3584, 4096
C_WQ, C_WK = 4608, 5120
C_GQ, C_GK = 5376, 5888
N_PROJ = C_GK + 256
VT_GLOBAL, VT_WINDOW = 0, 1
LOG2E = 1.4426950408889634
WIN_LEAD = 3
N_SCORE_BUFS = 4
SPAN_CHUNKS = 4
SAFE_LOG2_BOUND = 50.0

NT_DIMS = (((1,), (1,)), ((), ()))


def _params(sem, vmem=VMEM_LIMIT):
    return pltpu.CompilerParams(dimension_semantics=sem, vmem_limit_bytes=vmem)


def _resident(shape):
    nd = len(shape)
    return pl.BlockSpec(shape, lambda *_: (0,) * nd, pipeline_mode=pl.Buffered(1))


def _sigmoid(x):
    return 1.0 / (1.0 + jnp.exp(-x))


def _log_sigmoid(x):
    return jnp.minimum(x, 0.0) - jnp.log(1.0 + jnp.exp(-jnp.abs(x)))


def _mod_kernel(c_ref, w_ref, b_ref, o_ref):
    c = c_ref[...]
    s = c * _sigmoid(c)
    o_ref[...] = jnp.dot(s, w_ref[...], preferred_element_type=F32) + b_ref[...]


def _mod_vectors(cvec, w_mod, b_mod):
    n_out = w_mod.shape[1]
    tn = 1536
    return pl.pallas_call(
        _mod_kernel,
        out_shape=jax.ShapeDtypeStruct((cvec.shape[0], n_out), F32),
        grid=(n_out // tn,),
        in_specs=[pl.BlockSpec(cvec.shape, lambda j: (0, 0)),
                  pl.BlockSpec((D_MODEL, tn), lambda j: (0, j)),
                  pl.BlockSpec((1, tn), lambda j: (0, j))],
        out_specs=pl.BlockSpec((cvec.shape[0], tn), lambda j: (0, j)),
        compiler_params=_params(("parallel",)),
        name="mod_vectors",
    )(cvec, w_mod, b_mod.reshape(1, n_out))


def _rope(acc, cos, sin, first_half):
    w = acc.shape[1]
    reps = w // LANES
    if reps > 1:
        cos = jnp.concatenate([cos] * reps, axis=1)
        sin = jnp.concatenate([sin] * reps, axis=1)
    ahead = pltpu.roll(acc, w - 16, axis=1)
    behind = pltpu.roll(acc, 16, axis=1)
    return acc * cos + jnp.where(first_half, ahead, behind) * sin


def _head_rms(acc, avg, gain):
    sq = (acc * acc).astype(BF16)
    outs = []
    for t in range(acc.shape[1] // LANES):
        sl = slice(t * LANES, (t + 1) * LANES)
        ms = jnp.dot(sq[:, sl], avg, preferred_element_type=F32)
        outs.append(acc[:, sl] * lax.rsqrt(ms + EPS) * gain)
    return jnp.concatenate(outs, axis=1)


def _inproj_kernel(*refs, lat_tiles):
    n_x = 1 if lat_tiles is None else 2
    x_refs = refs[:n_x]
    (mod_ref, g1_ref, w_ref, b_ref, wgt_ref, bgt_ref, wkt_ref, bkt_ref, wvt_ref, bvt_ref,
     cos_ref, sin_ref, qg_ref, kg_ref, avg_ref, p_ref, gt_ref, kt_ref, vt_ref, h_ref) = refs[n_x:]
    tm = h_ref.shape[0]

    def modulated_norm(x_ref):
        for sb in range(tm // ROW_BLOCK):
            r0 = sb * ROW_BLOCK
            xs = x_ref[r0:r0 + ROW_BLOCK, :]
            ms = jnp.mean(xs * xs, axis=-1, keepdims=True)
            y = xs * lax.rsqrt(ms + EPS) * g1_ref[...]
            shift = mod_ref[sb, 0:1, :]
            scale = mod_ref[sb, 1:2, :]
            h_ref[r0:r0 + ROW_BLOCK, :] = (y * (1.0 + scale) + shift).astype(BF16)

    if lat_tiles is None:
        modulated_norm(x_refs[0])
    else:
        pl.when(pl.program_id(0) < lat_tiles)(lambda: modulated_norm(x_refs[0]))
        pl.when(pl.program_id(0) >= lat_tiles)(lambda: modulated_norm(x_refs[1]))

    def first_half(width):
        return (lax.broadcasted_iota(jnp.int32, (1, width), 1) % 32) < 16

    avg = avg_ref[...]
    qg = qg_ref[...]
    kg = kg_ref[...]
    q_scale = HEAD_DIM ** -0.5

    h = h_ref[...]
    cos = cos_ref[...]
    sin = sin_ref[...]

    def proj(c0, width):
        return (jnp.dot(h, w_ref[:, c0:c0 + width], preferred_element_type=F32)
                + b_ref[:, c0:c0 + width])

    def store_plain(cols):
        for c0 in cols:
            p_ref[:, c0:c0 + WIDTH] = proj(c0, WIDTH).astype(BF16)

    plain = list(range(C_GATE, C_GATE + 3 * D_MODEL, WIDTH)) + [C_MLQ, C_MLV, C_MLO]
    raw_gq, raw_gk = proj(C_GQ, WIDTH), proj(C_GK, 256)
    raw_wq, raw_wk = proj(C_WQ, WIDTH), proj(C_WK, 256)
    store_plain(plain[:3])
    gq = _rope(_head_rms(raw_gq, avg, qg), cos, sin, first_half(WIDTH)) * (q_scale * LOG2E)
    p_ref[:, C_GQ:C_GQ + WIDTH] = gq.astype(BF16)
    gk = _rope(_head_rms(raw_gk, avg, kg), cos, sin, first_half(256))
    p_ref[:, C_GK:C_GK + 256] = gk.astype(BF16)
    p_ref[:, C_WQ:C_WQ + WIDTH] = (_rope(raw_wq, cos, sin, first_half(WIDTH)) * (q_scale * LOG2E)).astype(BF16)
    p_ref[:, C_WK:C_WK + 256] = _rope(raw_wk, cos, sin, first_half(256)).astype(BF16)
    store_plain(plain[3:])

    def proj_t(wt_ref, bt_ref):
        return lax.dot_general(wt_ref[...], h_ref[...], NT_DIMS, preferred_element_type=F32) + bt_ref[...]

    gta = proj_t(wgt_ref, bgt_ref)
    grow = lax.broadcasted_iota(jnp.int32, (2 * N_GATE, 1), 0)
    gt_ref[...] = jnp.where(grow >= N_GATE, _log_sigmoid(gta), gta)
    kt_ref[...] = proj_t(wkt_ref, bkt_ref).astype(BF16)
    vt_ref[...] = proj_t(wvt_ref, bvt_ref).astype(BF16)


def _token_tile_specs(xs, tm, lat_tiles):
    if len(xs) == 1:
        return [pl.BlockSpec((tm, D_MODEL), lambda i: (i, 0))]
    ctx_tiles = xs[1].shape[0] // tm
    return [pl.BlockSpec((tm, D_MODEL), lambda i: (jnp.minimum(i, lat_tiles - 1), 0)),
            pl.BlockSpec((tm, D_MODEL), lambda i: (jnp.clip(i - lat_tiles, 0, ctx_tiles - 1), 0))]


def _in_projection(xs, modtab, g1, w, b, wgt, bgt, wkt, bkt, wvt, bvt, cos, sin, qg, kg, avg, batch, n_lat):
    rows = sum(x.shape[0] for x in xs)
    tm = ROW_TILE
    nb = tm // ROW_BLOCK
    kv = 2 * KV_HEADS * HEAD_DIM
    seq_tiles = n_lat // tm
    lat_tiles = batch * seq_tiles

    def rope_block(i):
        return (jnp.where(i < lat_tiles, i % seq_tiles, seq_tiles), 0)

    return pl.pallas_call(
        functools.partial(_inproj_kernel, lat_tiles=lat_tiles if len(xs) == 2 else None),
        out_shape=(jax.ShapeDtypeStruct((rows, N_PROJ), BF16),
                   jax.ShapeDtypeStruct((2 * N_GATE, rows), F32),
                   jax.ShapeDtypeStruct((WIDTH, rows), BF16),
                   jax.ShapeDtypeStruct((kv, rows), BF16)),
        grid=(rows // tm,),
        in_specs=_token_tile_specs(xs, tm, lat_tiles) + [
                  pl.BlockSpec((nb, 6, D_MODEL), lambda i: (i, 0, 0)),
                  _resident((1, D_MODEL)),
                  _resident((D_MODEL, N_PROJ)),
                  _resident((1, N_PROJ)),
                  _resident((2 * N_GATE, D_MODEL)),
                  _resident((2 * N_GATE, 1)),
                  _resident((WIDTH, D_MODEL)),
                  _resident((WIDTH, 1)),
                  _resident((kv, D_MODEL)),
                  _resident((kv, 1)),
                  pl.BlockSpec((tm, LANES), rope_block),
                  pl.BlockSpec((tm, LANES), rope_block),
                  _resident((1, LANES)),
                  _resident((1, LANES)),
                  _resident((LANES, LANES))],
        out_specs=(pl.BlockSpec((tm, N_PROJ), lambda i: (i, 0)),
                   pl.BlockSpec((2 * N_GATE, tm), lambda i: (0, i)),
                   pl.BlockSpec((WIDTH, tm), lambda i: (0, i)),
                   pl.BlockSpec((kv, tm), lambda i: (0, i))),
        scratch_shapes=[pltpu.VMEM((tm, D_MODEL), BF16)],
        compiler_params=_params(("parallel",)),
        name="in_projection",
    )(*xs, modtab, g1, w, b, wgt, bgt, wkt, bkt, wvt, bvt, cos, sin, qg, kg, avg)


def _scan_lanes(x, op, fill, reverse):
    n = x.shape[1]
    lane = lax.broadcasted_iota(jnp.int32, x.shape, 1)
    sh = 1
    while sh < n:
        if reverse:
            moved = jnp.where(lane < n - sh, pltpu.roll(x, n - sh, axis=1), fill)
        else:
            moved = jnp.where(lane >= sh, pltpu.roll(x, sh, axis=1), fill)
        x = op(x, moved)
        sh *= 2
    return x


def _mlstm_kernel(qvf_ref, ktf_ref, gtf_ref, qvb_ref, ktb_ref, gtb_ref, hf_ref, hb_ref, cn_ref, m_ref):
    @pl.when(pl.program_id(1) == 0)
    def _():
        cn_ref[...] = jnp.zeros_like(cn_ref)
        m_ref[...] = jnp.zeros_like(m_ref)

    L = CHUNK
    row = lax.broadcasted_iota(jnp.int32, (L, L), 0)
    col = lax.broadcasted_iota(jnp.int32, (L, L), 1)
    scale = ML_DIM ** -0.5
    ones_v = jnp.ones((L, ML_DIM), BF16)
    pending = []
    for d, (qv_ref, kt_ref, gt_ref, h_ref) in enumerate(((qvf_ref, ktf_ref, gtf_ref, hf_ref),
                                                          (qvb_ref, ktb_ref, gtb_ref, hb_ref))):
        seen = (col <= row) if d == 0 else (col >= row)
        g0 = d * ML_HEADS
        ic = gt_ref[g0:g0 + ML_HEADS, :]
        lf = gt_ref[N_GATE + g0:N_GATE + g0 + ML_HEADS, :]
        r = ic - _scan_lanes(lf, jnp.add, 0.0, d == 1)
        r_max = jnp.max(r, axis=1, keepdims=True)
        b_end = jnp.sum(lf, axis=1, keepdims=True)
        for hh in range(ML_HEADS):
            idx = g0 + hh
            lanes = slice(hh * ML_DIM, (hh + 1) * ML_DIM)
            r_row = r[hh:hh + 1, :]
            m_old = m_ref[idx]
            m_end = jnp.maximum(m_old, r_max[hh:hh + 1, :])
            q = qv_ref[:, lanes]
            vo = jnp.concatenate([qv_ref[:, WIDTH + hh * ML_DIM:WIDTH + (hh + 1) * ML_DIM], ones_v], axis=1)
            kt = kt_ref[lanes, :]
            qsb = (q.astype(F32) * scale).astype(BF16)
            cn = cn_ref[idx]
            s_raw = jnp.dot(qsb, kt, preferred_element_type=F32)
            q_cn = jnp.dot(qsb, cn.astype(BF16), preferred_element_type=F32)
            kwt = (kt.astype(F32) * jnp.exp(r_row - m_end)).astype(BF16)
            cn_ref[idx] = jnp.exp(m_old - m_end) * cn + jnp.dot(kwt, vo, preferred_element_type=F32)
            m_ref[idx] = b_end[hh:hh + 1, :] + m_end
            pending.append((h_ref, lanes, seen, r_row, lf[hh:hh + 1, :], m_old, s_raw, q_cn, vo))

    for h_ref, lanes, seen, r_row, lf_row, m_old, s_raw, q_cn, vo in pending:
        b_col = jnp.sum(jnp.where(seen, lf_row, 0.0), axis=1, keepdims=True)
        m_col = jnp.maximum(m_old, jnp.max(jnp.where(seen, r_row, -jnp.inf), axis=1, keepdims=True))
        w = jnp.exp(jnp.where(seen, r_row - m_col, -jnp.inf))
        a = jnp.exp(m_old - m_col)
        s_vo = jnp.dot((s_raw * w).astype(BF16), vo, preferred_element_type=F32)
        num = a * q_cn[:, :ML_DIM] + s_vo[:, :ML_DIM]
        den = a * q_cn[:, ML_DIM:] + s_vo[:, ML_DIM:]
        hc = num / jnp.maximum(jnp.abs(den), jnp.exp(-(b_col + m_col)))
        h_ref[:, lanes] = hc.astype(h_ref.dtype)


def _mlstm(p, kt, gt, batch, n_lat, n_ctx):
    rows = p.shape[0]
    cl, cc = n_lat // CHUNK, n_ctx // CHUNK
    lat_base, ctx_base = 0, batch * cl

    def fwd_chunk(b, i):
        return jnp.where(i < cc, ctx_base + b * cc + i, lat_base + b * cl + (i - cc))

    def bwd_chunk(b, i):
        return jnp.where(i < cc, ctx_base + b * cc + (cc - 1 - i), lat_base + b * cl + (cl - 1 - (i - cc)))

    def specs(chunk):
        return [pl.BlockSpec((CHUNK, 2 * WIDTH), lambda b, i: (chunk(b, i), C_MLQ // (2 * WIDTH))),
                pl.BlockSpec((WIDTH, CHUNK), lambda b, i: (0, chunk(b, i))),
                pl.BlockSpec((2 * N_GATE, CHUNK), lambda b, i: (0, chunk(b, i)))]

    n_state = 2 * ML_HEADS
    return pl.pallas_call(
        _mlstm_kernel,
        out_shape=(jax.ShapeDtypeStruct((rows, WIDTH), BF16), jax.ShapeDtypeStruct((rows, WIDTH), BF16)),
        grid=(batch, cl + cc),
        in_specs=specs(fwd_chunk) + specs(bwd_chunk),
        out_specs=(pl.BlockSpec((CHUNK, WIDTH), lambda b, i: (fwd_chunk(b, i), 0)),
                   pl.BlockSpec((CHUNK, WIDTH), lambda b, i: (bwd_chunk(b, i), 0))),
        scratch_shapes=[pltpu.VMEM((n_state, ML_DIM, 2 * ML_DIM), F32),
                        pltpu.VMEM((n_state, 1, 1), F32)],
        compiler_params=_params(("parallel", "arbitrary")),
        name="mlstm_scan",
    )(p, kt, gt, p, kt, gt)


def _half_mask(e):
    lane = lax.broadcasted_iota(jnp.int32, (1, LANES), 1)
    return (lane < HEAD_DIM) if e == 0 else (lane >= HEAD_DIM)


def _win_kernel(sink_ref, q_ref, kc_ref, vtc_ref, k0_ref, k1_ref, k2_ref, k3_ref,
                vt0_ref, vt1_ref, vt2_ref, vt3_ref, o_ref, *, n_lat_tiles, n_lat_blocks):
    t = pl.program_id(1)
    tq = q_ref.shape[0]
    n_ctx = kc_ref.shape[0]
    is_lat = t < n_lat_tiles
    kk = lax.broadcasted_iota(jnp.int32, (CHUNK, tq), 0)
    qq = lax.broadcasted_iota(jnp.int32, (CHUNK, tq), 1)
    band = (kk >= qq, qq <= kk + CHUNK, kk <= qq, kk + CHUNK <= qq)
    first = 2 * t - 1
    present = [jnp.logical_and(is_lat, jnp.logical_and(first + i >= 0, first + i < n_lat_blocks))
               for i in range(4)]
    visible = [jnp.logical_and(band[i], present[i]) for i in range(4)]
    k_refs = (k0_ref, k1_ref, k2_ref, k3_ref)
    vt_refs = (vt0_ref, vt1_ref, vt2_ref, vt3_ref)
    ones = jnp.ones((16, n_ctx + 4 * CHUNK), BF16)

    def group_operands(g):
        lanes = slice(g * LANES, (g + 1) * LANES)
        k_all = jnp.concatenate([kc_ref[:, lanes]] + [r[:, lanes] for r in k_refs], axis=0)
        vrows = slice(g * HEAD_DIM, (g + 1) * HEAD_DIM)
        vta = jnp.concatenate(
            [jnp.concatenate([vtc_ref[vrows, :]] + [r[vrows, :] for r in vt_refs], axis=1), ones], axis=0)
        return k_all, vta

    operands = [group_operands(g) for g in range(KV_HEADS)]

    def score(h):
        qf = q_ref[:, (h // 2) * LANES:(h // 2 + 1) * LANES].astype(F32)
        qt = jnp.where(_half_mask(h % 2), qf, 0.0).T.astype(BF16)
        return jnp.dot(operands[h // (Q_HEADS // KV_HEADS)][0], qt, preferred_element_type=F32)

    scores = {h: score(h) for h in range(WIN_LEAD)}
    outs = []
    for h in range(Q_HEADS):
        if h + WIN_LEAD < Q_HEADS:
            scores[h + WIN_LEAD] = score(h + WIN_LEAD)
        s = scores.pop(h)
        sink = sink_ref[h] * LOG2E
        parts = [s[:n_ctx, :]]
        for i in range(4):
            blk = s[n_ctx + i * CHUNK:n_ctx + (i + 1) * CHUNK, :]
            parts.append(jnp.where(visible[i], blk, -jnp.inf))
        s = jnp.concatenate(parts, axis=0)
        m = jnp.maximum(jnp.max(s, axis=0, keepdims=True), sink)
        pt = jnp.exp2(s - m).astype(BF16)
        ol = jnp.dot(operands[h // (Q_HEADS // KV_HEADS)][1], pt, preferred_element_type=F32)
        den = ol[HEAD_DIM:HEAD_DIM + 1, :] + jnp.exp2(sink - m)
        outs.append(ol[:HEAD_DIM, :] / den)
    for tt in range(Q_HEADS // 2):
        o_ref[:, tt * LANES:(tt + 1) * LANES] = (
            jnp.concatenate(outs[2 * tt:2 * tt + 2], axis=0).T.astype(o_ref.dtype))


def _window_attention(p, vt, sink, batch, n_lat, n_ctx, ctx_queries):
    tq = 256
    nlt, nct = n_lat // tq, n_ctx // tq
    nlb = n_lat // CHUNK
    nq = nlt + (nct if ctx_queries else 0)
    out_rows = batch * (n_lat + (n_ctx if ctx_queries else 0))
    ctx_blk = batch * n_lat // n_ctx
    k_col = C_WK // 256

    def q_block(b, t):
        return jnp.where(t < nlt, b * nlt + t, batch * nlt + b * nct + (t - nlt))

    def near(i):
        return lambda b, t: b * nlb + jnp.clip(2 * t - 1 + i, 0, nlb - 1)

    k_specs = [pl.BlockSpec((CHUNK, 256), (lambda f: lambda b, t: (f(b, t), k_col))(near(i))) for i in range(4)]
    vt_specs = [pl.BlockSpec((LANES, CHUNK), (lambda f: lambda b, t: (VT_WINDOW, f(b, t)))(near(i)))
                for i in range(4)]
    return pl.pallas_call(
        functools.partial(_win_kernel, n_lat_tiles=nlt, n_lat_blocks=nlb),
        out_shape=jax.ShapeDtypeStruct((out_rows, WIDTH), BF16),
        grid=(batch, nq),
        in_specs=[pl.BlockSpec(memory_space=pltpu.SMEM),
                  pl.BlockSpec((tq, WIDTH), lambda b, t: (q_block(b, t), C_WQ // WIDTH)),
                  pl.BlockSpec((n_ctx, 256), lambda b, t: (ctx_blk + b, k_col)),
                  pl.BlockSpec((LANES, n_ctx), lambda b, t: (VT_WINDOW, ctx_blk + b))]
                 + k_specs + vt_specs,
        out_specs=pl.BlockSpec((tq, WIDTH), lambda b, t: (q_block(b, t), 0)),
        compiler_params=_params(("parallel", "parallel")),
        name="window_attention",
    )(sink, p, p, vt, p, p, p, p, vt, vt, vt, vt)


def _glb_kernel(q_ref, kc_ref, vtc_ref, kl_ref, vtl_ref, o_ref, qt_ref, knorm_ref, *scratch, n_lat_tiles, tk, span_chunks):
    heads = Q_HEADS // KV_HEADS
    acc_refs = scratch[0:heads]
    sc_refs = scratch[heads:2 * heads]
    ring = [scratch[(2 + i) * heads:(3 + i) * heads] for i in range(N_SCORE_BUFS)]
    base = (2 + N_SCORE_BUFS) * heads
    pc_refs = scratch[base:base + heads]
    p_ring = scratch[base + heads:base + heads + 2]
    qi = pl.program_id(2)
    tq = q_ref.shape[0]
    n_chunks = kl_ref.shape[0] // tk

    def k_chunk(c):
        off = pl.multiple_of(jnp.minimum(c, n_chunks - 1) * tk, tk)
        return kl_ref[pl.ds(off, tk), :]

    def vt_chunk(c):
        return vtl_ref[:, pl.ds(pl.multiple_of(c * tk, tk), tk)]

    def with_ones(vt):
        ones = jnp.ones((acc_refs[0].shape[0] - HEAD_DIM, vt.shape[1]), BF16)
        return jnp.concatenate([vt, ones], axis=0)

    @pl.when(qi == 0)
    def _():
        def sq_norm(k):
            kf = k.astype(F32)
            return jnp.max(jnp.sum(kf * kf, axis=1, keepdims=True), axis=0, keepdims=True)

        def body(c, best):
            return jnp.maximum(best, sq_norm(k_chunk(c)))
        best = lax.fori_loop(0, n_chunks, body, sq_norm(kc_ref[...]))
        knorm_ref[...] = jnp.sqrt(0.5 * best)

    for t in range(heads // 2):
        qf = q_ref[:, t * LANES:(t + 1) * LANES].astype(F32)
        for e in range(2):
            h = 2 * t + e
            qt_ref[:, h * tq:(h + 1) * tq] = jnp.where(_half_mask(e), qf, 0.0).T.astype(BF16)
    for h in range(heads):
        acc_refs[h][...] = jnp.zeros_like(acc_refs[h])

    def q_t(h):
        return qt_ref[:, h * tq:(h + 1) * tq]

    bound = [jnp.sqrt(jnp.sum(jnp.square(q_t(h).astype(F32)), axis=0, keepdims=True)) * knorm_ref[...]
             for h in range(heads)]
    bounded = jnp.max(functools.reduce(jnp.maximum, bound)) <= SAFE_LOG2_BOUND

    span = span_chunks * tk
    n_spans = n_chunks // span_chunks
    n_steps = n_spans * heads

    def span_of(step):
        return jnp.minimum(step // heads, n_spans - 1)

    def produce_ctx():
        for h in range(heads):
            s = jnp.dot(kc_ref[...], q_t(h), preferred_element_type=F32)
            pc_refs[h][...] = jnp.exp2(s - bound[h]).astype(BF16)

    def consume_ctx():
        vta = with_ones(vtc_ref[...])
        for h in range(heads):
            acc_refs[h][...] += jnp.dot(vta, pc_refs[h][...], preferred_element_type=F32)

    def produce(p_ref, sp, h):
        k = kl_ref[pl.ds(pl.multiple_of(sp * span, span), span), :]
        s = jnp.dot(k, q_t(h), preferred_element_type=F32)
        p_ref[...] = jnp.exp2(s - bound[h]).astype(BF16)

    def consume(p_ref, sp, h):
        vta = with_ones(vtl_ref[:, pl.ds(pl.multiple_of(sp * span, span), span)])
        acc_refs[h][...] += jnp.dot(vta, p_ref[...], preferred_element_type=F32)

    @pl.when(bounded)
    def _():
        produce_ctx()

        @pl.when(qi >= n_lat_tiles)
        def _():
            consume_ctx()

        @pl.when(qi < n_lat_tiles)
        def _():
            produce(p_ring[0], 0, 0)
            consume_ctx()

            unroll = heads * (2 if n_spans % 2 == 0 else 1)

            def body(j, carry):
                for i in range(unroll):
                    step = unroll * j + i
                    produce(p_ring[(i + 1) % 2], span_of(step + 1), (i + 1) % heads)
                    consume(p_ring[i % 2], span_of(step), i % heads)
                return carry
            lax.fori_loop(0, n_steps // unroll, body, 0)

    neg_inf = (jnp.full((1, tq), -jnp.inf, F32),) * heads

    def fill(s_refs, k, m_run):
        out = []
        for h in range(heads):
            s = jnp.dot(k, q_t(h), preferred_element_type=F32)
            s_refs[h][...] = s
            out.append(jnp.maximum(m_run[h], jnp.max(s, axis=0, keepdims=True)))
        return tuple(out)

    def drain(s_refs, vt, m_run, m_acc):
        vta = with_ones(vt)
        for h in range(heads):
            alpha = jnp.exp2(m_acc[h] - m_run[h])
            pt = jnp.exp2(s_refs[h][...] - m_run[h]).astype(BF16)
            acc_refs[h][...] = alpha * acc_refs[h][...] + jnp.dot(vta, pt, preferred_element_type=F32)
        return m_run

    @pl.when(jnp.logical_not(bounded))
    def _():
        m_ctx = fill(sc_refs, kc_ref[...], neg_inf)

        @pl.when(qi >= n_lat_tiles)
        def _():
            drain(sc_refs, vtc_ref[...], m_ctx, neg_inf)

        @pl.when(qi < n_lat_tiles)
        def _():
            m_one = fill(ring[0], k_chunk(0), m_ctx)
            m_two = fill(ring[1], k_chunk(1), m_one)
            m_acc = drain(sc_refs, vtc_ref[...], m_one, neg_inf)

            def body(j, carry):
                m_run, m_acc = carry
                for i in range(N_SCORE_BUFS):
                    c = N_SCORE_BUFS * j + i
                    m_next = fill(ring[(i + 2) % N_SCORE_BUFS], k_chunk(c + 2), m_run)
                    m_acc = drain(ring[i], vt_chunk(c), m_run, m_acc)
                    m_run = m_next
                return m_run, m_acc
            lax.fori_loop(0, n_chunks // N_SCORE_BUFS, body, (m_two, m_acc))

    for t in range(heads // 2):
        pair = []
        for e in range(2):
            a = acc_refs[2 * t + e][...]
            pair.append(a[:HEAD_DIM, :] / a[HEAD_DIM:HEAD_DIM + 1, :])
        o_ref[:, t * LANES:(t + 1) * LANES] = jnp.concatenate(pair, axis=0).T.astype(o_ref.dtype)


def _global_attention(p, vt, batch, n_lat, n_ctx, ctx_queries):
    tq = 256
    tk = min(512, n_lat // N_SCORE_BUFS)
    assert n_lat % (tk * N_SCORE_BUFS) == 0 and tk % LANES == 0
    span_chunks = math.gcd(SPAN_CHUNKS, n_lat // tk)
    nl, nc = n_lat // tq, n_ctx // tq
    nq = nl + (nc if ctx_queries else 0)
    out_rows = batch * (n_lat + (n_ctx if ctx_queries else 0))
    ctx_base = batch * nl
    heads = Q_HEADS // KV_HEADS
    acc_rows = HEAD_DIM + 16

    def q_block(b, qi):
        return jnp.where(qi < nl, b * nl + qi, ctx_base + b * nc + (qi - nl))

    ctx_blk = batch * n_lat // n_ctx
    return pl.pallas_call(
        functools.partial(_glb_kernel, n_lat_tiles=nl, tk=tk, span_chunks=span_chunks),
        out_shape=jax.ShapeDtypeStruct((out_rows, WIDTH), BF16),
        grid=(batch, KV_HEADS, nq),
        in_specs=[pl.BlockSpec((tq, 2 * LANES), lambda b, g, qi: (q_block(b, qi), C_GQ // 256 + g)),
                  pl.BlockSpec((n_ctx, LANES), lambda b, g, qi: (ctx_blk + b, C_GK // LANES + g)),
                  pl.BlockSpec((HEAD_DIM, n_ctx), lambda b, g, qi: (KV_HEADS * VT_GLOBAL + g, ctx_blk + b)),
                  pl.BlockSpec((n_lat, LANES), lambda b, g, qi: (b, C_GK // LANES + g)),
                  pl.BlockSpec((HEAD_DIM, n_lat), lambda b, g, qi: (KV_HEADS * VT_GLOBAL + g, b))],
        out_specs=pl.BlockSpec((tq, 2 * LANES), lambda b, g, qi: (q_block(b, qi), g)),
        scratch_shapes=([pltpu.VMEM((LANES, heads * tq), BF16), pltpu.VMEM((1, 1), F32)]
                        + [pltpu.VMEM((acc_rows, tq), F32)] * heads
                        + [pltpu.VMEM((n_ctx, tq), F32)] * heads
                        + [pltpu.VMEM((tk, tq), F32)] * (N_SCORE_BUFS * heads)
                        + [pltpu.VMEM((n_ctx, tq), BF16)] * heads
                        + [pltpu.VMEM((span_chunks * tk, tq), BF16)] * 2),
        compiler_params=_params(("parallel", "parallel", "arbitrary")),
        name="global_attention",
    )(p, p, vt, p, vt)


def _merge_kernel(*refs, lat_tiles):
    n_x = 1 if lat_tiles is None else 2
    x_refs = refs[:n_x]
    (mod_ref, hf_ref, hb_ref, og_ref, yb_ref, yc_ref, ga_ref, gb_ref, gc_ref,
     mlg_ref, wbr_ref, wo_ref, o_ref) = refs[n_x:]
    tm = o_ref.shape[0]
    proj_b = jnp.dot(yb_ref[...], wbr_ref[1], preferred_element_type=F32)
    proj_c = jnp.dot(yc_ref[...], wbr_ref[2], preferred_element_type=F32)
    hs = hf_ref[...].astype(F32) + hb_ref[...].astype(F32)
    parts = []
    for t in range(ML_HEADS):
        ht = hs[:, t * ML_DIM:(t + 1) * ML_DIM]
        ms = jnp.mean(ht * ht, axis=-1, keepdims=True)
        parts.append(ht * lax.rsqrt(ms + EPS))
    ya = (jnp.concatenate(parts, axis=1) * mlg_ref[...] * _sigmoid(og_ref[...].astype(F32))).astype(BF16)
    proj_a = jnp.dot(ya, wbr_ref[0], preferred_element_type=F32)
    merged = (_sigmoid(gb_ref[...].astype(F32)) * proj_b + _sigmoid(gc_ref[...].astype(F32)) * proj_c
              + _sigmoid(ga_ref[...].astype(F32)) * proj_a)
    out = jnp.dot(merged.astype(BF16), wo_ref[...], preferred_element_type=F32)

    def gated_residual(x_ref):
        for sb in range(tm // ROW_BLOCK):
            rows = slice(sb * ROW_BLOCK, (sb + 1) * ROW_BLOCK)
            o_ref[rows, :] = x_ref[rows, :] + mod_ref[sb, 2:3, :] * out[rows, :]

    if lat_tiles is None:
        gated_residual(x_refs[0])
    else:
        pl.when(pl.program_id(0) < lat_tiles)(lambda: gated_residual(x_refs[0]))
        pl.when(pl.program_id(0) >= lat_tiles)(lambda: gated_residual(x_refs[1]))


def _merge(xs, modtab, p, hf, hb, yb, yc, mlg, wbr, wo, n_rows, lat_tiles):
    tm = ROW_TILE
    nb = tm // ROW_BLOCK
    rows = sum(x.shape[0] for x in xs)
    row = lambda i: (i, 0)
    gate = lambda k: pl.BlockSpec((tm, D_MODEL), lambda i: (i, C_GATE // D_MODEL + k))
    return pl.pallas_call(
        functools.partial(_merge_kernel, lat_tiles=lat_tiles if len(xs) == 2 else None),
        out_shape=jax.ShapeDtypeStruct((rows, D_MODEL), F32),
        grid=(n_rows // tm,),
        in_specs=_token_tile_specs(xs, tm, lat_tiles) + [
                  pl.BlockSpec((nb, 6, D_MODEL), lambda i: (i, 0, 0)),
                  pl.BlockSpec((tm, WIDTH), row),
                  pl.BlockSpec((tm, WIDTH), row),
                  pl.BlockSpec((tm, WIDTH), lambda i: (i, C_MLO // WIDTH)),
                  pl.BlockSpec((tm, WIDTH), row),
                  pl.BlockSpec((tm, WIDTH), row),
                  gate(0), gate(1), gate(2),
                  _resident((1, WIDTH)),
                  _resident((3, WIDTH, D_MODEL)),
                  _resident((D_MODEL, D_MODEL))],
        out_specs=pl.BlockSpec((tm, D_MODEL), row),
        input_output_aliases={0: 0} if len(xs) == 1 else {},
        compiler_params=_params(("parallel",)),
        name="merge",
    )(*xs, modtab, hf, hb, p, yb, yc, p, p, p, mlg, wbr, wo)


def _ffn_kernel(x_ref, mod_ref, g2_ref, w1_ref, w3_ref, w2_ref, gf_ref, o_ref, *, final):
    tm = x_ref.shape[0]
    d_ff = w1_ref.shape[1]
    split = pl.cdiv(d_ff // MXU_TILE, 2) * MXU_TILE
    for sb in range(tm // ROW_BLOCK):
        rows = slice(sb * ROW_BLOCK, (sb + 1) * ROW_BLOCK)
        xs = x_ref[rows, :]
        ms = jnp.mean(xs * xs, axis=-1, keepdims=True)
        y = xs * lax.rsqrt(ms + EPS) * g2_ref[...]
        h = (y * (1.0 + mod_ref[sb, 4:5, :]) + mod_ref[sb, 3:4, :]).astype(BF16)
        out = None
        for c0, c1 in ((0, split), (split, d_ff)):
            a = jnp.dot(h, w1_ref[:, c0:c1], preferred_element_type=F32)
            b = jnp.dot(h, w3_ref[:, c0:c1], preferred_element_type=F32)
            z = (a * _sigmoid(a) * b).astype(BF16)
            part = jnp.dot(z, w2_ref[c0:c1, :], preferred_element_type=F32)
            out = part if out is None else out + part
        xn = xs + mod_ref[sb, 5:6, :] * out
        if final:
            ms = jnp.mean(xn * xn, axis=-1, keepdims=True)
            xn = xn * lax.rsqrt(ms + EPS) * gf_ref[...]
        o_ref[rows, :] = xn


def _ffn(x, modtab, g2, w1, w3, w2, gfin, n_rows, final):
    tm = ROW_TILE
    nb = tm // ROW_BLOCK
    d_ff = w1.shape[1]
    row = lambda i: (i, 0)
    out_rows = n_rows if final else x.shape[0]
    return pl.pallas_call(
        functools.partial(_ffn_kernel, final=final),
        out_shape=jax.ShapeDtypeStruct((out_rows, D_MODEL), F32),
        grid=(n_rows // tm,),
        in_specs=[pl.BlockSpec((tm, D_MODEL), row),
                  pl.BlockSpec((nb, 6, D_MODEL), lambda i: (i, 0, 0)),
                  _resident((1, D_MODEL)),
                  _resident((D_MODEL, d_ff)),
                  _resident((D_MODEL, d_ff)),
                  _resident((d_ff, D_MODEL)),
                  _resident((1, D_MODEL))],
        out_specs=pl.BlockSpec((tm, D_MODEL), row),
        input_output_aliases={} if final else {0: 0},
        compiler_params=_params(("parallel",)),
        name="ffn",
    )(x, modtab, g2, w1, w3, w2, gfin)


def _dup_halves(w, base):
    h0 = w[..., base:base + HEAD_DIM]
    h1 = w[..., base + HEAD_DIM:base + 2 * HEAD_DIM]
    return [h0, h0, h1, h1]


def _arrange_in_proj(w):
    o_gate_ml = 4 * WIDTH
    o_wq = o_gate_ml + 2 * N_GATE
    o_wk, o_wv = o_wq + WIDTH, o_wq + WIDTH + 128
    o_gq = o_wv + 128
    o_gk, o_gv = o_gq + WIDTH, o_gq + WIDTH + 128
    o_gate = o_gv + 128
    main = jnp.concatenate(
        [w[..., o_gate:o_gate + 3 * D_MODEL], w[..., 0:WIDTH], w[..., 2 * WIDTH:4 * WIDTH],
         w[..., o_wq:o_wq + WIDTH]]
        + _dup_halves(w, o_wk) + [w[..., o_gq:o_gq + WIDTH]] + _dup_halves(w, o_gk), axis=-1)
    values = jnp.concatenate([w[..., o_gv:o_gv + 128], w[..., o_wv:o_wv + 128]], axis=-1)
    return main, w[..., o_gate_ml:o_gate_ml + 2 * N_GATE], w[..., WIDTH:2 * WIDTH], values


def _rope_tables(n_lat):
    t = jnp.arange(n_lat)
    quarter = HEAD_DIM // 4
    inv = ROPE_THETA ** (-jnp.arange(0, 2 * quarter, 2, dtype=F32) / (2 * quarter))
    ang_r = (t // GRID_W).astype(F32)[:, None] * inv
    ang_c = (t % GRID_W).astype(F32)[:, None] * inv
    cos = jnp.concatenate([jnp.cos(ang_r)] * 2 + [jnp.cos(ang_c)] * 2, axis=1)
    sin = jnp.concatenate([-jnp.sin(ang_r), jnp.sin(ang_r), -jnp.sin(ang_c), jnp.sin(ang_c)], axis=1)
    reps = LANES // HEAD_DIM
    return (jnp.concatenate([jnp.tile(cos, (1, reps)), jnp.ones((ROW_TILE, LANES), F32)], axis=0),
            jnp.concatenate([jnp.tile(sin, (1, reps)), jnp.zeros((ROW_TILE, LANES), F32)], axis=0))


def kernel(x, c, ctx, c_ctx, w_mod, b_mod, norm1_g, w_in, b_in, ml_norm_g, win_sink, qn_g, kn_g,
           w_br, w_o, norm2_g, w_ff1, w_ff3, w_ff2, final_g):
    batch, n_lat, d = x.shape
    n_ctx = ctx.shape[1]
    depth = w_mod.shape[0]
    assert d == D_MODEL and n_lat % ROW_TILE == 0 and n_lat % GRID_W == 0
    assert (batch * n_ctx) % ROW_TILE == 0 and n_ctx % ROW_BLOCK == 0 and (batch * n_lat) % n_ctx == 0
    lat_rows, ctx_rows = batch * n_lat, batch * n_ctx

    xs = (x.reshape(lat_rows, d), ctx.reshape(ctx_rows, d))
    cvec = jnp.concatenate([c, c_ctx[None, :], jnp.zeros((8 - batch - 1, d), F32)], axis=0)
    block_class = np.concatenate([np.repeat(np.arange(batch), n_lat // ROW_BLOCK),
                                  np.full(ctx_rows // ROW_BLOCK, batch)])
    cos, sin = _rope_tables(n_lat)
    lat_tiles = lat_rows // ROW_TILE
    avg = jnp.asarray(np.kron(np.eye(LANES // HEAD_DIM), np.full((HEAD_DIM, HEAD_DIM), 1.0 / HEAD_DIM)), BF16)

    out = None
    for l in range(depth):
        last = l == depth - 1
        mod = _mod_vectors(cvec, w_mod[l], b_mod[l]).reshape(8, 6, d)
        modtab = mod[block_class]

        w_main, w_gate, w_mlk, w_val = _arrange_in_proj(w_in[l])
        b_main, b_gate, b_mlk, b_val = _arrange_in_proj(b_in[l][None, :])
        p, gt, kt, vt = _in_projection(
            xs, modtab, norm1_g[l][None, :], w_main.astype(BF16), b_main,
            w_gate.T.astype(BF16), b_gate.T, w_mlk.T.astype(BF16), b_mlk.T, w_val.T.astype(BF16), b_val.T,
            cos, sin, jnp.tile(qn_g[l], 2)[None, :], jnp.tile(kn_g[l], 2)[None, :], avg, batch, n_lat)

        hf, hb = _mlstm(p, kt, gt, batch, n_lat, n_ctx)
        yb = _window_attention(p, vt, win_sink[l], batch, n_lat, n_ctx, ctx_queries=not last)
        yc = _global_attention(p, vt, batch, n_lat, n_ctx, ctx_queries=not last)

        n_rows = lat_rows if last else lat_rows + ctx_rows
        merged = _merge(xs, modtab, p, hf, hb, yb, yc, ml_norm_g[l].reshape(1, WIDTH),
                        w_br[l].astype(BF16), w_o[l].astype(BF16), n_rows, lat_tiles)
        out = _ffn(merged, modtab, norm2_g[l][None, :], w_ff1[l].astype(BF16), w_ff3[l].astype(BF16),
                   w_ff2[l].astype(BF16), final_g[None, :], n_rows, final=last)
        xs = (out,)
    return out.reshape(batch, n_lat, d)
```

```python
import functools
import math

import jax
import jax.numpy as jnp
import numpy as np
from jax import lax
from jax.experimental import pallas as pl
from jax.experimental.pallas import tpu as pltpu

F32 = jnp.float32
BF16 = jnp.bfloat16

D_MODEL = 1024
GRID_W = 64
CHUNK = 128
HEAD_DIM = 64
ROPE_THETA = 10000.0
EPS = 1e-6
ML_HEADS = 4
ML_DIM = 128
Q_HEADS = 8
KV_HEADS = 2
WIDTH = 512
N_GATE = 2 * ML_HEADS

LANES = 128
MXU_TILE = 256
ROW_BLOCK = 256
ROW_TILE = 512
VMEM_LIMIT = 56 * 1024 * 1024

C_GATE = 0
C_MLQ, C_MLV, C_MLO = 3072, 3584, 4096
C_WQ, C_WK = 4608, 5120
C_GQ, C_GK = 5376, 5888
N_PROJ = C_GK + 256
VT_GLOBAL, VT_WINDOW = 0, 1
LOG2E = 1.4426950408889634
WIN_LEAD = 3
N_SCORE_BUFS = 4
SPAN_CHUNKS = 4
SAFE_LOG2_BOUND = 50.0

NT_DIMS = (((1,), (1,)), ((), ()))


def _params(sem, vmem=VMEM_LIMIT):
    return pltpu.CompilerParams(dimension_semantics=sem, vmem_limit_bytes=vmem)


def _resident(shape):
    nd = len(shape)
    return pl.BlockSpec(shape, lambda *_: (0,) * nd, pipeline_mode=pl.Buffered(1))


def _sigmoid(x):
    return 1.0 / (1.0 + jnp.exp(-x))


def _log_sigmoid(x):
    return jnp.minimum(x, 0.0) - jnp.log(1.0 + jnp.exp(-jnp.abs(x)))


def _mod_kernel(c_ref, w_ref, b_ref, o_ref):
    c = c_ref[...]
    s = c * _sigmoid(c)
    o_ref[...] = jnp.dot(s, w_ref[...], preferred_element_type=F32) + b_ref[...]


def _mod_vectors(cvec, w_mod, b_mod):
    n_out = w_mod.shape[1]
    tn = 1536
    return pl.pallas_call(
        _mod_kernel,
        out_shape=jax.ShapeDtypeStruct((cvec.shape[0], n_out), F32),
        grid=(n_out // tn,),
        in_specs=[pl.BlockSpec(cvec.shape, lambda j: (0, 0)),
                  pl.BlockSpec((D_MODEL, tn), lambda j: (0, j)),
                  pl.BlockSpec((1, tn), lambda j: (0, j))],
        out_specs=pl.BlockSpec((cvec.shape[0], tn), lambda j: (0, j)),
        compiler_params=_params(("parallel",)),
        name="mod_vectors",
    )(cvec, w_mod, b_mod.reshape(1, n_out))


def _rope(acc, cos, sin, first_half):
    w = acc.shape[1]
    reps = w // LANES
    if reps > 1:
        cos = jnp.concatenate([cos] * reps, axis=1)
        sin = jnp.concatenate([sin] * reps, axis=1)
    ahead = pltpu.roll(acc, w - 16, axis=1)
    behind = pltpu.roll(acc, 16, axis=1)
    return acc * cos + jnp.where(first_half, ahead, behind) * sin


def _head_rms(acc, avg, gain):
    sq = (acc * acc).astype(BF16)
    outs = []
    for t in range(acc.shape[1] // LANES):
        sl = slice(t * LANES, (t + 1) * LANES)
        ms = jnp.dot(sq[:, sl], avg, preferred_element_type=F32)
        outs.append(acc[:, sl] * lax.rsqrt(ms + EPS) * gain)
    return jnp.concatenate(outs, axis=1)


def _inproj_kernel(*refs, lat_tiles):
    n_x = 1 if lat_tiles is None else 2
    x_refs = refs[:n_x]
    (mod_ref, g1_ref, w_ref, b_ref, wgt_ref, bgt_ref, wkt_ref, bkt_ref, wvt_ref, bvt_ref,
     cos_ref, sin_ref, qg_ref, kg_ref, avg_ref, p_ref, gt_ref, kt_ref, vt_ref, h_ref) = refs[n_x:]
    tm = h_ref.shape[0]

    def modulated_norm(x_ref):
        for sb in range(tm // ROW_BLOCK):
            r0 = sb * ROW_BLOCK
            xs = x_ref[r0:r0 + ROW_BLOCK, :]
            ms = jnp.mean(xs * xs, axis=-1, keepdims=True)
            y = xs * lax.rsqrt(ms + EPS) * g1_ref[...]
            shift = mod_ref[sb, 0:1, :]
            scale = mod_ref[sb, 1:2, :]
            h_ref[r0:r0 + ROW_BLOCK, :] = (y * (1.0 + scale) + shift).astype(BF16)

    if lat_tiles is None:
        modulated_norm(x_refs[0])
    else:
        pl.when(pl.program_id(0) < lat_tiles)(lambda: modulated_norm(x_refs[0]))
        pl.when(pl.program_id(0) >= lat_tiles)(lambda: modulated_norm(x_refs[1]))

    def first_half(width):
        return (lax.broadcasted_iota(jnp.int32, (1, width), 1) % 32) < 16

    avg = avg_ref[...]
    qg = qg_ref[...]
    kg = kg_ref[...]
    q_scale = HEAD_DIM ** -0.5

    h = h_ref[...]
    cos = cos_ref[...]
    sin = sin_ref[...]

    def proj(c0, width):
        return (jnp.dot(h, w_ref[:, c0:c0 + width], preferred_element_type=F32)
                + b_ref[:, c0:c0 + width])

    def store_plain(cols):
        for c0 in cols:
            p_ref[:, c0:c0 + WIDTH] = proj(c0, WIDTH).astype(BF16)

    plain = list(range(C_GATE, C_GATE + 3 * D_MODEL, WIDTH)) + [C_MLQ, C_MLV, C_MLO]
    raw_gq, raw_gk = proj(C_GQ, WIDTH), proj(C_GK, 256)
    raw_wq, raw_wk = proj(C_WQ, WIDTH), proj(C_WK, 256)
    store_plain(plain[:3])
    gq = _rope(_head_rms(raw_gq, avg, qg), cos, sin, first_half(WIDTH)) * (q_scale * LOG2E)
    p_ref[:, C_GQ:C_GQ + WIDTH] = gq.astype(BF16)
    gk = _rope(_head_rms(raw_gk, avg, kg), cos, sin, first_half(256))
    p_ref[:, C_GK:C_GK + 256] = gk.astype(BF16)
    p_ref[:, C_WQ:C_WQ + WIDTH] = (_rope(raw_wq, cos, sin, first_half(WIDTH)) * (q_scale * LOG2E)).astype(BF16)
    p_ref[:, C_WK:C_WK + 256] = _rope(raw_wk, cos, sin, first_half(256)).astype(BF16)
    store_plain(plain[3:])

    def proj_t(wt_ref, bt_ref):
        return lax.dot_general(wt_ref[...], h_ref[...], NT_DIMS, preferred_element_type=F32) + bt_ref[...]

    gta = proj_t(wgt_ref, bgt_ref)
    grow = lax.broadcasted_iota(jnp.int32, (2 * N_GATE, 1), 0)
    gt_ref[...] = jnp.where(grow >= N_GATE, _log_sigmoid(gta), gta)
    kt_ref[...] = proj_t(wkt_ref, bkt_ref).astype(BF16)
    vt_ref[...] = proj_t(wvt_ref, bvt_ref).astype(BF16)


def _token_tile_specs(xs, tm, lat_tiles):
    if len(xs) == 1:
        return [pl.BlockSpec((tm, D_MODEL), lambda i: (i, 0))]
    ctx_tiles = xs[1].shape[0] // tm
    return [pl.BlockSpec((tm, D_MODEL), lambda i: (jnp.minimum(i, lat_tiles - 1), 0)),
            pl.BlockSpec((tm, D_MODEL), lambda i: (jnp.clip(i - lat_tiles, 0, ctx_tiles - 1), 0))]


def _in_projection(xs, modtab, g1, w, b, wgt, bgt, wkt, bkt, wvt, bvt, cos, sin, qg, kg, avg, batch, n_lat):
    rows = sum(x.shape[0] for x in xs)
    tm = ROW_TILE
    nb = tm // ROW_BLOCK
    kv = 2 * KV_HEADS * HEAD_DIM
    seq_tiles = n_lat // tm
    lat_tiles = batch * seq_tiles

    def rope_block(i):
        return (jnp.where(i < lat_tiles, i % seq_tiles, seq_tiles), 0)

    return pl.pallas_call(
        functools.partial(_inproj_kernel, lat_tiles=lat_tiles if len(xs) == 2 else None),
        out_shape=(jax.ShapeDtypeStruct((rows, N_PROJ), BF16),
                   jax.ShapeDtypeStruct((2 * N_GATE, rows), F32),
                   jax.ShapeDtypeStruct((WIDTH, rows), BF16),
                   jax.ShapeDtypeStruct((kv, rows), BF16)),
        grid=(rows // tm,),
        in_specs=_token_tile_specs(xs, tm, lat_tiles) + [
                  pl.BlockSpec((nb, 6, D_MODEL), lambda i: (i, 0, 0)),
                  _resident((1, D_MODEL)),
                  _resident((D_MODEL, N_PROJ)),
                  _resident((1, N_PROJ)),
                  _resident((2 * N_GATE, D_MODEL)),
                  _resident((2 * N_GATE, 1)),
                  _resident((WIDTH, D_MODEL)),
                  _resident((WIDTH, 1)),
                  _resident((kv, D_MODEL)),
                  _resident((kv, 1)),
                  pl.BlockSpec((tm, LANES), rope_block),
                  pl.BlockSpec((tm, LANES), rope_block),
                  _resident((1, LANES)),
                  _resident((1, LANES)),
                  _resident((LANES, LANES))],
        out_specs=(pl.BlockSpec((tm, N_PROJ), lambda i: (i, 0)),
                   pl.BlockSpec((2 * N_GATE, tm), lambda i: (0, i)),
                   pl.BlockSpec((WIDTH, tm), lambda i: (0, i)),
                   pl.BlockSpec((kv, tm), lambda i: (0, i))),
        scratch_shapes=[pltpu.VMEM((tm, D_MODEL), BF16)],
        compiler_params=_params(("parallel",)),
        name="in_projection",
    )(*xs, modtab, g1, w, b, wgt, bgt, wkt, bkt, wvt, bvt, cos, sin, qg, kg, avg)


def _scan_lanes(x, op, fill, reverse):
    n = x.shape[1]
    lane = lax.broadcasted_iota(jnp.int32, x.shape, 1)
    sh = 1
    while sh < n:
        if reverse:
            moved = jnp.where(lane < n - sh, pltpu.roll(x, n - sh, axis=1), fill)
        else:
            moved = jnp.where(lane >= sh, pltpu.roll(x, sh, axis=1), fill)
        x = op(x, moved)
        sh *= 2
    return x


def _mlstm_kernel(qvf_ref, ktf_ref, gtf_ref, qvb_ref, ktb_ref, gtb_ref, hf_ref, hb_ref, cn_ref, m_ref):
    @pl.when(pl.program_id(1) == 0)
    def _():
        cn_ref[...] = jnp.zeros_like(cn_ref)
        m_ref[...] = jnp.zeros_like(m_ref)

    L = CHUNK
    row = lax.broadcasted_iota(jnp.int32, (L, L), 0)
    col = lax.broadcasted_iota(jnp.int32, (L, L), 1)
    scale = ML_DIM ** -0.5
    ones_v = jnp.ones((L, ML_DIM), BF16)
    pending = []
    for d, (qv_ref, kt_ref, gt_ref, h_ref) in enumerate(((qvf_ref, ktf_ref, gtf_ref, hf_ref),
                                                          (qvb_ref, ktb_ref, gtb_ref, hb_ref))):
        seen = (col <= row) if d == 0 else (col >= row)
        g0 = d * ML_HEADS
        ic = gt_ref[g0:g0 + ML_HEADS, :]
        lf = gt_ref[N_GATE + g0:N_GATE + g0 + ML_HEADS, :]
        r = ic - _scan_lanes(lf, jnp.add, 0.0, d == 1)
        r_max = jnp.max(r, axis=1, keepdims=True)
        b_end = jnp.sum(lf, axis=1, keepdims=True)
        for hh in range(ML_HEADS):
            idx = g0 + hh
            lanes = slice(hh * ML_DIM, (hh + 1) * ML_DIM)
            r_row = r[hh:hh + 1, :]
            m_old = m_ref[idx]
            m_end = jnp.maximum(m_old, r_max[hh:hh + 1, :])
            q = qv_ref[:, lanes]
            vo = jnp.concatenate([qv_ref[:, WIDTH + hh * ML_DIM:WIDTH + (hh + 1) * ML_DIM], ones_v], axis=1)
            kt = kt_ref[lanes, :]
            qsb = (q.astype(F32) * scale).astype(BF16)
            cn = cn_ref[idx]
            s_raw = jnp.dot(qsb, kt, preferred_element_type=F32)
            q_cn = jnp.dot(qsb, cn.astype(BF16), preferred_element_type=F32)
            kwt = (kt.astype(F32) * jnp.exp(r_row - m_end)).astype(BF16)
            cn_ref[idx] = jnp.exp(m_old - m_end) * cn + jnp.dot(kwt, vo, preferred_element_type=F32)
            m_ref[idx] = b_end[hh:hh + 1, :] + m_end
            pending.append((h_ref, lanes, seen, r_row, lf[hh:hh + 1, :], m_old, s_raw, q_cn, vo))

    for h_ref, lanes, seen, r_row, lf_row, m_old, s_raw, q_cn, vo in pending:
        b_col = jnp.sum(jnp.where(seen, lf_row, 0.0), axis=1, keepdims=True)
        m_col = jnp.maximum(m_old, jnp.max(jnp.where(seen, r_row, -jnp.inf), axis=1, keepdims=True))
        w = jnp.exp(jnp.where(seen, r_row - m_col, -jnp.inf))
        a = jnp.exp(m_old - m_col)
        s_vo = jnp.dot((s_raw * w).astype(BF16), vo, preferred_element_type=F32)
        num = a * q_cn[:, :ML_DIM] + s_vo[:, :ML_DIM]
        den = a * q_cn[:, ML_DIM:] + s_vo[:, ML_DIM:]
        hc = num / jnp.maximum(jnp.abs(den), jnp.exp(-(b_col + m_col)))
        h_ref[:, lanes] = hc.astype(h_ref.dtype)


def _mlstm(p, kt, gt, batch, n_lat, n_ctx):
    rows = p.shape[0]
    cl, cc = n_lat // CHUNK, n_ctx // CHUNK
    lat_base, ctx_base = 0, batch * cl

    def fwd_chunk(b, i):
        return jnp.where(i < cc, ctx_base + b * cc + i, lat_base + b * cl + (i - cc))

    def bwd_chunk(b, i):
        return jnp.where(i < cc, ctx_base + b * cc + (cc - 1 - i), lat_base + b * cl + (cl - 1 - (i - cc)))

    def specs(chunk):
        return [pl.BlockSpec((CHUNK, 2 * WIDTH), lambda b, i: (chunk(b, i), C_MLQ // (2 * WIDTH))),
                pl.BlockSpec((WIDTH, CHUNK), lambda b, i: (0, chunk(b, i))),
                pl.BlockSpec((2 * N_GATE, CHUNK), lambda b, i: (0, chunk(b, i)))]

    n_state = 2 * ML_HEADS
    return pl.pallas_call(
        _mlstm_kernel,
        out_shape=(jax.ShapeDtypeStruct((rows, WIDTH), BF16), jax.ShapeDtypeStruct((rows, WIDTH), BF16)),
        grid=(batch, cl + cc),
        in_specs=specs(fwd_chunk) + specs(bwd_chunk),
        out_specs=(pl.BlockSpec((CHUNK, WIDTH), lambda b, i: (fwd_chunk(b, i), 0)),
                   pl.BlockSpec((CHUNK, WIDTH), lambda b, i: (bwd_chunk(b, i), 0))),
        scratch_shapes=[pltpu.VMEM((n_state, ML_DIM, 2 * ML_DIM), F32),
                        pltpu.VMEM((n_state, 1, 1), F32)],
        compiler_params=_params(("parallel", "arbitrary")),
        name="mlstm_scan",
    )(p, kt, gt, p, kt, gt)


def _half_mask(e):
    lane = lax.broadcasted_iota(jnp.int32, (1, LANES), 1)
    return (lane < HEAD_DIM) if e == 0 else (lane >= HEAD_DIM)


def _win_kernel(sink_ref, q_ref, kc_ref, vtc_ref, k0_ref, k1_ref, k2_ref, k3_ref,
                vt0_ref, vt1_ref, vt2_ref, vt3_ref, o_ref, *, n_lat_tiles, n_lat_blocks):
    t = pl.program_id(1)
    tq = q_ref.shape[0]
    n_ctx = kc_ref.shape[0]
    is_lat = t < n_lat_tiles
    kk = lax.broadcasted_iota(jnp.int32, (CHUNK, tq), 0)
    qq = lax.broadcasted_iota(jnp.int32, (CHUNK, tq), 1)
    band = (kk >= qq, qq <= kk + CHUNK, kk <= qq, kk + CHUNK <= qq)
    first = 2 * t - 1
    present = [jnp.logical_and(is_lat, jnp.logical_and(first + i >= 0, first + i < n_lat_blocks))
               for i in range(4)]
    visible = [jnp.logical_and(band[i], present[i]) for i in range(4)]
    k_refs = (k0_ref, k1_ref, k2_ref, k3_ref)
    vt_refs = (vt0_ref, vt1_ref, vt2_ref, vt3_ref)
    ones = jnp.ones((16, n_ctx + 4 * CHUNK), BF16)

    def group_operands(g):
        lanes = slice(g * LANES, (g + 1) * LANES)
        k_all = jnp.concatenate([kc_ref[:, lanes]] + [r[:, lanes] for r in k_refs], axis=0)
        vrows = slice(g * HEAD_DIM, (g + 1) * HEAD_DIM)
        vta = jnp.concatenate(
            [jnp.concatenate([vtc_ref[vrows, :]] + [r[vrows, :] for r in vt_refs], axis=1), ones], axis=0)
        return k_all, vta

    operands = [group_operands(g) for g in range(KV_HEADS)]

    def score(h):
        qf = q_ref[:, (h // 2) * LANES:(h // 2 + 1) * LANES].astype(F32)
        qt = jnp.where(_half_mask(h % 2), qf, 0.0).T.astype(BF16)
        return jnp.dot(operands[h // (Q_HEADS // KV_HEADS)][0], qt, preferred_element_type=F32)

    scores = {h: score(h) for h in range(WIN_LEAD)}
    outs = []
    for h in range(Q_HEADS):
        if h + WIN_LEAD < Q_HEADS:
            scores[h + WIN_LEAD] = score(h + WIN_LEAD)
        s = scores.pop(h)
        sink = sink_ref[h] * LOG2E
        parts = [s[:n_ctx, :]]
        for i in range(4):
            blk = s[n_ctx + i * CHUNK:n_ctx + (i + 1) * CHUNK, :]
            parts.append(jnp.where(visible[i], blk, -jnp.inf))
        s = jnp.concatenate(parts, axis=0)
        m = jnp.maximum(jnp.max(s, axis=0, keepdims=True), sink)
        pt = jnp.exp2(s - m).astype(BF16)
        ol = jnp.dot(operands[h // (Q_HEADS // KV_HEADS)][1], pt, preferred_element_type=F32)
        den = ol[HEAD_DIM:HEAD_DIM + 1, :] + jnp.exp2(sink - m)
        outs.append(ol[:HEAD_DIM, :] / den)
    for tt in range(Q_HEADS // 2):
        o_ref[:, tt * LANES:(tt + 1) * LANES] = (
            jnp.concatenate(outs[2 * tt:2 * tt + 2], axis=0).T.astype(o_ref.dtype))


def _window_attention(p, vt, sink, batch, n_lat, n_ctx, ctx_queries):
    tq = 256
    nlt, nct = n_lat // tq, n_ctx // tq
    nlb = n_lat // CHUNK
    nq = nlt + (nct if ctx_queries else 0)
    out_rows = batch * (n_lat + (n_ctx if ctx_queries else 0))
    ctx_blk = batch * n_lat // n_ctx
    k_col = C_WK // 256

    def q_block(b, t):
        return jnp.where(t < nlt, b * nlt + t, batch * nlt + b * nct + (t - nlt))

    def near(i):
        return lambda b, t: b * nlb + jnp.clip(2 * t - 1 + i, 0, nlb - 1)

    k_specs = [pl.BlockSpec((CHUNK, 256), (lambda f: lambda b, t: (f(b, t), k_col))(near(i))) for i in range(4)]
    vt_specs = [pl.BlockSpec((LANES, CHUNK), (lambda f: lambda b, t: (VT_WINDOW, f(b, t)))(near(i)))
                for i in range(4)]
    return pl.pallas_call(
        functools.partial(_win_kernel, n_lat_tiles=nlt, n_lat_blocks=nlb),
        out_shape=jax.ShapeDtypeStruct((out_rows, WIDTH), BF16),
        grid=(batch, nq),
        in_specs=[pl.BlockSpec(memory_space=pltpu.SMEM),
                  pl.BlockSpec((tq, WIDTH), lambda b, t: (q_block(b, t), C_WQ // WIDTH)),
                  pl.BlockSpec((n_ctx, 256), lambda b, t: (ctx_blk + b, k_col)),
                  pl.BlockSpec((LANES, n_ctx), lambda b, t: (VT_WINDOW, ctx_blk + b))]
                 + k_specs + vt_specs,
        out_specs=pl.BlockSpec((tq, WIDTH), lambda b, t: (q_block(b, t), 0)),
        compiler_params=_params(("parallel", "parallel")),
        name="window_attention",
    )(sink, p, p, vt, p, p, p, p, vt, vt, vt, vt)


def _glb_kernel(q_ref, kc_ref, vtc_ref, kl_ref, vtl_ref, o_ref, qt_ref, knorm_ref, *scratch, n_lat_tiles, tk, span_chunks):
    heads = Q_HEADS // KV_HEADS
    acc_refs = scratch[0:heads]
    sc_refs = scratch[heads:2 * heads]
    ring = [scratch[(2 + i) * heads:(3 + i) * heads] for i in range(N_SCORE_BUFS)]
    base = (2 + N_SCORE_BUFS) * heads
    pc_refs = scratch[base:base + heads]
    p_ring = scratch[base + heads:base + heads + 2]
    qi = pl.program_id(2)
    tq = q_ref.shape[0]
    n_chunks = kl_ref.shape[0] // tk

    def k_chunk(c):
        off = pl.multiple_of(jnp.minimum(c, n_chunks - 1) * tk, tk)
        return kl_ref[pl.ds(off, tk), :]

    def vt_chunk(c):
        return vtl_ref[:, pl.ds(pl.multiple_of(c * tk, tk), tk)]

    def with_ones(vt):
        ones = jnp.ones((acc_refs[0].shape[0] - HEAD_DIM, vt.shape[1]), BF16)
        return jnp.concatenate([vt, ones], axis=0)

    @pl.when(qi == 0)
    def _():
        def sq_norm(k):
            kf = k.astype(F32)
            return jnp.max(jnp.sum(kf * kf, axis=1, keepdims=True), axis=0, keepdims=True)

        def body(c, best):
            return jnp.maximum(best, sq_norm(k_chunk(c)))
        best = lax.fori_loop(0, n_chunks, body, sq_norm(kc_ref[...]))
        knorm_ref[...] = jnp.sqrt(0.5 * best)

    for t in range(heads // 2):
        qf = q_ref[:, t * LANES:(t + 1) * LANES].astype(F32)
        for e in range(2):
            h = 2 * t + e
            qt_ref[:, h * tq:(h + 1) * tq] = jnp.where(_half_mask(e), qf, 0.0).T.astype(BF16)
    for h in range(heads):
        acc_refs[h][...] = jnp.zeros_like(acc_refs[h])

    def q_t(h):
        return qt_ref[:, h * tq:(h + 1) * tq]

    bound = [jnp.sqrt(jnp.sum(jnp.square(q_t(h).astype(F32)), axis=0, keepdims=True)) * knorm_ref[...]
             for h in range(heads)]
    bounded = jnp.max(functools.reduce(jnp.maximum, bound)) <= SAFE_LOG2_BOUND

    span = span_chunks * tk
    n_spans = n_chunks // span_chunks
    n_steps = n_spans * heads

    def produce_ctx():
        for h in range(heads):
            s = jnp.dot(kc_ref[...], q_t(h), preferred_element_type=F32)
            pc_refs[h][...] = jnp.exp2(s - bound[h]).astype(BF16)

    def consume_ctx():
        vta = with_ones(vtc_ref[...])
        for h in range(heads):
            acc_refs[h][...] += jnp.dot(vta, pc_refs[h][...], preferred_element_type=F32)

    def produce(p_ref, sp, h):
        k = kl_ref[pl.ds(pl.multiple_of(sp * span, span), span), :]
        s = jnp.dot(k, q_t(h), preferred_element_type=F32)
        p_ref[...] = jnp.exp2(s - bound[h]).astype(BF16)

    def consume(p_ref, sp, h):
        vta = with_ones(vtl_ref[:, pl.ds(pl.multiple_of(sp * span, span), span)])
        acc_refs[h][...] += jnp.dot(vta, p_ref[...], preferred_element_type=F32)

    @pl.when(bounded)
    def _():
        produce_ctx()

        @pl.when(qi >= n_lat_tiles)
        def _():
            consume_ctx()

        @pl.when(qi < n_lat_tiles)
        def _():
            produce(p_ring[0], 0, 0)
            consume_ctx()

            unroll = heads * (2 if n_spans % 2 == 0 else 1)

            def steps(j, last):
                for i in range(unroll):
                    step = unroll * j + i
                    if not (last and i == unroll - 1):
                        produce(p_ring[(i + 1) % 2], (step + 1) // heads, (i + 1) % heads)
                    consume(p_ring[i % 2], step // heads, i % heads)

            def body(j, carry):
                steps(j, False)
                return carry
            n_iter = n_steps // unroll
            lax.fori_loop(0, n_iter - 1, body, 0)
            steps(n_iter - 1, True)

    neg_inf = (jnp.full((1, tq), -jnp.inf, F32),) * heads

    def fill(s_refs, k, m_run):
        out = []
        for h in range(heads):
            s = jnp.dot(k, q_t(h), preferred_element_type=F32)
            s_refs[h][...] = s
            out.append(jnp.maximum(m_run[h], jnp.max(s, axis=0, keepdims=True)))
        return tuple(out)

    def drain(s_refs, vt, m_run, m_acc):
        vta = with_ones(vt)
        for h in range(heads):
            alpha = jnp.exp2(m_acc[h] - m_run[h])
            pt = jnp.exp2(s_refs[h][...] - m_run[h]).astype(BF16)
            acc_refs[h][...] = alpha * acc_refs[h][...] + jnp.dot(vta, pt, preferred_element_type=F32)
        return m_run

    @pl.when(jnp.logical_not(bounded))
    def _():
        m_ctx = fill(sc_refs, kc_ref[...], neg_inf)

        @pl.when(qi >= n_lat_tiles)
        def _():
            drain(sc_refs, vtc_ref[...], m_ctx, neg_inf)

        @pl.when(qi < n_lat_tiles)
        def _():
            m_one = fill(ring[0], k_chunk(0), m_ctx)
            m_two = fill(ring[1], k_chunk(1), m_one)
            m_acc = drain(sc_refs, vtc_ref[...], m_one, neg_inf)

            def body(j, carry):
                m_run, m_acc = carry
                for i in range(N_SCORE_BUFS):
                    c = N_SCORE_BUFS * j + i
                    m_next = fill(ring[(i + 2) % N_SCORE_BUFS], k_chunk(c + 2), m_run)
                    m_acc = drain(ring[i], vt_chunk(c), m_run, m_acc)
                    m_run = m_next
                return m_run, m_acc
            lax.fori_loop(0, n_chunks // N_SCORE_BUFS, body, (m_two, m_acc))

    for t in range(heads // 2):
        pair = []
        for e in range(2):
            a = acc_refs[2 * t + e][...]
            pair.append(a[:HEAD_DIM, :] / a[HEAD_DIM:HEAD_DIM + 1, :])
        o_ref[:, t * LANES:(t + 1) * LANES] = jnp.concatenate(pair, axis=0).T.astype(o_ref.dtype)


def _global_attention(p, vt, batch, n_lat, n_ctx, ctx_queries):
    tq = 256
    tk = min(512, n_lat // N_SCORE_BUFS)
    assert n_lat % (tk * N_SCORE_BUFS) == 0 and tk % LANES == 0
    span_chunks = math.gcd(SPAN_CHUNKS, n_lat // tk)
    nl, nc = n_lat // tq, n_ctx // tq
    nq = nl + (nc if ctx_queries else 0)
    out_rows = batch * (n_lat + (n_ctx if ctx_queries else 0))
    ctx_base = batch * nl
    heads = Q_HEADS // KV_HEADS
    acc_rows = HEAD_DIM + 16

    def q_block(b, qi):
        return jnp.where(qi < nl, b * nl + qi, ctx_base + b * nc + (qi - nl))

    ctx_blk = batch * n_lat // n_ctx
    return pl.pallas_call(
        functools.partial(_glb_kernel, n_lat_tiles=nl, tk=tk, span_chunks=span_chunks),
        out_shape=jax.ShapeDtypeStruct((out_rows, WIDTH), BF16),
        grid=(batch, KV_HEADS, nq),
        in_specs=[pl.BlockSpec((tq, 2 * LANES), lambda b, g, qi: (q_block(b, qi), C_GQ // 256 + g)),
                  pl.BlockSpec((n_ctx, LANES), lambda b, g, qi: (ctx_blk + b, C_GK // LANES + g)),
                  pl.BlockSpec((HEAD_DIM, n_ctx), lambda b, g, qi: (KV_HEADS * VT_GLOBAL + g, ctx_blk + b)),
                  pl.BlockSpec((n_lat, LANES), lambda b, g, qi: (b, C_GK // LANES + g)),
                  pl.BlockSpec((HEAD_DIM, n_lat), lambda b, g, qi: (KV_HEADS * VT_GLOBAL + g, b))],
        out_specs=pl.BlockSpec((tq, 2 * LANES), lambda b, g, qi: (q_block(b, qi), g)),
        scratch_shapes=([pltpu.VMEM((LANES, heads * tq), BF16), pltpu.VMEM((1, 1), F32)]
                        + [pltpu.VMEM((acc_rows, tq), F32)] * heads
                        + [pltpu.VMEM((n_ctx, tq), F32)] * heads
                        + [pltpu.VMEM((tk, tq), F32)] * (N_SCORE_BUFS * heads)
                        + [pltpu.VMEM((n_ctx, tq), BF16)] * heads
                        + [pltpu.VMEM((span_chunks * tk, tq), BF16)] * 2),
        compiler_params=_params(("parallel", "parallel", "arbitrary")),
        name="global_attention",
    )(p, p, vt, p, vt)


def _merge_kernel(*refs, lat_tiles):
    n_x = 1 if lat_tiles is None else 2
    x_refs = refs[:n_x]
    (mod_ref, hf_ref, hb_ref, og_ref, yb_ref, yc_ref, ga_ref, gb_ref, gc_ref,
     mlg_ref, wbr_ref, wo_ref, o_ref) = refs[n_x:]
    tm = o_ref.shape[0]
    proj_b = jnp.dot(yb_ref[...], wbr_ref[1], preferred_element_type=F32)
    proj_c = jnp.dot(yc_ref[...], wbr_ref[2], preferred_element_type=F32)
    hs = hf_ref[...].astype(F32) + hb_ref[...].astype(F32)
    parts = []
    for t in range(ML_HEADS):
        ht = hs[:, t * ML_DIM:(t + 1) * ML_DIM]
        ms = jnp.mean(ht * ht, axis=-1, keepdims=True)
        parts.append(ht * lax.rsqrt(ms + EPS))
    ya = (jnp.concatenate(parts, axis=1) * mlg_ref[...] * _sigmoid(og_ref[...].astype(F32))).astype(BF16)
    proj_a = jnp.dot(ya, wbr_ref[0], preferred_element_type=F32)
    merged = (_sigmoid(gb_ref[...].astype(F32)) * proj_b + _sigmoid(gc_ref[...].astype(F32)) * proj_c
              + _sigmoid(ga_ref[...].astype(F32)) * proj_a)
    out = jnp.dot(merged.astype(BF16), wo_ref[...], preferred_element_type=F32)

    def gated_residual(x_ref):
        for sb in range(tm // ROW_BLOCK):
            rows = slice(sb * ROW_BLOCK, (sb + 1) * ROW_BLOCK)
            o_ref[rows, :] = x_ref[rows, :] + mod_ref[sb, 2:3, :] * out[rows, :]

    if lat_tiles is None:
        gated_residual(x_refs[0])
    else:
        pl.when(pl.program_id(0) < lat_tiles)(lambda: gated_residual(x_refs[0]))
        pl.when(pl.program_id(0) >= lat_tiles)(lambda: gated_residual(x_refs[1]))


def _merge(xs, modtab, p, hf, hb, yb, yc, mlg, wbr, wo, n_rows, lat_tiles):
    tm = ROW_TILE
    nb = tm // ROW_BLOCK
    rows = sum(x.shape[0] for x in xs)
    row = lambda i: (i, 0)
    gate = lambda k: pl.BlockSpec((tm, D_MODEL), lambda i: (i, C_GATE // D_MODEL + k))
    return pl.pallas_call(
        functools.partial(_merge_kernel, lat_tiles=lat_tiles if len(xs) == 2 else None),
        out_shape=jax.ShapeDtypeStruct((rows, D_MODEL), F32),
        grid=(n_rows // tm,),
        in_specs=_token_tile_specs(xs, tm, lat_tiles) + [
                  pl.BlockSpec((nb, 6, D_MODEL), lambda i: (i, 0, 0)),
                  pl.BlockSpec((tm, WIDTH), row),
                  pl.BlockSpec((tm, WIDTH), row),
                  pl.BlockSpec((tm, WIDTH), lambda i: (i, C_MLO // WIDTH)),
                  pl.BlockSpec((tm, WIDTH), row),
                  pl.BlockSpec((tm, WIDTH), row),
                  gate(0), gate(1), gate(2),
                  _resident((1, WIDTH)),
                  _resident((3, WIDTH, D_MODEL)),
                  _resident((D_MODEL, D_MODEL))],
        out_specs=pl.BlockSpec((tm, D_MODEL), row),
        input_output_aliases={0: 0} if len(xs) == 1 else {},
        compiler_params=_params(("parallel",)),
        name="merge",
    )(*xs, modtab, hf, hb, p, yb, yc, p, p, p, mlg, wbr, wo)


def _ffn_kernel(x_ref, mod_ref, g2_ref, w1_ref, w3_ref, w2_ref, gf_ref, o_ref, *, final):
    tm = x_ref.shape[0]
    d_ff = w1_ref.shape[1]
    split = pl.cdiv(d_ff // MXU_TILE, 2) * MXU_TILE
    for sb in range(tm // ROW_BLOCK):
        rows = slice(sb * ROW_BLOCK, (sb + 1) * ROW_BLOCK)
        xs = x_ref[rows, :]
        ms = jnp.mean(xs * xs, axis=-1, keepdims=True)
        y = xs * lax.rsqrt(ms + EPS) * g2_ref[...]
        h = (y * (1.0 + mod_ref[sb, 4:5, :]) + mod_ref[sb, 3:4, :]).astype(BF16)
        out = None
        for c0, c1 in ((0, split), (split, d_ff)):
            a = jnp.dot(h, w1_ref[:, c0:c1], preferred_element_type=F32)
            b = jnp.dot(h, w3_ref[:, c0:c1], preferred_element_type=F32)
            z = (a * _sigmoid(a) * b).astype(BF16)
            part = jnp.dot(z, w2_ref[c0:c1, :], preferred_element_type=F32)
            out = part if out is None else out + part
        xn = xs + mod_ref[sb, 5:6, :] * out
        if final:
            ms = jnp.mean(xn * xn, axis=-1, keepdims=True)
            xn = xn * lax.rsqrt(ms + EPS) * gf_ref[...]
        o_ref[rows, :] = xn


def _ffn(x, modtab, g2, w1, w3, w2, gfin, n_rows, final):
    tm = ROW_TILE
    nb = tm // ROW_BLOCK
    d_ff = w1.shape[1]
    row = lambda i: (i, 0)
    out_rows = n_rows if final else x.shape[0]
    return pl.pallas_call(
        functools.partial(_ffn_kernel, final=final),
        out_shape=jax.ShapeDtypeStruct((out_rows, D_MODEL), F32),
        grid=(n_rows // tm,),
        in_specs=[pl.BlockSpec((tm, D_MODEL), row),
                  pl.BlockSpec((nb, 6, D_MODEL), lambda i: (i, 0, 0)),
                  _resident((1, D_MODEL)),
                  _resident((D_MODEL, d_ff)),
                  _resident((D_MODEL, d_ff)),
                  _resident((d_ff, D_MODEL)),
                  _resident((1, D_MODEL))],
        out_specs=pl.BlockSpec((tm, D_MODEL), row),
        input_output_aliases={} if final else {0: 0},
        compiler_params=_params(("parallel",)),
        name="ffn",
    )(x, modtab, g2, w1, w3, w2, gfin)


def _dup_halves(w, base):
    h0 = w[..., base:base + HEAD_DIM]
    h1 = w[..., base + HEAD_DIM:base + 2 * HEAD_DIM]
    return [h0, h0, h1, h1]


def _arrange_in_proj(w):
    o_gate_ml = 4 * WIDTH
    o_wq = o_gate_ml + 2 * N_GATE
    o_wk, o_wv = o_wq + WIDTH, o_wq + WIDTH + 128
    o_gq = o_wv + 128
    o_gk, o_gv = o_gq + WIDTH, o_gq + WIDTH + 128
    o_gate = o_gv + 128
    main = jnp.concatenate(
        [w[..., o_gate:o_gate + 3 * D_MODEL], w[..., 0:WIDTH], w[..., 2 * WIDTH:4 * WIDTH],
         w[..., o_wq:o_wq + WIDTH]]
        + _dup_halves(w, o_wk) + [w[..., o_gq:o_gq + WIDTH]] + _dup_halves(w, o_gk), axis=-1)
    values = jnp.concatenate([w[..., o_gv:o_gv + 128], w[..., o_wv:o_wv + 128]], axis=-1)
    return main, w[..., o_gate_ml:o_gate_ml + 2 * N_GATE], w[..., WIDTH:2 * WIDTH], values


def _rope_tables(n_lat):
    t = jnp.arange(n_lat)
    quarter = HEAD_DIM // 4
    inv = ROPE_THETA ** (-jnp.arange(0, 2 * quarter, 2, dtype=F32) / (2 * quarter))
    ang_r = (t // GRID_W).astype(F32)[:, None] * inv
    ang_c = (t % GRID_W).astype(F32)[:, None] * inv
    cos = jnp.concatenate([jnp.cos(ang_r)] * 2 + [jnp.cos(ang_c)] * 2, axis=1)
    sin = jnp.concatenate([-jnp.sin(ang_r), jnp.sin(ang_r), -jnp.sin(ang_c), jnp.sin(ang_c)], axis=1)
    reps = LANES // HEAD_DIM
    return (jnp.concatenate([jnp.tile(cos, (1, reps)), jnp.ones((ROW_TILE, LANES), F32)], axis=0),
            jnp.concatenate([jnp.tile(sin, (1, reps)), jnp.zeros((ROW_TILE, LANES), F32)], axis=0))


def kernel(x, c, ctx, c_ctx, w_mod, b_mod, norm1_g, w_in, b_in, ml_norm_g, win_sink, qn_g, kn_g,
           w_br, w_o, norm2_g, w_ff1, w_ff3, w_ff2, final_g):
    batch, n_lat, d = x.shape
    n_ctx = ctx.shape[1]
    depth = w_mod.shape[0]
    assert d == D_MODEL and n_lat % ROW_TILE == 0 and n_lat % GRID_W == 0
    assert (batch * n_ctx) % ROW_TILE == 0 and n_ctx % ROW_BLOCK == 0 and (batch * n_lat) % n_ctx == 0
    lat_rows, ctx_rows = batch * n_lat, batch * n_ctx

    xs = (x.reshape(lat_rows, d), ctx.reshape(ctx_rows, d))
    cvec = jnp.concatenate([c, c_ctx[None, :], jnp.zeros((8 - batch - 1, d), F32)], axis=0)
    block_class = np.concatenate([np.repeat(np.arange(batch), n_lat // ROW_BLOCK),
                                  np.full(ctx_rows // ROW_BLOCK, batch)])
    cos, sin = _rope_tables(n_lat)
    lat_tiles = lat_rows // ROW_TILE
    avg = jnp.asarray(np.kron(np.eye(LANES // HEAD_DIM), np.full((HEAD_DIM, HEAD_DIM), 1.0 / HEAD_DIM)), BF16)

    out = None
    for l in range(depth):
        last = l == depth - 1
        mod = _mod_vectors(cvec, w_mod[l], b_mod[l]).reshape(8, 6, d)
        modtab = mod[block_class]

        w_main, w_gate, w_mlk, w_val = _arrange_in_proj(w_in[l])
        b_main, b_gate, b_mlk, b_val = _arrange_in_proj(b_in[l][None, :])
        p, gt, kt, vt = _in_projection(
            xs, modtab, norm1_g[l][None, :], w_main.astype(BF16), b_main,
            w_gate.T.astype(BF16), b_gate.T, w_mlk.T.astype(BF16), b_mlk.T, w_val.T.astype(BF16), b_val.T,
            cos, sin, jnp.tile(qn_g[l], 2)[None, :], jnp.tile(kn_g[l], 2)[None, :], avg, batch, n_lat)

        hf, hb = _mlstm(p, kt, gt, batch, n_lat, n_ctx)
        yb = _window_attention(p, vt, win_sink[l], batch, n_lat, n_ctx, ctx_queries=not last)
        yc = _global_attention(p, vt, batch, n_lat, n_ctx, ctx_queries=not last)

        n_rows = lat_rows if last else lat_rows + ctx_rows
        merged = _merge(xs, modtab, p, hf, hb, yb, yc, ml_norm_g[l].reshape(1, WIDTH),
                        w_br[l].astype(BF16), w_o[l].astype(BF16), n_rows, lat_tiles)
        out = _ffn(merged, modtab, norm2_g[l][None, :], w_ff1[l].astype(BF16), w_ff3[l].astype(BF16),
                   w_ff2[l].astype(BF16), final_g[None, :], n_rows, final=last)
        xs = (out,)
    return out.reshape(batch, n_lat, d)
```

```python
import functools
import math

import jax
import jax.numpy as jnp
import numpy as np
from jax import lax
from jax.experimental import pallas as pl
from jax.experimental.pallas import tpu as pltpu

F32 = jnp.float32
BF16 = jnp.bfloat16

D_MODEL = 1024
GRID_W = 64
CHUNK = 128
HEAD_DIM = 64
ROPE_THETA = 10000.0
EPS = 1e-6
ML_HEADS = 4
ML_DIM = 128
Q_HEADS = 8
KV_HEADS = 2
WIDTH = 512
N_GATE = 2 * ML_HEADS

LANES = 128
MXU_TILE = 256
ROW_BLOCK = 256
ROW_TILE = 512
VMEM_LIMIT = 56 * 1024 * 1024

C_GATE = 0
C_MLQ, C_MLV, C_MLO = 3072, 3584, 4096
C_WQ, C_WK = 4608, 5120
C_GQ, C_GK = 5376, 5888
N_PROJ = C_GK + 256
VT_GLOBAL, VT_WINDOW = 0, 1
LOG2E = 1.4426950408889634
WIN_LEAD = 3
N_SCORE_BUFS = 4
SPAN_CHUNKS = 4
BOUNDED_UNROLL_SPANS = 8
SAFE_LOG2_BOUND = 50.0

NT_DIMS = (((1,), (1,)), ((), ()))


def _params(sem, vmem=VMEM_LIMIT):
    return pltpu.CompilerParams(dimension_semantics=sem, vmem_limit_bytes=vmem)


def _resident(shape):
    nd = len(shape)
    return pl.BlockSpec(shape, lambda *_: (0,) * nd, pipeline_mode=pl.Buffered(1))


def _sigmoid(x):
    return 1.0 / (1.0 + jnp.exp(-x))


def _log_sigmoid(x):
    return jnp.minimum(x, 0.0) - jnp.log(1.0 + jnp.exp(-jnp.abs(x)))


def _mod_kernel(c_ref, w_ref, b_ref, o_ref):
    c = c_ref[...]
    s = c * _sigmoid(c)
    o_ref[...] = jnp.dot(s, w_ref[...], preferred_element_type=F32) + b_ref[...]


def _mod_vectors(cvec, w_mod, b_mod):
    n_out = w_mod.shape[1]
    tn = 1536
    return pl.pallas_call(
        _mod_kernel,
        out_shape=jax.ShapeDtypeStruct((cvec.shape[0], n_out), F32),
        grid=(n_out // tn,),
        in_specs=[pl.BlockSpec(cvec.shape, lambda j: (0, 0)),
                  pl.BlockSpec((D_MODEL, tn), lambda j: (0, j)),
                  pl.BlockSpec((1, tn), lambda j: (0, j))],
        out_specs=pl.BlockSpec((cvec.shape[0], tn), lambda j: (0, j)),
        compiler_params=_params(("parallel",)),
        name="mod_vectors",
    )(cvec, w_mod, b_mod.reshape(1, n_out))


def _rope(acc, cos, sin, first_half):
    w = acc.shape[1]
    reps = w // LANES
    if reps > 1:
        cos = jnp.concatenate([cos] * reps, axis=1)
        sin = jnp.concatenate([sin] * reps, axis=1)
    ahead = pltpu.roll(acc, w - 16, axis=1)
    behind = pltpu.roll(acc, 16, axis=1)
    return acc * cos + jnp.where(first_half, ahead, behind) * sin


def _head_rms(acc, avg, gain):
    sq = (acc * acc).astype(BF16)
    outs = []
    for t in range(acc.shape[1] // LANES):
        sl = slice(t * LANES, (t + 1) * LANES)
        ms = jnp.dot(sq[:, sl], avg, preferred_element_type=F32)
        outs.append(acc[:, sl] * lax.rsqrt(ms + EPS) * gain)
    return jnp.concatenate(outs, axis=1)


def _inproj_kernel(*refs, lat_tiles):
    n_x = 1 if lat_tiles is None else 2
    x_refs = refs[:n_x]
    (mod_ref, g1_ref, w_ref, b_ref, wgt_ref, bgt_ref, wkt_ref, bkt_ref, wvt_ref, bvt_ref,
     cos_ref, sin_ref, qg_ref, kg_ref, avg_ref, p_ref, gt_ref, kt_ref, vt_ref, h_ref) = refs[n_x:]
    tm = h_ref.shape[0]

    def modulated_norm(x_ref):
        for sb in range(tm // ROW_BLOCK):
            r0 = sb * ROW_BLOCK
            xs = x_ref[r0:r0 + ROW_BLOCK, :]
            ms = jnp.mean(xs * xs, axis=-1, keepdims=True)
            y = xs * lax.rsqrt(ms + EPS) * g1_ref[...]
            shift = mod_ref[sb, 0:1, :]
            scale = mod_ref[sb, 1:2, :]
            h_ref[r0:r0 + ROW_BLOCK, :] = (y * (1.0 + scale) + shift).astype(BF16)

    if lat_tiles is None:
        modulated_norm(x_refs[0])
    else:
        pl.when(pl.program_id(0) < lat_tiles)(lambda: modulated_norm(x_refs[0]))
        pl.when(pl.program_id(0) >= lat_tiles)(lambda: modulated_norm(x_refs[1]))

    def first_half(width):
        return (lax.broadcasted_iota(jnp.int32, (1, width), 1) % 32) < 16

    avg = avg_ref[...]
    qg = qg_ref[...]
    kg = kg_ref[...]
    q_scale = HEAD_DIM ** -0.5

    h = h_ref[...]
    cos = cos_ref[...]
    sin = sin_ref[...]

    def proj(c0, width):
        return (jnp.dot(h, w_ref[:, c0:c0 + width], preferred_element_type=F32)
                + b_ref[:, c0:c0 + width])

    def store_plain(cols):
        for c0 in cols:
            p_ref[:, c0:c0 + WIDTH] = proj(c0, WIDTH).astype(BF16)

    plain = list(range(C_GATE, C_GATE + 3 * D_MODEL, WIDTH)) + [C_MLQ, C_MLV, C_MLO]
    raw_gq, raw_gk = proj(C_GQ, WIDTH), proj(C_GK, 256)
    raw_wq, raw_wk = proj(C_WQ, WIDTH), proj(C_WK, 256)
    store_plain(plain[:3])
    gq = _rope(_head_rms(raw_gq, avg, qg), cos, sin, first_half(WIDTH)) * (q_scale * LOG2E)
    p_ref[:, C_GQ:C_GQ + WIDTH] = gq.astype(BF16)
    gk = _rope(_head_rms(raw_gk, avg, kg), cos, sin, first_half(256))
    p_ref[:, C_GK:C_GK + 256] = gk.astype(BF16)
    p_ref[:, C_WQ:C_WQ + WIDTH] = (_rope(raw_wq, cos, sin, first_half(WIDTH)) * (q_scale * LOG2E)).astype(BF16)
    p_ref[:, C_WK:C_WK + 256] = _rope(raw_wk, cos, sin, first_half(256)).astype(BF16)
    store_plain(plain[3:])

    def proj_t(wt_ref, bt_ref):
        return lax.dot_general(wt_ref[...], h_ref[...], NT_DIMS, preferred_element_type=F32) + bt_ref[...]

    gta = proj_t(wgt_ref, bgt_ref)
    grow = lax.broadcasted_iota(jnp.int32, (2 * N_GATE, 1), 0)
    gt_ref[...] = jnp.where(grow >= N_GATE, _log_sigmoid(gta), gta)
    kt_ref[...] = proj_t(wkt_ref, bkt_ref).astype(BF16)
    vt_ref[...] = proj_t(wvt_ref, bvt_ref).astype(BF16)


def _token_tile_specs(xs, tm, lat_tiles):
    if len(xs) == 1:
        return [pl.BlockSpec((tm, D_MODEL), lambda i: (i, 0))]
    ctx_tiles = xs[1].shape[0] // tm
    return [pl.BlockSpec((tm, D_MODEL), lambda i: (jnp.minimum(i, lat_tiles - 1), 0)),
            pl.BlockSpec((tm, D_MODEL), lambda i: (jnp.clip(i - lat_tiles, 0, ctx_tiles - 1), 0))]


def _in_projection(xs, modtab, g1, w, b, wgt, bgt, wkt, bkt, wvt, bvt, cos, sin, qg, kg, avg, batch, n_lat):
    rows = sum(x.shape[0] for x in xs)
    tm = ROW_TILE
    nb = tm // ROW_BLOCK
    kv = 2 * KV_HEADS * HEAD_DIM
    seq_tiles = n_lat // tm
    lat_tiles = batch * seq_tiles

    def rope_block(i):
        return (jnp.where(i < lat_tiles, i % seq_tiles, seq_tiles), 0)

    return pl.pallas_call(
        functools.partial(_inproj_kernel, lat_tiles=lat_tiles if len(xs) == 2 else None),
        out_shape=(jax.ShapeDtypeStruct((rows, N_PROJ), BF16),
                   jax.ShapeDtypeStruct((2 * N_GATE, rows), F32),
                   jax.ShapeDtypeStruct((WIDTH, rows), BF16),
                   jax.ShapeDtypeStruct((kv, rows), BF16)),
        grid=(rows // tm,),
        in_specs=_token_tile_specs(xs, tm, lat_tiles) + [
                  pl.BlockSpec((nb, 6, D_MODEL), lambda i: (i, 0, 0)),
                  _resident((1, D_MODEL)),
                  _resident((D_MODEL, N_PROJ)),
                  _resident((1, N_PROJ)),
                  _resident((2 * N_GATE, D_MODEL)),
                  _resident((2 * N_GATE, 1)),
                  _resident((WIDTH, D_MODEL)),
                  _resident((WIDTH, 1)),
                  _resident((kv, D_MODEL)),
                  _resident((kv, 1)),
                  pl.BlockSpec((tm, LANES), rope_block),
                  pl.BlockSpec((tm, LANES), rope_block),
                  _resident((1, LANES)),
                  _resident((1, LANES)),
                  _resident((LANES, LANES))],
        out_specs=(pl.BlockSpec((tm, N_PROJ), lambda i: (i, 0)),
                   pl.BlockSpec((2 * N_GATE, tm), lambda i: (0, i)),
                   pl.BlockSpec((WIDTH, tm), lambda i: (0, i)),
                   pl.BlockSpec((kv, tm), lambda i: (0, i))),
        scratch_shapes=[pltpu.VMEM((tm, D_MODEL), BF16)],
        compiler_params=_params(("parallel",)),
        name="in_projection",
    )(*xs, modtab, g1, w, b, wgt, bgt, wkt, bkt, wvt, bvt, cos, sin, qg, kg, avg)


def _scan_lanes(x, op, fill, reverse):
    n = x.shape[1]
    lane = lax.broadcasted_iota(jnp.int32, x.shape, 1)
    sh = 1
    while sh < n:
        if reverse:
            moved = jnp.where(lane < n - sh, pltpu.roll(x, n - sh, axis=1), fill)
        else:
            moved = jnp.where(lane >= sh, pltpu.roll(x, sh, axis=1), fill)
        x = op(x, moved)
        sh *= 2
    return x


def _mlstm_kernel(qvf_ref, ktf_ref, gtf_ref, qvb_ref, ktb_ref, gtb_ref, hf_ref, hb_ref, cn_ref, m_ref):
    @pl.when(pl.program_id(1) == 0)
    def _():
        cn_ref[...] = jnp.zeros_like(cn_ref)
        m_ref[...] = jnp.zeros_like(m_ref)

    L = CHUNK
    row = lax.broadcasted_iota(jnp.int32, (L, L), 0)
    col = lax.broadcasted_iota(jnp.int32, (L, L), 1)
    scale = ML_DIM ** -0.5
    ones_v = jnp.ones((L, ML_DIM), BF16)
    pending = []
    for d, (qv_ref, kt_ref, gt_ref, h_ref) in enumerate(((qvf_ref, ktf_ref, gtf_ref, hf_ref),
                                                          (qvb_ref, ktb_ref, gtb_ref, hb_ref))):
        seen = (col <= row) if d == 0 else (col >= row)
        g0 = d * ML_HEADS
        ic = gt_ref[g0:g0 + ML_HEADS, :]
        lf = gt_ref[N_GATE + g0:N_GATE + g0 + ML_HEADS, :]
        r = ic - _scan_lanes(lf, jnp.add, 0.0, d == 1)
        r_max = jnp.max(r, axis=1, keepdims=True)
        b_end = jnp.sum(lf, axis=1, keepdims=True)
        for hh in range(ML_HEADS):
            idx = g0 + hh
            lanes = slice(hh * ML_DIM, (hh + 1) * ML_DIM)
            r_row = r[hh:hh + 1, :]
            m_old = m_ref[idx]
            m_end = jnp.maximum(m_old, r_max[hh:hh + 1, :])
            q = qv_ref[:, lanes]
            vo = jnp.concatenate([qv_ref[:, WIDTH + hh * ML_DIM:WIDTH + (hh + 1) * ML_DIM], ones_v], axis=1)
            kt = kt_ref[lanes, :]
            qsb = (q.astype(F32) * scale).astype(BF16)
            cn = cn_ref[idx]
            s_raw = jnp.dot(qsb, kt, preferred_element_type=F32)
            q_cn = jnp.dot(qsb, cn.astype(BF16), preferred_element_type=F32)
            kwt = (kt.astype(F32) * jnp.exp(r_row - m_end)).astype(BF16)
            cn_ref[idx] = jnp.exp(m_old - m_end) * cn + jnp.dot(kwt, vo, preferred_element_type=F32)
            m_ref[idx] = b_end[hh:hh + 1, :] + m_end
            pending.append((h_ref, lanes, seen, r_row, lf[hh:hh + 1, :], m_old, s_raw, q_cn, vo))

    for h_ref, lanes, seen, r_row, lf_row, m_old, s_raw, q_cn, vo in pending:
        b_col = jnp.sum(jnp.where(seen, lf_row, 0.0), axis=1, keepdims=True)
        m_col = jnp.maximum(m_old, jnp.max(jnp.where(seen, r_row, -jnp.inf), axis=1, keepdims=True))
        w = jnp.exp(jnp.where(seen, r_row - m_col, -jnp.inf))
        a = jnp.exp(m_old - m_col)
        s_vo = jnp.dot((s_raw * w).astype(BF16), vo, preferred_element_type=F32)
        num = a * q_cn[:, :ML_DIM] + s_vo[:, :ML_DIM]
        den = a * q_cn[:, ML_DIM:] + s_vo[:, ML_DIM:]
        hc = num / jnp.maximum(jnp.abs(den), jnp.exp(-(b_col + m_col)))
        h_ref[:, lanes] = hc.astype(h_ref.dtype)


def _mlstm(p, kt, gt, batch, n_lat, n_ctx):
    rows = p.shape[0]
    cl, cc = n_lat // CHUNK, n_ctx // CHUNK
    lat_base, ctx_base = 0, batch * cl

    def fwd_chunk(b, i):
        return jnp.where(i < cc, ctx_base + b * cc + i, lat_base + b * cl + (i - cc))

    def bwd_chunk(b, i):
        return jnp.where(i < cc, ctx_base + b * cc + (cc - 1 - i), lat_base + b * cl + (cl - 1 - (i - cc)))

    def specs(chunk):
        return [pl.BlockSpec((CHUNK, 2 * WIDTH), lambda b, i: (chunk(b, i), C_MLQ // (2 * WIDTH))),
                pl.BlockSpec((WIDTH, CHUNK), lambda b, i: (0, chunk(b, i))),
                pl.BlockSpec((2 * N_GATE, CHUNK), lambda b, i: (0, chunk(b, i)))]

    n_state = 2 * ML_HEADS
    return pl.pallas_call(
        _mlstm_kernel,
        out_shape=(jax.ShapeDtypeStruct((rows, WIDTH), BF16), jax.ShapeDtypeStruct((rows, WIDTH), BF16)),
        grid=(batch, cl + cc),
        in_specs=specs(fwd_chunk) + specs(bwd_chunk),
        out_specs=(pl.BlockSpec((CHUNK, WIDTH), lambda b, i: (fwd_chunk(b, i), 0)),
                   pl.BlockSpec((CHUNK, WIDTH), lambda b, i: (bwd_chunk(b, i), 0))),
        scratch_shapes=[pltpu.VMEM((n_state, ML_DIM, 2 * ML_DIM), F32),
                        pltpu.VMEM((n_state, 1, 1), F32)],
        compiler_params=_params(("parallel", "arbitrary")),
        name="mlstm_scan",
    )(p, kt, gt, p, kt, gt)


def _half_mask(e):
    lane = lax.broadcasted_iota(jnp.int32, (1, LANES), 1)
    return (lane < HEAD_DIM) if e == 0 else (lane >= HEAD_DIM)


def _win_kernel(sink_ref, q_ref, kc_ref, vtc_ref, k0_ref, k1_ref, k2_ref, k3_ref,
                vt0_ref, vt1_ref, vt2_ref, vt3_ref, o_ref, *, n_lat_tiles, n_lat_blocks):
    t = pl.program_id(1)
    tq = q_ref.shape[0]
    n_ctx = kc_ref.shape[0]
    is_lat = t < n_lat_tiles
    kk = lax.broadcasted_iota(jnp.int32, (CHUNK, tq), 0)
    qq = lax.broadcasted_iota(jnp.int32, (CHUNK, tq), 1)
    band = (kk >= qq, qq <= kk + CHUNK, kk <= qq, kk + CHUNK <= qq)
    first = 2 * t - 1
    present = [jnp.logical_and(is_lat, jnp.logical_and(first + i >= 0, first + i < n_lat_blocks))
               for i in range(4)]
    visible = [jnp.logical_and(band[i], present[i]) for i in range(4)]
    k_refs = (k0_ref, k1_ref, k2_ref, k3_ref)
    vt_refs = (vt0_ref, vt1_ref, vt2_ref, vt3_ref)
    ones = jnp.ones((16, n_ctx + 4 * CHUNK), BF16)

    def group_operands(g):
        lanes = slice(g * LANES, (g + 1) * LANES)
        k_all = jnp.concatenate([kc_ref[:, lanes]] + [r[:, lanes] for r in k_refs], axis=0)
        vrows = slice(g * HEAD_DIM, (g + 1) * HEAD_DIM)
        vta = jnp.concatenate(
            [jnp.concatenate([vtc_ref[vrows, :]] + [r[vrows, :] for r in vt_refs], axis=1), ones], axis=0)
        return k_all, vta

    operands = [group_operands(g) for g in range(KV_HEADS)]

    def score(h):
        qf = q_ref[:, (h // 2) * LANES:(h // 2 + 1) * LANES].astype(F32)
        qt = jnp.where(_half_mask(h % 2), qf, 0.0).T.astype(BF16)
        return jnp.dot(operands[h // (Q_HEADS // KV_HEADS)][0], qt, preferred_element_type=F32)

    scores = {h: score(h) for h in range(WIN_LEAD)}
    outs = []
    for h in range(Q_HEADS):
        if h + WIN_LEAD < Q_HEADS:
            scores[h + WIN_LEAD] = score(h + WIN_LEAD)
        s = scores.pop(h)
        sink = sink_ref[h] * LOG2E
        parts = [s[:n_ctx, :]]
        for i in range(4):
            blk = s[n_ctx + i * CHUNK:n_ctx + (i + 1) * CHUNK, :]
            parts.append(jnp.where(visible[i], blk, -jnp.inf))
        s = jnp.concatenate(parts, axis=0)
        m = jnp.maximum(jnp.max(s, axis=0, keepdims=True), sink)
        pt = jnp.exp2(s - m).astype(BF16)
        ol = jnp.dot(operands[h // (Q_HEADS // KV_HEADS)][1], pt, preferred_element_type=F32)
        den = ol[HEAD_DIM:HEAD_DIM + 1, :] + jnp.exp2(sink - m)
        outs.append(ol[:HEAD_DIM, :] / den)
    for tt in range(Q_HEADS // 2):
        o_ref[:, tt * LANES:(tt + 1) * LANES] = (
            jnp.concatenate(outs[2 * tt:2 * tt + 2], axis=0).T.astype(o_ref.dtype))


def _window_attention(p, vt, sink, batch, n_lat, n_ctx, ctx_queries):
    tq = 256
    nlt, nct = n_lat // tq, n_ctx // tq
    nlb = n_lat // CHUNK
    nq = nlt + (nct if ctx_queries else 0)
    out_rows = batch * (n_lat + (n_ctx if ctx_queries else 0))
    ctx_blk = batch * n_lat // n_ctx
    k_col = C_WK // 256

    def q_block(b, t):
        return jnp.where(t < nlt, b * nlt + t, batch * nlt + b * nct + (t - nlt))

    def near(i):
        return lambda b, t: b * nlb + jnp.clip(2 * t - 1 + i, 0, nlb - 1)

    k_specs = [pl.BlockSpec((CHUNK, 256), (lambda f: lambda b, t: (f(b, t), k_col))(near(i))) for i in range(4)]
    vt_specs = [pl.BlockSpec((LANES, CHUNK), (lambda f: lambda b, t: (VT_WINDOW, f(b, t)))(near(i)))
                for i in range(4)]
    return pl.pallas_call(
        functools.partial(_win_kernel, n_lat_tiles=nlt, n_lat_blocks=nlb),
        out_shape=jax.ShapeDtypeStruct((out_rows, WIDTH), BF16),
        grid=(batch, nq),
        in_specs=[pl.BlockSpec(memory_space=pltpu.SMEM),
                  pl.BlockSpec((tq, WIDTH), lambda b, t: (q_block(b, t), C_WQ // WIDTH)),
                  pl.BlockSpec((n_ctx, 256), lambda b, t: (ctx_blk + b, k_col)),
                  pl.BlockSpec((LANES, n_ctx), lambda b, t: (VT_WINDOW, ctx_blk + b))]
                 + k_specs + vt_specs,
        out_specs=pl.BlockSpec((tq, WIDTH), lambda b, t: (q_block(b, t), 0)),
        compiler_params=_params(("parallel", "parallel")),
        name="window_attention",
    )(sink, p, p, vt, p, p, p, p, vt, vt, vt, vt)


def _glb_kernel(q_ref, kc_ref, vtc_ref, kl_ref, vtl_ref, o_ref, qt_ref, knorm_ref, *scratch, n_lat_tiles, tk, span_chunks):
    heads = Q_HEADS // KV_HEADS
    acc_refs = scratch[0:heads]
    sc_refs = scratch[heads:2 * heads]
    ring = [scratch[(2 + i) * heads:(3 + i) * heads] for i in range(N_SCORE_BUFS)]
    base = (2 + N_SCORE_BUFS) * heads
    pc_refs = scratch[base:base + heads]
    p_ring = scratch[base + heads:base + heads + 2]
    qi = pl.program_id(2)
    tq = q_ref.shape[0]
    n_chunks = kl_ref.shape[0] // tk

    def k_chunk(c):
        off = pl.multiple_of(jnp.minimum(c, n_chunks - 1) * tk, tk)
        return kl_ref[pl.ds(off, tk), :]

    def vt_chunk(c):
        return vtl_ref[:, pl.ds(pl.multiple_of(c * tk, tk), tk)]

    def with_ones(vt):
        ones = jnp.ones((acc_refs[0].shape[0] - HEAD_DIM, vt.shape[1]), BF16)
        return jnp.concatenate([vt, ones], axis=0)

    @pl.when(qi == 0)
    def _():
        def sq_norm(k):
            kf = k.astype(F32)
            return jnp.max(jnp.sum(kf * kf, axis=1, keepdims=True), axis=0, keepdims=True)

        def body(c, best):
            return jnp.maximum(best, sq_norm(k_chunk(c)))
        best = lax.fori_loop(0, n_chunks, body, sq_norm(kc_ref[...]))
        knorm_ref[...] = jnp.sqrt(0.5 * best)

    for t in range(heads // 2):
        qf = q_ref[:, t * LANES:(t + 1) * LANES].astype(F32)
        for e in range(2):
            h = 2 * t + e
            qt_ref[:, h * tq:(h + 1) * tq] = jnp.where(_half_mask(e), qf, 0.0).T.astype(BF16)
    for h in range(heads):
        acc_refs[h][...] = jnp.zeros_like(acc_refs[h])

    def q_t(h):
        return qt_ref[:, h * tq:(h + 1) * tq]

    bound = [jnp.sqrt(jnp.sum(jnp.square(q_t(h).astype(F32)), axis=0, keepdims=True)) * knorm_ref[...]
             for h in range(heads)]
    bounded = jnp.max(functools.reduce(jnp.maximum, bound)) <= SAFE_LOG2_BOUND

    span = span_chunks * tk
    n_spans = n_chunks // span_chunks
    n_steps = n_spans * heads

    def produce_ctx():
        for h in range(heads):
            s = jnp.dot(kc_ref[...], q_t(h), preferred_element_type=F32)
            pc_refs[h][...] = jnp.exp2(s - bound[h]).astype(BF16)

    def consume_ctx():
        vta = with_ones(vtc_ref[...])
        for h in range(heads):
            acc_refs[h][...] += jnp.dot(vta, pc_refs[h][...], preferred_element_type=F32)

    def produce(p_ref, sp, h):
        k = kl_ref[pl.ds(pl.multiple_of(sp * span, span), span), :]
        s = jnp.dot(k, q_t(h), preferred_element_type=F32)
        p_ref[...] = jnp.exp2(s - bound[h]).astype(BF16)

    def consume(p_ref, sp, h):
        vta = with_ones(vtl_ref[:, pl.ds(pl.multiple_of(sp * span, span), span)])
        acc_refs[h][...] += jnp.dot(vta, p_ref[...], preferred_element_type=F32)

    @pl.when(bounded)
    def _():
        produce_ctx()

        @pl.when(qi >= n_lat_tiles)
        def _():
            consume_ctx()

        @pl.when(qi < n_lat_tiles)
        def _():
            produce(p_ring[0], 0, 0)
            consume_ctx()

            unroll = heads * math.gcd(n_spans, BOUNDED_UNROLL_SPANS)

            def steps(j, last):
                for i in range(unroll):
                    step = unroll * j + i
                    if not (last and i == unroll - 1):
                        produce(p_ring[(i + 1) % 2], (step + 1) // heads, (i + 1) % heads)
                    consume(p_ring[i % 2], step // heads, i % heads)

            def body(j, carry):
                steps(j, False)
                return carry
            n_iter = n_steps // unroll
            lax.fori_loop(0, n_iter - 1, body, 0)
            steps(n_iter - 1, True)

    neg_inf = (jnp.full((1, tq), -jnp.inf, F32),) * heads

    def fill(s_refs, k, m_run):
        out = []
        for h in range(heads):
            s = jnp.dot(k, q_t(h), preferred_element_type=F32)
            s_refs[h][...] = s
            out.append(jnp.maximum(m_run[h], jnp.max(s, axis=0, keepdims=True)))
        return tuple(out)

    def drain(s_refs, vt, m_run, m_acc):
        vta = with_ones(vt)
        for h in range(heads):
            alpha = jnp.exp2(m_acc[h] - m_run[h])
            pt = jnp.exp2(s_refs[h][...] - m_run[h]).astype(BF16)
            acc_refs[h][...] = alpha * acc_refs[h][...] + jnp.dot(vta, pt, preferred_element_type=F32)
        return m_run

    @pl.when(jnp.logical_not(bounded))
    def _():
        m_ctx = fill(sc_refs, kc_ref[...], neg_inf)

        @pl.when(qi >= n_lat_tiles)
        def _():
            drain(sc_refs, vtc_ref[...], m_ctx, neg_inf)

        @pl.when(qi < n_lat_tiles)
        def _():
            m_one = fill(ring[0], k_chunk(0), m_ctx)
            m_two = fill(ring[1], k_chunk(1), m_one)
            m_acc = drain(sc_refs, vtc_ref[...], m_one, neg_inf)

            def body(j, carry):
                m_run, m_acc = carry
                for i in range(N_SCORE_BUFS):
                    c = N_SCORE_BUFS * j + i
                    m_next = fill(ring[(i + 2) % N_SCORE_BUFS], k_chunk(c + 2), m_run)
                    m_acc = drain(ring[i], vt_chunk(c), m_run, m_acc)
                    m_run = m_next
                return m_run, m_acc
            lax.fori_loop(0, n_chunks // N_SCORE_BUFS, body, (m_two, m_acc))

    for t in range(heads // 2):
        pair = []
        for e in range(2):
            a = acc_refs[2 * t + e][...]
            pair.append(a[:HEAD_DIM, :] / a[HEAD_DIM:HEAD_DIM + 1, :])
        o_ref[:, t * LANES:(t + 1) * LANES] = jnp.concatenate(pair, axis=0).T.astype(o_ref.dtype)


def _global_attention(p, vt, batch, n_lat, n_ctx, ctx_queries):
    tq = 256
    tk = min(512, n_lat // N_SCORE_BUFS)
    assert n_lat % (tk * N_SCORE_BUFS) == 0 and tk % LANES == 0
    span_chunks = math.gcd(SPAN_CHUNKS, n_lat // tk)
    nl, nc = n_lat // tq, n_ctx // tq
    nq = nl + (nc if ctx_queries else 0)
    out_rows = batch * (n_lat + (n_ctx if ctx_queries else 0))
    ctx_base = batch * nl
    heads = Q_HEADS // KV_HEADS
    acc_rows = HEAD_DIM + 16

    def q_block(b, qi):
        return jnp.where(qi < nl, b * nl + qi, ctx_base + b * nc + (qi - nl))

    ctx_blk = batch * n_lat // n_ctx
    return pl.pallas_call(
        functools.partial(_glb_kernel, n_lat_tiles=nl, tk=tk, span_chunks=span_chunks),
        out_shape=jax.ShapeDtypeStruct((out_rows, WIDTH), BF16),
        grid=(batch, KV_HEADS, nq),
        in_specs=[pl.BlockSpec((tq, 2 * LANES), lambda b, g, qi: (q_block(b, qi), C_GQ // 256 + g)),
                  pl.BlockSpec((n_ctx, LANES), lambda b, g, qi: (ctx_blk + b, C_GK // LANES + g)),
                  pl.BlockSpec((HEAD_DIM, n_ctx), lambda b, g, qi: (KV_HEADS * VT_GLOBAL + g, ctx_blk + b)),
                  pl.BlockSpec((n_lat, LANES), lambda b, g, qi: (b, C_GK // LANES + g)),
                  pl.BlockSpec((HEAD_DIM, n_lat), lambda b, g, qi: (KV_HEADS * VT_GLOBAL + g, b))],
        out_specs=pl.BlockSpec((tq, 2 * LANES), lambda b, g, qi: (q_block(b, qi), g)),
        scratch_shapes=([pltpu.VMEM((LANES, heads * tq), BF16), pltpu.VMEM((1, 1), F32)]
                        + [pltpu.VMEM((acc_rows, tq), F32)] * heads
                        + [pltpu.VMEM((n_ctx, tq), F32)] * heads
                        + [pltpu.VMEM((tk, tq), F32)] * (N_SCORE_BUFS * heads)
                        + [pltpu.VMEM((n_ctx, tq), BF16)] * heads
                        + [pltpu.VMEM((span_chunks * tk, tq), BF16)] * 2),
        compiler_params=_params(("parallel", "parallel", "arbitrary")),
        name="global_attention",
    )(p, p, vt, p, vt)


def _merge_kernel(*refs, lat_tiles):
    n_x = 1 if lat_tiles is None else 2
    x_refs = refs[:n_x]
    (mod_ref, hf_ref, hb_ref, og_ref, yb_ref, yc_ref, ga_ref, gb_ref, gc_ref,
     mlg_ref, wbr_ref, wo_ref, o_ref) = refs[n_x:]
    tm = o_ref.shape[0]
    proj_b = jnp.dot(yb_ref[...], wbr_ref[1], preferred_element_type=F32)
    proj_c = jnp.dot(yc_ref[...], wbr_ref[2], preferred_element_type=F32)
    hs = hf_ref[...].astype(F32) + hb_ref[...].astype(F32)
    parts = []
    for t in range(ML_HEADS):
        ht = hs[:, t * ML_DIM:(t + 1) * ML_DIM]
        ms = jnp.mean(ht * ht, axis=-1, keepdims=True)
        parts.append(ht * lax.rsqrt(ms + EPS))
    ya = (jnp.concatenate(parts, axis=1) * mlg_ref[...] * _sigmoid(og_ref[...].astype(F32))).astype(BF16)
    proj_a = jnp.dot(ya, wbr_ref[0], preferred_element_type=F32)
    merged = (_sigmoid(gb_ref[...].astype(F32)) * proj_b + _sigmoid(gc_ref[...].astype(F32)) * proj_c
              + _sigmoid(ga_ref[...].astype(F32)) * proj_a)
    out = jnp.dot(merged.astype(BF16), wo_ref[...], preferred_element_type=F32)

    def gated_residual(x_ref):
        for sb in range(tm // ROW_BLOCK):
            rows = slice(sb * ROW_BLOCK, (sb + 1) * ROW_BLOCK)
            o_ref[rows, :] = x_ref[rows, :] + mod_ref[sb, 2:3, :] * out[rows, :]

    if lat_tiles is None:
        gated_residual(x_refs[0])
    else:
        pl.when(pl.program_id(0) < lat_tiles)(lambda: gated_residual(x_refs[0]))
        pl.when(pl.program_id(0) >= lat_tiles)(lambda: gated_residual(x_refs[1]))


def _merge(xs, modtab, p, hf, hb, yb, yc, mlg, wbr, wo, n_rows, lat_tiles):
    tm = ROW_TILE
    nb = tm // ROW_BLOCK
    rows = sum(x.shape[0] for x in xs)
    row = lambda i: (i, 0)
    gate = lambda k: pl.BlockSpec((tm, D_MODEL), lambda i: (i, C_GATE // D_MODEL + k))
    return pl.pallas_call(
        functools.partial(_merge_kernel, lat_tiles=lat_tiles if len(xs) == 2 else None),
        out_shape=jax.ShapeDtypeStruct((rows, D_MODEL), F32),
        grid=(n_rows // tm,),
        in_specs=_token_tile_specs(xs, tm, lat_tiles) + [
                  pl.BlockSpec((nb, 6, D_MODEL), lambda i: (i, 0, 0)),
                  pl.BlockSpec((tm, WIDTH), row),
                  pl.BlockSpec((tm, WIDTH), row),
                  pl.BlockSpec((tm, WIDTH), lambda i: (i, C_MLO // WIDTH)),
                  pl.BlockSpec((tm, WIDTH), row),
                  pl.BlockSpec((tm, WIDTH), row),
                  gate(0), gate(1), gate(2),
                  _resident((1, WIDTH)),
                  _resident((3, WIDTH, D_MODEL)),
                  _resident((D_MODEL, D_MODEL))],
        out_specs=pl.BlockSpec((tm, D_MODEL), row),
        input_output_aliases={0: 0} if len(xs) == 1 else {},
        compiler_params=_params(("parallel",)),
        name="merge",
    )(*xs, modtab, hf, hb, p, yb, yc, p, p, p, mlg, wbr, wo)


def _ffn_kernel(x_ref, mod_ref, g2_ref, w1_ref, w3_ref, w2_ref, gf_ref, o_ref, *, final):
    tm = x_ref.shape[0]
    d_ff = w1_ref.shape[1]
    split = pl.cdiv(d_ff // MXU_TILE, 2) * MXU_TILE
    for sb in range(tm // ROW_BLOCK):
        rows = slice(sb * ROW_BLOCK, (sb + 1) * ROW_BLOCK)
        xs = x_ref[rows, :]
        ms = jnp.mean(xs * xs, axis=-1, keepdims=True)
        y = xs * lax.rsqrt(ms + EPS) * g2_ref[...]
        h = (y * (1.0 + mod_ref[sb, 4:5, :]) + mod_ref[sb, 3:4, :]).astype(BF16)
        out = None
        for c0, c1 in ((0, split), (split, d_ff)):
            a = jnp.dot(h, w1_ref[:, c0:c1], preferred_element_type=F32)
            b = jnp.dot(h, w3_ref[:, c0:c1], preferred_element_type=F32)
            z = (a * _sigmoid(a) * b).astype(BF16)
            part = jnp.dot(z, w2_ref[c0:c1, :], preferred_element_type=F32)
            out = part if out is None else out + part
        xn = xs + mod_ref[sb, 5:6, :] * out
        if final:
            ms = jnp.mean(xn * xn, axis=-1, keepdims=True)
            xn = xn * lax.rsqrt(ms + EPS) * gf_ref[...]
        o_ref[rows, :] = xn


def _ffn(x, modtab, g2, w1, w3, w2, gfin, n_rows, final):
    tm = ROW_TILE
    nb = tm // ROW_BLOCK
    d_ff = w1.shape[1]
    row = lambda i: (i, 0)
    out_rows = n_rows if final else x.shape[0]
    return pl.pallas_call(
        functools.partial(_ffn_kernel, final=final),
        out_shape=jax.ShapeDtypeStruct((out_rows, D_MODEL), F32),
        grid=(n_rows // tm,),
        in_specs=[pl.BlockSpec((tm, D_MODEL), row),
                  pl.BlockSpec((nb, 6, D_MODEL), lambda i: (i, 0, 0)),
                  _resident((1, D_MODEL)),
                  _resident((D_MODEL, d_ff)),
                  _resident((D_MODEL, d_ff)),
                  _resident((d_ff, D_MODEL)),
                  _resident((1, D_MODEL))],
        out_specs=pl.BlockSpec((tm, D_MODEL), row),
        input_output_aliases={} if final else {0: 0},
        compiler_params=_params(("parallel",)),
        name="ffn",
    )(x, modtab, g2, w1, w3, w2, gfin)


def _dup_halves(w, base):
    h0 = w[..., base:base + HEAD_DIM]
    h1 = w[..., base + HEAD_DIM:base + 2 * HEAD_DIM]
    return [h0, h0, h1, h1]


def _arrange_in_proj(w):
    o_gate_ml = 4 * WIDTH
    o_wq = o_gate_ml + 2 * N_GATE
    o_wk, o_wv = o_wq + WIDTH, o_wq + WIDTH + 128
    o_gq = o_wv + 128
    o_gk, o_gv = o_gq + WIDTH, o_gq + WIDTH + 128
    o_gate = o_gv + 128
    main = jnp.concatenate(
        [w[..., o_gate:o_gate + 3 * D_MODEL], w[..., 0:WIDTH], w[..., 2 * WIDTH:4 * WIDTH],
         w[..., o_wq:o_wq + WIDTH]]
        + _dup_halves(w, o_wk) + [w[..., o_gq:o_gq + WIDTH]] + _dup_halves(w, o_gk), axis=-1)
    values = jnp.concatenate([w[..., o_gv:o_gv + 128], w[..., o_wv:o_wv + 128]], axis=-1)
    return main, w[..., o_gate_ml:o_gate_ml + 2 * N_GATE], w[..., WIDTH:2 * WIDTH], values


def _rope_tables(n_lat):
    t = jnp.arange(n_lat)
    quarter = HEAD_DIM // 4
    inv = ROPE_THETA ** (-jnp.arange(0, 2 * quarter, 2, dtype=F32) / (2 * quarter))
    ang_r = (t // GRID_W).astype(F32)[:, None] * inv
    ang_c = (t % GRID_W).astype(F32)[:, None] * inv
    cos = jnp.concatenate([jnp.cos(ang_r)] * 2 + [jnp.cos(ang_c)] * 2, axis=1)
    sin = jnp.concatenate([-jnp.sin(ang_r), jnp.sin(ang_r), -jnp.sin(ang_c), jnp.sin(ang_c)], axis=1)
    reps = LANES // HEAD_DIM
    return (jnp.concatenate([jnp.tile(cos, (1, reps)), jnp.ones((ROW_TILE, LANES), F32)], axis=0),
            jnp.concatenate([jnp.tile(sin, (1, reps)), jnp.zeros((ROW_TILE, LANES), F32)], axis=0))


def kernel(x, c, ctx, c_ctx, w_mod, b_mod, norm1_g, w_in, b_in, ml_norm_g, win_sink, qn_g, kn_g,
           w_br, w_o, norm2_g, w_ff1, w_ff3, w_ff2, final_g):
    batch, n_lat, d = x.shape
    n_ctx = ctx.shape[1]
    depth = w_mod.shape[0]
    assert d == D_MODEL and n_lat % ROW_TILE == 0 and n_lat % GRID_W == 0
    assert (batch * n_ctx) % ROW_TILE == 0 and n_ctx % ROW_BLOCK == 0 and (batch * n_lat) % n_ctx == 0
    lat_rows, ctx_rows = batch * n_lat, batch * n_ctx

    xs = (x.reshape(lat_rows, d), ctx.reshape(ctx_rows, d))
    cvec = jnp.concatenate([c, c_ctx[None, :], jnp.zeros((8 - batch - 1, d), F32)], axis=0)
    block_class = np.concatenate([np.repeat(np.arange(batch), n_lat // ROW_BLOCK),
                                  np.full(ctx_rows // ROW_BLOCK, batch)])
    cos, sin = _rope_tables(n_lat)
    lat_tiles = lat_rows // ROW_TILE
    avg = jnp.asarray(np.kron(np.eye(LANES // HEAD_DIM), np.full((HEAD_DIM, HEAD_DIM), 1.0 / HEAD_DIM)), BF16)

    out = None
    for l in range(depth):
        last = l == depth - 1
        mod = _mod_vectors(cvec, w_mod[l], b_mod[l]).reshape(8, 6, d)
        modtab = mod[block_class]

        w_main, w_gate, w_mlk, w_val = _arrange_in_proj(w_in[l])
        b_main, b_gate, b_mlk, b_val = _arrange_in_proj(b_in[l][None, :])
        p, gt, kt, vt = _in_projection(
            xs, modtab, norm1_g[l][None, :], w_main.astype(BF16), b_main,
            w_gate.T.astype(BF16), b_gate.T, w_mlk.T.astype(BF16), b_mlk.T, w_val.T.astype(BF16), b_val.T,
            cos, sin, jnp.tile(qn_g[l], 2)[None, :], jnp.tile(kn_g[l], 2)[None, :], avg, batch, n_lat)

        hf, hb = _mlstm(p, kt, gt, batch, n_lat, n_ctx)
        yb = _window_attention(p, vt, win_sink[l], batch, n_lat, n_ctx, ctx_queries=not last)
        yc = _global_attention(p, vt, batch, n_lat, n_ctx, ctx_queries=not last)

        n_rows = lat_rows if last else lat_rows + ctx_rows
        merged = _merge(xs, modtab, p, hf, hb, yb, yc, ml_norm_g[l].reshape(1, WIDTH),
                        w_br[l].astype(BF16), w_o[l].astype(BF16), n_rows, lat_tiles)
        out = _ffn(merged, modtab, norm2_g[l][None, :], w_ff1[l].astype(BF16), w_ff3[l].astype(BF16),
                   w_ff2[l].astype(BF16), final_g[None, :], n_rows, final=last)
        xs = (out,)
    return out.reshape(batch, n_lat, d)
```

```python
import functools
import math

import jax
import jax.numpy as jnp
import numpy as np
from jax import lax
from jax.experimental import pallas as pl
from jax.experimental.pallas import tpu as pltpu

F32 = jnp.float32
BF16 = jnp.bfloat16

D_MODEL = 1024
GRID_W = 64
CHUNK = 128
HEAD_DIM = 64
ROPE_THETA = 10000.0
EPS = 1e-6
ML_HEADS = 4
ML_DIM = 128
Q_HEADS = 8
KV_HEADS = 2
WIDTH = 512
N_GATE = 2 * ML_HEADS

LANES = 128
MXU_TILE = 256
ROW_BLOCK = 256
ROW_TILE = 512
VMEM_LIMIT = 56 * 1024 * 1024

C_GATE = 0
C_MLQ, C_MLV, C_MLO = 3072, 3584, 4096
C_WQ, C_WK = 4608, 5120
C_GQ, C_GK = 5376, 5888
N_PROJ = C_GK + 256
VT_GLOBAL, VT_WINDOW = 0, 1
LOG2E = 1.4426950408889634
WIN_LEAD = 3
N_SCORE_BUFS = 4
SPAN_CHUNKS = 4
SAFE_LOG2_BOUND = 50.0

NT_DIMS = (((1,), (1,)), ((), ()))


def _params(sem, vmem=VMEM_LIMIT):
    return pltpu.CompilerParams(dimension_semantics=sem, vmem_limit_bytes=vmem)


def _resident(shape):
    nd = len(shape)
    return pl.BlockSpec(shape, lambda *_: (0,) * nd, pipeline_mode=pl.Buffered(1))


def _sigmoid(x):
    return 1.0 / (1.0 + jnp.exp(-x))


def _log_sigmoid(x):
    return jnp.minimum(x, 0.0) - jnp.log(1.0 + jnp.exp(-jnp.abs(x)))


def _mod_kernel(c_ref, w_ref, b_ref, o_ref):
    c = c_ref[...]
    s = c * _sigmoid(c)
    o_ref[...] = jnp.dot(s, w_ref[...], preferred_element_type=F32) + b_ref[...]


def _mod_vectors(cvec, w_mod, b_mod):
    n_out = w_mod.shape[1]
    tn = 1536
    return pl.pallas_call(
        _mod_kernel,
        out_shape=jax.ShapeDtypeStruct((cvec.shape[0], n_out), F32),
        grid=(n_out // tn,),
        in_specs=[pl.BlockSpec(cvec.shape, lambda j: (0, 0)),
                  pl.BlockSpec((D_MODEL, tn), lambda j: (0, j)),
                  pl.BlockSpec((1, tn), lambda j: (0, j))],
        out_specs=pl.BlockSpec((cvec.shape[0], tn), lambda j: (0, j)),
        compiler_params=_params(("parallel",)),
        name="mod_vectors",
    )(cvec, w_mod, b_mod.reshape(1, n_out))


def _rope(acc, cos, sin, first_half):
    w = acc.shape[1]
    reps = w // LANES
    if reps > 1:
        cos = jnp.concatenate([cos] * reps, axis=1)
        sin = jnp.concatenate([sin] * reps, axis=1)
    ahead = pltpu.roll(acc, w - 16, axis=1)
    behind = pltpu.roll(acc, 16, axis=1)
    return acc * cos + jnp.where(first_half, ahead, behind) * sin


def _head_rms(acc, avg, gain):
    sq = (acc * acc).astype(BF16)
    outs = []
    for t in range(acc.shape[1] // LANES):
        sl = slice(t * LANES, (t + 1) * LANES)
        ms = jnp.dot(sq[:, sl], avg, preferred_element_type=F32)
        outs.append(acc[:, sl] * lax.rsqrt(ms + EPS) * gain)
    return jnp.concatenate(outs, axis=1)


def _inproj_kernel(*refs, lat_tiles):
    n_x = 1 if lat_tiles is None else 2
    x_refs = refs[:n_x]
    (mod_ref, g1_ref, w_ref, b_ref, wgt_ref, bgt_ref, wkt_ref, bkt_ref, wvt_ref, bvt_ref,
     cos_ref, sin_ref, qg_ref, kg_ref, avg_ref, p_ref, gt_ref, kt_ref, vt_ref, h_ref) = refs[n_x:]
    tm = h_ref.shape[0]

    def modulated_norm(x_ref):
        for sb in range(tm // ROW_BLOCK):
            r0 = sb * ROW_BLOCK
            xs = x_ref[r0:r0 + ROW_BLOCK, :]
            ms = jnp.mean(xs * xs, axis=-1, keepdims=True)
            y = xs * lax.rsqrt(ms + EPS) * g1_ref[...]
            shift = mod_ref[sb, 0:1, :]
            scale = mod_ref[sb, 1:2, :]
            h_ref[r0:r0 + ROW_BLOCK, :] = (y * (1.0 + scale) + shift).astype(BF16)

    if lat_tiles is None:
        modulated_norm(x_refs[0])
    else:
        pl.when(pl.program_id(0) < lat_tiles)(lambda: modulated_norm(x_refs[0]))
        pl.when(pl.program_id(0) >= lat_tiles)(lambda: modulated_norm(x_refs[1]))

    def first_half(width):
        return (lax.broadcasted_iota(jnp.int32, (1, width), 1) % 32) < 16

    avg = avg_ref[...]
    qg = qg_ref[...]
    kg = kg_ref[...]
    q_scale = HEAD_DIM ** -0.5

    h = h_ref[...]
    cos = cos_ref[...]
    sin = sin_ref[...]

    def proj(c0, width):
        return (jnp.dot(h, w_ref[:, c0:c0 + width], preferred_element_type=F32)
                + b_ref[:, c0:c0 + width])

    def store_plain(cols):
        for c0 in cols:
            p_ref[:, c0:c0 + WIDTH] = proj(c0, WIDTH).astype(BF16)

    plain = list(range(C_GATE, C_GATE + 3 * D_MODEL, WIDTH)) + [C_MLQ, C_MLV, C_MLO]
    raw_gq, raw_gk = proj(C_GQ, WIDTH), proj(C_GK, 256)
    raw_wq, raw_wk = proj(C_WQ, WIDTH), proj(C_WK, 256)
    store_plain(plain[:3])
    gq = _rope(_head_rms(raw_gq, avg, qg), cos, sin, first_half(WIDTH)) * (q_scale * LOG2E)
    p_ref[:, C_GQ:C_GQ + WIDTH] = gq.astype(BF16)
    gk = _rope(_head_rms(raw_gk, avg, kg), cos, sin, first_half(256))
    p_ref[:, C_GK:C_GK + 256] = gk.astype(BF16)
    p_ref[:, C_WQ:C_WQ + WIDTH] = (_rope(raw_wq, cos, sin, first_half(WIDTH)) * (q_scale * LOG2E)).astype(BF16)
    p_ref[:, C_WK:C_WK + 256] = _rope(raw_wk, cos, sin, first_half(256)).astype(BF16)
    store_plain(plain[3:])

    wt = jnp.concatenate([wgt_ref[...], wkt_ref[...], wvt_ref[...]], axis=0)
    bt = jnp.concatenate([bgt_ref[...], bkt_ref[...], bvt_ref[...]], axis=0)
    pt = lax.dot_general(wt, h, NT_DIMS, preferred_element_type=F32) + bt
    n_g, n_k = 2 * N_GATE, 2 * N_GATE + WIDTH
    grow = lax.broadcasted_iota(jnp.int32, (n_g, 1), 0)
    gt_ref[...] = jnp.where(grow >= N_GATE, _log_sigmoid(pt[:n_g, :]), pt[:n_g, :])
    kt_ref[...] = pt[n_g:n_k, :].astype(BF16)
    vt_ref[...] = pt[n_k:, :].astype(BF16)


def _token_tile_specs(xs, tm, lat_tiles):
    if len(xs) == 1:
        return [pl.BlockSpec((tm, D_MODEL), lambda i: (i, 0))]
    ctx_tiles = xs[1].shape[0] // tm
    return [pl.BlockSpec((tm, D_MODEL), lambda i: (jnp.minimum(i, lat_tiles - 1), 0)),
            pl.BlockSpec((tm, D_MODEL), lambda i: (jnp.clip(i - lat_tiles, 0, ctx_tiles - 1), 0))]


def _in_projection(xs, modtab, g1, w, b, wgt, bgt, wkt, bkt, wvt, bvt, cos, sin, qg, kg, avg, batch, n_lat):
    rows = sum(x.shape[0] for x in xs)
    tm = ROW_TILE
    nb = tm // ROW_BLOCK
    kv = 2 * KV_HEADS * HEAD_DIM
    seq_tiles = n_lat // tm
    lat_tiles = batch * seq_tiles

    def rope_block(i):
        return (jnp.where(i < lat_tiles, i % seq_tiles, seq_tiles), 0)

    return pl.pallas_call(
        functools.partial(_inproj_kernel, lat_tiles=lat_tiles if len(xs) == 2 else None),
        out_shape=(jax.ShapeDtypeStruct((rows, N_PROJ), BF16),
                   jax.ShapeDtypeStruct((2 * N_GATE, rows), F32),
                   jax.ShapeDtypeStruct((WIDTH, rows), BF16),
                   jax.ShapeDtypeStruct((kv, rows), BF16)),
        grid=(rows // tm,),
        in_specs=_token_tile_specs(xs, tm, lat_tiles) + [
                  pl.BlockSpec((nb, 6, D_MODEL), lambda i: (i, 0, 0)),
                  _resident((1, D_MODEL)),
                  _resident((D_MODEL, N_PROJ)),
                  _resident((1, N_PROJ)),
                  _resident((2 * N_GATE, D_MODEL)),
                  _resident((2 * N_GATE, 1)),
                  _resident((WIDTH, D_MODEL)),
                  _resident((WIDTH, 1)),
                  _resident((kv, D_MODEL)),
                  _resident((kv, 1)),
                  pl.BlockSpec((tm, LANES), rope_block),
                  pl.BlockSpec((tm, LANES), rope_block),
                  _resident((1, LANES)),
                  _resident((1, LANES)),
                  _resident((LANES, LANES))],
        out_specs=(pl.BlockSpec((tm, N_PROJ), lambda i: (i, 0)),
                   pl.BlockSpec((2 * N_GATE, tm), lambda i: (0, i)),
                   pl.BlockSpec((WIDTH, tm), lambda i: (0, i)),
                   pl.BlockSpec((kv, tm), lambda i: (0, i))),
        scratch_shapes=[pltpu.VMEM((tm, D_MODEL), BF16)],
        compiler_params=_params(("parallel",)),
        name="in_projection",
    )(*xs, modtab, g1, w, b, wgt, bgt, wkt, bkt, wvt, bvt, cos, sin, qg, kg, avg)


def _scan_lanes(x, op, fill, reverse):
    n = x.shape[1]
    lane = lax.broadcasted_iota(jnp.int32, x.shape, 1)
    sh = 1
    while sh < n:
        if reverse:
            moved = jnp.where(lane < n - sh, pltpu.roll(x, n - sh, axis=1), fill)
        else:
            moved = jnp.where(lane >= sh, pltpu.roll(x, sh, axis=1), fill)
        x = op(x, moved)
        sh *= 2
    return x


def _mlstm_kernel(qvf_ref, ktf_ref, gtf_ref, qvb_ref, ktb_ref, gtb_ref, hf_ref, hb_ref, cn_ref, m_ref):
    @pl.when(pl.program_id(1) == 0)
    def _():
        cn_ref[...] = jnp.zeros_like(cn_ref)
        m_ref[...] = jnp.zeros_like(m_ref)

    L = CHUNK
    row = lax.broadcasted_iota(jnp.int32, (L, L), 0)
    col = lax.broadcasted_iota(jnp.int32, (L, L), 1)
    scale = ML_DIM ** -0.5
    ones_v = jnp.ones((L, ML_DIM), BF16)
    pending = []
    for d, (qv_ref, kt_ref, gt_ref, h_ref) in enumerate(((qvf_ref, ktf_ref, gtf_ref, hf_ref),
                                                          (qvb_ref, ktb_ref, gtb_ref, hb_ref))):
        seen = (col <= row) if d == 0 else (col >= row)
        g0 = d * ML_HEADS
        ic = gt_ref[g0:g0 + ML_HEADS, :]
        lf = gt_ref[N_GATE + g0:N_GATE + g0 + ML_HEADS, :]
        r = ic - _scan_lanes(lf, jnp.add, 0.0, d == 1)
        r_max = jnp.max(r, axis=1, keepdims=True)
        b_end = jnp.sum(lf, axis=1, keepdims=True)
        for hh in range(ML_HEADS):
            idx = g0 + hh
            lanes = slice(hh * ML_DIM, (hh + 1) * ML_DIM)
            r_row = r[hh:hh + 1, :]
            m_old = m_ref[idx]
            m_end = jnp.maximum(m_old, r_max[hh:hh + 1, :])
            q = qv_ref[:, lanes]
            vo = jnp.concatenate([qv_ref[:, WIDTH + hh * ML_DIM:WIDTH + (hh + 1) * ML_DIM], ones_v], axis=1)
            kt = kt_ref[lanes, :]
            qsb = (q.astype(F32) * scale).astype(BF16)
            cn = cn_ref[idx]
            s_raw = jnp.dot(qsb, kt, preferred_element_type=F32)
            q_cn = jnp.dot(qsb, cn.astype(BF16), preferred_element_type=F32)
            kwt = (kt.astype(F32) * jnp.exp(r_row - m_end)).astype(BF16)
            cn_ref[idx] = jnp.exp(m_old - m_end) * cn + jnp.dot(kwt, vo, preferred_element_type=F32)
            m_ref[idx] = b_end[hh:hh + 1, :] + m_end
            pending.append((h_ref, lanes, seen, r_row, lf[hh:hh + 1, :], m_old, s_raw, q_cn, vo))

    for h_ref, lanes, seen, r_row, lf_row, m_old, s_raw, q_cn, vo in pending:
        b_col = jnp.sum(jnp.where(seen, lf_row, 0.0), axis=1, keepdims=True)
        m_col = jnp.maximum(m_old, jnp.max(jnp.where(seen, r_row, -jnp.inf), axis=1, keepdims=True))
        w = jnp.exp(jnp.where(seen, r_row - m_col, -jnp.inf))
        a = jnp.exp(m_old - m_col)
        s_vo = jnp.dot((s_raw * w).astype(BF16), vo, preferred_element_type=F32)
        num = a * q_cn[:, :ML_DIM] + s_vo[:, :ML_DIM]
        den = a * q_cn[:, ML_DIM:] + s_vo[:, ML_DIM:]
        hc = num / jnp.maximum(jnp.abs(den), jnp.exp(-(b_col + m_col)))
        h_ref[:, lanes] = hc.astype(h_ref.dtype)


def _mlstm(p, kt, gt, batch, n_lat, n_ctx):
    rows = p.shape[0]
    cl, cc = n_lat // CHUNK, n_ctx // CHUNK
    lat_base, ctx_base = 0, batch * cl

    def fwd_chunk(b, i):
        return jnp.where(i < cc, ctx_base + b * cc + i, lat_base + b * cl + (i - cc))

    def bwd_chunk(b, i):
        return jnp.where(i < cc, ctx_base + b * cc + (cc - 1 - i), lat_base + b * cl + (cl - 1 - (i - cc)))

    def specs(chunk):
        return [pl.BlockSpec((CHUNK, 2 * WIDTH), lambda b, i: (chunk(b, i), C_MLQ // (2 * WIDTH))),
                pl.BlockSpec((WIDTH, CHUNK), lambda b, i: (0, chunk(b, i))),
                pl.BlockSpec((2 * N_GATE, CHUNK), lambda b, i: (0, chunk(b, i)))]

    n_state = 2 * ML_HEADS
    return pl.pallas_call(
        _mlstm_kernel,
        out_shape=(jax.ShapeDtypeStruct((rows, WIDTH), BF16), jax.ShapeDtypeStruct((rows, WIDTH), BF16)),
        grid=(batch, cl + cc),
        in_specs=specs(fwd_chunk) + specs(bwd_chunk),
        out_specs=(pl.BlockSpec((CHUNK, WIDTH), lambda b, i: (fwd_chunk(b, i), 0)),
                   pl.BlockSpec((CHUNK, WIDTH), lambda b, i: (bwd_chunk(b, i), 0))),
        scratch_shapes=[pltpu.VMEM((n_state, ML_DIM, 2 * ML_DIM), F32),
                        pltpu.VMEM((n_state, 1, 1), F32)],
        compiler_params=_params(("parallel", "arbitrary")),
        name="mlstm_scan",
    )(p, kt, gt, p, kt, gt)


def _half_mask(e):
    lane = lax.broadcasted_iota(jnp.int32, (1, LANES), 1)
    return (lane < HEAD_DIM) if e == 0 else (lane >= HEAD_DIM)


def _win_kernel(sink_ref, q_ref, kc_ref, vtc_ref, k0_ref, k1_ref, k2_ref, k3_ref,
                vt0_ref, vt1_ref, vt2_ref, vt3_ref, o_ref, *, n_lat_tiles, n_lat_blocks):
    t = pl.program_id(1)
    tq = q_ref.shape[0]
    n_ctx = kc_ref.shape[0]
    is_lat = t < n_lat_tiles
    kk = lax.broadcasted_iota(jnp.int32, (CHUNK, tq), 0)
    qq = lax.broadcasted_iota(jnp.int32, (CHUNK, tq), 1)
    band = (kk >= qq, qq <= kk + CHUNK, kk <= qq, kk + CHUNK <= qq)
    first = 2 * t - 1
    present = [jnp.logical_and(is_lat, jnp.logical_and(first + i >= 0, first + i < n_lat_blocks))
               for i in range(4)]
    visible = [jnp.logical_and(band[i], present[i]) for i in range(4)]
    k_refs = (k0_ref, k1_ref, k2_ref, k3_ref)
    vt_refs = (vt0_ref, vt1_ref, vt2_ref, vt3_ref)
    ones = jnp.ones((16, n_ctx + 4 * CHUNK), BF16)

    def group_operands(g):
        lanes = slice(g * LANES, (g + 1) * LANES)
        k_all = jnp.concatenate([kc_ref[:, lanes]] + [r[:, lanes] for r in k_refs], axis=0)
        vrows = slice(g * HEAD_DIM, (g + 1) * HEAD_DIM)
        vta = jnp.concatenate(
            [jnp.concatenate([vtc_ref[vrows, :]] + [r[vrows, :] for r in vt_refs], axis=1), ones], axis=0)
        return k_all, vta

    operands = [group_operands(g) for g in range(KV_HEADS)]

    def score(h):
        qf = q_ref[:, (h // 2) * LANES:(h // 2 + 1) * LANES].astype(F32)
        qt = jnp.where(_half_mask(h % 2), qf, 0.0).T.astype(BF16)
        return jnp.dot(operands[h // (Q_HEADS // KV_HEADS)][0], qt, preferred_element_type=F32)

    scores = {h: score(h) for h in range(WIN_LEAD)}
    outs = []
    for h in range(Q_HEADS):
        if h + WIN_LEAD < Q_HEADS:
            scores[h + WIN_LEAD] = score(h + WIN_LEAD)
        s = scores.pop(h)
        sink = sink_ref[h] * LOG2E
        parts = [s[:n_ctx, :]]
        for i in range(4):
            blk = s[n_ctx + i * CHUNK:n_ctx + (i + 1) * CHUNK, :]
            parts.append(jnp.where(visible[i], blk, -jnp.inf))
        s = jnp.concatenate(parts, axis=0)
        m = jnp.maximum(jnp.max(s, axis=0, keepdims=True), sink)
        pt = jnp.exp2(s - m).astype(BF16)
        ol = jnp.dot(operands[h // (Q_HEADS // KV_HEADS)][1], pt, preferred_element_type=F32)
        den = ol[HEAD_DIM:HEAD_DIM + 1, :] + jnp.exp2(sink - m)
        outs.append(ol[:HEAD_DIM, :] / den)
    for tt in range(Q_HEADS // 2):
        o_ref[:, tt * LANES:(tt + 1) * LANES] = (
            jnp.concatenate(outs[2 * tt:2 * tt + 2], axis=0).T.astype(o_ref.dtype))


def _window_attention(p, vt, sink, batch, n_lat, n_ctx, ctx_queries):
    tq = 256
    nlt, nct = n_lat // tq, n_ctx // tq
    nlb = n_lat // CHUNK
    nq = nlt + (nct if ctx_queries else 0)
    out_rows = batch * (n_lat + (n_ctx if ctx_queries else 0))
    ctx_blk = batch * n_lat // n_ctx
    k_col = C_WK // 256

    def q_block(b, t):
        return jnp.where(t < nlt, b * nlt + t, batch * nlt + b * nct + (t - nlt))

    def near(i):
        return lambda b, t: b * nlb + jnp.clip(2 * t - 1 + i, 0, nlb - 1)

    k_specs = [pl.BlockSpec((CHUNK, 256), (lambda f: lambda b, t: (f(b, t), k_col))(near(i))) for i in range(4)]
    vt_specs = [pl.BlockSpec((LANES, CHUNK), (lambda f: lambda b, t: (VT_WINDOW, f(b, t)))(near(i)))
                for i in range(4)]
    return pl.pallas_call(
        functools.partial(_win_kernel, n_lat_tiles=nlt, n_lat_blocks=nlb),
        out_shape=jax.ShapeDtypeStruct((out_rows, WIDTH), BF16),
        grid=(batch, nq),
        in_specs=[pl.BlockSpec(memory_space=pltpu.SMEM),
                  pl.BlockSpec((tq, WIDTH), lambda b, t: (q_block(b, t), C_WQ // WIDTH)),
                  pl.BlockSpec((n_ctx, 256), lambda b, t: (ctx_blk + b, k_col)),
                  pl.BlockSpec((LANES, n_ctx), lambda b, t: (VT_WINDOW, ctx_blk + b))]
                 + k_specs + vt_specs,
        out_specs=pl.BlockSpec((tq, WIDTH), lambda b, t: (q_block(b, t), 0)),
        compiler_params=_params(("parallel", "parallel")),
        name="window_attention",
    )(sink, p, p, vt, p, p, p, p, vt, vt, vt, vt)


def _glb_kernel(q_ref, kc_ref, vtc_ref, kl_ref, vtl_ref, o_ref, qt_ref, knorm_ref, *scratch, n_lat_tiles, tk, span_chunks):
    heads = Q_HEADS // KV_HEADS
    acc_refs = scratch[0:heads]
    sc_refs = scratch[heads:2 * heads]
    ring = [scratch[(2 + i) * heads:(3 + i) * heads] for i in range(N_SCORE_BUFS)]
    base = (2 + N_SCORE_BUFS) * heads
    pc_refs = scratch[base:base + heads]
    p_ring = scratch[base + heads:base + heads + 2]
    qi = pl.program_id(2)
    tq = q_ref.shape[0]
    n_chunks = kl_ref.shape[0] // tk

    def k_chunk(c):
        off = pl.multiple_of(jnp.minimum(c, n_chunks - 1) * tk, tk)
        return kl_ref[pl.ds(off, tk), :]

    def vt_chunk(c):
        return vtl_ref[:, pl.ds(pl.multiple_of(c * tk, tk), tk)]

    def with_ones(vt):
        ones = jnp.ones((acc_refs[0].shape[0] - HEAD_DIM, vt.shape[1]), BF16)
        return jnp.concatenate([vt, ones], axis=0)

    @pl.when(qi == 0)
    def _():
        def sq_norm(k):
            kf = k.astype(F32)
            return jnp.max(jnp.sum(kf * kf, axis=1, keepdims=True), axis=0, keepdims=True)

        def body(c, best):
            return jnp.maximum(best, sq_norm(k_chunk(c)))
        best = lax.fori_loop(0, n_chunks, body, sq_norm(kc_ref[...]))
        knorm_ref[...] = jnp.sqrt(0.5 * best)

    for t in range(heads // 2):
        qf = q_ref[:, t * LANES:(t + 1) * LANES].astype(F32)
        for e in range(2):
            h = 2 * t + e
            qt_ref[:, h * tq:(h + 1) * tq] = jnp.where(_half_mask(e), qf, 0.0).T.astype(BF16)
    for h in range(heads):
        acc_refs[h][...] = jnp.zeros_like(acc_refs[h])

    def q_t(h):
        return qt_ref[:, h * tq:(h + 1) * tq]

    bound = [jnp.sqrt(jnp.sum(jnp.square(q_t(h).astype(F32)), axis=0, keepdims=True)) * knorm_ref[...]
             for h in range(heads)]
    bounded = jnp.max(functools.reduce(jnp.maximum, bound)) <= SAFE_LOG2_BOUND

    span = span_chunks * tk
    n_spans = n_chunks // span_chunks
    n_steps = n_spans * heads

    def produce_ctx():
        for h in range(heads):
            s = jnp.dot(kc_ref[...], q_t(h), preferred_element_type=F32)
            pc_refs[h][...] = jnp.exp2(s - bound[h]).astype(BF16)

    def consume_ctx():
        vta = with_ones(vtc_ref[...])
        for h in range(heads):
            acc_refs[h][...] += jnp.dot(vta, pc_refs[h][...], preferred_element_type=F32)

    def produce(p_ref, sp, h):
        k = kl_ref[pl.ds(pl.multiple_of(sp * span, span), span), :]
        s = jnp.dot(k, q_t(h), preferred_element_type=F32)
        p_ref[...] = jnp.exp2(s - bound[h]).astype(BF16)

    def consume(p_ref, sp, h):
        vta = with_ones(vtl_ref[:, pl.ds(pl.multiple_of(sp * span, span), span)])
        acc_refs[h][...] += jnp.dot(vta, p_ref[...], preferred_element_type=F32)

    @pl.when(bounded)
    def _():
        produce_ctx()

        @pl.when(qi >= n_lat_tiles)
        def _():
            consume_ctx()

        @pl.when(qi < n_lat_tiles)
        def _():
            produce(p_ring[0], 0, 0)
            consume_ctx()

            unroll = heads * (2 if n_spans % 2 == 0 else 1)

            def steps(j, last):
                for i in range(unroll):
                    step = unroll * j + i
                    if not (last and i == unroll - 1):
                        produce(p_ring[(i + 1) % 2], (step + 1) // heads, (i + 1) % heads)
                    consume(p_ring[i % 2], step // heads, i % heads)

            def body(j, carry):
                steps(j, False)
                return carry
            n_iter = n_steps // unroll
            lax.fori_loop(0, n_iter - 1, body, 0)
            steps(n_iter - 1, True)

    neg_inf = (jnp.full((1, tq), -jnp.inf, F32),) * heads

    def fill(s_refs, k, m_run):
        out = []
        for h in range(heads):
            s = jnp.dot(k, q_t(h), preferred_element_type=F32)
            s_refs[h][...] = s
            out.append(jnp.maximum(m_run[h], jnp.max(s, axis=0, keepdims=True)))
        return tuple(out)

    def drain(s_refs, vt, m_run, m_acc):
        vta = with_ones(vt)
        for h in range(heads):
            alpha = jnp.exp2(m_acc[h] - m_run[h])
            pt = jnp.exp2(s_refs[h][...] - m_run[h]).astype(BF16)
            acc_refs[h][...] = alpha * acc_refs[h][...] + jnp.dot(vta, pt, preferred_element_type=F32)
        return m_run

    @pl.when(jnp.logical_not(bounded))
    def _():
        m_ctx = fill(sc_refs, kc_ref[...], neg_inf)

        @pl.when(qi >= n_lat_tiles)
        def _():
            drain(sc_refs, vtc_ref[...], m_ctx, neg_inf)

        @pl.when(qi < n_lat_tiles)
        def _():
            m_one = fill(ring[0], k_chunk(0), m_ctx)
            m_two = fill(ring[1], k_chunk(1), m_one)
            m_acc = drain(sc_refs, vtc_ref[...], m_one, neg_inf)

            def body(j, carry):
                m_run, m_acc = carry
                for i in range(N_SCORE_BUFS):
                    c = N_SCORE_BUFS * j + i
                    m_next = fill(ring[(i + 2) % N_SCORE_BUFS], k_chunk(c + 2), m_run)
                    m_acc = drain(ring[i], vt_chunk(c), m_run, m_acc)
                    m_run = m_next
                return m_run, m_acc
            lax.fori_loop(0, n_chunks // N_SCORE_BUFS, body, (m_two, m_acc))

    for t in range(heads // 2):
        pair = []
        for e in range(2):
            a = acc_refs[2 * t + e][...]
            pair.append(a[:HEAD_DIM, :] / a[HEAD_DIM:HEAD_DIM + 1, :])
        o_ref[:, t * LANES:(t + 1) * LANES] = jnp.concatenate(pair, axis=0).T.astype(o_ref.dtype)


def _global_attention(p, vt, batch, n_lat, n_ctx, ctx_queries):
    tq = 256
    tk = min(512, n_lat // N_SCORE_BUFS)
    assert n_lat % (tk * N_SCORE_BUFS) == 0 and tk % LANES == 0
    span_chunks = math.gcd(SPAN_CHUNKS, n_lat // tk)
    nl, nc = n_lat // tq, n_ctx // tq
    nq = nl + (nc if ctx_queries else 0)
    out_rows = batch * (n_lat + (n_ctx if ctx_queries else 0))
    ctx_base = batch * nl
    heads = Q_HEADS // KV_HEADS
    acc_rows = HEAD_DIM + 16

    def q_block(b, qi):
        return jnp.where(qi < nl, b * nl + qi, ctx_base + b * nc + (qi - nl))

    ctx_blk = batch * n_lat // n_ctx
    return pl.pallas_call(
        functools.partial(_glb_kernel, n_lat_tiles=nl, tk=tk, span_chunks=span_chunks),
        out_shape=jax.ShapeDtypeStruct((out_rows, WIDTH), BF16),
        grid=(batch, KV_HEADS, nq),
        in_specs=[pl.BlockSpec((tq, 2 * LANES), lambda b, g, qi: (q_block(b, qi), C_GQ // 256 + g)),
                  pl.BlockSpec((n_ctx, LANES), lambda b, g, qi: (ctx_blk + b, C_GK // LANES + g)),
                  pl.BlockSpec((HEAD_DIM, n_ctx), lambda b, g, qi: (KV_HEADS * VT_GLOBAL + g, ctx_blk + b)),
                  pl.BlockSpec((n_lat, LANES), lambda b, g, qi: (b, C_GK // LANES + g)),
                  pl.BlockSpec((HEAD_DIM, n_lat), lambda b, g, qi: (KV_HEADS * VT_GLOBAL + g, b))],
        out_specs=pl.BlockSpec((tq, 2 * LANES), lambda b, g, qi: (q_block(b, qi), g)),
        scratch_shapes=([pltpu.VMEM((LANES, heads * tq), BF16), pltpu.VMEM((1, 1), F32)]
                        + [pltpu.VMEM((acc_rows, tq), F32)] * heads
                        + [pltpu.VMEM((n_ctx, tq), F32)] * heads
                        + [pltpu.VMEM((tk, tq), F32)] * (N_SCORE_BUFS * heads)
                        + [pltpu.VMEM((n_ctx, tq), BF16)] * heads
                        + [pltpu.VMEM((span_chunks * tk, tq), BF16)] * 2),
        compiler_params=_params(("parallel", "parallel", "arbitrary")),
        name="global_attention",
    )(p, p, vt, p, vt)


def _merge_kernel(*refs, lat_tiles):
    n_x = 1 if lat_tiles is None else 2
    x_refs = refs[:n_x]
    (mod_ref, hf_ref, hb_ref, og_ref, yb_ref, yc_ref, ga_ref, gb_ref, gc_ref,
     mlg_ref, wbr_ref, wo_ref, o_ref) = refs[n_x:]
    tm = o_ref.shape[0]
    proj_b = jnp.dot(yb_ref[...], wbr_ref[1], preferred_element_type=F32)
    proj_c = jnp.dot(yc_ref[...], wbr_ref[2], preferred_element_type=F32)
    hs = hf_ref[...].astype(F32) + hb_ref[...].astype(F32)
    parts = []
    for t in range(ML_HEADS):
        ht = hs[:, t * ML_DIM:(t + 1) * ML_DIM]
        ms = jnp.mean(ht * ht, axis=-1, keepdims=True)
        parts.append(ht * lax.rsqrt(ms + EPS))
    ya = (jnp.concatenate(parts, axis=1) * mlg_ref[...] * _sigmoid(og_ref[...].astype(F32))).astype(BF16)
    proj_a = jnp.dot(ya, wbr_ref[0], preferred_element_type=F32)
    merged = (_sigmoid(gb_ref[...].astype(F32)) * proj_b + _sigmoid(gc_ref[...].astype(F32)) * proj_c
              + _sigmoid(ga_ref[...].astype(F32)) * proj_a)
    out = jnp.dot(merged.astype(BF16), wo_ref[...], preferred_element_type=F32)

    def gated_residual(x_ref):
        for sb in range(tm // ROW_BLOCK):
            rows = slice(sb * ROW_BLOCK, (sb + 1) * ROW_BLOCK)
            o_ref[rows, :] = x_ref[rows, :] + mod_ref[sb, 2:3, :] * out[rows, :]

    if lat_tiles is None:
        gated_residual(x_refs[0])
    else:
        pl.when(pl.program_id(0) < lat_tiles)(lambda: gated_residual(x_refs[0]))
        pl.when(pl.program_id(0) >= lat_tiles)(lambda: gated_residual(x_refs[1]))


def _merge(xs, modtab, p, hf, hb, yb, yc, mlg, wbr, wo, n_rows, lat_tiles):
    tm = ROW_TILE
    nb = tm // ROW_BLOCK
    rows = sum(x.shape[0] for x in xs)
    row = lambda i: (i, 0)
    gate = lambda k: pl.BlockSpec((tm, D_MODEL), lambda i: (i, C_GATE // D_MODEL + k))
    return pl.pallas_call(
        functools.partial(_merge_kernel, lat_tiles=lat_tiles if len(xs) == 2 else None),
        out_shape=jax.ShapeDtypeStruct((rows, D_MODEL), F32),
        grid=(n_rows // tm,),
        in_specs=_token_tile_specs(xs, tm, lat_tiles) + [
                  pl.BlockSpec((nb, 6, D_MODEL), lambda i: (i, 0, 0)),
                  pl.BlockSpec((tm, WIDTH), row),
                  pl.BlockSpec((tm, WIDTH), row),
                  pl.BlockSpec((tm, WIDTH), lambda i: (i, C_MLO // WIDTH)),
                  pl.BlockSpec((tm, WIDTH), row),
                  pl.BlockSpec((tm, WIDTH), row),
                  gate(0), gate(1), gate(2),
                  _resident((1, WIDTH)),
                  _resident((3, WIDTH, D_MODEL)),
                  _resident((D_MODEL, D_MODEL))],
        out_specs=pl.BlockSpec((tm, D_MODEL), row),
        input_output_aliases={0: 0} if len(xs) == 1 else {},
        compiler_params=_params(("parallel",)),
        name="merge",
    )(*xs, modtab, hf, hb, p, yb, yc, p, p, p, mlg, wbr, wo)


def _ffn_kernel(x_ref, mod_ref, g2_ref, w1_ref, w3_ref, w2_ref, gf_ref, o_ref, *, final):
    tm = x_ref.shape[0]
    d_ff = w1_ref.shape[1]
    split = pl.cdiv(d_ff // MXU_TILE, 2) * MXU_TILE
    for sb in range(tm // ROW_BLOCK):
        rows = slice(sb * ROW_BLOCK, (sb + 1) * ROW_BLOCK)
        xs = x_ref[rows, :]
        ms = jnp.mean(xs * xs, axis=-1, keepdims=True)
        y = xs * lax.rsqrt(ms + EPS) * g2_ref[...]
        h = (y * (1.0 + mod_ref[sb, 4:5, :]) + mod_ref[sb, 3:4, :]).astype(BF16)
        out = None
        for c0, c1 in ((0, split), (split, d_ff)):
            a = jnp.dot(h, w1_ref[:, c0:c1], preferred_element_type=F32)
            b = jnp.dot(h, w3_ref[:, c0:c1], preferred_element_type=F32)
            z = (a * _sigmoid(a) * b).astype(BF16)
            part = jnp.dot(z, w2_ref[c0:c1, :], preferred_element_type=F32)
            out = part if out is None else out + part
        xn = xs + mod_ref[sb, 5:6, :] * out
        if final:
            ms = jnp.mean(xn * xn, axis=-1, keepdims=True)
            xn = xn * lax.rsqrt(ms + EPS) * gf_ref[...]
        o_ref[rows, :] = xn


def _ffn(x, modtab, g2, w1, w3, w2, gfin, n_rows, final):
    tm = ROW_TILE
    nb = tm // ROW_BLOCK
    d_ff = w1.shape[1]
    row = lambda i: (i, 0)
    out_rows = n_rows if final else x.shape[0]
    return pl.pallas_call(
        functools.partial(_ffn_kernel, final=final),
        out_shape=jax.ShapeDtypeStruct((out_rows, D_MODEL), F32),
        grid=(n_rows // tm,),
        in_specs=[pl.BlockSpec((tm, D_MODEL), row),
                  pl.BlockSpec((nb, 6, D_MODEL), lambda i: (i, 0, 0)),
                  _resident((1, D_MODEL)),
                  _resident((D_MODEL, d_ff)),
                  _resident((D_MODEL, d_ff)),
                  _resident((d_ff, D_MODEL)),
                  _resident((1, D_MODEL))],
        out_specs=pl.BlockSpec((tm, D_MODEL), row),
        input_output_aliases={} if final else {0: 0},
        compiler_params=_params(("parallel",)),
        name="ffn",
    )(x, modtab, g2, w1, w3, w2, gfin)


def _dup_halves(w, base):
    h0 = w[..., base:base + HEAD_DIM]
    h1 = w[..., base + HEAD_DIM:base + 2 * HEAD_DIM]
    return [h0, h0, h1, h1]


def _arrange_in_proj(w):
    o_gate_ml = 4 * WIDTH
    o_wq = o_gate_ml + 2 * N_GATE
    o_wk, o_wv = o_wq + WIDTH, o_wq + WIDTH + 128
    o_gq = o_wv + 128
    o_gk, o_gv = o_gq + WIDTH, o_gq + WIDTH + 128
    o_gate = o_gv + 128
    main = jnp.concatenate(
        [w[..., o_gate:o_gate + 3 * D_MODEL], w[..., 0:WIDTH], w[..., 2 * WIDTH:4 * WIDTH],
         w[..., o_wq:o_wq + WIDTH]]
        + _dup_halves(w, o_wk) + [w[..., o_gq:o_gq + WIDTH]] + _dup_halves(w, o_gk), axis=-1)
    values = jnp.concatenate([w[..., o_gv:o_gv + 128], w[..., o_wv:o_wv + 128]], axis=-1)
    return main, w[..., o_gate_ml:o_gate_ml + 2 * N_GATE], w[..., WIDTH:2 * WIDTH], values


def _rope_tables(n_lat):
    t = jnp.arange(n_lat)
    quarter = HEAD_DIM // 4
    inv = ROPE_THETA ** (-jnp.arange(0, 2 * quarter, 2, dtype=F32) / (2 * quarter))
    ang_r = (t // GRID_W).astype(F32)[:, None] * inv
    ang_c = (t % GRID_W).astype(F32)[:, None] * inv
    cos = jnp.concatenate([jnp.cos(ang_r)] * 2 + [jnp.cos(ang_c)] * 2, axis=1)
    sin = jnp.concatenate([-jnp.sin(ang_r), jnp.sin(ang_r), -jnp.sin(ang_c), jnp.sin(ang_c)], axis=1)
    reps = LANES // HEAD_DIM
    return (jnp.concatenate([jnp.tile(cos, (1, reps)), jnp.ones((ROW_TILE, LANES), F32)], axis=0),
            jnp.concatenate([jnp.tile(sin, (1, reps)), jnp.zeros((ROW_TILE, LANES), F32)], axis=0))


def kernel(x, c, ctx, c_ctx, w_mod, b_mod, norm1_g, w_in, b_in, ml_norm_g, win_sink, qn_g, kn_g,
           w_br, w_o, norm2_g, w_ff1, w_ff3, w_ff2, final_g):
    batch, n_lat, d = x.shape
    n_ctx = ctx.shape[1]
    depth = w_mod.shape[0]
    assert d == D_MODEL and n_lat % ROW_TILE == 0 and n_lat % GRID_W == 0
    assert (batch * n_ctx) % ROW_TILE == 0 and n_ctx % ROW_BLOCK == 0 and (batch * n_lat) % n_ctx == 0
    lat_rows, ctx_rows = batch * n_lat, batch * n_ctx

    xs = (x.reshape(lat_rows, d), ctx.reshape(ctx_rows, d))
    cvec = jnp.concatenate([c, c_ctx[None, :], jnp.zeros((8 - batch - 1, d), F32)], axis=0)
    block_class = np.concatenate([np.repeat(np.arange(batch), n_lat // ROW_BLOCK),
                                  np.full(ctx_rows // ROW_BLOCK, batch)])
    cos, sin = _rope_tables(n_lat)
    lat_tiles = lat_rows // ROW_TILE
    avg = jnp.asarray(np.kron(np.eye(LANES // HEAD_DIM), np.full((HEAD_DIM, HEAD_DIM), 1.0 / HEAD_DIM)), BF16)

    out = None
    for l in range(depth):
        last = l == depth - 1
        mod = _mod_vectors(cvec, w_mod[l], b_mod[l]).reshape(8, 6, d)
        modtab = mod[block_class]

        w_main, w_gate, w_mlk, w_val = _arrange_in_proj(w_in[l])
        b_main, b_gate, b_mlk, b_val = _arrange_in_proj(b_in[l][None, :])
        p, gt, kt, vt = _in_projection(
            xs, modtab, norm1_g[l][None, :], w_main.astype(BF16), b_main,
            w_gate.T.astype(BF16), b_gate.T, w_mlk.T.astype(BF16), b_mlk.T, w_val.T.astype(BF16), b_val.T,
            cos, sin, jnp.tile(qn_g[l], 2)[None, :], jnp.tile(kn_g[l], 2)[None, :], avg, batch, n_lat)

        hf, hb = _mlstm(p, kt, gt, batch, n_lat, n_ctx)
        yb = _window_attention(p, vt, win_sink[l], batch, n_lat, n_ctx, ctx_queries=not last)
        yc = _global_attention(p, vt, batch, n_lat, n_ctx, ctx_queries=not last)

        n_rows = lat_rows if last else lat_rows + ctx_rows
        merged = _merge(xs, modtab, p, hf, hb, yb, yc, ml_norm_g[l].reshape(1, WIDTH),
                        w_br[l].astype(BF16), w_o[l].astype(BF16), n_rows, lat_tiles)
        out = _ffn(merged, modtab, norm2_g[l][None, :], w_ff1[l].astype(BF16), w_ff3[l].astype(BF16),
                   w_ff2[l].astype(BF16), final_g[None, :], n_rows, final=last)
        xs = (out,)
    return out.reshape(batch, n_lat, d)
```

```python
import functools
import math

import jax
import jax.numpy as jnp
import numpy as np
from jax import lax
from jax.experimental import pallas as pl
from jax.experimental.pallas import tpu as pltpu

F32 = jnp.float32
BF16 = jnp.bfloat16

D_MODEL = 1024
GRID_W = 64
CHUNK = 128
HEAD_DIM = 64
ROPE_THETA = 10000.0
EPS = 1e-6
ML_HEADS = 4
ML_DIM = 128
Q_HEADS = 8
KV_HEADS = 2
WIDTH = 512
N_GATE = 2 * ML_HEADS

LANES = 128
MXU_TILE = 256
ROW_BLOCK = 256
ROW_TILE = 512
VMEM_LIMIT = 56 * 1024 * 1024

C_GATE = 0
C_MLQ, C_MLV, C_MLO = 3072, 3584, 4096
C_WQ, C_WK = 4608, 5120
C_GQ, C_GK = 5376, 5888
KV_COLS = KV_HEADS * LANES
N_PROJ = C_GK + KV_COLS
Q_TILE = 256
VT_GLOBAL, VT_WINDOW = 0, 1
LOG2E = 1.4426950408889634
WIN_LEAD = 3
KEY_CHUNK = 512
MOD_TILE = 1536
N_SCORE_BUFS = 4
SPAN_CHUNKS = 4
SAFE_LOG2_BOUND = 50.0

NT_DIMS = (((1,), (1,)), ((), ()))


def _params(sem, vmem=VMEM_LIMIT):
    return pltpu.CompilerParams(dimension_semantics=sem, vmem_limit_bytes=vmem)


def _resident(shape):
    nd = len(shape)
    return pl.BlockSpec(shape, lambda *_: (0,) * nd, pipeline_mode=pl.Buffered(1))


def _sigmoid(x):
    return 1.0 / (1.0 + jnp.exp2(x * -LOG2E))


def _log_sigmoid(x):
    return jnp.minimum(x, 0.0) - jnp.log(1.0 + jnp.exp(-jnp.abs(x)))


def _mod_kernel(c_ref, w_ref, b_ref, o_ref):
    c = c_ref[...]
    s = c * _sigmoid(c)
    o_ref[...] = jnp.dot(s, w_ref[...], preferred_element_type=F32) + b_ref[...]


def _mod_vectors(cvec, w_mod, b_mod):
    n_out = w_mod.shape[1]
    tn = MOD_TILE
    return pl.pallas_call(
        _mod_kernel,
        out_shape=jax.ShapeDtypeStruct((cvec.shape[0], n_out), F32),
        grid=(n_out // tn,),
        in_specs=[pl.BlockSpec(cvec.shape, lambda j: (0, 0)),
                  pl.BlockSpec((D_MODEL, tn), lambda j: (0, j)),
                  pl.BlockSpec((1, tn), lambda j: (0, j))],
        out_specs=pl.BlockSpec((cvec.shape[0], tn), lambda j: (0, j)),
        compiler_params=_params(("parallel",)),
        name="mod_vectors",
    )(cvec, w_mod, b_mod.reshape(1, n_out))


def _rope(acc, cos, sin, first_half):
    w = acc.shape[1]
    reps = w // LANES
    if reps > 1:
        cos = jnp.concatenate([cos] * reps, axis=1)
        sin = jnp.concatenate([sin] * reps, axis=1)
    ahead = pltpu.roll(acc, w - 16, axis=1)
    behind = pltpu.roll(acc, 16, axis=1)
    return acc * cos + jnp.where(first_half, ahead, behind) * sin


def _head_rms(acc, avg, gain):
    sq = (acc * acc).astype(BF16)
    outs = []
    for t in range(acc.shape[1] // LANES):
        sl = slice(t * LANES, (t + 1) * LANES)
        ms = jnp.dot(sq[:, sl], avg, preferred_element_type=F32)
        outs.append(acc[:, sl] * lax.rsqrt(ms + EPS) * gain)
    return jnp.concatenate(outs, axis=1)


def _inproj_kernel(*refs, lat_tiles):
    n_x = 1 if lat_tiles is None else 2
    x_refs = refs[:n_x]
    (mod_ref, g1_ref, w_ref, b_ref, wgt_ref, bgt_ref, wkt_ref, bkt_ref, wvt_ref, bvt_ref,
     cos_ref, sin_ref, qg_ref, kg_ref, avg_ref, p_ref, gt_ref, kt_ref, vt_ref, h_ref) = refs[n_x:]
    tm = h_ref.shape[0]

    def modulated_norm(x_ref):
        for sb in range(tm // ROW_BLOCK):
            r0 = sb * ROW_BLOCK
            xs = x_ref[r0:r0 + ROW_BLOCK, :]
            ms = jnp.mean(xs * xs, axis=-1, keepdims=True)
            y = xs * lax.rsqrt(ms + EPS) * g1_ref[...]
            shift = mod_ref[sb, 0:1, :]
            scale = mod_ref[sb, 1:2, :]
            h_ref[r0:r0 + ROW_BLOCK, :] = (y * (1.0 + scale) + shift).astype(BF16)

    if lat_tiles is None:
        modulated_norm(x_refs[0])
    else:
        pl.when(pl.program_id(0) < lat_tiles)(lambda: modulated_norm(x_refs[0]))
        pl.when(pl.program_id(0) >= lat_tiles)(lambda: modulated_norm(x_refs[1]))

    def first_half(width):
        return (lax.broadcasted_iota(jnp.int32, (1, width), 1) % 32) < 16

    avg = avg_ref[...]
    qg = qg_ref[...]
    kg = kg_ref[...]
    q_scale = HEAD_DIM ** -0.5

    h = h_ref[...]
    cos = cos_ref[...]
    sin = sin_ref[...]

    def proj(c0, width):
        return (jnp.dot(h, w_ref[:, c0:c0 + width], preferred_element_type=F32)
                + b_ref[:, c0:c0 + width])

    def store_plain(cols):
        for c0 in cols:
            p_ref[:, c0:c0 + WIDTH] = proj(c0, WIDTH).astype(BF16)

    plain = list(range(C_GATE, C_GATE + 3 * D_MODEL, WIDTH)) + [C_MLQ, C_MLV, C_MLO]
    raw_gq, raw_gk = proj(C_GQ, WIDTH), proj(C_GK, KV_COLS)
    raw_wq, raw_wk = proj(C_WQ, WIDTH), proj(C_WK, KV_COLS)
    store_plain(plain[:3])
    gq = _rope(_head_rms(raw_gq, avg, qg), cos, sin, first_half(WIDTH)) * (q_scale * LOG2E)
    p_ref[:, C_GQ:C_GQ + WIDTH] = gq.astype(BF16)
    gk = _rope(_head_rms(raw_gk, avg, kg), cos, sin, first_half(KV_COLS))
    p_ref[:, C_GK:C_GK + KV_COLS] = gk.astype(BF16)
    p_ref[:, C_WQ:C_WQ + WIDTH] = (_rope(raw_wq, cos, sin, first_half(WIDTH)) * (q_scale * LOG2E)).astype(BF16)
    p_ref[:, C_WK:C_WK + KV_COLS] = _rope(raw_wk, cos, sin, first_half(KV_COLS)).astype(BF16)
    store_plain(plain[3:])

    wt = jnp.concatenate([wgt_ref[...], wkt_ref[...], wvt_ref[...]], axis=0)
    bt = jnp.concatenate([bgt_ref[...], bkt_ref[...], bvt_ref[...]], axis=0)
    pt = lax.dot_general(wt, h, NT_DIMS, preferred_element_type=F32) + bt
    n_g, n_k = 2 * N_GATE, 2 * N_GATE + WIDTH
    grow = lax.broadcasted_iota(jnp.int32, (n_g, 1), 0)
    gt_ref[...] = jnp.where(grow >= N_GATE, _log_sigmoid(pt[:n_g, :]), pt[:n_g, :])
    kt_ref[...] = pt[n_g:n_k, :].astype(BF16)
    vt_ref[...] = pt[n_k:, :].astype(BF16)


def _token_tile_specs(xs, tm, lat_tiles):
    if len(xs) == 1:
        return [pl.BlockSpec((tm, D_MODEL), lambda i: (i, 0))]
    ctx_tiles = xs[1].shape[0] // tm
    return [pl.BlockSpec((tm, D_MODEL), lambda i: (jnp.minimum(i, lat_tiles - 1), 0)),
            pl.BlockSpec((tm, D_MODEL), lambda i: (jnp.clip(i - lat_tiles, 0, ctx_tiles - 1), 0))]


def _in_projection(xs, modtab, g1, w, b, wgt, bgt, wkt, bkt, wvt, bvt, cos, sin, qg, kg, avg, batch, n_lat):
    rows = sum(x.shape[0] for x in xs)
    tm = ROW_TILE
    nb = tm // ROW_BLOCK
    kv = 2 * KV_HEADS * HEAD_DIM
    seq_tiles = n_lat // tm
    lat_tiles = batch * seq_tiles

    def rope_block(i):
        return (jnp.where(i < lat_tiles, i % seq_tiles, seq_tiles), 0)

    return pl.pallas_call(
        functools.partial(_inproj_kernel, lat_tiles=lat_tiles if len(xs) == 2 else None),
        out_shape=(jax.ShapeDtypeStruct((rows, N_PROJ), BF16),
                   jax.ShapeDtypeStruct((2 * N_GATE, rows), F32),
                   jax.ShapeDtypeStruct((WIDTH, rows), BF16),
                   jax.ShapeDtypeStruct((kv, rows), BF16)),
        grid=(rows // tm,),
        in_specs=_token_tile_specs(xs, tm, lat_tiles) + [
                  pl.BlockSpec((nb, 6, D_MODEL), lambda i: (i, 0, 0)),
                  _resident((1, D_MODEL)),
                  _resident((D_MODEL, N_PROJ)),
                  _resident((1, N_PROJ)),
                  _resident((2 * N_GATE, D_MODEL)),
                  _resident((2 * N_GATE, 1)),
                  _resident((WIDTH, D_MODEL)),
                  _resident((WIDTH, 1)),
                  _resident((kv, D_MODEL)),
                  _resident((kv, 1)),
                  pl.BlockSpec((tm, LANES), rope_block),
                  pl.BlockSpec((tm, LANES), rope_block),
                  _resident((1, LANES)),
                  _resident((1, LANES)),
                  _resident((LANES, LANES))],
        out_specs=(pl.BlockSpec((tm, N_PROJ), lambda i: (i, 0)),
                   pl.BlockSpec((2 * N_GATE, tm), lambda i: (0, i)),
                   pl.BlockSpec((WIDTH, tm), lambda i: (0, i)),
                   pl.BlockSpec((kv, tm), lambda i: (0, i))),
        scratch_shapes=[pltpu.VMEM((tm, D_MODEL), BF16)],
        compiler_params=_params(("parallel",)),
        name="in_projection",
    )(*xs, modtab, g1, w, b, wgt, bgt, wkt, bkt, wvt, bvt, cos, sin, qg, kg, avg)


def _scan_lanes(x, op, fill, reverse):
    n = x.shape[1]
    lane = lax.broadcasted_iota(jnp.int32, x.shape, 1)
    sh = 1
    while sh < n:
        if reverse:
            moved = jnp.where(lane < n - sh, pltpu.roll(x, n - sh, axis=1), fill)
        else:
            moved = jnp.where(lane >= sh, pltpu.roll(x, sh, axis=1), fill)
        x = op(x, moved)
        sh *= 2
    return x


def _mlstm_kernel(qvf_ref, ktf_ref, gtf_ref, qvb_ref, ktb_ref, gtb_ref, hf_ref, hb_ref, cn_ref, m_ref):
    @pl.when(pl.program_id(1) == 0)
    def _():
        cn_ref[...] = jnp.zeros_like(cn_ref)
        m_ref[...] = jnp.zeros_like(m_ref)

    L = CHUNK
    row = lax.broadcasted_iota(jnp.int32, (L, L), 0)
    col = lax.broadcasted_iota(jnp.int32, (L, L), 1)
    scale = ML_DIM ** -0.5
    ones_v = jnp.ones((L, ML_DIM), BF16)
    pending = []
    for d, (qv_ref, kt_ref, gt_ref, h_ref) in enumerate(((qvf_ref, ktf_ref, gtf_ref, hf_ref),
                                                          (qvb_ref, ktb_ref, gtb_ref, hb_ref))):
        seen = (col <= row) if d == 0 else (col >= row)
        g0 = d * ML_HEADS
        ic = gt_ref[g0:g0 + ML_HEADS, :]
        lf = gt_ref[N_GATE + g0:N_GATE + g0 + ML_HEADS, :]
        r = ic - _scan_lanes(lf, jnp.add, 0.0, d == 1)
        r_max = jnp.max(r, axis=1, keepdims=True)
        b_end = jnp.sum(lf, axis=1, keepdims=True)
        for hh in range(ML_HEADS):
            idx = g0 + hh
            lanes = slice(hh * ML_DIM, (hh + 1) * ML_DIM)
            r_row = r[hh:hh + 1, :]
            m_old = m_ref[idx]
            m_end = jnp.maximum(m_old, r_max[hh:hh + 1, :])
            q = qv_ref[:, lanes]
            vo = jnp.concatenate([qv_ref[:, WIDTH + hh * ML_DIM:WIDTH + (hh + 1) * ML_DIM], ones_v], axis=1)
            kt = kt_ref[lanes, :]
            qsb = (q.astype(F32) * scale).astype(BF16)
            cn = cn_ref[idx]
            s_raw = jnp.dot(qsb, kt, preferred_element_type=F32)
            q_cn = jnp.dot(qsb, cn.astype(BF16), preferred_element_type=F32)
            kwt = (kt.astype(F32) * jnp.exp(r_row - m_end)).astype(BF16)
            cn_ref[idx] = jnp.exp(m_old - m_end) * cn + jnp.dot(kwt, vo, preferred_element_type=F32)
            m_ref[idx] = b_end[hh:hh + 1, :] + m_end
            pending.append((h_ref, lanes, seen, r_row, lf[hh:hh + 1, :], m_old, s_raw, q_cn, vo))

    for h_ref, lanes, seen, r_row, lf_row, m_old, s_raw, q_cn, vo in pending:
        b_col = jnp.sum(jnp.where(seen, lf_row, 0.0), axis=1, keepdims=True)
        m_col = jnp.maximum(m_old, jnp.max(jnp.where(seen, r_row, -jnp.inf), axis=1, keepdims=True))
        w = jnp.exp(jnp.where(seen, r_row - m_col, -jnp.inf))
        a = jnp.exp(m_old - m_col)
        s_vo = jnp.dot((s_raw * w).astype(BF16), vo, preferred_element_type=F32)
        num = a * q_cn[:, :ML_DIM] + s_vo[:, :ML_DIM]
        den = a * q_cn[:, ML_DIM:] + s_vo[:, ML_DIM:]
        hc = num / jnp.maximum(jnp.abs(den), jnp.exp(-(b_col + m_col)))
        h_ref[:, lanes] = hc.astype(h_ref.dtype)


def _mlstm(p, kt, gt, batch, n_lat, n_ctx):
    rows = p.shape[0]
    cl, cc = n_lat // CHUNK, n_ctx // CHUNK
    lat_base, ctx_base = 0, batch * cl

    def fwd_chunk(b, i):
        return jnp.where(i < cc, ctx_base + b * cc + i, lat_base + b * cl + (i - cc))

    def bwd_chunk(b, i):
        return jnp.where(i < cc, ctx_base + b * cc + (cc - 1 - i), lat_base + b * cl + (cl - 1 - (i - cc)))

    def specs(chunk):
        return [pl.BlockSpec((CHUNK, 2 * WIDTH), lambda b, i: (chunk(b, i), C_MLQ // (2 * WIDTH))),
                pl.BlockSpec((WIDTH, CHUNK), lambda b, i: (0, chunk(b, i))),
                pl.BlockSpec((2 * N_GATE, CHUNK), lambda b, i: (0, chunk(b, i)))]

    n_state = 2 * ML_HEADS
    return pl.pallas_call(
        _mlstm_kernel,
        out_shape=(jax.ShapeDtypeStruct((rows, WIDTH), BF16), jax.ShapeDtypeStruct((rows, WIDTH), BF16)),
        grid=(batch, cl + cc),
        in_specs=specs(fwd_chunk) + specs(bwd_chunk),
        out_specs=(pl.BlockSpec((CHUNK, WIDTH), lambda b, i: (fwd_chunk(b, i), 0)),
                   pl.BlockSpec((CHUNK, WIDTH), lambda b, i: (bwd_chunk(b, i), 0))),
        scratch_shapes=[pltpu.VMEM((n_state, ML_DIM, 2 * ML_DIM), F32),
                        pltpu.VMEM((n_state, 1, 1), F32)],
        compiler_params=_params(("parallel", "arbitrary")),
        name="mlstm_scan",
    )(p, kt, gt, p, kt, gt)


def _half_mask(e):
    lane = lax.broadcasted_iota(jnp.int32, (1, LANES), 1)
    return (lane < HEAD_DIM) if e == 0 else (lane >= HEAD_DIM)


def _win_kernel(sink_ref, q_ref, kc_ref, vtc_ref, k0_ref, k1_ref, k2_ref, k3_ref,
                vt0_ref, vt1_ref, vt2_ref, vt3_ref, o_ref, *, n_lat_tiles, n_lat_blocks):
    t = pl.program_id(1)
    tq = q_ref.shape[0]
    n_ctx = kc_ref.shape[0]
    is_lat = t < n_lat_tiles
    kk = lax.broadcasted_iota(jnp.int32, (CHUNK, tq), 0)
    qq = lax.broadcasted_iota(jnp.int32, (CHUNK, tq), 1)
    band = (kk >= qq, qq <= kk + CHUNK, kk <= qq, kk + CHUNK <= qq)
    first = 2 * t - 1
    present = [jnp.logical_and(is_lat, jnp.logical_and(first + i >= 0, first + i < n_lat_blocks))
               for i in range(4)]
    visible = [jnp.logical_and(band[i], present[i]) for i in range(4)]
    k_refs = (k0_ref, k1_ref, k2_ref, k3_ref)
    vt_refs = (vt0_ref, vt1_ref, vt2_ref, vt3_ref)
    ones = jnp.ones((16, n_ctx + 4 * CHUNK), BF16)

    def group_operands(g):
        lanes = slice(g * LANES, (g + 1) * LANES)
        k_all = jnp.concatenate([kc_ref[:, lanes]] + [r[:, lanes] for r in k_refs], axis=0)
        vrows = slice(g * HEAD_DIM, (g + 1) * HEAD_DIM)
        vta = jnp.concatenate(
            [jnp.concatenate([vtc_ref[vrows, :]] + [r[vrows, :] for r in vt_refs], axis=1), ones], axis=0)
        return k_all, vta

    operands = [group_operands(g) for g in range(KV_HEADS)]

    def score(h):
        qf = q_ref[:, (h // 2) * LANES:(h // 2 + 1) * LANES].astype(F32)
        qt = jnp.where(_half_mask(h % 2), qf, 0.0).T.astype(BF16)
        return jnp.dot(operands[h // (Q_HEADS // KV_HEADS)][0], qt, preferred_element_type=F32)

    scores = {h: score(h) for h in range(WIN_LEAD)}
    outs = []
    for h in range(Q_HEADS):
        if h + WIN_LEAD < Q_HEADS:
            scores[h + WIN_LEAD] = score(h + WIN_LEAD)
        s = scores.pop(h)
        sink = sink_ref[h] * LOG2E
        parts = [s[:n_ctx, :]]
        for i in range(4):
            blk = s[n_ctx + i * CHUNK:n_ctx + (i + 1) * CHUNK, :]
            parts.append(jnp.where(visible[i], blk, -jnp.inf))
        s = jnp.concatenate(parts, axis=0)
        m = jnp.maximum(jnp.max(s, axis=0, keepdims=True), sink)
        pt = jnp.exp2(s - m).astype(BF16)
        ol = jnp.dot(operands[h // (Q_HEADS // KV_HEADS)][1], pt, preferred_element_type=F32)
        den = ol[HEAD_DIM:HEAD_DIM + 1, :] + jnp.exp2(sink - m)
        outs.append(ol[:HEAD_DIM, :] / den)
    for tt in range(Q_HEADS // 2):
        o_ref[:, tt * LANES:(tt + 1) * LANES] = (
            jnp.concatenate(outs[2 * tt:2 * tt + 2], axis=0).T.astype(o_ref.dtype))


def _window_attention(p, vt, sink, batch, n_lat, n_ctx, ctx_queries):
    tq = Q_TILE
    nlt, nct = n_lat // tq, n_ctx // tq
    nlb = n_lat // CHUNK
    nq = nlt + (nct if ctx_queries else 0)
    out_rows = batch * (n_lat + (n_ctx if ctx_queries else 0))
    ctx_blk = batch * n_lat // n_ctx
    k_col = C_WK // KV_COLS

    def q_block(b, t):
        return jnp.where(t < nlt, b * nlt + t, batch * nlt + b * nct + (t - nlt))

    def near(i):
        return lambda b, t: b * nlb + jnp.clip(2 * t - 1 + i, 0, nlb - 1)

    k_specs = [pl.BlockSpec((CHUNK, KV_COLS), (lambda f: lambda b, t: (f(b, t), k_col))(near(i))) for i in range(4)]
    vt_specs = [pl.BlockSpec((LANES, CHUNK), (lambda f: lambda b, t: (VT_WINDOW, f(b, t)))(near(i)))
                for i in range(4)]
    return pl.pallas_call(
        functools.partial(_win_kernel, n_lat_tiles=nlt, n_lat_blocks=nlb),
        out_shape=jax.ShapeDtypeStruct((out_rows, WIDTH), BF16),
        grid=(batch, nq),
        in_specs=[pl.BlockSpec(memory_space=pltpu.SMEM),
                  pl.BlockSpec((tq, WIDTH), lambda b, t: (q_block(b, t), C_WQ // WIDTH)),
                  pl.BlockSpec((n_ctx, KV_COLS), lambda b, t: (ctx_blk + b, k_col)),
                  pl.BlockSpec((LANES, n_ctx), lambda b, t: (VT_WINDOW, ctx_blk + b))]
                 + k_specs + vt_specs,
        out_specs=pl.BlockSpec((tq, WIDTH), lambda b, t: (q_block(b, t), 0)),
        compiler_params=_params(("parallel", "parallel")),
        name="window_attention",
    )(sink, p, p, vt, p, p, p, p, vt, vt, vt, vt)


def _glb_kernel(q_ref, kc_ref, vtc_ref, kl_ref, vtl_ref, o_ref, qt_ref, knorm_ref, *scratch, n_lat_tiles, tk, span_chunks):
    heads = Q_HEADS // KV_HEADS
    acc_refs = scratch[0:heads]
    sc_refs = scratch[heads:2 * heads]
    ring = [scratch[(2 + i) * heads:(3 + i) * heads] for i in range(N_SCORE_BUFS)]
    base = (2 + N_SCORE_BUFS) * heads
    pc_refs = scratch[base:base + heads]
    p_ring = scratch[base + heads:base + heads + 2]
    qi = pl.program_id(2)
    tq = q_ref.shape[0]
    n_chunks = kl_ref.shape[0] // tk

    def k_chunk(c):
        off = pl.multiple_of(jnp.minimum(c, n_chunks - 1) * tk, tk)
        return kl_ref[pl.ds(off, tk), :]

    def vt_chunk(c):
        return vtl_ref[:, pl.ds(pl.multiple_of(c * tk, tk), tk)]

    def with_ones(vt):
        ones = jnp.ones((acc_refs[0].shape[0] - HEAD_DIM, vt.shape[1]), BF16)
        return jnp.concatenate([vt, ones], axis=0)

    @pl.when(qi == 0)
    def _():
        def sq_norm(k):
            kf = k.astype(F32)
            return jnp.max(jnp.sum(kf * kf, axis=1, keepdims=True), axis=0, keepdims=True)

        def body(c, best):
            return jnp.maximum(best, sq_norm(k_chunk(c)))
        best = lax.fori_loop(0, n_chunks, body, sq_norm(kc_ref[...]))
        knorm_ref[...] = jnp.sqrt(0.5 * best)

    for t in range(heads // 2):
        qf = q_ref[:, t * LANES:(t + 1) * LANES].astype(F32)
        for e in range(2):
            h = 2 * t + e
            qt_ref[:, h * tq:(h + 1) * tq] = jnp.where(_half_mask(e), qf, 0.0).T.astype(BF16)
    for h in range(heads):
        acc_refs[h][...] = jnp.zeros_like(acc_refs[h])

    def q_t(h):
        return qt_ref[:, h * tq:(h + 1) * tq]

    bound = [jnp.sqrt(jnp.sum(jnp.square(q_t(h).astype(F32)), axis=0, keepdims=True)) * knorm_ref[...]
             for h in range(heads)]
    bounded = jnp.max(functools.reduce(jnp.maximum, bound)) <= SAFE_LOG2_BOUND

    span = span_chunks * tk
    n_spans = n_chunks // span_chunks
    n_steps = n_spans * heads

    def produce_ctx():
        for h in range(heads):
            s = jnp.dot(kc_ref[...], q_t(h), preferred_element_type=F32)
            pc_refs[h][...] = jnp.exp2(s - bound[h]).astype(BF16)

    def consume_ctx():
        vta = with_ones(vtc_ref[...])
        for h in range(heads):
            acc_refs[h][...] += jnp.dot(vta, pc_refs[h][...], preferred_element_type=F32)

    def produce(p_ref, sp, h):
        k = kl_ref[pl.ds(pl.multiple_of(sp * span, span), span), :]
        s = jnp.dot(k, q_t(h), preferred_element_type=F32)
        p_ref[...] = jnp.exp2(s - bound[h]).astype(BF16)

    def consume(p_ref, sp, h):
        vta = with_ones(vtl_ref[:, pl.ds(pl.multiple_of(sp * span, span), span)])
        acc_refs[h][...] += jnp.dot(vta, p_ref[...], preferred_element_type=F32)

    @pl.when(bounded)
    def _():
        produce_ctx()

        @pl.when(qi >= n_lat_tiles)
        def _():
            consume_ctx()

        @pl.when(qi < n_lat_tiles)
        def _():
            produce(p_ring[0], 0, 0)
            consume_ctx()

            unroll = heads * (2 if n_spans % 2 == 0 else 1)

            def steps(j, last):
                for i in range(unroll):
                    step = unroll * j + i
                    if not (last and i == unroll - 1):
                        produce(p_ring[(i + 1) % 2], (step + 1) // heads, (i + 1) % heads)
                    consume(p_ring[i % 2], step // heads, i % heads)

            def body(j, carry):
                steps(j, False)
                return carry
            n_iter = n_steps // unroll
            lax.fori_loop(0, n_iter - 1, body, 0)
            steps(n_iter - 1, True)

    neg_inf = (jnp.full((1, tq), -jnp.inf, F32),) * heads

    def fill(s_refs, k, m_run):
        out = []
        for h in range(heads):
            s = jnp.dot(k, q_t(h), preferred_element_type=F32)
            s_refs[h][...] = s
            out.append(jnp.maximum(m_run[h], jnp.max(s, axis=0, keepdims=True)))
        return tuple(out)

    def drain(s_refs, vt, m_run, m_acc):
        vta = with_ones(vt)
        for h in range(heads):
            alpha = jnp.exp2(m_acc[h] - m_run[h])
            pt = jnp.exp2(s_refs[h][...] - m_run[h]).astype(BF16)
            acc_refs[h][...] = alpha * acc_refs[h][...] + jnp.dot(vta, pt, preferred_element_type=F32)
        return m_run

    @pl.when(jnp.logical_not(bounded))
    def _():
        m_ctx = fill(sc_refs, kc_ref[...], neg_inf)

        @pl.when(qi >= n_lat_tiles)
        def _():
            drain(sc_refs, vtc_ref[...], m_ctx, neg_inf)

        @pl.when(qi < n_lat_tiles)
        def _():
            m_one = fill(ring[0], k_chunk(0), m_ctx)
            m_two = fill(ring[1], k_chunk(1), m_one)
            m_acc = drain(sc_refs, vtc_ref[...], m_one, neg_inf)

            def body(j, carry):
                m_run, m_acc = carry
                for i in range(N_SCORE_BUFS):
                    c = N_SCORE_BUFS * j + i
                    m_next = fill(ring[(i + 2) % N_SCORE_BUFS], k_chunk(c + 2), m_run)
                    m_acc = drain(ring[i], vt_chunk(c), m_run, m_acc)
                    m_run = m_next
                return m_run, m_acc
            lax.fori_loop(0, n_chunks // N_SCORE_BUFS, body, (m_two, m_acc))

    for t in range(heads // 2):
        pair = []
        for e in range(2):
            a = acc_refs[2 * t + e][...]
            pair.append(a[:HEAD_DIM, :] / a[HEAD_DIM:HEAD_DIM + 1, :])
        o_ref[:, t * LANES:(t + 1) * LANES] = jnp.concatenate(pair, axis=0).T.astype(o_ref.dtype)


def _global_attention(p, vt, batch, n_lat, n_ctx, ctx_queries):
    tq = Q_TILE
    tk = min(KEY_CHUNK, n_lat // N_SCORE_BUFS)
    assert n_lat % (tk * N_SCORE_BUFS) == 0 and tk % LANES == 0
    span_chunks = math.gcd(SPAN_CHUNKS, n_lat // tk)
    nl, nc = n_lat // tq, n_ctx // tq
    nq = nl + (nc if ctx_queries else 0)
    out_rows = batch * (n_lat + (n_ctx if ctx_queries else 0))
    ctx_base = batch * nl
    heads = Q_HEADS // KV_HEADS
    acc_rows = HEAD_DIM + 16

    def q_block(b, qi):
        return jnp.where(qi < nl, b * nl + qi, ctx_base + b * nc + (qi - nl))

    ctx_blk = batch * n_lat // n_ctx
    return pl.pallas_call(
        functools.partial(_glb_kernel, n_lat_tiles=nl, tk=tk, span_chunks=span_chunks),
        out_shape=jax.ShapeDtypeStruct((out_rows, WIDTH), BF16),
        grid=(batch, KV_HEADS, nq),
        in_specs=[pl.BlockSpec((tq, 2 * LANES), lambda b, g, qi: (q_block(b, qi), C_GQ // (2 * LANES) + g)),
                  pl.BlockSpec((n_ctx, LANES), lambda b, g, qi: (ctx_blk + b, C_GK // LANES + g)),
                  pl.BlockSpec((HEAD_DIM, n_ctx), lambda b, g, qi: (KV_HEADS * VT_GLOBAL + g, ctx_blk + b)),
                  pl.BlockSpec((n_lat, LANES), lambda b, g, qi: (b, C_GK // LANES + g)),
                  pl.BlockSpec((HEAD_DIM, n_lat), lambda b, g, qi: (KV_HEADS * VT_GLOBAL + g, b))],
        out_specs=pl.BlockSpec((tq, 2 * LANES), lambda b, g, qi: (q_block(b, qi), g)),
        scratch_shapes=([pltpu.VMEM((LANES, heads * tq), BF16), pltpu.VMEM((1, 1), F32)]
                        + [pltpu.VMEM((acc_rows, tq), F32)] * heads
                        + [pltpu.VMEM((n_ctx, tq), F32)] * heads
                        + [pltpu.VMEM((tk, tq), F32)] * (N_SCORE_BUFS * heads)
                        + [pltpu.VMEM((n_ctx, tq), BF16)] * heads
                        + [pltpu.VMEM((span_chunks * tk, tq), BF16)] * 2),
        compiler_params=_params(("parallel", "parallel", "arbitrary")),
        name="global_attention",
    )(p, p, vt, p, vt)


def _merge_kernel(*refs, lat_tiles):
    n_x = 1 if lat_tiles is None else 2
    x_refs = refs[:n_x]
    (mod_ref, hf_ref, hb_ref, og_ref, yb_ref, yc_ref, ga_ref, gb_ref, gc_ref,
     mlg_ref, wbr_ref, wo_ref, o_ref) = refs[n_x:]
    tm = o_ref.shape[0]
    proj_b = jnp.dot(yb_ref[...], wbr_ref[1], preferred_element_type=F32)
    proj_c = jnp.dot(yc_ref[...], wbr_ref[2], preferred_element_type=F32)
    hs = hf_ref[...].astype(F32) + hb_ref[...].astype(F32)
    parts = []
    for t in range(ML_HEADS):
        ht = hs[:, t * ML_DIM:(t + 1) * ML_DIM]
        ms = jnp.mean(ht * ht, axis=-1, keepdims=True)
        parts.append(ht * lax.rsqrt(ms + EPS))
    ya = (jnp.concatenate(parts, axis=1) * mlg_ref[...] * _sigmoid(og_ref[...].astype(F32))).astype(BF16)
    proj_a = jnp.dot(ya, wbr_ref[0], preferred_element_type=F32)
    merged = (_sigmoid(gb_ref[...].astype(F32)) * proj_b + _sigmoid(gc_ref[...].astype(F32)) * proj_c
              + _sigmoid(ga_ref[...].astype(F32)) * proj_a)
    out = jnp.dot(merged.astype(BF16), wo_ref[...], preferred_element_type=F32)

    def gated_residual(x_ref):
        for sb in range(tm // ROW_BLOCK):
            rows = slice(sb * ROW_BLOCK, (sb + 1) * ROW_BLOCK)
            o_ref[rows, :] = x_ref[rows, :] + mod_ref[sb, 2:3, :] * out[rows, :]

    if lat_tiles is None:
        gated_residual(x_refs[0])
    else:
        pl.when(pl.program_id(0) < lat_tiles)(lambda: gated_residual(x_refs[0]))
        pl.when(pl.program_id(0) >= lat_tiles)(lambda: gated_residual(x_refs[1]))


def _merge(xs, modtab, p, hf, hb, yb, yc, mlg, wbr, wo, n_rows, lat_tiles):
    tm = ROW_TILE
    nb = tm // ROW_BLOCK
    rows = sum(x.shape[0] for x in xs)
    row = lambda i: (i, 0)
    gate = lambda k: pl.BlockSpec((tm, D_MODEL), lambda i: (i, C_GATE // D_MODEL + k))
    return pl.pallas_call(
        functools.partial(_merge_kernel, lat_tiles=lat_tiles if len(xs) == 2 else None),
        out_shape=jax.ShapeDtypeStruct((rows, D_MODEL), F32),
        grid=(n_rows // tm,),
        in_specs=_token_tile_specs(xs, tm, lat_tiles) + [
                  pl.BlockSpec((nb, 6, D_MODEL), lambda i: (i, 0, 0)),
                  pl.BlockSpec((tm, WIDTH), row),
                  pl.BlockSpec((tm, WIDTH), row),
                  pl.BlockSpec((tm, WIDTH), lambda i: (i, C_MLO // WIDTH)),
                  pl.BlockSpec((tm, WIDTH), row),
                  pl.BlockSpec((tm, WIDTH), row),
                  gate(0), gate(1), gate(2),
                  _resident((1, WIDTH)),
                  _resident((3, WIDTH, D_MODEL)),
                  _resident((D_MODEL, D_MODEL))],
        out_specs=pl.BlockSpec((tm, D_MODEL), row),
        input_output_aliases={0: 0} if len(xs) == 1 else {},
        compiler_params=_params(("parallel",)),
        name="merge",
    )(*xs, modtab, hf, hb, p, yb, yc, p, p, p, mlg, wbr, wo)


def _ffn_kernel(x_ref, mod_ref, g2_ref, w1_ref, w3_ref, w2_ref, gf_ref, o_ref, *, final):
    tm = x_ref.shape[0]
    d_ff = w1_ref.shape[1]
    split = pl.cdiv(d_ff // MXU_TILE, 2) * MXU_TILE
    for sb in range(tm // ROW_BLOCK):
        rows = slice(sb * ROW_BLOCK, (sb + 1) * ROW_BLOCK)
        xs = x_ref[rows, :]
        ms = jnp.mean(xs * xs, axis=-1, keepdims=True)
        y = xs * lax.rsqrt(ms + EPS) * g2_ref[...]
        h = (y * (1.0 + mod_ref[sb, 4:5, :]) + mod_ref[sb, 3:4, :]).astype(BF16)
        out = None
        for c0, c1 in ((0, split), (split, d_ff)):
            a = jnp.dot(h, w1_ref[:, c0:c1], preferred_element_type=F32)
            b = jnp.dot(h, w3_ref[:, c0:c1], preferred_element_type=F32)
            z = (a * _sigmoid(a) * b).astype(BF16)
            part = jnp.dot(z, w2_ref[c0:c1, :], preferred_element_type=F32)
            out = part if out is None else out + part
        xn = xs + mod_ref[sb, 5:6, :] * out
        if final:
            ms = jnp.mean(xn * xn, axis=-1, keepdims=True)
            xn = xn * lax.rsqrt(ms + EPS) * gf_ref[...]
        o_ref[rows, :] = xn


def _ffn(x, modtab, g2, w1, w3, w2, gfin, n_rows, final):
    tm = ROW_TILE
    nb = tm // ROW_BLOCK
    d_ff = w1.shape[1]
    row = lambda i: (i, 0)
    out_rows = n_rows if final else x.shape[0]
    return pl.pallas_call(
        functools.partial(_ffn_kernel, final=final),
        out_shape=jax.ShapeDtypeStruct((out_rows, D_MODEL), F32),
        grid=(n_rows // tm,),
        in_specs=[pl.BlockSpec((tm, D_MODEL), row),
                  pl.BlockSpec((nb, 6, D_MODEL), lambda i: (i, 0, 0)),
                  _resident((1, D_MODEL)),
                  _resident((D_MODEL, d_ff)),
                  _resident((D_MODEL, d_ff)),
                  _resident((d_ff, D_MODEL)),
                  _resident((1, D_MODEL))],
        out_specs=pl.BlockSpec((tm, D_MODEL), row),
        input_output_aliases={} if final else {0: 0},
        compiler_params=_params(("parallel",)),
        name="ffn",
    )(x, modtab, g2, w1, w3, w2, gfin)


def _dup_halves(w, base):
    h0 = w[..., base:base + HEAD_DIM]
    h1 = w[..., base + HEAD_DIM:base + 2 * HEAD_DIM]
    return [h0, h0, h1, h1]


def _arrange_in_proj(w):
    o_gate_ml = 4 * WIDTH
    o_wq = o_gate_ml + 2 * N_GATE
    o_wk, o_wv = o_wq + WIDTH, o_wq + WIDTH + 128
    o_gq = o_wv + 128
    o_gk, o_gv = o_gq + WIDTH, o_gq + WIDTH + 128
    o_gate = o_gv + 128
    main = jnp.concatenate(
        [w[..., o_gate:o_gate + 3 * D_MODEL], w[..., 0:WIDTH], w[..., 2 * WIDTH:4 * WIDTH],
         w[..., o_wq:o_wq + WIDTH]]
        + _dup_halves(w, o_wk) + [w[..., o_gq:o_gq + WIDTH]] + _dup_halves(w, o_gk), axis=-1)
    values = jnp.concatenate([w[..., o_gv:o_gv + 128], w[..., o_wv:o_wv + 128]], axis=-1)
    return main, w[..., o_gate_ml:o_gate_ml + 2 * N_GATE], w[..., WIDTH:2 * WIDTH], values


def _rope_tables(n_lat):
    t = jnp.arange(n_lat)
    quarter = HEAD_DIM // 4
    inv = ROPE_THETA ** (-jnp.arange(0, 2 * quarter, 2, dtype=F32) / (2 * quarter))
    ang_r = (t // GRID_W).astype(F32)[:, None] * inv
    ang_c = (t % GRID_W).astype(F32)[:, None] * inv
    cos = jnp.concatenate([jnp.cos(ang_r)] * 2 + [jnp.cos(ang_c)] * 2, axis=1)
    sin = jnp.concatenate([-jnp.sin(ang_r), jnp.sin(ang_r), -jnp.sin(ang_c), jnp.sin(ang_c)], axis=1)
    reps = LANES // HEAD_DIM
    return (jnp.concatenate([jnp.tile(cos, (1, reps)), jnp.ones((ROW_TILE, LANES), F32)], axis=0),
            jnp.concatenate([jnp.tile(sin, (1, reps)), jnp.zeros((ROW_TILE, LANES), F32)], axis=0))


def kernel(x, c, ctx, c_ctx, w_mod, b_mod, norm1_g, w_in, b_in, ml_norm_g, win_sink, qn_g, kn_g,
           w_br, w_o, norm2_g, w_ff1, w_ff3, w_ff2, final_g):
    batch, n_lat, d = x.shape
    n_ctx = ctx.shape[1]
    depth = w_mod.shape[0]
    assert d == D_MODEL and n_lat % ROW_TILE == 0 and n_lat % GRID_W == 0
    assert (batch * n_ctx) % ROW_TILE == 0 and n_ctx % ROW_BLOCK == 0 and (batch * n_lat) % n_ctx == 0
    lat_rows, ctx_rows = batch * n_lat, batch * n_ctx

    xs = (x.reshape(lat_rows, d), ctx.reshape(ctx_rows, d))
    cvec = jnp.concatenate([c, c_ctx[None, :], jnp.zeros((8 - batch - 1, d), F32)], axis=0)
    block_class = np.concatenate([np.repeat(np.arange(batch), n_lat // ROW_BLOCK),
                                  np.full(ctx_rows // ROW_BLOCK, batch)])
    cos, sin = _rope_tables(n_lat)
    lat_tiles = lat_rows // ROW_TILE
    avg = jnp.asarray(np.kron(np.eye(LANES // HEAD_DIM), np.full((HEAD_DIM, HEAD_DIM), 1.0 / HEAD_DIM)), BF16)

    out = None
    for l in range(depth):
        last = l == depth - 1
        mod = _mod_vectors(cvec, w_mod[l], b_mod[l]).reshape(8, 6, d)
        modtab = mod[block_class]

        w_main, w_gate, w_mlk, w_val = _arrange_in_proj(w_in[l])
        b_main, b_gate, b_mlk, b_val = _arrange_in_proj(b_in[l][None, :])
        p, gt, kt, vt = _in_projection(
            xs, modtab, norm1_g[l][None, :], w_main.astype(BF16), b_main,
            w_gate.T.astype(BF16), b_gate.T, w_mlk.T.astype(BF16), b_mlk.T, w_val.T.astype(BF16), b_val.T,
            cos, sin, jnp.tile(qn_g[l], 2)[None, :], jnp.tile(kn_g[l], 2)[None, :], avg, batch, n_lat)

        hf, hb = _mlstm(p, kt, gt, batch, n_lat, n_ctx)
        yb = _window_attention(p, vt, win_sink[l], batch, n_lat, n_ctx, ctx_queries=not last)
        yc = _global_attention(p, vt, batch, n_lat, n_ctx, ctx_queries=not last)

        n_rows = lat_rows if last else lat_rows + ctx_rows
        merged = _merge(xs, modtab, p, hf, hb, yb, yc, ml_norm_g[l].reshape(1, WIDTH),
                        w_br[l].astype(BF16), w_o[l].astype(BF16), n_rows, lat_tiles)
        out = _ffn(merged, modtab, norm2_g[l][None, :], w_ff1[l].astype(BF16), w_ff3[l].astype(BF16),
                   w_ff2[l].astype(BF16), final_g[None, :], n_rows, final=last)
        xs = (out,)
    return out.reshape(batch, n_lat, d)
```

```python
import functools
import math

import jax
import jax.numpy as jnp
import numpy as np
from jax import lax
from jax.experimental import pallas as pl
from jax.experimental.pallas import tpu as pltpu

F32 = jnp.float32
BF16 = jnp.bfloat16

D_MODEL = 1024
GRID_W = 64
CHUNK = 128
HEAD_DIM = 64
ROPE_THETA = 10000.0
EPS = 1e-6
ML_HEADS = 4
ML_DIM = 128
Q_HEADS = 8
KV_HEADS = 2
WIDTH = 512
N_GATE = 2 * ML_HEADS

LANES = 128
MXU_TILE = 256
ROW_BLOCK = 256
ROW_TILE = 512
VMEM_LIMIT = 56 * 1024 * 1024

C_GATE = 0
C_MLQ, C_MLV, C_MLO = 3072, 3584, 4096
C_WQ, C_WK = 4608, 5120
C_GQ, C_GK = 5376, 5888
KV_COLS = KV_HEADS * LANES
N_PROJ = C_GK + KV_COLS
Q_TILE = 256
VT_GLOBAL, VT_WINDOW = 0, 1
LOG2E = 1.4426950408889634
WIN_LEAD = 3
KEY_CHUNK = 512
MOD_TILE = 1536
N_SCORE_BUFS = 4
SPAN_CHUNKS = 4
SAFE_LOG2_BOUND = 50.0

NT_DIMS = (((1,), (1,)), ((), ()))


def _params(sem, vmem=VMEM_LIMIT):
    return pltpu.CompilerParams(dimension_semantics=sem, vmem_limit_bytes=vmem)


def _resident(shape):
    nd = len(shape)
    return pl.BlockSpec(shape, lambda *_: (0,) * nd, pipeline_mode=pl.Buffered(1))


def _sigmoid(x):
    return 1.0 / (1.0 + jnp.exp2(x * -LOG2E))


def _log_sigmoid(x):
    return jnp.minimum(x, 0.0) - jnp.log(1.0 + jnp.exp(-jnp.abs(x)))


def _mod_kernel(c_ref, w_ref, b_ref, o_ref):
    c = c_ref[...]
    s = c * _sigmoid(c)
    o_ref[...] = jnp.dot(s, w_ref[...], preferred_element_type=F32) + b_ref[...]


def _mod_vectors(cvec, w_mod, b_mod):
    n_out = w_mod.shape[1]
    tn = MOD_TILE
    return pl.pallas_call(
        _mod_kernel,
        out_shape=jax.ShapeDtypeStruct((cvec.shape[0], n_out), F32),
        grid=(n_out // tn,),
        in_specs=[pl.BlockSpec(cvec.shape, lambda j: (0, 0)),
                  pl.BlockSpec((D_MODEL, tn), lambda j: (0, j)),
                  pl.BlockSpec((1, tn), lambda j: (0, j))],
        out_specs=pl.BlockSpec((cvec.shape[0], tn), lambda j: (0, j)),
        compiler_params=_params(("parallel",)),
        name="mod_vectors",
    )(cvec, w_mod, b_mod.reshape(1, n_out))


def _rope(acc, cos, sin, first_half):
    w = acc.shape[1]
    reps = w // LANES
    if reps > 1:
        cos = jnp.concatenate([cos] * reps, axis=1)
        sin = jnp.concatenate([sin] * reps, axis=1)
    ahead = pltpu.roll(acc, w - 16, axis=1)
    behind = pltpu.roll(acc, 16, axis=1)
    return acc * cos + jnp.where(first_half, ahead, behind) * sin


def _head_rms(acc, avg, gain):
    sq = (acc * acc).astype(BF16)
    outs = []
    for t in range(acc.shape[1] // LANES):
        sl = slice(t * LANES, (t + 1) * LANES)
        ms = jnp.dot(sq[:, sl], avg, preferred_element_type=F32)
        outs.append(acc[:, sl] * lax.rsqrt(ms + EPS) * gain)
    return jnp.concatenate(outs, axis=1)


def _inproj_kernel(*refs, lat_tiles):
    n_x = 1 if lat_tiles is None else 2
    x_refs = refs[:n_x]
    (mod_ref, g1_ref, w_ref, b_ref, wgt_ref, bgt_ref, wkt_ref, bkt_ref, wvt_ref, bvt_ref,
     cos_ref, sin_ref, qg_ref, kg_ref, avg_ref, p_ref, gt_ref, kt_ref, vt_ref, h_ref) = refs[n_x:]
    tm = h_ref.shape[0]

    def modulated_norm(x_ref):
        for sb in range(tm // ROW_BLOCK):
            r0 = sb * ROW_BLOCK
            xs = x_ref[r0:r0 + ROW_BLOCK, :]
            ms = jnp.mean(xs * xs, axis=-1, keepdims=True)
            y = xs * lax.rsqrt(ms + EPS) * g1_ref[...]
            shift = mod_ref[sb, 0:1, :]
            scale = mod_ref[sb, 1:2, :]
            h_ref[r0:r0 + ROW_BLOCK, :] = (y * (1.0 + scale) + shift).astype(BF16)

    if lat_tiles is None:
        modulated_norm(x_refs[0])
    else:
        pl.when(pl.program_id(0) < lat_tiles)(lambda: modulated_norm(x_refs[0]))
        pl.when(pl.program_id(0) >= lat_tiles)(lambda: modulated_norm(x_refs[1]))

    def first_half(width):
        return (lax.broadcasted_iota(jnp.int32, (1, width), 1) % 32) < 16

    avg = avg_ref[...]
    qg = qg_ref[...]
    kg = kg_ref[...]
    q_scale = HEAD_DIM ** -0.5

    h = h_ref[...]
    cos = cos_ref[...]
    sin = sin_ref[...]

    def proj(c0, width):
        return (jnp.dot(h, w_ref[:, c0:c0 + width], preferred_element_type=F32)
                + b_ref[:, c0:c0 + width])

    def store_plain(cols):
        for c0 in cols:
            p_ref[:, c0:c0 + WIDTH] = proj(c0, WIDTH).astype(BF16)

    plain = list(range(C_GATE, C_GATE + 3 * D_MODEL, WIDTH)) + [C_MLQ, C_MLV, C_MLO]
    raw_gq, raw_gk = proj(C_GQ, WIDTH), proj(C_GK, KV_COLS)
    raw_wq, raw_wk = proj(C_WQ, WIDTH), proj(C_WK, KV_COLS)
    store_plain(plain[:3])
    gq = _rope(_head_rms(raw_gq, avg, qg), cos, sin, first_half(WIDTH)) * (q_scale * LOG2E)
    p_ref[:, C_GQ:C_GQ + WIDTH] = gq.astype(BF16)
    gk = _rope(_head_rms(raw_gk, avg, kg), cos, sin, first_half(KV_COLS))
    p_ref[:, C_GK:C_GK + KV_COLS] = gk.astype(BF16)
    p_ref[:, C_WQ:C_WQ + WIDTH] = (_rope(raw_wq, cos, sin, first_half(WIDTH)) * (q_scale * LOG2E)).astype(BF16)
    p_ref[:, C_WK:C_WK + KV_COLS] = _rope(raw_wk, cos, sin, first_half(KV_COLS)).astype(BF16)
    store_plain(plain[3:])

    wt = jnp.concatenate([wgt_ref[...], wkt_ref[...], wvt_ref[...]], axis=0)
    bt = jnp.concatenate([bgt_ref[...], bkt_ref[...], bvt_ref[...]], axis=0)
    pt = lax.dot_general(wt, h, NT_DIMS, preferred_element_type=F32) + bt
    n_g, n_k = 2 * N_GATE, 2 * N_GATE + WIDTH
    grow = lax.broadcasted_iota(jnp.int32, (n_g, 1), 0)
    gt_ref[...] = jnp.where(grow >= N_GATE, _log_sigmoid(pt[:n_g, :]), pt[:n_g, :])
    kt_ref[...] = pt[n_g:n_k, :].astype(BF16)
    vt_ref[...] = pt[n_k:, :].astype(BF16)


def _token_tile_specs(xs, tm, lat_tiles):
    if len(xs) == 1:
        return [pl.BlockSpec((tm, D_MODEL), lambda i: (i, 0))]
    ctx_tiles = xs[1].shape[0] // tm
    return [pl.BlockSpec((tm, D_MODEL), lambda i: (jnp.minimum(i, lat_tiles - 1), 0)),
            pl.BlockSpec((tm, D_MODEL), lambda i: (jnp.clip(i - lat_tiles, 0, ctx_tiles - 1), 0))]


def _in_projection(xs, modtab, g1, w, b, wgt, bgt, wkt, bkt, wvt, bvt, cos, sin, qg, kg, avg, batch, n_lat):
    rows = sum(x.shape[0] for x in xs)
    tm = ROW_TILE
    nb = tm // ROW_BLOCK
    kv = 2 * KV_HEADS * HEAD_DIM
    seq_tiles = n_lat // tm
    lat_tiles = batch * seq_tiles

    def rope_block(i):
        return (jnp.where(i < lat_tiles, i % seq_tiles, seq_tiles), 0)

    return pl.pallas_call(
        functools.partial(_inproj_kernel, lat_tiles=lat_tiles if len(xs) == 2 else None),
        out_shape=(jax.ShapeDtypeStruct((rows, N_PROJ), BF16),
                   jax.ShapeDtypeStruct((2 * N_GATE, rows), F32),
                   jax.ShapeDtypeStruct((WIDTH, rows), BF16),
                   jax.ShapeDtypeStruct((kv, rows), BF16)),
        grid=(rows // tm,),
        in_specs=_token_tile_specs(xs, tm, lat_tiles) + [
                  pl.BlockSpec((nb, 6, D_MODEL), lambda i: (i, 0, 0)),
                  _resident((1, D_MODEL)),
                  _resident((D_MODEL, N_PROJ)),
                  _resident((1, N_PROJ)),
                  _resident((2 * N_GATE, D_MODEL)),
                  _resident((2 * N_GATE, 1)),
                  _resident((WIDTH, D_MODEL)),
                  _resident((WIDTH, 1)),
                  _resident((kv, D_MODEL)),
                  _resident((kv, 1)),
                  pl.BlockSpec((tm, LANES), rope_block),
                  pl.BlockSpec((tm, LANES), rope_block),
                  _resident((1, LANES)),
                  _resident((1, LANES)),
                  _resident((LANES, LANES))],
        out_specs=(pl.BlockSpec((tm, N_PROJ), lambda i: (i, 0)),
                   pl.BlockSpec((2 * N_GATE, tm), lambda i: (0, i)),
                   pl.BlockSpec((WIDTH, tm), lambda i: (0, i)),
                   pl.BlockSpec((kv, tm), lambda i: (0, i))),
        scratch_shapes=[pltpu.VMEM((tm, D_MODEL), BF16)],
        compiler_params=_params(("parallel",)),
        name="in_projection",
    )(*xs, modtab, g1, w, b, wgt, bgt, wkt, bkt, wvt, bvt, cos, sin, qg, kg, avg)


def _scan_lanes(x, op, fill, reverse):
    n = x.shape[1]
    lane = lax.broadcasted_iota(jnp.int32, x.shape, 1)
    sh = 1
    while sh < n:
        if reverse:
            moved = jnp.where(lane < n - sh, pltpu.roll(x, n - sh, axis=1), fill)
        else:
            moved = jnp.where(lane >= sh, pltpu.roll(x, sh, axis=1), fill)
        x = op(x, moved)
        sh *= 2
    return x


def _mlstm_kernel(qvf_ref, ktf_ref, gtf_ref, qvb_ref, ktb_ref, gtb_ref, hf_ref, hb_ref, cn_ref, m_ref):
    @pl.when(pl.program_id(1) == 0)
    def _():
        cn_ref[...] = jnp.zeros_like(cn_ref)
        m_ref[...] = jnp.zeros_like(m_ref)

    L = CHUNK
    row = lax.broadcasted_iota(jnp.int32, (L, L), 0)
    col = lax.broadcasted_iota(jnp.int32, (L, L), 1)
    scale = ML_DIM ** -0.5
    ones_v = jnp.ones((L, ML_DIM), BF16)
    pending = []
    for d, (qv_ref, kt_ref, gt_ref, h_ref) in enumerate(((qvf_ref, ktf_ref, gtf_ref, hf_ref),
                                                          (qvb_ref, ktb_ref, gtb_ref, hb_ref))):
        seen = (col <= row) if d == 0 else (col >= row)
        g0 = d * ML_HEADS
        ic = gt_ref[g0:g0 + ML_HEADS, :]
        lf = gt_ref[N_GATE + g0:N_GATE + g0 + ML_HEADS, :]
        r = ic - _scan_lanes(lf, jnp.add, 0.0, d == 1)
        r_max = jnp.max(r, axis=1, keepdims=True)
        b_end = jnp.sum(lf, axis=1, keepdims=True)
        for hh in range(ML_HEADS):
            idx = g0 + hh
            lanes = slice(hh * ML_DIM, (hh + 1) * ML_DIM)
            r_row = r[hh:hh + 1, :]
            m_old = m_ref[idx]
            m_end = jnp.maximum(m_old, r_max[hh:hh + 1, :])
            q = qv_ref[:, lanes]
            vo = jnp.concatenate([qv_ref[:, WIDTH + hh * ML_DIM:WIDTH + (hh + 1) * ML_DIM], ones_v], axis=1)
            kt = kt_ref[lanes, :]
            qsb = (q.astype(F32) * scale).astype(BF16)
            cn = cn_ref[idx]
            s_raw = jnp.dot(qsb, kt, preferred_element_type=F32)
            q_cn = jnp.dot(qsb, cn.astype(BF16), preferred_element_type=F32)
            kwt = (kt.astype(F32) * jnp.exp(r_row - m_end)).astype(BF16)
            cn_ref[idx] = jnp.exp(m_old - m_end) * cn + jnp.dot(kwt, vo, preferred_element_type=F32)
            m_ref[idx] = b_end[hh:hh + 1, :] + m_end
            pending.append((h_ref, lanes, seen, r_row, lf[hh:hh + 1, :], m_old, s_raw, q_cn, vo))

    for h_ref, lanes, seen, r_row, lf_row, m_old, s_raw, q_cn, vo in pending:
        b_col = jnp.sum(jnp.where(seen, lf_row, 0.0), axis=1, keepdims=True)
        m_col = jnp.maximum(m_old, jnp.max(jnp.where(seen, r_row, -jnp.inf), axis=1, keepdims=True))
        w = jnp.exp(jnp.where(seen, r_row - m_col, -jnp.inf))
        a = jnp.exp(m_old - m_col)
        s_vo = jnp.dot((s_raw * w).astype(BF16), vo, preferred_element_type=F32)
        num = a * q_cn[:, :ML_DIM] + s_vo[:, :ML_DIM]
        den = a * q_cn[:, ML_DIM:] + s_vo[:, ML_DIM:]
        hc = num / jnp.maximum(jnp.abs(den), jnp.exp(-(b_col + m_col)))
        h_ref[:, lanes] = hc.astype(h_ref.dtype)


def _mlstm(p, kt, gt, batch, n_lat, n_ctx):
    rows = p.shape[0]
    cl, cc = n_lat // CHUNK, n_ctx // CHUNK
    lat_base, ctx_base = 0, batch * cl

    def fwd_chunk(b, i):
        return jnp.where(i < cc, ctx_base + b * cc + i, lat_base + b * cl + (i - cc))

    def bwd_chunk(b, i):
        return jnp.where(i < cc, ctx_base + b * cc + (cc - 1 - i), lat_base + b * cl + (cl - 1 - (i - cc)))

    def specs(chunk):
        return [pl.BlockSpec((CHUNK, 2 * WIDTH), lambda b, i: (chunk(b, i), C_MLQ // (2 * WIDTH))),
                pl.BlockSpec((WIDTH, CHUNK), lambda b, i: (0, chunk(b, i))),
                pl.BlockSpec((2 * N_GATE, CHUNK), lambda b, i: (0, chunk(b, i)))]

    n_state = 2 * ML_HEADS
    return pl.pallas_call(
        _mlstm_kernel,
        out_shape=(jax.ShapeDtypeStruct((rows, WIDTH), BF16), jax.ShapeDtypeStruct((rows, WIDTH), BF16)),
        grid=(batch, cl + cc),
        in_specs=specs(fwd_chunk) + specs(bwd_chunk),
        out_specs=(pl.BlockSpec((CHUNK, WIDTH), lambda b, i: (fwd_chunk(b, i), 0)),
                   pl.BlockSpec((CHUNK, WIDTH), lambda b, i: (bwd_chunk(b, i), 0))),
        scratch_shapes=[pltpu.VMEM((n_state, ML_DIM, 2 * ML_DIM), F32),
                        pltpu.VMEM((n_state, 1, 1), F32)],
        compiler_params=_params(("parallel", "arbitrary")),
        name="mlstm_scan",
    )(p, kt, gt, p, kt, gt)


def _half_mask(e):
    lane = lax.broadcasted_iota(jnp.int32, (1, LANES), 1)
    return (lane < HEAD_DIM) if e == 0 else (lane >= HEAD_DIM)


def _win_kernel(sink_ref, q_ref, kc_ref, vtc_ref, k0_ref, k1_ref, k2_ref, k3_ref,
                vt0_ref, vt1_ref, vt2_ref, vt3_ref, o_ref, *, n_lat_tiles, n_lat_blocks):
    t = pl.program_id(1)
    tq = q_ref.shape[0]
    n_ctx = kc_ref.shape[0]
    is_lat = t < n_lat_tiles
    kk = lax.broadcasted_iota(jnp.int32, (CHUNK, tq), 0)
    qq = lax.broadcasted_iota(jnp.int32, (CHUNK, tq), 1)
    band = (kk >= qq, qq <= kk + CHUNK, kk <= qq, kk + CHUNK <= qq)
    first = 2 * t - 1
    present = [jnp.logical_and(is_lat, jnp.logical_and(first + i >= 0, first + i < n_lat_blocks))
               for i in range(4)]
    visible = [jnp.logical_and(band[i], present[i]) for i in range(4)]
    k_refs = (k0_ref, k1_ref, k2_ref, k3_ref)
    vt_refs = (vt0_ref, vt1_ref, vt2_ref, vt3_ref)
    ones = jnp.ones((16, n_ctx + 4 * CHUNK), BF16)

    def group_operands(g):
        lanes = slice(g * LANES, (g + 1) * LANES)
        k_all = jnp.concatenate([kc_ref[:, lanes]] + [r[:, lanes] for r in k_refs], axis=0)
        vrows = slice(g * HEAD_DIM, (g + 1) * HEAD_DIM)
        vta = jnp.concatenate(
            [jnp.concatenate([vtc_ref[vrows, :]] + [r[vrows, :] for r in vt_refs], axis=1), ones], axis=0)
        return k_all, vta

    operands = [group_operands(g) for g in range(KV_HEADS)]

    def score(h):
        qf = q_ref[:, (h // 2) * LANES:(h // 2 + 1) * LANES].astype(F32)
        qt = jnp.where(_half_mask(h % 2), qf, 0.0).T.astype(BF16)
        return jnp.dot(operands[h // (Q_HEADS // KV_HEADS)][0], qt, preferred_element_type=F32)

    scores = {h: score(h) for h in range(WIN_LEAD)}
    outs = []
    for h in range(Q_HEADS):
        if h + WIN_LEAD < Q_HEADS:
            scores[h + WIN_LEAD] = score(h + WIN_LEAD)
        s = scores.pop(h)
        sink = sink_ref[h] * LOG2E
        parts = [s[:n_ctx, :]]
        for i in range(4):
            blk = s[n_ctx + i * CHUNK:n_ctx + (i + 1) * CHUNK, :]
            parts.append(jnp.where(visible[i], blk, -jnp.inf))
        s = jnp.concatenate(parts, axis=0)
        m = jnp.maximum(jnp.max(s, axis=0, keepdims=True), sink)
        pt = jnp.exp2(s - m).astype(BF16)
        ol = jnp.dot(operands[h // (Q_HEADS // KV_HEADS)][1], pt, preferred_element_type=F32)
        den = ol[HEAD_DIM:HEAD_DIM + 1, :] + jnp.exp2(sink - m)
        outs.append(ol[:HEAD_DIM, :] / den)
    for tt in range(Q_HEADS // 2):
        o_ref[:, tt * LANES:(tt + 1) * LANES] = (
            jnp.concatenate(outs[2 * tt:2 * tt + 2], axis=0).T.astype(o_ref.dtype))


def _window_attention(p, vt, sink, batch, n_lat, n_ctx, ctx_queries):
    tq = Q_TILE
    nlt, nct = n_lat // tq, n_ctx // tq
    nlb = n_lat // CHUNK
    nq = nlt + (nct if ctx_queries else 0)
    out_rows = batch * (n_lat + (n_ctx if ctx_queries else 0))
    ctx_blk = batch * n_lat // n_ctx
    k_col = C_WK // KV_COLS

    def q_block(b, t):
        return jnp.where(t < nlt, b * nlt + t, batch * nlt + b * nct + (t - nlt))

    def near(i):
        return lambda b, t: b * nlb + jnp.clip(2 * t - 1 + i, 0, nlb - 1)

    k_specs = [pl.BlockSpec((CHUNK, KV_COLS), (lambda f: lambda b, t: (f(b, t), k_col))(near(i))) for i in range(4)]
    vt_specs = [pl.BlockSpec((LANES, CHUNK), (lambda f: lambda b, t: (VT_WINDOW, f(b, t)))(near(i)))
                for i in range(4)]
    return pl.pallas_call(
        functools.partial(_win_kernel, n_lat_tiles=nlt, n_lat_blocks=nlb),
        out_shape=jax.ShapeDtypeStruct((out_rows, WIDTH), BF16),
        grid=(batch, nq),
        in_specs=[pl.BlockSpec(memory_space=pltpu.SMEM),
                  pl.BlockSpec((tq, WIDTH), lambda b, t: (q_block(b, t), C_WQ // WIDTH)),
                  pl.BlockSpec((n_ctx, KV_COLS), lambda b, t: (ctx_blk + b, k_col)),
                  pl.BlockSpec((LANES, n_ctx), lambda b, t: (VT_WINDOW, ctx_blk + b))]
                 + k_specs + vt_specs,
        out_specs=pl.BlockSpec((tq, WIDTH), lambda b, t: (q_block(b, t), 0)),
        compiler_params=_params(("parallel", "parallel")),
        name="window_attention",
    )(sink, p, p, vt, p, p, p, p, vt, vt, vt, vt)


def _glb_kernel(q_ref, kc_ref, vtc_ref, kl_ref, vtl_ref, o_ref, qt_ref, knorm_ref, *scratch, n_lat_tiles, tk, span_chunks):
    heads = Q_HEADS // KV_HEADS
    acc_refs = scratch[0:heads]
    sc_refs = scratch[heads:2 * heads]
    ring = [scratch[(2 + i) * heads:(3 + i) * heads] for i in range(N_SCORE_BUFS)]
    base = (2 + N_SCORE_BUFS) * heads
    pc_refs = scratch[base:base + heads]
    p_ring = scratch[base + heads:base + heads + 2]
    qi = pl.program_id(2)
    tq = q_ref.shape[0]
    n_chunks = kl_ref.shape[0] // tk

    def k_chunk(c):
        off = pl.multiple_of(jnp.minimum(c, n_chunks - 1) * tk, tk)
        return kl_ref[pl.ds(off, tk), :]

    def vt_chunk(c):
        return vtl_ref[:, pl.ds(pl.multiple_of(c * tk, tk), tk)]

    def with_ones(vt):
        ones = jnp.ones((acc_refs[0].shape[0] - HEAD_DIM, vt.shape[1]), BF16)
        return jnp.concatenate([vt, ones], axis=0)

    @pl.when(qi == 0)
    def _():
        def sq_norm(k):
            kf = k.astype(F32)
            return jnp.max(jnp.sum(kf * kf, axis=1, keepdims=True), axis=0, keepdims=True)

        def body(c, best):
            return jnp.maximum(best, sq_norm(k_chunk(c)))
        best = lax.fori_loop(0, n_chunks, body, sq_norm(kc_ref[...]))
        knorm_ref[...] = jnp.sqrt(0.5 * best)

    for t in range(heads // 2):
        qf = q_ref[:, t * LANES:(t + 1) * LANES].astype(F32)
        for e in range(2):
            h = 2 * t + e
            qt_ref[:, h * tq:(h + 1) * tq] = jnp.where(_half_mask(e), qf, 0.0).T.astype(BF16)
    for h in range(heads):
        acc_refs[h][...] = jnp.zeros_like(acc_refs[h])

    def q_t(h):
        return qt_ref[:, h * tq:(h + 1) * tq]

    bound = [jnp.sqrt(jnp.sum(jnp.square(q_t(h).astype(F32)), axis=0, keepdims=True)) * knorm_ref[...]
             for h in range(heads)]
    bounded = jnp.max(functools.reduce(jnp.maximum, bound)) <= SAFE_LOG2_BOUND

    span = span_chunks * tk
    n_spans = n_chunks // span_chunks
    n_steps = n_spans * (heads // 2)

    def produce_ctx():
        for h in range(heads):
            s = jnp.dot(kc_ref[...], q_t(h), preferred_element_type=F32)
            pc_refs[h][...] = jnp.exp2(s - bound[h]).astype(BF16)

    def consume_ctx():
        vta = with_ones(vtc_ref[...])
        for h in range(heads):
            acc_refs[h][...] += jnp.dot(vta, pc_refs[h][...], preferred_element_type=F32)

    def produce(p_ref, sp, hp):
        k = kl_ref[pl.ds(pl.multiple_of(sp * span, span), span), :]
        s = jnp.dot(k, qt_ref[:, 2 * hp * tq:(2 * hp + 2) * tq], preferred_element_type=F32)
        p_ref[...] = jnp.exp2(s - jnp.concatenate([bound[2 * hp], bound[2 * hp + 1]], axis=1)).astype(BF16)

    def consume(p_ref, sp, hp):
        vta = with_ones(vtl_ref[:, pl.ds(pl.multiple_of(sp * span, span), span)])
        o2 = jnp.dot(vta, p_ref[...], preferred_element_type=F32)
        acc_refs[2 * hp][...] += o2[:, :tq]
        acc_refs[2 * hp + 1][...] += o2[:, tq:]

    @pl.when(bounded)
    def _():
        produce_ctx()

        @pl.when(qi >= n_lat_tiles)
        def _():
            consume_ctx()

        @pl.when(qi < n_lat_tiles)
        def _():
            produce(p_ring[0], 0, 0)
            consume_ctx()

            unroll = (heads // 2) * math.gcd(n_spans, 2)

            def steps(j, last):
                for i in range(unroll):
                    step = unroll * j + i
                    if not (last and i == unroll - 1):
                        produce(p_ring[(i + 1) % 2], (step + 1) // (heads // 2), (i + 1) % (heads // 2))
                    consume(p_ring[i % 2], step // (heads // 2), i % (heads // 2))

            def body(j, carry):
                steps(j, False)
                return carry
            n_iter = n_steps // unroll
            lax.fori_loop(0, n_iter - 1, body, 0)
            steps(n_iter - 1, True)

    neg_inf = (jnp.full((1, tq), -jnp.inf, F32),) * heads

    def fill(s_refs, k, m_run):
        out = []
        for h in range(heads):
            s = jnp.dot(k, q_t(h), preferred_element_type=F32)
            s_refs[h][...] = s
            out.append(jnp.maximum(m_run[h], jnp.max(s, axis=0, keepdims=True)))
        return tuple(out)

    def drain(s_refs, vt, m_run, m_acc):
        vta = with_ones(vt)
        for h in range(heads):
            alpha = jnp.exp2(m_acc[h] - m_run[h])
            pt = jnp.exp2(s_refs[h][...] - m_run[h]).astype(BF16)
            acc_refs[h][...] = alpha * acc_refs[h][...] + jnp.dot(vta, pt, preferred_element_type=F32)
        return m_run

    @pl.when(jnp.logical_not(bounded))
    def _():
        m_ctx = fill(sc_refs, kc_ref[...], neg_inf)

        @pl.when(qi >= n_lat_tiles)
        def _():
            drain(sc_refs, vtc_ref[...], m_ctx, neg_inf)

        @pl.when(qi < n_lat_tiles)
        def _():
            m_one = fill(ring[0], k_chunk(0), m_ctx)
            m_two = fill(ring[1], k_chunk(1), m_one)
            m_acc = drain(sc_refs, vtc_ref[...], m_one, neg_inf)

            def body(j, carry):
                m_run, m_acc = carry
                for i in range(N_SCORE_BUFS):
                    c = N_SCORE_BUFS * j + i
                    m_next = fill(ring[(i + 2) % N_SCORE_BUFS], k_chunk(c + 2), m_run)
                    m_acc = drain(ring[i], vt_chunk(c), m_run, m_acc)
                    m_run = m_next
                return m_run, m_acc
            lax.fori_loop(0, n_chunks // N_SCORE_BUFS, body, (m_two, m_acc))

    for t in range(heads // 2):
        pair = []
        for e in range(2):
            a = acc_refs[2 * t + e][...]
            pair.append(a[:HEAD_DIM, :] / a[HEAD_DIM:HEAD_DIM + 1, :])
        o_ref[:, t * LANES:(t + 1) * LANES] = jnp.concatenate(pair, axis=0).T.astype(o_ref.dtype)


def _global_attention(p, vt, batch, n_lat, n_ctx, ctx_queries):
    tq = Q_TILE
    tk = min(KEY_CHUNK, n_lat // N_SCORE_BUFS)
    assert n_lat % (tk * N_SCORE_BUFS) == 0 and tk % LANES == 0
    span_chunks = math.gcd(SPAN_CHUNKS, n_lat // tk)
    nl, nc = n_lat // tq, n_ctx // tq
    nq = nl + (nc if ctx_queries else 0)
    out_rows = batch * (n_lat + (n_ctx if ctx_queries else 0))
    ctx_base = batch * nl
    heads = Q_HEADS // KV_HEADS
    acc_rows = HEAD_DIM + 16

    def q_block(b, qi):
        return jnp.where(qi < nl, b * nl + qi, ctx_base + b * nc + (qi - nl))

    ctx_blk = batch * n_lat // n_ctx
    return pl.pallas_call(
        functools.partial(_glb_kernel, n_lat_tiles=nl, tk=tk, span_chunks=span_chunks),
        out_shape=jax.ShapeDtypeStruct((out_rows, WIDTH), BF16),
        grid=(batch, KV_HEADS, nq),
        in_specs=[pl.BlockSpec((tq, 2 * LANES), lambda b, g, qi: (q_block(b, qi), C_GQ // (2 * LANES) + g)),
                  pl.BlockSpec((n_ctx, LANES), lambda b, g, qi: (ctx_blk + b, C_GK // LANES + g)),
                  pl.BlockSpec((HEAD_DIM, n_ctx), lambda b, g, qi: (KV_HEADS * VT_GLOBAL + g, ctx_blk + b)),
                  pl.BlockSpec((n_lat, LANES), lambda b, g, qi: (b, C_GK // LANES + g)),
                  pl.BlockSpec((HEAD_DIM, n_lat), lambda b, g, qi: (KV_HEADS * VT_GLOBAL + g, b))],
        out_specs=pl.BlockSpec((tq, 2 * LANES), lambda b, g, qi: (q_block(b, qi), g)),
        scratch_shapes=([pltpu.VMEM((LANES, heads * tq), BF16), pltpu.VMEM((1, 1), F32)]
                        + [pltpu.VMEM((acc_rows, tq), F32)] * heads
                        + [pltpu.VMEM((n_ctx, tq), F32)] * heads
                        + [pltpu.VMEM((tk, tq), F32)] * (N_SCORE_BUFS * heads)
                        + [pltpu.VMEM((n_ctx, tq), BF16)] * heads
                        + [pltpu.VMEM((span_chunks * tk, 2 * tq), BF16)] * 2),
        compiler_params=_params(("parallel", "parallel", "arbitrary")),
        name="global_attention",
    )(p, p, vt, p, vt)


def _merge_kernel(*refs, lat_tiles):
    n_x = 1 if lat_tiles is None else 2
    x_refs = refs[:n_x]
    (mod_ref, hf_ref, hb_ref, og_ref, yb_ref, yc_ref, ga_ref, gb_ref, gc_ref,
     mlg_ref, wbr_ref, wo_ref, o_ref) = refs[n_x:]
    tm = o_ref.shape[0]
    proj_b = jnp.dot(yb_ref[...], wbr_ref[1], preferred_element_type=F32)
    proj_c = jnp.dot(yc_ref[...], wbr_ref[2], preferred_element_type=F32)
    hs = hf_ref[...].astype(F32) + hb_ref[...].astype(F32)
    parts = []
    for t in range(ML_HEADS):
        ht = hs[:, t * ML_DIM:(t + 1) * ML_DIM]
        ms = jnp.mean(ht * ht, axis=-1, keepdims=True)
        parts.append(ht * lax.rsqrt(ms + EPS))
    ya = (jnp.concatenate(parts, axis=1) * mlg_ref[...] * _sigmoid(og_ref[...].astype(F32))).astype(BF16)
    proj_a = jnp.dot(ya, wbr_ref[0], preferred_element_type=F32)
    merged = (_sigmoid(gb_ref[...].astype(F32)) * proj_b + _sigmoid(gc_ref[...].astype(F32)) * proj_c
              + _sigmoid(ga_ref[...].astype(F32)) * proj_a)
    out = jnp.dot(merged.astype(BF16), wo_ref[...], preferred_element_type=F32)

    def gated_residual(x_ref):
        for sb in range(tm // ROW_BLOCK):
            rows = slice(sb * ROW_BLOCK, (sb + 1) * ROW_BLOCK)
            o_ref[rows, :] = x_ref[rows, :] + mod_ref[sb, 2:3, :] * out[rows, :]

    if lat_tiles is None:
        gated_residual(x_refs[0])
    else:
        pl.when(pl.program_id(0) < lat_tiles)(lambda: gated_residual(x_refs[0]))
        pl.when(pl.program_id(0) >= lat_tiles)(lambda: gated_residual(x_refs[1]))


def _merge(xs, modtab, p, hf, hb, yb, yc, mlg, wbr, wo, n_rows, lat_tiles):
    tm = ROW_TILE
    nb = tm // ROW_BLOCK
    rows = sum(x.shape[0] for x in xs)
    row = lambda i: (i, 0)
    gate = lambda k: pl.BlockSpec((tm, D_MODEL), lambda i: (i, C_GATE // D_MODEL + k))
    return pl.pallas_call(
        functools.partial(_merge_kernel, lat_tiles=lat_tiles if len(xs) == 2 else None),
        out_shape=jax.ShapeDtypeStruct((rows, D_MODEL), F32),
        grid=(n_rows // tm,),
        in_specs=_token_tile_specs(xs, tm, lat_tiles) + [
                  pl.BlockSpec((nb, 6, D_MODEL), lambda i: (i, 0, 0)),
                  pl.BlockSpec((tm, WIDTH), row),
                  pl.BlockSpec((tm, WIDTH), row),
                  pl.BlockSpec((tm, WIDTH), lambda i: (i, C_MLO // WIDTH)),
                  pl.BlockSpec((tm, WIDTH), row),
                  pl.BlockSpec((tm, WIDTH), row),
                  gate(0), gate(1), gate(2),
                  _resident((1, WIDTH)),
                  _resident((3, WIDTH, D_MODEL)),
                  _resident((D_MODEL, D_MODEL))],
        out_specs=pl.BlockSpec((tm, D_MODEL), row),
        input_output_aliases={0: 0} if len(xs) == 1 else {},
        compiler_params=_params(("parallel",)),
        name="merge",
    )(*xs, modtab, hf, hb, p, yb, yc, p, p, p, mlg, wbr, wo)


def _ffn_kernel(x_ref, mod_ref, g2_ref, w1_ref, w3_ref, w2_ref, gf_ref, o_ref, *, final):
    tm = x_ref.shape[0]
    d_ff = w1_ref.shape[1]
    split = pl.cdiv(d_ff // MXU_TILE, 2) * MXU_TILE
    for sb in range(tm // ROW_BLOCK):
        rows = slice(sb * ROW_BLOCK, (sb + 1) * ROW_BLOCK)
        xs = x_ref[rows, :]
        ms = jnp.mean(xs * xs, axis=-1, keepdims=True)
        y = xs * lax.rsqrt(ms + EPS) * g2_ref[...]
        h = (y * (1.0 + mod_ref[sb, 4:5, :]) + mod_ref[sb, 3:4, :]).astype(BF16)
        out = None
        for c0, c1 in ((0, split), (split, d_ff)):
            a = jnp.dot(h, w1_ref[:, c0:c1], preferred_element_type=F32)
            b = jnp.dot(h, w3_ref[:, c0:c1], preferred_element_type=F32)
            z = (a * _sigmoid(a) * b).astype(BF16)
            part = jnp.dot(z, w2_ref[c0:c1, :], preferred_element_type=F32)
            out = part if out is None else out + part
        xn = xs + mod_ref[sb, 5:6, :] * out
        if final:
            ms = jnp.mean(xn * xn, axis=-1, keepdims=True)
            xn = xn * lax.rsqrt(ms + EPS) * gf_ref[...]
        o_ref[rows, :] = xn


def _ffn(x, modtab, g2, w1, w3, w2, gfin, n_rows, final):
    tm = ROW_TILE
    nb = tm // ROW_BLOCK
    d_ff = w1.shape[1]
    row = lambda i: (i, 0)
    out_rows = n_rows if final else x.shape[0]
    return pl.pallas_call(
        functools.partial(_ffn_kernel, final=final),
        out_shape=jax.ShapeDtypeStruct((out_rows, D_MODEL), F32),
        grid=(n_rows // tm,),
        in_specs=[pl.BlockSpec((tm, D_MODEL), row),
                  pl.BlockSpec((nb, 6, D_MODEL), lambda i: (i, 0, 0)),
                  _resident((1, D_MODEL)),
                  _resident((D_MODEL, d_ff)),
                  _resident((D_MODEL, d_ff)),
                  _resident((d_ff, D_MODEL)),
                  _resident((1, D_MODEL))],
        out_specs=pl.BlockSpec((tm, D_MODEL), row),
        input_output_aliases={} if final else {0: 0},
        compiler_params=_params(("parallel",)),
        name="ffn",
    )(x, modtab, g2, w1, w3, w2, gfin)


def _dup_halves(w, base):
    h0 = w[..., base:base + HEAD_DIM]
    h1 = w[..., base + HEAD_DIM:base + 2 * HEAD_DIM]
    return [h0, h0, h1, h1]


def _arrange_in_proj(w):
    o_gate_ml = 4 * WIDTH
    o_wq = o_gate_ml + 2 * N_GATE
    o_wk, o_wv = o_wq + WIDTH, o_wq + WIDTH + 128
    o_gq = o_wv + 128
    o_gk, o_gv = o_gq + WIDTH, o_gq + WIDTH + 128
    o_gate = o_gv + 128
    main = jnp.concatenate(
        [w[..., o_gate:o_gate + 3 * D_MODEL], w[..., 0:WIDTH], w[..., 2 * WIDTH:4 * WIDTH],
         w[..., o_wq:o_wq + WIDTH]]
        + _dup_halves(w, o_wk) + [w[..., o_gq:o_gq + WIDTH]] + _dup_halves(w, o_gk), axis=-1)
    values = jnp.concatenate([w[..., o_gv:o_gv + 128], w[..., o_wv:o_wv + 128]], axis=-1)
    return main, w[..., o_gate_ml:o_gate_ml + 2 * N_GATE], w[..., WIDTH:2 * WIDTH], values


def _rope_tables(n_lat):
    t = jnp.arange(n_lat)
    quarter = HEAD_DIM // 4
    inv = ROPE_THETA ** (-jnp.arange(0, 2 * quarter, 2, dtype=F32) / (2 * quarter))
    ang_r = (t // GRID_W).astype(F32)[:, None] * inv
    ang_c = (t % GRID_W).astype(F32)[:, None] * inv
    cos = jnp.concatenate([jnp.cos(ang_r)] * 2 + [jnp.cos(ang_c)] * 2, axis=1)
    sin = jnp.concatenate([-jnp.sin(ang_r), jnp.sin(ang_r), -jnp.sin(ang_c), jnp.sin(ang_c)], axis=1)
    reps = LANES // HEAD_DIM
    return (jnp.concatenate([jnp.tile(cos, (1, reps)), jnp.ones((ROW_TILE, LANES), F32)], axis=0),
            jnp.concatenate([jnp.tile(sin, (1, reps)), jnp.zeros((ROW_TILE, LANES), F32)], axis=0))


def kernel(x, c, ctx, c_ctx, w_mod, b_mod, norm1_g, w_in, b_in, ml_norm_g, win_sink, qn_g, kn_g,
           w_br, w_o, norm2_g, w_ff1, w_ff3, w_ff2, final_g):
    batch, n_lat, d = x.shape
    n_ctx = ctx.shape[1]
    depth = w_mod.shape[0]
    assert d == D_MODEL and n_lat % ROW_TILE == 0 and n_lat % GRID_W == 0
    assert (batch * n_ctx) % ROW_TILE == 0 and n_ctx % ROW_BLOCK == 0 and (batch * n_lat) % n_ctx == 0
    lat_rows, ctx_rows = batch * n_lat, batch * n_ctx

    xs = (x.reshape(lat_rows, d), ctx.reshape(ctx_rows, d))
    cvec = jnp.concatenate([c, c_ctx[None, :], jnp.zeros((8 - batch - 1, d), F32)], axis=0)
    block_class = np.concatenate([np.repeat(np.arange(batch), n_lat // ROW_BLOCK),
                                  np.full(ctx_rows // ROW_BLOCK, batch)])
    cos, sin = _rope_tables(n_lat)
    lat_tiles = lat_rows // ROW_TILE
    avg = jnp.asarray(np.kron(np.eye(LANES // HEAD_DIM), np.full((HEAD_DIM, HEAD_DIM), 1.0 / HEAD_DIM)), BF16)

    out = None
    for l in range(depth):
        last = l == depth - 1
        mod = _mod_vectors(cvec, w_mod[l], b_mod[l]).reshape(8, 6, d)
        modtab = mod[block_class]

        w_main, w_gate, w_mlk, w_val = _arrange_in_proj(w_in[l])
        b_main, b_gate, b_mlk, b_val = _arrange_in_proj(b_in[l][None, :])
        p, gt, kt, vt = _in_projection(
            xs, modtab, norm1_g[l][None, :], w_main.astype(BF16), b_main,
            w_gate.T.astype(BF16), b_gate.T, w_mlk.T.astype(BF16), b_mlk.T, w_val.T.astype(BF16), b_val.T,
            cos, sin, jnp.tile(qn_g[l], 2)[None, :], jnp.tile(kn_g[l], 2)[None, :], avg, batch, n_lat)

        hf, hb = _mlstm(p, kt, gt, batch, n_lat, n_ctx)
        yb = _window_attention(p, vt, win_sink[l], batch, n_lat, n_ctx, ctx_queries=not last)
        yc = _global_attention(p, vt, batch, n_lat, n_ctx, ctx_queries=not last)

        n_rows = lat_rows if last else lat_rows + ctx_rows
        merged = _merge(xs, modtab, p, hf, hb, yb, yc, ml_norm_g[l].reshape(1, WIDTH),
                        w_br[l].astype(BF16), w_o[l].astype(BF16), n_rows, lat_tiles)
        out = _ffn(merged, modtab, norm2_g[l][None, :], w_ff1[l].astype(BF16), w_ff3[l].astype(BF16),
                   w_ff2[l].astype(BF16), final_g[None, :], n_rows, final=last)
        xs = (out,)
    return out.reshape(batch, n_lat, d)
```

```python
import functools
import math

import jax
import jax.numpy as jnp
import numpy as np
from jax import lax
from jax.experimental import pallas as pl
from jax.experimental.pallas import tpu as pltpu

F32 = jnp.float32
BF16 = jnp.bfloat16

D_MODEL = 1024
GRID_W = 64
CHUNK = 128
HEAD_DIM = 64
ROPE_THETA = 10000.0
EPS = 1e-6
ML_HEADS = 4
ML_DIM = 128
Q_HEADS = 8
KV_HEADS = 2
WIDTH = 512
N_GATE = 2 * ML_HEADS

LANES = 128
MXU_TILE = 256
ROW_BLOCK = 256
ROW_TILE = 512
VMEM_LIMIT = 56 * 1024 * 1024

C_GATE = 0
C_MLQ, C_MLV, C_MLO = 3072, 3584, 4096
C_WQ, C_WK = 4608, 5120
C_GQ, C_GK = 5376, 5888
KV_COLS = KV_HEADS * LANES
N_PROJ = C_GK + KV_COLS
Q_TILE = 256
VT_GLOBAL, VT_WINDOW = 0, 1
LOG2E = 1.4426950408889634
WIN_LEAD = 3
KEY_CHUNK = 512
MOD_TILE = 1536
N_SCORE_BUFS = 4
SPAN_CHUNKS = 2
SAFE_LOG2_BOUND = 50.0

NT_DIMS = (((1,), (1,)), ((), ()))


def _params(sem, vmem=VMEM_LIMIT):
    return pltpu.CompilerParams(dimension_semantics=sem, vmem_limit_bytes=vmem)


def _resident(shape):
    nd = len(shape)
    return pl.BlockSpec(shape, lambda *_: (0,) * nd, pipeline_mode=pl.Buffered(1))


def _sigmoid(x):
    return 1.0 / (1.0 + jnp.exp2(x * -LOG2E))


def _log_sigmoid(x):
    return jnp.minimum(x, 0.0) - jnp.log(1.0 + jnp.exp(-jnp.abs(x)))


def _mod_kernel(c_ref, w_ref, b_ref, o_ref):
    c = c_ref[...]
    s = c * _sigmoid(c)
    o_ref[...] = jnp.dot(s, w_ref[...], preferred_element_type=F32) + b_ref[...]


def _mod_vectors(cvec, w_mod, b_mod):
    n_out = w_mod.shape[1]
    tn = MOD_TILE
    return pl.pallas_call(
        _mod_kernel,
        out_shape=jax.ShapeDtypeStruct((cvec.shape[0], n_out), F32),
        grid=(n_out // tn,),
        in_specs=[pl.BlockSpec(cvec.shape, lambda j: (0, 0)),
                  pl.BlockSpec((D_MODEL, tn), lambda j: (0, j)),
                  pl.BlockSpec((1, tn), lambda j: (0, j))],
        out_specs=pl.BlockSpec((cvec.shape[0], tn), lambda j: (0, j)),
        compiler_params=_params(("parallel",)),
        name="mod_vectors",
    )(cvec, w_mod, b_mod.reshape(1, n_out))


def _rope(acc, cos, sin, first_half):
    w = acc.shape[1]
    reps = w // LANES
    if reps > 1:
        cos = jnp.concatenate([cos] * reps, axis=1)
        sin = jnp.concatenate([sin] * reps, axis=1)
    ahead = pltpu.roll(acc, w - 16, axis=1)
    behind = pltpu.roll(acc, 16, axis=1)
    return acc * cos + jnp.where(first_half, ahead, behind) * sin


def _head_rms(acc, avg, gain):
    sq = (acc * acc).astype(BF16)
    outs = []
    for t in range(acc.shape[1] // LANES):
        sl = slice(t * LANES, (t + 1) * LANES)
        ms = jnp.dot(sq[:, sl], avg, preferred_element_type=F32)
        outs.append(acc[:, sl] * lax.rsqrt(ms + EPS) * gain)
    return jnp.concatenate(outs, axis=1)


def _inproj_kernel(*refs, lat_tiles):
    n_x = 1 if lat_tiles is None else 2
    x_refs = refs[:n_x]
    (mod_ref, g1_ref, w_ref, b_ref, wgt_ref, bgt_ref, wkt_ref, bkt_ref, wvt_ref, bvt_ref,
     cos_ref, sin_ref, qg_ref, kg_ref, avg_ref, p_ref, gt_ref, kt_ref, vt_ref, h_ref) = refs[n_x:]
    tm = h_ref.shape[0]

    def modulated_norm(x_ref):
        for sb in range(tm // ROW_BLOCK):
            r0 = sb * ROW_BLOCK
            xs = x_ref[r0:r0 + ROW_BLOCK, :]
            ms = jnp.mean(xs * xs, axis=-1, keepdims=True)
            y = xs * lax.rsqrt(ms + EPS) * g1_ref[...]
            shift = mod_ref[sb, 0:1, :]
            scale = mod_ref[sb, 1:2, :]
            h_ref[r0:r0 + ROW_BLOCK, :] = (y * (1.0 + scale) + shift).astype(BF16)

    if lat_tiles is None:
        modulated_norm(x_refs[0])
    else:
        pl.when(pl.program_id(0) < lat_tiles)(lambda: modulated_norm(x_refs[0]))
        pl.when(pl.program_id(0) >= lat_tiles)(lambda: modulated_norm(x_refs[1]))

    def first_half(width):
        return (lax.broadcasted_iota(jnp.int32, (1, width), 1) % 32) < 16

    avg = avg_ref[...]
    qg = qg_ref[...]
    kg = kg_ref[...]
    q_scale = HEAD_DIM ** -0.5

    h = h_ref[...]
    cos = cos_ref[...]
    sin = sin_ref[...]

    def proj(c0, width):
        return (jnp.dot(h, w_ref[:, c0:c0 + width], preferred_element_type=F32)
                + b_ref[:, c0:c0 + width])

    def store_plain(cols):
        for c0 in cols:
            p_ref[:, c0:c0 + WIDTH] = proj(c0, WIDTH).astype(BF16)

    plain = list(range(C_GATE, C_GATE + 3 * D_MODEL, WIDTH)) + [C_MLQ, C_MLV, C_MLO]
    raw_gq, raw_gk = proj(C_GQ, WIDTH), proj(C_GK, KV_COLS)
    raw_wq, raw_wk = proj(C_WQ, WIDTH), proj(C_WK, KV_COLS)
    store_plain(plain[:3])
    gq = _rope(_head_rms(raw_gq, avg, qg), cos, sin, first_half(WIDTH)) * (q_scale * LOG2E)
    p_ref[:, C_GQ:C_GQ + WIDTH] = gq.astype(BF16)
    gk = _rope(_head_rms(raw_gk, avg, kg), cos, sin, first_half(KV_COLS))
    p_ref[:, C_GK:C_GK + KV_COLS] = gk.astype(BF16)
    p_ref[:, C_WQ:C_WQ + WIDTH] = (_rope(raw_wq, cos, sin, first_half(WIDTH)) * (q_scale * LOG2E)).astype(BF16)
    p_ref[:, C_WK:C_WK + KV_COLS] = _rope(raw_wk, cos, sin, first_half(KV_COLS)).astype(BF16)
    store_plain(plain[3:])

    wt = jnp.concatenate([wgt_ref[...], wkt_ref[...], wvt_ref[...]], axis=0)
    bt = jnp.concatenate([bgt_ref[...], bkt_ref[...], bvt_ref[...]], axis=0)
    pt = lax.dot_general(wt, h, NT_DIMS, preferred_element_type=F32) + bt
    n_g, n_k = 2 * N_GATE, 2 * N_GATE + WIDTH
    grow = lax.broadcasted_iota(jnp.int32, (n_g, 1), 0)
    gt_ref[...] = jnp.where(grow >= N_GATE, _log_sigmoid(pt[:n_g, :]), pt[:n_g, :])
    kt_ref[...] = pt[n_g:n_k, :].astype(BF16)
    vt_ref[...] = pt[n_k:, :].astype(BF16)


def _token_tile_specs(xs, tm, lat_tiles):
    if len(xs) == 1:
        return [pl.BlockSpec((tm, D_MODEL), lambda i: (i, 0))]
    ctx_tiles = xs[1].shape[0] // tm
    return [pl.BlockSpec((tm, D_MODEL), lambda i: (jnp.minimum(i, lat_tiles - 1), 0)),
            pl.BlockSpec((tm, D_MODEL), lambda i: (jnp.clip(i - lat_tiles, 0, ctx_tiles - 1), 0))]


def _in_projection(xs, modtab, g1, w, b, wgt, bgt, wkt, bkt, wvt, bvt, cos, sin, qg, kg, avg, batch, n_lat):
    rows = sum(x.shape[0] for x in xs)
    tm = ROW_TILE
    nb = tm // ROW_BLOCK
    kv = 2 * KV_HEADS * HEAD_DIM
    seq_tiles = n_lat // tm
    lat_tiles = batch * seq_tiles

    def rope_block(i):
        return (jnp.where(i < lat_tiles, i % seq_tiles, seq_tiles), 0)

    return pl.pallas_call(
        functools.partial(_inproj_kernel, lat_tiles=lat_tiles if len(xs) == 2 else None),
        out_shape=(jax.ShapeDtypeStruct((rows, N_PROJ), BF16),
                   jax.ShapeDtypeStruct((2 * N_GATE, rows), F32),
                   jax.ShapeDtypeStruct((WIDTH, rows), BF16),
                   jax.ShapeDtypeStruct((kv, rows), BF16)),
        grid=(rows // tm,),
        in_specs=_token_tile_specs(xs, tm, lat_tiles) + [
                  pl.BlockSpec((nb, 6, D_MODEL), lambda i: (i, 0, 0)),
                  _resident((1, D_MODEL)),
                  _resident((D_MODEL, N_PROJ)),
                  _resident((1, N_PROJ)),
                  _resident((2 * N_GATE, D_MODEL)),
                  _resident((2 * N_GATE, 1)),
                  _resident((WIDTH, D_MODEL)),
                  _resident((WIDTH, 1)),
                  _resident((kv, D_MODEL)),
                  _resident((kv, 1)),
                  pl.BlockSpec((tm, LANES), rope_block),
                  pl.BlockSpec((tm, LANES), rope_block),
                  _resident((1, LANES)),
                  _resident((1, LANES)),
                  _resident((LANES, LANES))],
        out_specs=(pl.BlockSpec((tm, N_PROJ), lambda i: (i, 0)),
                   pl.BlockSpec((2 * N_GATE, tm), lambda i: (0, i)),
                   pl.BlockSpec((WIDTH, tm), lambda i: (0, i)),
                   pl.BlockSpec((kv, tm), lambda i: (0, i))),
        scratch_shapes=[pltpu.VMEM((tm, D_MODEL), BF16)],
        compiler_params=_params(("parallel",)),
        name="in_projection",
    )(*xs, modtab, g1, w, b, wgt, bgt, wkt, bkt, wvt, bvt, cos, sin, qg, kg, avg)


def _scan_lanes(x, op, fill, reverse):
    n = x.shape[1]
    lane = lax.broadcasted_iota(jnp.int32, x.shape, 1)
    sh = 1
    while sh < n:
        if reverse:
            moved = jnp.where(lane < n - sh, pltpu.roll(x, n - sh, axis=1), fill)
        else:
            moved = jnp.where(lane >= sh, pltpu.roll(x, sh, axis=1), fill)
        x = op(x, moved)
        sh *= 2
    return x


def _mlstm_kernel(qvf_ref, ktf_ref, gtf_ref, qvb_ref, ktb_ref, gtb_ref, hf_ref, hb_ref, cn_ref, m_ref):
    @pl.when(pl.program_id(1) == 0)
    def _():
        cn_ref[...] = jnp.zeros_like(cn_ref)
        m_ref[...] = jnp.zeros_like(m_ref)

    L = CHUNK
    row = lax.broadcasted_iota(jnp.int32, (L, L), 0)
    col = lax.broadcasted_iota(jnp.int32, (L, L), 1)
    scale = ML_DIM ** -0.5
    ones_v = jnp.ones((L, ML_DIM), BF16)
    pending = []
    for d, (qv_ref, kt_ref, gt_ref, h_ref) in enumerate(((qvf_ref, ktf_ref, gtf_ref, hf_ref),
                                                          (qvb_ref, ktb_ref, gtb_ref, hb_ref))):
        seen = (col <= row) if d == 0 else (col >= row)
        g0 = d * ML_HEADS
        ic = gt_ref[g0:g0 + ML_HEADS, :]
        lf = gt_ref[N_GATE + g0:N_GATE + g0 + ML_HEADS, :]
        r = ic - _scan_lanes(lf, jnp.add, 0.0, d == 1)
        r_max = jnp.max(r, axis=1, keepdims=True)
        b_end = jnp.sum(lf, axis=1, keepdims=True)
        for hh in range(ML_HEADS):
            idx = g0 + hh
            lanes = slice(hh * ML_DIM, (hh + 1) * ML_DIM)
            r_row = r[hh:hh + 1, :]
            m_old = m_ref[idx]
            m_end = jnp.maximum(m_old, r_max[hh:hh + 1, :])
            q = qv_ref[:, lanes]
            vo = jnp.concatenate([qv_ref[:, WIDTH + hh * ML_DIM:WIDTH + (hh + 1) * ML_DIM], ones_v], axis=1)
            kt = kt_ref[lanes, :]
            qsb = (q.astype(F32) * scale).astype(BF16)
            cn = cn_ref[idx]
            s_raw = jnp.dot(qsb, kt, preferred_element_type=F32)
            q_cn = jnp.dot(qsb, cn.astype(BF16), preferred_element_type=F32)
            kwt = (kt.astype(F32) * jnp.exp(r_row - m_end)).astype(BF16)
            cn_ref[idx] = jnp.exp(m_old - m_end) * cn + jnp.dot(kwt, vo, preferred_element_type=F32)
            m_ref[idx] = b_end[hh:hh + 1, :] + m_end
            pending.append((h_ref, lanes, seen, r_row, lf[hh:hh + 1, :], m_old, s_raw, q_cn, vo))

    for h_ref, lanes, seen, r_row, lf_row, m_old, s_raw, q_cn, vo in pending:
        b_col = jnp.sum(jnp.where(seen, lf_row, 0.0), axis=1, keepdims=True)
        m_col = jnp.maximum(m_old, jnp.max(jnp.where(seen, r_row, -jnp.inf), axis=1, keepdims=True))
        w = jnp.exp(jnp.where(seen, r_row - m_col, -jnp.inf))
        a = jnp.exp(m_old - m_col)
        s_vo = jnp.dot((s_raw * w).astype(BF16), vo, preferred_element_type=F32)
        num = a * q_cn[:, :ML_DIM] + s_vo[:, :ML_DIM]
        den = a * q_cn[:, ML_DIM:] + s_vo[:, ML_DIM:]
        hc = num / jnp.maximum(jnp.abs(den), jnp.exp(-(b_col + m_col)))
        h_ref[:, lanes] = hc.astype(h_ref.dtype)


def _mlstm(p, kt, gt, batch, n_lat, n_ctx):
    rows = p.shape[0]
    cl, cc = n_lat // CHUNK, n_ctx // CHUNK
    lat_base, ctx_base = 0, batch * cl

    def fwd_chunk(b, i):
        return jnp.where(i < cc, ctx_base + b * cc + i, lat_base + b * cl + (i - cc))

    def bwd_chunk(b, i):
        return jnp.where(i < cc, ctx_base + b * cc + (cc - 1 - i), lat_base + b * cl + (cl - 1 - (i - cc)))

    def specs(chunk):
        return [pl.BlockSpec((CHUNK, 2 * WIDTH), lambda b, i: (chunk(b, i), C_MLQ // (2 * WIDTH))),
                pl.BlockSpec((WIDTH, CHUNK), lambda b, i: (0, chunk(b, i))),
                pl.BlockSpec((2 * N_GATE, CHUNK), lambda b, i: (0, chunk(b, i)))]

    n_state = 2 * ML_HEADS
    return pl.pallas_call(
        _mlstm_kernel,
        out_shape=(jax.ShapeDtypeStruct((rows, WIDTH), BF16), jax.ShapeDtypeStruct((rows, WIDTH), BF16)),
        grid=(batch, cl + cc),
        in_specs=specs(fwd_chunk) + specs(bwd_chunk),
        out_specs=(pl.BlockSpec((CHUNK, WIDTH), lambda b, i: (fwd_chunk(b, i), 0)),
                   pl.BlockSpec((CHUNK, WIDTH), lambda b, i: (bwd_chunk(b, i), 0))),
        scratch_shapes=[pltpu.VMEM((n_state, ML_DIM, 2 * ML_DIM), F32),
                        pltpu.VMEM((n_state, 1, 1), F32)],
        compiler_params=_params(("parallel", "arbitrary")),
        name="mlstm_scan",
    )(p, kt, gt, p, kt, gt)


def _half_mask(e):
    lane = lax.broadcasted_iota(jnp.int32, (1, LANES), 1)
    return (lane < HEAD_DIM) if e == 0 else (lane >= HEAD_DIM)


def _win_kernel(sink_ref, q_ref, kc_ref, vtc_ref, k0_ref, k1_ref, k2_ref, k3_ref,
                vt0_ref, vt1_ref, vt2_ref, vt3_ref, o_ref, *, n_lat_tiles, n_lat_blocks):
    t = pl.program_id(1)
    tq = q_ref.shape[0]
    n_ctx = kc_ref.shape[0]
    is_lat = t < n_lat_tiles
    kk = lax.broadcasted_iota(jnp.int32, (CHUNK, tq), 0)
    qq = lax.broadcasted_iota(jnp.int32, (CHUNK, tq), 1)
    band = (kk >= qq, qq <= kk + CHUNK, kk <= qq, kk + CHUNK <= qq)
    first = 2 * t - 1
    present = [jnp.logical_and(is_lat, jnp.logical_and(first + i >= 0, first + i < n_lat_blocks))
               for i in range(4)]
    visible = [jnp.logical_and(band[i], present[i]) for i in range(4)]
    k_refs = (k0_ref, k1_ref, k2_ref, k3_ref)
    vt_refs = (vt0_ref, vt1_ref, vt2_ref, vt3_ref)
    ones = jnp.ones((16, n_ctx + 4 * CHUNK), BF16)

    def group_operands(g):
        lanes = slice(g * LANES, (g + 1) * LANES)
        k_all = jnp.concatenate([kc_ref[:, lanes]] + [r[:, lanes] for r in k_refs], axis=0)
        vrows = slice(g * HEAD_DIM, (g + 1) * HEAD_DIM)
        vta = jnp.concatenate(
            [jnp.concatenate([vtc_ref[vrows, :]] + [r[vrows, :] for r in vt_refs], axis=1), ones], axis=0)
        return k_all, vta

    operands = [group_operands(g) for g in range(KV_HEADS)]

    def score(h):
        qf = q_ref[:, (h // 2) * LANES:(h // 2 + 1) * LANES].astype(F32)
        qt = jnp.where(_half_mask(h % 2), qf, 0.0).T.astype(BF16)
        return jnp.dot(operands[h // (Q_HEADS // KV_HEADS)][0], qt, preferred_element_type=F32)

    scores = {h: score(h) for h in range(WIN_LEAD)}
    outs = []
    for h in range(Q_HEADS):
        if h + WIN_LEAD < Q_HEADS:
            scores[h + WIN_LEAD] = score(h + WIN_LEAD)
        s = scores.pop(h)
        sink = sink_ref[h] * LOG2E
        parts = [s[:n_ctx, :]]
        for i in range(4):
            blk = s[n_ctx + i * CHUNK:n_ctx + (i + 1) * CHUNK, :]
            parts.append(jnp.where(visible[i], blk, -jnp.inf))
        s = jnp.concatenate(parts, axis=0)
        m = jnp.maximum(jnp.max(s, axis=0, keepdims=True), sink)
        pt = jnp.exp2(s - m).astype(BF16)
        ol = jnp.dot(operands[h // (Q_HEADS // KV_HEADS)][1], pt, preferred_element_type=F32)
        den = ol[HEAD_DIM:HEAD_DIM + 1, :] + jnp.exp2(sink - m)
        outs.append(ol[:HEAD_DIM, :] / den)
    for tt in range(Q_HEADS // 2):
        o_ref[:, tt * LANES:(tt + 1) * LANES] = (
            jnp.concatenate(outs[2 * tt:2 * tt + 2], axis=0).T.astype(o_ref.dtype))


def _window_attention(p, vt, sink, batch, n_lat, n_ctx, ctx_queries):
    tq = Q_TILE
    nlt, nct = n_lat // tq, n_ctx // tq
    nlb = n_lat // CHUNK
    nq = nlt + (nct if ctx_queries else 0)
    out_rows = batch * (n_lat + (n_ctx if ctx_queries else 0))
    ctx_blk = batch * n_lat // n_ctx
    k_col = C_WK // KV_COLS

    def q_block(b, t):
        return jnp.where(t < nlt, b * nlt + t, batch * nlt + b * nct + (t - nlt))

    def near(i):
        return lambda b, t: b * nlb + jnp.clip(2 * t - 1 + i, 0, nlb - 1)

    k_specs = [pl.BlockSpec((CHUNK, KV_COLS), (lambda f: lambda b, t: (f(b, t), k_col))(near(i))) for i in range(4)]
    vt_specs = [pl.BlockSpec((LANES, CHUNK), (lambda f: lambda b, t: (VT_WINDOW, f(b, t)))(near(i)))
                for i in range(4)]
    return pl.pallas_call(
        functools.partial(_win_kernel, n_lat_tiles=nlt, n_lat_blocks=nlb),
        out_shape=jax.ShapeDtypeStruct((out_rows, WIDTH), BF16),
        grid=(batch, nq),
        in_specs=[pl.BlockSpec(memory_space=pltpu.SMEM),
                  pl.BlockSpec((tq, WIDTH), lambda b, t: (q_block(b, t), C_WQ // WIDTH)),
                  pl.BlockSpec((n_ctx, KV_COLS), lambda b, t: (ctx_blk + b, k_col)),
                  pl.BlockSpec((LANES, n_ctx), lambda b, t: (VT_WINDOW, ctx_blk + b))]
                 + k_specs + vt_specs,
        out_specs=pl.BlockSpec((tq, WIDTH), lambda b, t: (q_block(b, t), 0)),
        compiler_params=_params(("parallel", "parallel")),
        name="window_attention",
    )(sink, p, p, vt, p, p, p, p, vt, vt, vt, vt)


def _glb_kernel(q_ref, kc_ref, vtc_ref, kl_ref, vtl_ref, o_ref, qt_ref, knorm_ref, *scratch, n_lat_tiles, tk, span_chunks):
    heads = Q_HEADS // KV_HEADS
    acc_refs = scratch[0:heads]
    sc_refs = scratch[heads:2 * heads]
    ring = [scratch[(2 + i) * heads:(3 + i) * heads] for i in range(N_SCORE_BUFS)]
    base = (2 + N_SCORE_BUFS) * heads
    pc_refs = scratch[base:base + heads]
    p_ring = scratch[base + heads:base + heads + 2]
    qi = pl.program_id(2)
    tq = q_ref.shape[0]
    n_chunks = kl_ref.shape[0] // tk

    def k_chunk(c):
        off = pl.multiple_of(jnp.minimum(c, n_chunks - 1) * tk, tk)
        return kl_ref[pl.ds(off, tk), :]

    def vt_chunk(c):
        return vtl_ref[:, pl.ds(pl.multiple_of(c * tk, tk), tk)]

    def with_ones(vt):
        ones = jnp.ones((acc_refs[0].shape[0] - HEAD_DIM, vt.shape[1]), BF16)
        return jnp.concatenate([vt, ones], axis=0)

    @pl.when(qi == 0)
    def _():
        def sq_norm(k):
            kf = k.astype(F32)
            return jnp.max(jnp.sum(kf * kf, axis=1, keepdims=True), axis=0, keepdims=True)

        def body(c, best):
            return jnp.maximum(best, sq_norm(k_chunk(c)))
        best = lax.fori_loop(0, n_chunks, body, sq_norm(kc_ref[...]))
        knorm_ref[...] = jnp.sqrt(0.5 * best)

    for t in range(heads // 2):
        qf = q_ref[:, t * LANES:(t + 1) * LANES].astype(F32)
        for e in range(2):
            h = 2 * t + e
            qt_ref[:, h * tq:(h + 1) * tq] = jnp.where(_half_mask(e), qf, 0.0).T.astype(BF16)
    for h in range(heads):
        acc_refs[h][...] = jnp.zeros_like(acc_refs[h])

    def q_t(h):
        return qt_ref[:, h * tq:(h + 1) * tq]

    bound = [jnp.sqrt(jnp.sum(jnp.square(q_t(h).astype(F32)), axis=0, keepdims=True)) * knorm_ref[...]
             for h in range(heads)]
    bounded = jnp.max(functools.reduce(jnp.maximum, bound)) <= SAFE_LOG2_BOUND

    span = span_chunks * tk
    n_spans = n_chunks // span_chunks
    n_steps = n_spans * heads

    def produce_ctx():
        for h in range(heads):
            s = jnp.dot(kc_ref[...], q_t(h), preferred_element_type=F32)
            pc_refs[h][...] = jnp.exp2(s - bound[h]).astype(BF16)

    def consume_ctx():
        vta = with_ones(vtc_ref[...])
        for h in range(heads):
            acc_refs[h][...] += jnp.dot(vta, pc_refs[h][...], preferred_element_type=F32)

    def produce(p_ref, sp, h):
        k = kl_ref[pl.ds(pl.multiple_of(sp * span, span), span), :]
        s = jnp.dot(k, q_t(h), preferred_element_type=F32)
        p_ref[...] = jnp.exp2(s - bound[h]).astype(BF16)

    def consume(p_ref, sp, h):
        vta = with_ones(vtl_ref[:, pl.ds(pl.multiple_of(sp * span, span), span)])
        acc_refs[h][...] += jnp.dot(vta, p_ref[...], preferred_element_type=F32)

    @pl.when(bounded)
    def _():
        produce_ctx()

        @pl.when(qi >= n_lat_tiles)
        def _():
            consume_ctx()

        @pl.when(qi < n_lat_tiles)
        def _():
            produce(p_ring[0], 0, 0)
            consume_ctx()

            unroll = heads * (2 if n_spans % 2 == 0 else 1)

            def steps(j, last):
                for i in range(unroll):
                    step = unroll * j + i
                    if not (last and i == unroll - 1):
                        produce(p_ring[(i + 1) % 2], (step + 1) // heads, (i + 1) % heads)
                    consume(p_ring[i % 2], step // heads, i % heads)

            def body(j, carry):
                steps(j, False)
                return carry
            n_iter = n_steps // unroll
            lax.fori_loop(0, n_iter - 1, body, 0)
            steps(n_iter - 1, True)

    neg_inf = (jnp.full((1, tq), -jnp.inf, F32),) * heads

    def fill(s_refs, k, m_run):
        out = []
        for h in range(heads):
            s = jnp.dot(k, q_t(h), preferred_element_type=F32)
            s_refs[h][...] = s
            out.append(jnp.maximum(m_run[h], jnp.max(s, axis=0, keepdims=True)))
        return tuple(out)

    def drain(s_refs, vt, m_run, m_acc):
        vta = with_ones(vt)
        for h in range(heads):
            alpha = jnp.exp2(m_acc[h] - m_run[h])
            pt = jnp.exp2(s_refs[h][...] - m_run[h]).astype(BF16)
            acc_refs[h][...] = alpha * acc_refs[h][...] + jnp.dot(vta, pt, preferred_element_type=F32)
        return m_run

    @pl.when(jnp.logical_not(bounded))
    def _():
        m_ctx = fill(sc_refs, kc_ref[...], neg_inf)

        @pl.when(qi >= n_lat_tiles)
        def _():
            drain(sc_refs, vtc_ref[...], m_ctx, neg_inf)

        @pl.when(qi < n_lat_tiles)
        def _():
            m_one = fill(ring[0], k_chunk(0), m_ctx)
            m_two = fill(ring[1], k_chunk(1), m_one)
            m_acc = drain(sc_refs, vtc_ref[...], m_one, neg_inf)

            def body(j, carry):
                m_run, m_acc = carry
                for i in range(N_SCORE_BUFS):
                    c = N_SCORE_BUFS * j + i
                    m_next = fill(ring[(i + 2) % N_SCORE_BUFS], k_chunk(c + 2), m_run)
                    m_acc = drain(ring[i], vt_chunk(c), m_run, m_acc)
                    m_run = m_next
                return m_run, m_acc
            lax.fori_loop(0, n_chunks // N_SCORE_BUFS, body, (m_two, m_acc))

    for t in range(heads // 2):
        pair = []
        for e in range(2):
            a = acc_refs[2 * t + e][...]
            pair.append(a[:HEAD_DIM, :] / a[HEAD_DIM:HEAD_DIM + 1, :])
        o_ref[:, t * LANES:(t + 1) * LANES] = jnp.concatenate(pair, axis=0).T.astype(o_ref.dtype)


def _global_attention(p, vt, batch, n_lat, n_ctx, ctx_queries):
    tq = Q_TILE
    tk = min(KEY_CHUNK, n_lat // N_SCORE_BUFS)
    assert n_lat % (tk * N_SCORE_BUFS) == 0 and tk % LANES == 0
    span_chunks = math.gcd(SPAN_CHUNKS, n_lat // tk)
    nl, nc = n_lat // tq, n_ctx // tq
    nq = nl + (nc if ctx_queries else 0)
    out_rows = batch * (n_lat + (n_ctx if ctx_queries else 0))
    ctx_base = batch * nl
    heads = Q_HEADS // KV_HEADS
    acc_rows = HEAD_DIM + 16

    def q_block(b, qi):
        return jnp.where(qi < nl, b * nl + qi, ctx_base + b * nc + (qi - nl))

    ctx_blk = batch * n_lat // n_ctx
    return pl.pallas_call(
        functools.partial(_glb_kernel, n_lat_tiles=nl, tk=tk, span_chunks=span_chunks),
        out_shape=jax.ShapeDtypeStruct((out_rows, WIDTH), BF16),
        grid=(batch, KV_HEADS, nq),
        in_specs=[pl.BlockSpec((tq, 2 * LANES), lambda b, g, qi: (q_block(b, qi), C_GQ // (2 * LANES) + g)),
                  pl.BlockSpec((n_ctx, LANES), lambda b, g, qi: (ctx_blk + b, C_GK // LANES + g)),
                  pl.BlockSpec((HEAD_DIM, n_ctx), lambda b, g, qi: (KV_HEADS * VT_GLOBAL + g, ctx_blk + b)),
                  pl.BlockSpec((n_lat, LANES), lambda b, g, qi: (b, C_GK // LANES + g)),
                  pl.BlockSpec((HEAD_DIM, n_lat), lambda b, g, qi: (KV_HEADS * VT_GLOBAL + g, b))],
        out_specs=pl.BlockSpec((tq, 2 * LANES), lambda b, g, qi: (q_block(b, qi), g)),
        scratch_shapes=([pltpu.VMEM((LANES, heads * tq), BF16), pltpu.VMEM((1, 1), F32)]
                        + [pltpu.VMEM((acc_rows, tq), F32)] * heads
                        + [pltpu.VMEM((n_ctx, tq), F32)] * heads
                        + [pltpu.VMEM((tk, tq), F32)] * (N_SCORE_BUFS * heads)
                        + [pltpu.VMEM((n_ctx, tq), BF16)] * heads
                        + [pltpu.VMEM((span_chunks * tk, tq), BF16)] * 2),
        compiler_params=_params(("parallel", "parallel", "arbitrary")),
        name="global_attention",
    )(p, p, vt, p, vt)


def _merge_kernel(*refs, lat_tiles):
    n_x = 1 if lat_tiles is None else 2
    x_refs = refs[:n_x]
    (mod_ref, hf_ref, hb_ref, og_ref, yb_ref, yc_ref, ga_ref, gb_ref, gc_ref,
     mlg_ref, wbr_ref, wo_ref, o_ref) = refs[n_x:]
    tm = o_ref.shape[0]
    proj_b = jnp.dot(yb_ref[...], wbr_ref[1], preferred_element_type=F32)
    proj_c = jnp.dot(yc_ref[...], wbr_ref[2], preferred_element_type=F32)
    hs = hf_ref[...].astype(F32) + hb_ref[...].astype(F32)
    parts = []
    for t in range(ML_HEADS):
        ht = hs[:, t * ML_DIM:(t + 1) * ML_DIM]
        ms = jnp.mean(ht * ht, axis=-1, keepdims=True)
        parts.append(ht * lax.rsqrt(ms + EPS))
    ya = (jnp.concatenate(parts, axis=1) * mlg_ref[...] * _sigmoid(og_ref[...].astype(F32))).astype(BF16)
    proj_a = jnp.dot(ya, wbr_ref[0], preferred_element_type=F32)
    merged = (_sigmoid(gb_ref[...].astype(F32)) * proj_b + _sigmoid(gc_ref[...].astype(F32)) * proj_c
              + _sigmoid(ga_ref[...].astype(F32)) * proj_a)
    out = jnp.dot(merged.astype(BF16), wo_ref[...], preferred_element_type=F32)

    def gated_residual(x_ref):
        for sb in range(tm // ROW_BLOCK):
            rows = slice(sb * ROW_BLOCK, (sb + 1) * ROW_BLOCK)
            o_ref[rows, :] = x_ref[rows, :] + mod_ref[sb, 2:3, :] * out[rows, :]

    if lat_tiles is None:
        gated_residual(x_refs[0])
    else:
        pl.when(pl.program_id(0) < lat_tiles)(lambda: gated_residual(x_refs[0]))
        pl.when(pl.program_id(0) >= lat_tiles)(lambda: gated_residual(x_refs[1]))


def _merge(xs, modtab, p, hf, hb, yb, yc, mlg, wbr, wo, n_rows, lat_tiles):
    tm = ROW_TILE
    nb = tm // ROW_BLOCK
    rows = sum(x.shape[0] for x in xs)
    row = lambda i: (i, 0)
    gate = lambda k: pl.BlockSpec((tm, D_MODEL), lambda i: (i, C_GATE // D_MODEL + k))
    return pl.pallas_call(
        functools.partial(_merge_kernel, lat_tiles=lat_tiles if len(xs) == 2 else None),
        out_shape=jax.ShapeDtypeStruct((rows, D_MODEL), F32),
        grid=(n_rows // tm,),
        in_specs=_token_tile_specs(xs, tm, lat_tiles) + [
                  pl.BlockSpec((nb, 6, D_MODEL), lambda i: (i, 0, 0)),
                  pl.BlockSpec((tm, WIDTH), row),
                  pl.BlockSpec((tm, WIDTH), row),
                  pl.BlockSpec((tm, WIDTH), lambda i: (i, C_MLO // WIDTH)),
                  pl.BlockSpec((tm, WIDTH), row),
                  pl.BlockSpec((tm, WIDTH), row),
                  gate(0), gate(1), gate(2),
                  _resident((1, WIDTH)),
                  _resident((3, WIDTH, D_MODEL)),
                  _resident((D_MODEL, D_MODEL))],
        out_specs=pl.BlockSpec((tm, D_MODEL), row),
        input_output_aliases={0: 0} if len(xs) == 1 else {},
        compiler_params=_params(("parallel",)),
        name="merge",
    )(*xs, modtab, hf, hb, p, yb, yc, p, p, p, mlg, wbr, wo)


def _ffn_kernel(x_ref, mod_ref, g2_ref, w1_ref, w3_ref, w2_ref, gf_ref, o_ref, *, final):
    tm = x_ref.shape[0]
    d_ff = w1_ref.shape[1]
    split = pl.cdiv(d_ff // MXU_TILE, 2) * MXU_TILE
    for sb in range(tm // ROW_BLOCK):
        rows = slice(sb * ROW_BLOCK, (sb + 1) * ROW_BLOCK)
        xs = x_ref[rows, :]
        ms = jnp.mean(xs * xs, axis=-1, keepdims=True)
        y = xs * lax.rsqrt(ms + EPS) * g2_ref[...]
        h = (y * (1.0 + mod_ref[sb, 4:5, :]) + mod_ref[sb, 3:4, :]).astype(BF16)
        out = None
        for c0, c1 in ((0, split), (split, d_ff)):
            a = jnp.dot(h, w1_ref[:, c0:c1], preferred_element_type=F32)
            b = jnp.dot(h, w3_ref[:, c0:c1], preferred_element_type=F32)
            z = (a * _sigmoid(a) * b).astype(BF16)
            part = jnp.dot(z, w2_ref[c0:c1, :], preferred_element_type=F32)
            out = part if out is None else out + part
        xn = xs + mod_ref[sb, 5:6, :] * out
        if final:
            ms = jnp.mean(xn * xn, axis=-1, keepdims=True)
            xn = xn * lax.rsqrt(ms + EPS) * gf_ref[...]
        o_ref[rows, :] = xn


def _ffn(x, modtab, g2, w1, w3, w2, gfin, n_rows, final):
    tm = ROW_TILE
    nb = tm // ROW_BLOCK
    d_ff = w1.shape[1]
    row = lambda i: (i, 0)
    out_rows = n_rows if final else x.shape[0]
    return pl.pallas_call(
        functools.partial(_ffn_kernel, final=final),
        out_shape=jax.ShapeDtypeStruct((out_rows, D_MODEL), F32),
        grid=(n_rows // tm,),
        in_specs=[pl.BlockSpec((tm, D_MODEL), row),
                  pl.BlockSpec((nb, 6, D_MODEL), lambda i: (i, 0, 0)),
                  _resident((1, D_MODEL)),
                  _resident((D_MODEL, d_ff)),
                  _resident((D_MODEL, d_ff)),
                  _resident((d_ff, D_MODEL)),
                  _resident((1, D_MODEL))],
        out_specs=pl.BlockSpec((tm, D_MODEL), row),
        input_output_aliases={} if final else {0: 0},
        compiler_params=_params(("parallel",)),
        name="ffn",
    )(x, modtab, g2, w1, w3, w2, gfin)


def _dup_halves(w, base):
    h0 = w[..., base:base + HEAD_DIM]
    h1 = w[..., base + HEAD_DIM:base + 2 * HEAD_DIM]
    return [h0, h0, h1, h1]


def _arrange_in_proj(w):
    o_gate_ml = 4 * WIDTH
    o_wq = o_gate_ml + 2 * N_GATE
    o_wk, o_wv = o_wq + WIDTH, o_wq + WIDTH + 128
    o_gq = o_wv + 128
    o_gk, o_gv = o_gq + WIDTH, o_gq + WIDTH + 128
    o_gate = o_gv + 128
    main = jnp.concatenate(
        [w[..., o_gate:o_gate + 3 * D_MODEL], w[..., 0:WIDTH], w[..., 2 * WIDTH:4 * WIDTH],
         w[..., o_wq:o_wq + WIDTH]]
        + _dup_halves(w, o_wk) + [w[..., o_gq:o_gq + WIDTH]] + _dup_halves(w, o_gk), axis=-1)
    values = jnp.concatenate([w[..., o_gv:o_gv + 128], w[..., o_wv:o_wv + 128]], axis=-1)
    return main, w[..., o_gate_ml:o_gate_ml + 2 * N_GATE], w[..., WIDTH:2 * WIDTH], values


def _rope_tables(n_lat):
    t = jnp.arange(n_lat)
    quarter = HEAD_DIM // 4
    inv = ROPE_THETA ** (-jnp.arange(0, 2 * quarter, 2, dtype=F32) / (2 * quarter))
    ang_r = (t // GRID_W).astype(F32)[:, None] * inv
    ang_c = (t % GRID_W).astype(F32)[:, None] * inv
    cos = jnp.concatenate([jnp.cos(ang_r)] * 2 + [jnp.cos(ang_c)] * 2, axis=1)
    sin = jnp.concatenate([-jnp.sin(ang_r), jnp.sin(ang_r), -jnp.sin(ang_c), jnp.sin(ang_c)], axis=1)
    reps = LANES // HEAD_DIM
    return (jnp.concatenate([jnp.tile(cos, (1, reps)), jnp.ones((ROW_TILE, LANES), F32)], axis=0),
            jnp.concatenate([jnp.tile(sin, (1, reps)), jnp.zeros((ROW_TILE, LANES), F32)], axis=0))


def kernel(x, c, ctx, c_ctx, w_mod, b_mod, norm1_g, w_in, b_in, ml_norm_g, win_sink, qn_g, kn_g,
           w_br, w_o, norm2_g, w_ff1, w_ff3, w_ff2, final_g):
    batch, n_lat, d = x.shape
    n_ctx = ctx.shape[1]
    depth = w_mod.shape[0]
    assert d == D_MODEL and n_lat % ROW_TILE == 0 and n_lat % GRID_W == 0
    assert (batch * n_ctx) % ROW_TILE == 0 and n_ctx % ROW_BLOCK == 0 and (batch * n_lat) % n_ctx == 0
    lat_rows, ctx_rows = batch * n_lat, batch * n_ctx

    xs = (x.reshape(lat_rows, d), ctx.reshape(ctx_rows, d))
    cvec = jnp.concatenate([c, c_ctx[None, :], jnp.zeros((8 - batch - 1, d), F32)], axis=0)
    block_class = np.concatenate([np.repeat(np.arange(batch), n_lat // ROW_BLOCK),
                                  np.full(ctx_rows // ROW_BLOCK, batch)])
    cos, sin = _rope_tables(n_lat)
    lat_tiles = lat_rows // ROW_TILE
    avg = jnp.asarray(np.kron(np.eye(LANES // HEAD_DIM), np.full((HEAD_DIM, HEAD_DIM), 1.0 / HEAD_DIM)), BF16)

    out = None
    for l in range(depth):
        last = l == depth - 1
        mod = _mod_vectors(cvec, w_mod[l], b_mod[l]).reshape(8, 6, d)
        modtab = mod[block_class]

        w_main, w_gate, w_mlk, w_val = _arrange_in_proj(w_in[l])
        b_main, b_gate, b_mlk, b_val = _arrange_in_proj(b_in[l][None, :])
        p, gt, kt, vt = _in_projection(
            xs, modtab, norm1_g[l][None, :], w_main.astype(BF16), b_main,
            w_gate.T.astype(BF16), b_gate.T, w_mlk.T.astype(BF16), b_mlk.T, w_val.T.astype(BF16), b_val.T,
            cos, sin, jnp.tile(qn_g[l], 2)[None, :], jnp.tile(kn_g[l], 2)[None, :], avg, batch, n_lat)

        hf, hb = _mlstm(p, kt, gt, batch, n_lat, n_ctx)
        yb = _window_attention(p, vt, win_sink[l], batch, n_lat, n_ctx, ctx_queries=not last)
        yc = _global_attention(p, vt, batch, n_lat, n_ctx, ctx_queries=not last)

        n_rows = lat_rows if last else lat_rows + ctx_rows
        merged = _merge(xs, modtab, p, hf, hb, yb, yc, ml_norm_g[l].reshape(1, WIDTH),
                        w_br[l].astype(BF16), w_o[l].astype(BF16), n_rows, lat_tiles)
        out = _ffn(merged, modtab, norm2_g[l][None, :], w_ff1[l].astype(BF16), w_ff3[l].astype(BF16),
                   w_ff2[l].astype(BF16), final_g[None, :], n_rows, final=last)
        xs = (out,)
    return out.reshape(batch, n_lat, d)
```

```python
import functools
import math

import jax
import jax.numpy as jnp
import numpy as np
from jax import lax
from jax.experimental import pallas as pl
from jax.experimental.pallas import tpu as pltpu

F32 = jnp.float32
BF16 = jnp.bfloat16

D_MODEL = 1024
GRID_W = 64
CHUNK = 128
HEAD_DIM = 64
ROPE_THETA = 10000.0
EPS = 1e-6
ML_HEADS = 4
ML_DIM = 128
Q_HEADS = 8
KV_HEADS = 2
WIDTH = 512
N_GATE = 2 * ML_HEADS

LANES = 128
MXU_TILE = 256
ROW_BLOCK = 256
ROW_TILE = 512
VMEM_LIMIT = 56 * 1024 * 1024

C_GATE = 0
C_MLQ, C_MLV, C_MLO = 3072, 3584, 4096
C_WQ, C_WK = 4608, 5120
C_GQ, C_GK = 5376, 5888
KV_COLS = KV_HEADS * LANES
N_PROJ = C_GK + KV_COLS
Q_TILE = 256
VT_GLOBAL, VT_WINDOW = 0, 1
LOG2E = 1.4426950408889634
WIN_LEAD = 3
KEY_CHUNK = 512
MOD_TILE = 1536
N_SCORE_BUFS = 4
SPAN_CHUNKS = 8
SAFE_LOG2_BOUND = 50.0

NT_DIMS = (((1,), (1,)), ((), ()))


def _params(sem, vmem=VMEM_LIMIT):
    return pltpu.CompilerParams(dimension_semantics=sem, vmem_limit_bytes=vmem)


def _resident(shape):
    nd = len(shape)
    return pl.BlockSpec(shape, lambda *_: (0,) * nd, pipeline_mode=pl.Buffered(1))


def _sigmoid(x):
    return 1.0 / (1.0 + jnp.exp2(x * -LOG2E))


def _log_sigmoid(x):
    return jnp.minimum(x, 0.0) - jnp.log(1.0 + jnp.exp(-jnp.abs(x)))


def _mod_kernel(c_ref, w_ref, b_ref, o_ref):
    c = c_ref[...]
    s = c * _sigmoid(c)
    o_ref[...] = jnp.dot(s, w_ref[...], preferred_element_type=F32) + b_ref[...]


def _mod_vectors(cvec, w_mod, b_mod):
    n_out = w_mod.shape[1]
    tn = MOD_TILE
    return pl.pallas_call(
        _mod_kernel,
        out_shape=jax.ShapeDtypeStruct((cvec.shape[0], n_out), F32),
        grid=(n_out // tn,),
        in_specs=[pl.BlockSpec(cvec.shape, lambda j: (0, 0)),
                  pl.BlockSpec((D_MODEL, tn), lambda j: (0, j)),
                  pl.BlockSpec((1, tn), lambda j: (0, j))],
        out_specs=pl.BlockSpec((cvec.shape[0], tn), lambda j: (0, j)),
        compiler_params=_params(("parallel",)),
        name="mod_vectors",
    )(cvec, w_mod, b_mod.reshape(1, n_out))


def _rope(acc, cos, sin, first_half):
    w = acc.shape[1]
    reps = w // LANES
    if reps > 1:
        cos = jnp.concatenate([cos] * reps, axis=1)
        sin = jnp.concatenate([sin] * reps, axis=1)
    ahead = pltpu.roll(acc, w - 16, axis=1)
    behind = pltpu.roll(acc, 16, axis=1)
    return acc * cos + jnp.where(first_half, ahead, behind) * sin


def _head_rms(acc, avg, gain):
    sq = (acc * acc).astype(BF16)
    outs = []
    for t in range(acc.shape[1] // LANES):
        sl = slice(t * LANES, (t + 1) * LANES)
        ms = jnp.dot(sq[:, sl], avg, preferred_element_type=F32)
        outs.append(acc[:, sl] * lax.rsqrt(ms + EPS) * gain)
    return jnp.concatenate(outs, axis=1)


def _inproj_kernel(*refs, lat_tiles):
    n_x = 1 if lat_tiles is None else 2
    x_refs = refs[:n_x]
    (mod_ref, g1_ref, w_ref, b_ref, wgt_ref, bgt_ref, wkt_ref, bkt_ref, wvt_ref, bvt_ref,
     cos_ref, sin_ref, qg_ref, kg_ref, avg_ref, p_ref, gt_ref, kt_ref, vt_ref, h_ref) = refs[n_x:]
    tm = h_ref.shape[0]

    def modulated_norm(x_ref):
        for sb in range(tm // ROW_BLOCK):
            r0 = sb * ROW_BLOCK
            xs = x_ref[r0:r0 + ROW_BLOCK, :]
            ms = jnp.mean(xs * xs, axis=-1, keepdims=True)
            y = xs * lax.rsqrt(ms + EPS) * g1_ref[...]
            shift = mod_ref[sb, 0:1, :]
            scale = mod_ref[sb, 1:2, :]
            h_ref[r0:r0 + ROW_BLOCK, :] = (y * (1.0 + scale) + shift).astype(BF16)

    if lat_tiles is None:
        modulated_norm(x_refs[0])
    else:
        pl.when(pl.program_id(0) < lat_tiles)(lambda: modulated_norm(x_refs[0]))
        pl.when(pl.program_id(0) >= lat_tiles)(lambda: modulated_norm(x_refs[1]))

    def first_half(width):
        return (lax.broadcasted_iota(jnp.int32, (1, width), 1) % 32) < 16

    avg = avg_ref[...]
    qg = qg_ref[...]
    kg = kg_ref[...]
    q_scale = HEAD_DIM ** -0.5

    h = h_ref[...]
    cos = cos_ref[...]
    sin = sin_ref[...]

    def proj(c0, width):
        return (jnp.dot(h, w_ref[:, c0:c0 + width], preferred_element_type=F32)
                + b_ref[:, c0:c0 + width])

    def store_plain(cols):
        for c0 in cols:
            p_ref[:, c0:c0 + WIDTH] = proj(c0, WIDTH).astype(BF16)

    plain = list(range(C_GATE, C_GATE + 3 * D_MODEL, WIDTH)) + [C_MLQ, C_MLV, C_MLO]
    raw_gq, raw_gk = proj(C_GQ, WIDTH), proj(C_GK, KV_COLS)
    raw_wq, raw_wk = proj(C_WQ, WIDTH), proj(C_WK, KV_COLS)
    store_plain(plain[:3])
    gq = _rope(_head_rms(raw_gq, avg, qg), cos, sin, first_half(WIDTH)) * (q_scale * LOG2E)
    p_ref[:, C_GQ:C_GQ + WIDTH] = gq.astype(BF16)
    gk = _rope(_head_rms(raw_gk, avg, kg), cos, sin, first_half(KV_COLS))
    p_ref[:, C_GK:C_GK + KV_COLS] = gk.astype(BF16)
    p_ref[:, C_WQ:C_WQ + WIDTH] = (_rope(raw_wq, cos, sin, first_half(WIDTH)) * (q_scale * LOG2E)).astype(BF16)
    p_ref[:, C_WK:C_WK + KV_COLS] = _rope(raw_wk, cos, sin, first_half(KV_COLS)).astype(BF16)
    store_plain(plain[3:])

    wt = jnp.concatenate([wgt_ref[...], wkt_ref[...], wvt_ref[...]], axis=0)
    bt = jnp.concatenate([bgt_ref[...], bkt_ref[...], bvt_ref[...]], axis=0)
    pt = lax.dot_general(wt, h, NT_DIMS, preferred_element_type=F32) + bt
    n_g, n_k = 2 * N_GATE, 2 * N_GATE + WIDTH
    grow = lax.broadcasted_iota(jnp.int32, (n_g, 1), 0)
    gt_ref[...] = jnp.where(grow >= N_GATE, _log_sigmoid(pt[:n_g, :]), pt[:n_g, :])
    kt_ref[...] = pt[n_g:n_k, :].astype(BF16)
    vt_ref[...] = pt[n_k:, :].astype(BF16)


def _token_tile_specs(xs, tm, lat_tiles):
    if len(xs) == 1:
        return [pl.BlockSpec((tm, D_MODEL), lambda i: (i, 0))]
    ctx_tiles = xs[1].shape[0] // tm
    return [pl.BlockSpec((tm, D_MODEL), lambda i: (jnp.minimum(i, lat_tiles - 1), 0)),
            pl.BlockSpec((tm, D_MODEL), lambda i: (jnp.clip(i - lat_tiles, 0, ctx_tiles - 1), 0))]


def _in_projection(xs, modtab, g1, w, b, wgt, bgt, wkt, bkt, wvt, bvt, cos, sin, qg, kg, avg, batch, n_lat):
    rows = sum(x.shape[0] for x in xs)
    tm = ROW_TILE
    nb = tm // ROW_BLOCK
    kv = 2 * KV_HEADS * HEAD_DIM
    seq_tiles = n_lat // tm
    lat_tiles = batch * seq_tiles

    def rope_block(i):
        return (jnp.where(i < lat_tiles, i % seq_tiles, seq_tiles), 0)

    return pl.pallas_call(
        functools.partial(_inproj_kernel, lat_tiles=lat_tiles if len(xs) == 2 else None),
        out_shape=(jax.ShapeDtypeStruct((rows, N_PROJ), BF16),
                   jax.ShapeDtypeStruct((2 * N_GATE, rows), F32),
                   jax.ShapeDtypeStruct((WIDTH, rows), BF16),
                   jax.ShapeDtypeStruct((kv, rows), BF16)),
        grid=(rows // tm,),
        in_specs=_token_tile_specs(xs, tm, lat_tiles) + [
                  pl.BlockSpec((nb, 6, D_MODEL), lambda i: (i, 0, 0)),
                  _resident((1, D_MODEL)),
                  _resident((D_MODEL, N_PROJ)),
                  _resident((1, N_PROJ)),
                  _resident((2 * N_GATE, D_MODEL)),
                  _resident((2 * N_GATE, 1)),
                  _resident((WIDTH, D_MODEL)),
                  _resident((WIDTH, 1)),
                  _resident((kv, D_MODEL)),
                  _resident((kv, 1)),
                  pl.BlockSpec((tm, LANES), rope_block),
                  pl.BlockSpec((tm, LANES), rope_block),
                  _resident((1, LANES)),
                  _resident((1, LANES)),
                  _resident((LANES, LANES))],
        out_specs=(pl.BlockSpec((tm, N_PROJ), lambda i: (i, 0)),
                   pl.BlockSpec((2 * N_GATE, tm), lambda i: (0, i)),
                   pl.BlockSpec((WIDTH, tm), lambda i: (0, i)),
                   pl.BlockSpec((kv, tm), lambda i: (0, i))),
        scratch_shapes=[pltpu.VMEM((tm, D_MODEL), BF16)],
        compiler_params=_params(("parallel",)),
        name="in_projection",
    )(*xs, modtab, g1, w, b, wgt, bgt, wkt, bkt, wvt, bvt, cos, sin, qg, kg, avg)


def _scan_lanes(x, op, fill, reverse):
    n = x.shape[1]
    lane = lax.broadcasted_iota(jnp.int32, x.shape, 1)
    sh = 1
    while sh < n:
        if reverse:
            moved = jnp.where(lane < n - sh, pltpu.roll(x, n - sh, axis=1), fill)
        else:
            moved = jnp.where(lane >= sh, pltpu.roll(x, sh, axis=1), fill)
        x = op(x, moved)
        sh *= 2
    return x


def _mlstm_kernel(qvf_ref, ktf_ref, gtf_ref, qvb_ref, ktb_ref, gtb_ref, hf_ref, hb_ref, cn_ref, m_ref):
    @pl.when(pl.program_id(1) == 0)
    def _():
        cn_ref[...] = jnp.zeros_like(cn_ref)
        m_ref[...] = jnp.zeros_like(m_ref)

    L = CHUNK
    row = lax.broadcasted_iota(jnp.int32, (L, L), 0)
    col = lax.broadcasted_iota(jnp.int32, (L, L), 1)
    scale = ML_DIM ** -0.5
    ones_v = jnp.ones((L, ML_DIM), BF16)
    pending = []
    for d, (qv_ref, kt_ref, gt_ref, h_ref) in enumerate(((qvf_ref, ktf_ref, gtf_ref, hf_ref),
                                                          (qvb_ref, ktb_ref, gtb_ref, hb_ref))):
        seen = (col <= row) if d == 0 else (col >= row)
        g0 = d * ML_HEADS
        ic = gt_ref[g0:g0 + ML_HEADS, :]
        lf = gt_ref[N_GATE + g0:N_GATE + g0 + ML_HEADS, :]
        r = ic - _scan_lanes(lf, jnp.add, 0.0, d == 1)
        r_max = jnp.max(r, axis=1, keepdims=True)
        b_end = jnp.sum(lf, axis=1, keepdims=True)
        for hh in range(ML_HEADS):
            idx = g0 + hh
            lanes = slice(hh * ML_DIM, (hh + 1) * ML_DIM)
            r_row = r[hh:hh + 1, :]
            m_old = m_ref[idx]
            m_end = jnp.maximum(m_old, r_max[hh:hh + 1, :])
            q = qv_ref[:, lanes]
            vo = jnp.concatenate([qv_ref[:, WIDTH + hh * ML_DIM:WIDTH + (hh + 1) * ML_DIM], ones_v], axis=1)
            kt = kt_ref[lanes, :]
            qsb = (q.astype(F32) * scale).astype(BF16)
            cn = cn_ref[idx]
            s_raw = jnp.dot(qsb, kt, preferred_element_type=F32)
            q_cn = jnp.dot(qsb, cn.astype(BF16), preferred_element_type=F32)
            kwt = (kt.astype(F32) * jnp.exp(r_row - m_end)).astype(BF16)
            cn_ref[idx] = jnp.exp(m_old - m_end) * cn + jnp.dot(kwt, vo, preferred_element_type=F32)
            m_ref[idx] = b_end[hh:hh + 1, :] + m_end
            pending.append((h_ref, lanes, seen, r_row, lf[hh:hh + 1, :], m_old, s_raw, q_cn, vo))

    for h_ref, lanes, seen, r_row, lf_row, m_old, s_raw, q_cn, vo in pending:
        b_col = jnp.sum(jnp.where(seen, lf_row, 0.0), axis=1, keepdims=True)
        m_col = jnp.maximum(m_old, jnp.max(jnp.where(seen, r_row, -jnp.inf), axis=1, keepdims=True))
        w = jnp.exp(jnp.where(seen, r_row - m_col, -jnp.inf))
        a = jnp.exp(m_old - m_col)
        s_vo = jnp.dot((s_raw * w).astype(BF16), vo, preferred_element_type=F32)
        num = a * q_cn[:, :ML_DIM] + s_vo[:, :ML_DIM]
        den = a * q_cn[:, ML_DIM:] + s_vo[:, ML_DIM:]
        hc = num / jnp.maximum(jnp.abs(den), jnp.exp(-(b_col + m_col)))
        h_ref[:, lanes] = hc.astype(h_ref.dtype)


def _mlstm(p, kt, gt, batch, n_lat, n_ctx):
    rows = p.shape[0]
    cl, cc = n_lat // CHUNK, n_ctx // CHUNK
    lat_base, ctx_base = 0, batch * cl

    def fwd_chunk(b, i):
        return jnp.where(i < cc, ctx_base + b * cc + i, lat_base + b * cl + (i - cc))

    def bwd_chunk(b, i):
        return jnp.where(i < cc, ctx_base + b * cc + (cc - 1 - i), lat_base + b * cl + (cl - 1 - (i - cc)))

    def specs(chunk):
        return [pl.BlockSpec((CHUNK, 2 * WIDTH), lambda b, i: (chunk(b, i), C_MLQ // (2 * WIDTH))),
                pl.BlockSpec((WIDTH, CHUNK), lambda b, i: (0, chunk(b, i))),
                pl.BlockSpec((2 * N_GATE, CHUNK), lambda b, i: (0, chunk(b, i)))]

    n_state = 2 * ML_HEADS
    return pl.pallas_call(
        _mlstm_kernel,
        out_shape=(jax.ShapeDtypeStruct((rows, WIDTH), BF16), jax.ShapeDtypeStruct((rows, WIDTH), BF16)),
        grid=(batch, cl + cc),
        in_specs=specs(fwd_chunk) + specs(bwd_chunk),
        out_specs=(pl.BlockSpec((CHUNK, WIDTH), lambda b, i: (fwd_chunk(b, i), 0)),
                   pl.BlockSpec((CHUNK, WIDTH), lambda b, i: (bwd_chunk(b, i), 0))),
        scratch_shapes=[pltpu.VMEM((n_state, ML_DIM, 2 * ML_DIM), F32),
                        pltpu.VMEM((n_state, 1, 1), F32)],
        compiler_params=_params(("parallel", "arbitrary")),
        name="mlstm_scan",
    )(p, kt, gt, p, kt, gt)


def _half_mask(e):
    lane = lax.broadcasted_iota(jnp.int32, (1, LANES), 1)
    return (lane < HEAD_DIM) if e == 0 else (lane >= HEAD_DIM)


def _win_kernel(sink_ref, q_ref, kc_ref, vtc_ref, k0_ref, k1_ref, k2_ref, k3_ref,
                vt0_ref, vt1_ref, vt2_ref, vt3_ref, o_ref, *, n_lat_tiles, n_lat_blocks):
    t = pl.program_id(1)
    tq = q_ref.shape[0]
    n_ctx = kc_ref.shape[0]
    is_lat = t < n_lat_tiles
    kk = lax.broadcasted_iota(jnp.int32, (CHUNK, tq), 0)
    qq = lax.broadcasted_iota(jnp.int32, (CHUNK, tq), 1)
    band = (kk >= qq, qq <= kk + CHUNK, kk <= qq, kk + CHUNK <= qq)
    first = 2 * t - 1
    present = [jnp.logical_and(is_lat, jnp.logical_and(first + i >= 0, first + i < n_lat_blocks))
               for i in range(4)]
    visible = [jnp.logical_and(band[i], present[i]) for i in range(4)]
    k_refs = (k0_ref, k1_ref, k2_ref, k3_ref)
    vt_refs = (vt0_ref, vt1_ref, vt2_ref, vt3_ref)
    ones = jnp.ones((16, n_ctx + 4 * CHUNK), BF16)

    def group_operands(g):
        lanes = slice(g * LANES, (g + 1) * LANES)
        k_all = jnp.concatenate([kc_ref[:, lanes]] + [r[:, lanes] for r in k_refs], axis=0)
        vrows = slice(g * HEAD_DIM, (g + 1) * HEAD_DIM)
        vta = jnp.concatenate(
            [jnp.concatenate([vtc_ref[vrows, :]] + [r[vrows, :] for r in vt_refs], axis=1), ones], axis=0)
        return k_all, vta

    operands = [group_operands(g) for g in range(KV_HEADS)]

    def score(h):
        qf = q_ref[:, (h // 2) * LANES:(h // 2 + 1) * LANES].astype(F32)
        qt = jnp.where(_half_mask(h % 2), qf, 0.0).T.astype(BF16)
        return jnp.dot(operands[h // (Q_HEADS // KV_HEADS)][0], qt, preferred_element_type=F32)

    scores = {h: score(h) for h in range(WIN_LEAD)}
    outs = []
    for h in range(Q_HEADS):
        if h + WIN_LEAD < Q_HEADS:
            scores[h + WIN_LEAD] = score(h + WIN_LEAD)
        s = scores.pop(h)
        sink = sink_ref[h] * LOG2E
        parts = [s[:n_ctx, :]]
        for i in range(4):
            blk = s[n_ctx + i * CHUNK:n_ctx + (i + 1) * CHUNK, :]
            parts.append(jnp.where(visible[i], blk, -jnp.inf))
        s = jnp.concatenate(parts, axis=0)
        m = jnp.maximum(jnp.max(s, axis=0, keepdims=True), sink)
        pt = jnp.exp2(s - m).astype(BF16)
        ol = jnp.dot(operands[h // (Q_HEADS // KV_HEADS)][1], pt, preferred_element_type=F32)
        den = ol[HEAD_DIM:HEAD_DIM + 1, :] + jnp.exp2(sink - m)
        outs.append(ol[:HEAD_DIM, :] / den)
    for tt in range(Q_HEADS // 2):
        o_ref[:, tt * LANES:(tt + 1) * LANES] = (
            jnp.concatenate(outs[2 * tt:2 * tt + 2], axis=0).T.astype(o_ref.dtype))


def _window_attention(p, vt, sink, batch, n_lat, n_ctx, ctx_queries):
    tq = Q_TILE
    nlt, nct = n_lat // tq, n_ctx // tq
    nlb = n_lat // CHUNK
    nq = nlt + (nct if ctx_queries else 0)
    out_rows = batch * (n_lat + (n_ctx if ctx_queries else 0))
    ctx_blk = batch * n_lat // n_ctx
    k_col = C_WK // KV_COLS

    def q_block(b, t):
        return jnp.where(t < nlt, b * nlt + t, batch * nlt + b * nct + (t - nlt))

    def near(i):
        return lambda b, t: b * nlb + jnp.clip(2 * t - 1 + i, 0, nlb - 1)

    k_specs = [pl.BlockSpec((CHUNK, KV_COLS), (lambda f: lambda b, t: (f(b, t), k_col))(near(i))) for i in range(4)]
    vt_specs = [pl.BlockSpec((LANES, CHUNK), (lambda f: lambda b, t: (VT_WINDOW, f(b, t)))(near(i)))
                for i in range(4)]
    return pl.pallas_call(
        functools.partial(_win_kernel, n_lat_tiles=nlt, n_lat_blocks=nlb),
        out_shape=jax.ShapeDtypeStruct((out_rows, WIDTH), BF16),
        grid=(batch, nq),
        in_specs=[pl.BlockSpec(memory_space=pltpu.SMEM),
                  pl.BlockSpec((tq, WIDTH), lambda b, t: (q_block(b, t), C_WQ // WIDTH)),
                  pl.BlockSpec((n_ctx, KV_COLS), lambda b, t: (ctx_blk + b, k_col)),
                  pl.BlockSpec((LANES, n_ctx), lambda b, t: (VT_WINDOW, ctx_blk + b))]
                 + k_specs + vt_specs,
        out_specs=pl.BlockSpec((tq, WIDTH), lambda b, t: (q_block(b, t), 0)),
        compiler_params=_params(("parallel", "parallel")),
        name="window_attention",
    )(sink, p, p, vt, p, p, p, p, vt, vt, vt, vt)


def _glb_kernel(q_ref, kc_ref, vtc_ref, kl_ref, vtl_ref, o_ref, qt_ref, knorm_ref, *scratch, n_lat_tiles, tk, span_chunks):
    heads = Q_HEADS // KV_HEADS
    acc_refs = scratch[0:heads]
    sc_refs = scratch[heads:2 * heads]
    ring = [scratch[(2 + i) * heads:(3 + i) * heads] for i in range(N_SCORE_BUFS)]
    base = (2 + N_SCORE_BUFS) * heads
    pc_refs = scratch[base:base + heads]
    p_ring = scratch[base + heads:base + heads + 2]
    qi = pl.program_id(2)
    tq = q_ref.shape[0]
    n_chunks = kl_ref.shape[0] // tk

    def k_chunk(c):
        off = pl.multiple_of(jnp.minimum(c, n_chunks - 1) * tk, tk)
        return kl_ref[pl.ds(off, tk), :]

    def vt_chunk(c):
        return vtl_ref[:, pl.ds(pl.multiple_of(c * tk, tk), tk)]

    def with_ones(vt):
        ones = jnp.ones((acc_refs[0].shape[0] - HEAD_DIM, vt.shape[1]), BF16)
        return jnp.concatenate([vt, ones], axis=0)

    @pl.when(qi == 0)
    def _():
        def sq_norm(k):
            kf = k.astype(F32)
            return jnp.max(jnp.sum(kf * kf, axis=1, keepdims=True), axis=0, keepdims=True)

        def body(c, best):
            return jnp.maximum(best, sq_norm(k_chunk(c)))
        best = lax.fori_loop(0, n_chunks, body, sq_norm(kc_ref[...]))
        knorm_ref[...] = jnp.sqrt(0.5 * best)

    for t in range(heads // 2):
        qf = q_ref[:, t * LANES:(t + 1) * LANES].astype(F32)
        for e in range(2):
            h = 2 * t + e
            qt_ref[:, h * tq:(h + 1) * tq] = jnp.where(_half_mask(e), qf, 0.0).T.astype(BF16)
    for h in range(heads):
        acc_refs[h][...] = jnp.zeros_like(acc_refs[h])

    def q_t(h):
        return qt_ref[:, h * tq:(h + 1) * tq]

    bound = [jnp.sqrt(jnp.sum(jnp.square(q_t(h).astype(F32)), axis=0, keepdims=True)) * knorm_ref[...]
             for h in range(heads)]
    bounded = jnp.max(functools.reduce(jnp.maximum, bound)) <= SAFE_LOG2_BOUND

    span = span_chunks * tk
    n_spans = n_chunks // span_chunks
    n_steps = n_spans * heads

    def produce_ctx():
        for h in range(heads):
            s = jnp.dot(kc_ref[...], q_t(h), preferred_element_type=F32)
            pc_refs[h][...] = jnp.exp2(s - bound[h]).astype(BF16)

    def consume_ctx():
        vta = with_ones(vtc_ref[...])
        for h in range(heads):
            acc_refs[h][...] += jnp.dot(vta, pc_refs[h][...], preferred_element_type=F32)

    def produce(p_ref, sp, h):
        k = kl_ref[pl.ds(pl.multiple_of(sp * span, span), span), :]
        s = jnp.dot(k, q_t(h), preferred_element_type=F32)
        p_ref[...] = jnp.exp2(s - bound[h]).astype(BF16)

    def consume(p_ref, sp, h):
        vta = with_ones(vtl_ref[:, pl.ds(pl.multiple_of(sp * span, span), span)])
        acc_refs[h][...] += jnp.dot(vta, p_ref[...], preferred_element_type=F32)

    @pl.when(bounded)
    def _():
        produce_ctx()

        @pl.when(qi >= n_lat_tiles)
        def _():
            consume_ctx()

        @pl.when(qi < n_lat_tiles)
        def _():
            produce(p_ring[0], 0, 0)
            consume_ctx()

            unroll = heads * (2 if n_spans % 2 == 0 else 1)

            def steps(j, last):
                for i in range(unroll):
                    step = unroll * j + i
                    if not (last and i == unroll - 1):
                        produce(p_ring[(i + 1) % 2], (step + 1) // heads, (i + 1) % heads)
                    consume(p_ring[i % 2], step // heads, i % heads)

            def body(j, carry):
                steps(j, False)
                return carry
            n_iter = n_steps // unroll
            lax.fori_loop(0, n_iter - 1, body, 0)
            steps(n_iter - 1, True)

    neg_inf = (jnp.full((1, tq), -jnp.inf, F32),) * heads

    def fill(s_refs, k, m_run):
        out = []
        for h in range(heads):
            s = jnp.dot(k, q_t(h), preferred_element_type=F32)
            s_refs[h][...] = s
            out.append(jnp.maximum(m_run[h], jnp.max(s, axis=0, keepdims=True)))
        return tuple(out)

    def drain(s_refs, vt, m_run, m_acc):
        vta = with_ones(vt)
        for h in range(heads):
            alpha = jnp.exp2(m_acc[h] - m_run[h])
            pt = jnp.exp2(s_refs[h][...] - m_run[h]).astype(BF16)
            acc_refs[h][...] = alpha * acc_refs[h][...] + jnp.dot(vta, pt, preferred_element_type=F32)
        return m_run

    @pl.when(jnp.logical_not(bounded))
    def _():
        m_ctx = fill(sc_refs, kc_ref[...], neg_inf)

        @pl.when(qi >= n_lat_tiles)
        def _():
            drain(sc_refs, vtc_ref[...], m_ctx, neg_inf)

        @pl.when(qi < n_lat_tiles)
        def _():
            m_one = fill(ring[0], k_chunk(0), m_ctx)
            m_two = fill(ring[1], k_chunk(1), m_one)
            m_acc = drain(sc_refs, vtc_ref[...], m_one, neg_inf)

            def body(j, carry):
                m_run, m_acc = carry
                for i in range(N_SCORE_BUFS):
                    c = N_SCORE_BUFS * j + i
                    m_next = fill(ring[(i + 2) % N_SCORE_BUFS], k_chunk(c + 2), m_run)
                    m_acc = drain(ring[i], vt_chunk(c), m_run, m_acc)
                    m_run = m_next
                return m_run, m_acc
            lax.fori_loop(0, n_chunks // N_SCORE_BUFS, body, (m_two, m_acc))

    for t in range(heads // 2):
        pair = []
        for e in range(2):
            a = acc_refs[2 * t + e][...]
            pair.append(a[:HEAD_DIM, :] / a[HEAD_DIM:HEAD_DIM + 1, :])
        o_ref[:, t * LANES:(t + 1) * LANES] = jnp.concatenate(pair, axis=0).T.astype(o_ref.dtype)


def _global_attention(p, vt, batch, n_lat, n_ctx, ctx_queries):
    tq = Q_TILE
    tk = min(KEY_CHUNK, n_lat // N_SCORE_BUFS)
    assert n_lat % (tk * N_SCORE_BUFS) == 0 and tk % LANES == 0
    span_chunks = math.gcd(SPAN_CHUNKS, n_lat // tk)
    nl, nc = n_lat // tq, n_ctx // tq
    nq = nl + (nc if ctx_queries else 0)
    out_rows = batch * (n_lat + (n_ctx if ctx_queries else 0))
    ctx_base = batch * nl
    heads = Q_HEADS // KV_HEADS
    acc_rows = HEAD_DIM + 16

    def q_block(b, qi):
        return jnp.where(qi < nl, b * nl + qi, ctx_base + b * nc + (qi - nl))

    ctx_blk = batch * n_lat // n_ctx
    return pl.pallas_call(
        functools.partial(_glb_kernel, n_lat_tiles=nl, tk=tk, span_chunks=span_chunks),
        out_shape=jax.ShapeDtypeStruct((out_rows, WIDTH), BF16),
        grid=(batch, KV_HEADS, nq),
        in_specs=[pl.BlockSpec((tq, 2 * LANES), lambda b, g, qi: (q_block(b, qi), C_GQ // (2 * LANES) + g)),
                  pl.BlockSpec((n_ctx, LANES), lambda b, g, qi: (ctx_blk + b, C_GK // LANES + g)),
                  pl.BlockSpec((HEAD_DIM, n_ctx), lambda b, g, qi: (KV_HEADS * VT_GLOBAL + g, ctx_blk + b)),
                  pl.BlockSpec((n_lat, LANES), lambda b, g, qi: (b, C_GK // LANES + g)),
                  pl.BlockSpec((HEAD_DIM, n_lat), lambda b, g, qi: (KV_HEADS * VT_GLOBAL + g, b))],
        out_specs=pl.BlockSpec((tq, 2 * LANES), lambda b, g, qi: (q_block(b, qi), g)),
        scratch_shapes=([pltpu.VMEM((LANES, heads * tq), BF16), pltpu.VMEM((1, 1), F32)]
                        + [pltpu.VMEM((acc_rows, tq), F32)] * heads
                        + [pltpu.VMEM((n_ctx, tq), F32)] * heads
                        + [pltpu.VMEM((tk, tq), F32)] * (N_SCORE_BUFS * heads)
                        + [pltpu.VMEM((n_ctx, tq), BF16)] * heads
                        + [pltpu.VMEM((span_chunks * tk, tq), BF16)] * 2),
        compiler_params=_params(("parallel", "parallel", "arbitrary")),
        name="global_attention",
    )(p, p, vt, p, vt)


def _merge_kernel(*refs, lat_tiles):
    n_x = 1 if lat_tiles is None else 2
    x_refs = refs[:n_x]
    (mod_ref, hf_ref, hb_ref, og_ref, yb_ref, yc_ref, ga_ref, gb_ref, gc_ref,
     mlg_ref, wbr_ref, wo_ref, o_ref) = refs[n_x:]
    tm = o_ref.shape[0]
    proj_b = jnp.dot(yb_ref[...], wbr_ref[1], preferred_element_type=F32)
    proj_c = jnp.dot(yc_ref[...], wbr_ref[2], preferred_element_type=F32)
    hs = hf_ref[...].astype(F32) + hb_ref[...].astype(F32)
    parts = []
    for t in range(ML_HEADS):
        ht = hs[:, t * ML_DIM:(t + 1) * ML_DIM]
        ms = jnp.mean(ht * ht, axis=-1, keepdims=True)
        parts.append(ht * lax.rsqrt(ms + EPS))
    ya = (jnp.concatenate(parts, axis=1) * mlg_ref[...] * _sigmoid(og_ref[...].astype(F32))).astype(BF16)
    proj_a = jnp.dot(ya, wbr_ref[0], preferred_element_type=F32)
    merged = (_sigmoid(gb_ref[...].astype(F32)) * proj_b + _sigmoid(gc_ref[...].astype(F32)) * proj_c
              + _sigmoid(ga_ref[...].astype(F32)) * proj_a)
    out = jnp.dot(merged.astype(BF16), wo_ref[...], preferred_element_type=F32)

    def gated_residual(x_ref):
        for sb in range(tm // ROW_BLOCK):
            rows = slice(sb * ROW_BLOCK, (sb + 1) * ROW_BLOCK)
            o_ref[rows, :] = x_ref[rows, :] + mod_ref[sb, 2:3, :] * out[rows, :]

    if lat_tiles is None:
        gated_residual(x_refs[0])
    else:
        pl.when(pl.program_id(0) < lat_tiles)(lambda: gated_residual(x_refs[0]))
        pl.when(pl.program_id(0) >= lat_tiles)(lambda: gated_residual(x_refs[1]))


def _merge(xs, modtab, p, hf, hb, yb, yc, mlg, wbr, wo, n_rows, lat_tiles):
    tm = ROW_TILE
    nb = tm // ROW_BLOCK
    rows = sum(x.shape[0] for x in xs)
    row = lambda i: (i, 0)
    gate = lambda k: pl.BlockSpec((tm, D_MODEL), lambda i: (i, C_GATE // D_MODEL + k))
    return pl.pallas_call(
        functools.partial(_merge_kernel, lat_tiles=lat_tiles if len(xs) == 2 else None),
        out_shape=jax.ShapeDtypeStruct((rows, D_MODEL), F32),
        grid=(n_rows // tm,),
        in_specs=_token_tile_specs(xs, tm, lat_tiles) + [
                  pl.BlockSpec((nb, 6, D_MODEL), lambda i: (i, 0, 0)),
                  pl.BlockSpec((tm, WIDTH), row),
                  pl.BlockSpec((tm, WIDTH), row),
                  pl.BlockSpec((tm, WIDTH), lambda i: (i, C_MLO // WIDTH)),
                  pl.BlockSpec((tm, WIDTH), row),
                  pl.BlockSpec((tm, WIDTH), row),
                  gate(0), gate(1), gate(2),
                  _resident((1, WIDTH)),
                  _resident((3, WIDTH, D_MODEL)),
                  _resident((D_MODEL, D_MODEL))],
        out_specs=pl.BlockSpec((tm, D_MODEL), row),
        input_output_aliases={0: 0} if len(xs) == 1 else {},
        compiler_params=_params(("parallel",)),
        name="merge",
    )(*xs, modtab, hf, hb, p, yb, yc, p, p, p, mlg, wbr, wo)


def _ffn_kernel(x_ref, mod_ref, g2_ref, w1_ref, w3_ref, w2_ref, gf_ref, o_ref, *, final):
    tm = x_ref.shape[0]
    d_ff = w1_ref.shape[1]
    split = pl.cdiv(d_ff // MXU_TILE, 2) * MXU_TILE
    for sb in range(tm // ROW_BLOCK):
        rows = slice(sb * ROW_BLOCK, (sb + 1) * ROW_BLOCK)
        xs = x_ref[rows, :]
        ms = jnp.mean(xs * xs, axis=-1, keepdims=True)
        y = xs * lax.rsqrt(ms + EPS) * g2_ref[...]
        h = (y * (1.0 + mod_ref[sb, 4:5, :]) + mod_ref[sb, 3:4, :]).astype(BF16)
        out = None
        for c0, c1 in ((0, split), (split, d_ff)):
            a = jnp.dot(h, w1_ref[:, c0:c1], preferred_element_type=F32)
            b = jnp.dot(h, w3_ref[:, c0:c1], preferred_element_type=F32)
            z = (a * _sigmoid(a) * b).astype(BF16)
            part = jnp.dot(z, w2_ref[c0:c1, :], preferred_element_type=F32)
            out = part if out is None else out + part
        xn = xs + mod_ref[sb, 5:6, :] * out
        if final:
            ms = jnp.mean(xn * xn, axis=-1, keepdims=True)
            xn = xn * lax.rsqrt(ms + EPS) * gf_ref[...]
        o_ref[rows, :] = xn


def _ffn(x, modtab, g2, w1, w3, w2, gfin, n_rows, final):
    tm = ROW_TILE
    nb = tm // ROW_BLOCK
    d_ff = w1.shape[1]
    row = lambda i: (i, 0)
    out_rows = n_rows if final else x.shape[0]
    return pl.pallas_call(
        functools.partial(_ffn_kernel, final=final),
        out_shape=jax.ShapeDtypeStruct((out_rows, D_MODEL), F32),
        grid=(n_rows // tm,),
        in_specs=[pl.BlockSpec((tm, D_MODEL), row),
                  pl.BlockSpec((nb, 6, D_MODEL), lambda i: (i, 0, 0)),
                  _resident((1, D_MODEL)),
                  _resident((D_MODEL, d_ff)),
                  _resident((D_MODEL, d_ff)),
                  _resident((d_ff, D_MODEL)),
                  _resident((1, D_MODEL))],
        out_specs=pl.BlockSpec((tm, D_MODEL), row),
        input_output_aliases={} if final else {0: 0},
        compiler_params=_params(("parallel",)),
        name="ffn",
    )(x, modtab, g2, w1, w3, w2, gfin)


def _dup_halves(w, base):
    h0 = w[..., base:base + HEAD_DIM]
    h1 = w[..., base + HEAD_DIM:base + 2 * HEAD_DIM]
    return [h0, h0, h1, h1]


def _arrange_in_proj(w):
    o_gate_ml = 4 * WIDTH
    o_wq = o_gate_ml + 2 * N_GATE
    o_wk, o_wv = o_wq + WIDTH, o_wq + WIDTH + 128
    o_gq = o_wv + 128
    o_gk, o_gv = o_gq + WIDTH, o_gq + WIDTH + 128
    o_gate = o_gv + 128
    main = jnp.concatenate(
        [w[..., o_gate:o_gate + 3 * D_MODEL], w[..., 0:WIDTH], w[..., 2 * WIDTH:4 * WIDTH],
         w[..., o_wq:o_wq + WIDTH]]
        + _dup_halves(w, o_wk) + [w[..., o_gq:o_gq + WIDTH]] + _dup_halves(w, o_gk), axis=-1)
    values = jnp.concatenate([w[..., o_gv:o_gv + 128], w[..., o_wv:o_wv + 128]], axis=-1)
    return main, w[..., o_gate_ml:o_gate_ml + 2 * N_GATE], w[..., WIDTH:2 * WIDTH], values


def _rope_tables(n_lat):
    t = jnp.arange(n_lat)
    quarter = HEAD_DIM // 4
    inv = ROPE_THETA ** (-jnp.arange(0, 2 * quarter, 2, dtype=F32) / (2 * quarter))
    ang_r = (t // GRID_W).astype(F32)[:, None] * inv
    ang_c = (t % GRID_W).astype(F32)[:, None] * inv
    cos = jnp.concatenate([jnp.cos(ang_r)] * 2 + [jnp.cos(ang_c)] * 2, axis=1)
    sin = jnp.concatenate([-jnp.sin(ang_r), jnp.sin(ang_r), -jnp.sin(ang_c), jnp.sin(ang_c)], axis=1)
    reps = LANES // HEAD_DIM
    return (jnp.concatenate([jnp.tile(cos, (1, reps)), jnp.ones((ROW_TILE, LANES), F32)], axis=0),
            jnp.concatenate([jnp.tile(sin, (1, reps)), jnp.zeros((ROW_TILE, LANES), F32)], axis=0))


def kernel(x, c, ctx, c_ctx, w_mod, b_mod, norm1_g, w_in, b_in, ml_norm_g, win_sink, qn_g, kn_g,
           w_br, w_o, norm2_g, w_ff1, w_ff3, w_ff2, final_g):
    batch, n_lat, d = x.shape
    n_ctx = ctx.shape[1]
    depth = w_mod.shape[0]
    assert d == D_MODEL and n_lat % ROW_TILE == 0 and n_lat % GRID_W == 0
    assert (batch * n_ctx) % ROW_TILE == 0 and n_ctx % ROW_BLOCK == 0 and (batch * n_lat) % n_ctx == 0
    lat_rows, ctx_rows = batch * n_lat, batch * n_ctx

    xs = (x.reshape(lat_rows, d), ctx.reshape(ctx_rows, d))
    cvec = jnp.concatenate([c, c_ctx[None, :], jnp.zeros((8 - batch - 1, d), F32)], axis=0)
    block_class = np.concatenate([np.repeat(np.arange(batch), n_lat // ROW_BLOCK),
                                  np.full(ctx_rows // ROW_BLOCK, batch)])
    cos, sin = _rope_tables(n_lat)
    lat_tiles = lat_rows // ROW_TILE
    avg = jnp.asarray(np.kron(np.eye(LANES // HEAD_DIM), np.full((HEAD_DIM, HEAD_DIM), 1.0 / HEAD_DIM)), BF16)

    out = None
    for l in range(depth):
        last = l == depth - 1
        mod = _mod_vectors(cvec, w_mod[l], b_mod[l]).reshape(8, 6, d)
        modtab = mod[block_class]

        w_main, w_gate, w_mlk, w_val = _arrange_in_proj(w_in[l])
        b_main, b_gate, b_mlk, b_val = _arrange_in_proj(b_in[l][None, :])
        p, gt, kt, vt = _in_projection(
            xs, modtab, norm1_g[l][None, :], w_main.astype(BF16), b_main,
            w_gate.T.astype(BF16), b_gate.T, w_mlk.T.astype(BF16), b_mlk.T, w_val.T.astype(BF16), b_val.T,
            cos, sin, jnp.tile(qn_g[l], 2)[None, :], jnp.tile(kn_g[l], 2)[None, :], avg, batch, n_lat)

        hf, hb = _mlstm(p, kt, gt, batch, n_lat, n_ctx)
        yb = _window_attention(p, vt, win_sink[l], batch, n_lat, n_ctx, ctx_queries=not last)
        yc = _global_attention(p, vt, batch, n_lat, n_ctx, ctx_queries=not last)

        n_rows = lat_rows if last else lat_rows + ctx_rows
        merged = _merge(xs, modtab, p, hf, hb, yb, yc, ml_norm_g[l].reshape(1, WIDTH),
                        w_br[l].astype(BF16), w_o[l].astype(BF16), n_rows, lat_tiles)
        out = _ffn(merged, modtab, norm2_g[l][None, :], w_ff1[l].astype(BF16), w_ff3[l].astype(BF16),
                   w_ff2[l].astype(BF16), final_g[None, :], n_rows, final=last)
        xs = (out,)
    return out.reshape(batch, n_lat, d)
```

```python
import functools
import math

import jax
import jax.numpy as jnp
import numpy as np
from jax import lax
from jax.experimental import pallas as pl
from jax.experimental.pallas import tpu as pltpu

F32 = jnp.float32
BF16 = jnp.bfloat16

D_MODEL = 1024
GRID_W = 64
CHUNK = 128
HEAD_DIM = 64
ROPE_THETA = 10000.0
EPS = 1e-6
ML_HEADS = 4
ML_DIM = 128
Q_HEADS = 8
KV_HEADS = 2
WIDTH = 512
N_GATE = 2 * ML_HEADS

LANES = 128
MXU_TILE = 256
ROW_BLOCK = 256
ROW_TILE = 512
VMEM_LIMIT = 56 * 1024 * 1024

C_GATE = 0
C_MLQ, C_MLV, C_MLO = 3072, 3584, 4096
C_WQ, C_WK = 4608, 5120
C_GQ, C_GK = 5376, 5888
KV_COLS = KV_HEADS * LANES
N_PROJ = C_GK + KV_COLS
Q_TILE = 256
VT_GLOBAL, VT_WINDOW = 0, 1
LOG2E = 1.4426950408889634
WIN_LEAD = 3
KEY_CHUNK = 512
MOD_TILE = 1536
N_SCORE_BUFS = 4
SPAN_CHUNKS = 4
SAFE_LOG2_BOUND = 50.0

NT_DIMS = (((1,), (1,)), ((), ()))


def _params(sem, vmem=VMEM_LIMIT):
    return pltpu.CompilerParams(dimension_semantics=sem, vmem_limit_bytes=vmem)


def _resident(shape):
    nd = len(shape)
    return pl.BlockSpec(shape, lambda *_: (0,) * nd, pipeline_mode=pl.Buffered(1))


def _sigmoid(x):
    return 1.0 / (1.0 + jnp.exp2(x * -LOG2E))


def _log_sigmoid(x):
    return jnp.minimum(x, 0.0) - jnp.log(1.0 + jnp.exp(-jnp.abs(x)))


def _mod_kernel(c_ref, w_ref, b_ref, o_ref):
    c = c_ref[...]
    s = c * _sigmoid(c)
    o_ref[...] = jnp.dot(s, w_ref[...], preferred_element_type=F32) + b_ref[...]


def _mod_vectors(cvec, w_mod, b_mod):
    n_out = w_mod.shape[1]
    tn = MOD_TILE
    return pl.pallas_call(
        _mod_kernel,
        out_shape=jax.ShapeDtypeStruct((cvec.shape[0], n_out), F32),
        grid=(n_out // tn,),
        in_specs=[pl.BlockSpec(cvec.shape, lambda j: (0, 0)),
                  pl.BlockSpec((D_MODEL, tn), lambda j: (0, j)),
                  pl.BlockSpec((1, tn), lambda j: (0, j))],
        out_specs=pl.BlockSpec((cvec.shape[0], tn), lambda j: (0, j)),
        compiler_params=_params(("parallel",)),
        name="mod_vectors",
    )(cvec, w_mod, b_mod.reshape(1, n_out))


def _rope(acc, cos, sin, first_half):
    w = acc.shape[1]
    reps = w // LANES
    if reps > 1:
        cos = jnp.concatenate([cos] * reps, axis=1)
        sin = jnp.concatenate([sin] * reps, axis=1)
    ahead = pltpu.roll(acc, w - 16, axis=1)
    behind = pltpu.roll(acc, 16, axis=1)
    return acc * cos + jnp.where(first_half, ahead, behind) * sin


def _head_rms(acc, avg, gain):
    sq = (acc * acc).astype(BF16)
    outs = []
    for t in range(acc.shape[1] // LANES):
        sl = slice(t * LANES, (t + 1) * LANES)
        ms = jnp.dot(sq[:, sl], avg, preferred_element_type=F32)
        outs.append(acc[:, sl] * lax.rsqrt(ms + EPS) * gain)
    return jnp.concatenate(outs, axis=1)


def _inproj_kernel(*refs, lat_tiles):
    n_x = 1 if lat_tiles is None else 2
    x_refs = refs[:n_x]
    (mod_ref, g1_ref, w_ref, b_ref, wgt_ref, bgt_ref, wkt_ref, bkt_ref, wvt_ref, bvt_ref,
     cos_ref, sin_ref, qg_ref, kg_ref, avg_ref, p_ref, gt_ref, kt_ref, vt_ref, h_ref) = refs[n_x:]
    tm = h_ref.shape[0]

    def modulated_norm(x_ref):
        for sb in range(tm // ROW_BLOCK):
            r0 = sb * ROW_BLOCK
            xs = x_ref[r0:r0 + ROW_BLOCK, :]
            ms = jnp.mean(xs * xs, axis=-1, keepdims=True)
            y = xs * lax.rsqrt(ms + EPS) * g1_ref[...]
            shift = mod_ref[sb, 0:1, :]
            scale = mod_ref[sb, 1:2, :]
            h_ref[r0:r0 + ROW_BLOCK, :] = (y * (1.0 + scale) + shift).astype(BF16)

    if lat_tiles is None:
        modulated_norm(x_refs[0])
    else:
        pl.when(pl.program_id(0) < lat_tiles)(lambda: modulated_norm(x_refs[0]))
        pl.when(pl.program_id(0) >= lat_tiles)(lambda: modulated_norm(x_refs[1]))

    def first_half(width):
        return (lax.broadcasted_iota(jnp.int32, (1, width), 1) % 32) < 16

    avg = avg_ref[...]
    qg = qg_ref[...]
    kg = kg_ref[...]
    q_scale = HEAD_DIM ** -0.5

    h = h_ref[...]
    cos = cos_ref[...]
    sin = sin_ref[...]

    def proj(c0, width):
        return (jnp.dot(h, w_ref[:, c0:c0 + width], preferred_element_type=F32)
                + b_ref[:, c0:c0 + width])

    def store_plain(cols):
        for c0 in cols:
            p_ref[:, c0:c0 + WIDTH] = proj(c0, WIDTH).astype(BF16)

    plain = list(range(C_GATE, C_GATE + 3 * D_MODEL, WIDTH)) + [C_MLQ, C_MLV, C_MLO]
    raw_gq, raw_gk = proj(C_GQ, WIDTH), proj(C_GK, KV_COLS)
    raw_wq, raw_wk = proj(C_WQ, WIDTH), proj(C_WK, KV_COLS)
    store_plain(plain[:3])
    gq = _rope(_head_rms(raw_gq, avg, qg), cos, sin, first_half(WIDTH)) * (q_scale * LOG2E)
    p_ref[:, C_GQ:C_GQ + WIDTH] = gq.astype(BF16)
    gk = _rope(_head_rms(raw_gk, avg, kg), cos, sin, first_half(KV_COLS))
    p_ref[:, C_GK:C_GK + KV_COLS] = gk.astype(BF16)
    p_ref[:, C_WQ:C_WQ + WIDTH] = (_rope(raw_wq, cos, sin, first_half(WIDTH)) * (q_scale * LOG2E)).astype(BF16)
    p_ref[:, C_WK:C_WK + KV_COLS] = _rope(raw_wk, cos, sin, first_half(KV_COLS)).astype(BF16)
    store_plain(plain[3:])

    wt = jnp.concatenate([wgt_ref[...], wkt_ref[...], wvt_ref[...]], axis=0)
    bt = jnp.concatenate([bgt_ref[...], bkt_ref[...], bvt_ref[...]], axis=0)
    pt = lax.dot_general(wt, h, NT_DIMS, preferred_element_type=F32) + bt
    n_g, n_k = 2 * N_GATE, 2 * N_GATE + WIDTH
    grow = lax.broadcasted_iota(jnp.int32, (n_g, 1), 0)
    gt_ref[...] = jnp.where(grow >= N_GATE, _log_sigmoid(pt[:n_g, :]), pt[:n_g, :])
    kt_ref[...] = pt[n_g:n_k, :].astype(BF16)
    vt_ref[...] = pt[n_k:, :].astype(BF16)


def _token_tile_specs(xs, tm, lat_tiles):
    if len(xs) == 1:
        return [pl.BlockSpec((tm, D_MODEL), lambda i: (i, 0))]
    ctx_tiles = xs[1].shape[0] // tm
    return [pl.BlockSpec((tm, D_MODEL), lambda i: (jnp.minimum(i, lat_tiles - 1), 0)),
            pl.BlockSpec((tm, D_MODEL), lambda i: (jnp.clip(i - lat_tiles, 0, ctx_tiles - 1), 0))]


def _in_projection(xs, modtab, g1, w, b, wgt, bgt, wkt, bkt, wvt, bvt, cos, sin, qg, kg, avg, batch, n_lat):
    rows = sum(x.shape[0] for x in xs)
    tm = ROW_TILE
    nb = tm // ROW_BLOCK
    kv = 2 * KV_HEADS * HEAD_DIM
    seq_tiles = n_lat // tm
    lat_tiles = batch * seq_tiles

    def rope_block(i):
        return (jnp.where(i < lat_tiles, i % seq_tiles, seq_tiles), 0)

    return pl.pallas_call(
        functools.partial(_inproj_kernel, lat_tiles=lat_tiles if len(xs) == 2 else None),
        out_shape=(jax.ShapeDtypeStruct((rows, N_PROJ), BF16),
                   jax.ShapeDtypeStruct((2 * N_GATE, rows), F32),
                   jax.ShapeDtypeStruct((WIDTH, rows), BF16),
                   jax.ShapeDtypeStruct((kv, rows), BF16)),
        grid=(rows // tm,),
        in_specs=_token_tile_specs(xs, tm, lat_tiles) + [
                  pl.BlockSpec((nb, 6, D_MODEL), lambda i: (i, 0, 0)),
                  _resident((1, D_MODEL)),
                  _resident((D_MODEL, N_PROJ)),
                  _resident((1, N_PROJ)),
                  _resident((2 * N_GATE, D_MODEL)),
                  _resident((2 * N_GATE, 1)),
                  _resident((WIDTH, D_MODEL)),
                  _resident((WIDTH, 1)),
                  _resident((kv, D_MODEL)),
                  _resident((kv, 1)),
                  pl.BlockSpec((tm, LANES), rope_block),
                  pl.BlockSpec((tm, LANES), rope_block),
                  _resident((1, LANES)),
                  _resident((1, LANES)),
                  _resident((LANES, LANES))],
        out_specs=(pl.BlockSpec((tm, N_PROJ), lambda i: (i, 0)),
                   pl.BlockSpec((2 * N_GATE, tm), lambda i: (0, i)),
                   pl.BlockSpec((WIDTH, tm), lambda i: (0, i)),
                   pl.BlockSpec((kv, tm), lambda i: (0, i))),
        scratch_shapes=[pltpu.VMEM((tm, D_MODEL), BF16)],
        compiler_params=_params(("parallel",)),
        name="in_projection",
    )(*xs, modtab, g1, w, b, wgt, bgt, wkt, bkt, wvt, bvt, cos, sin, qg, kg, avg)


def _scan_lanes(x, op, fill, reverse):
    n = x.shape[1]
    lane = lax.broadcasted_iota(jnp.int32, x.shape, 1)
    sh = 1
    while sh < n:
        if reverse:
            moved = jnp.where(lane < n - sh, pltpu.roll(x, n - sh, axis=1), fill)
        else:
            moved = jnp.where(lane >= sh, pltpu.roll(x, sh, axis=1), fill)
        x = op(x, moved)
        sh *= 2
    return x


def _mlstm_kernel(qvf_ref, ktf_ref, gtf_ref, qvb_ref, ktb_ref, gtb_ref, hf_ref, hb_ref, cn_ref, m_ref):
    @pl.when(pl.program_id(1) == 0)
    def _():
        cn_ref[...] = jnp.zeros_like(cn_ref)
        m_ref[...] = jnp.zeros_like(m_ref)

    L = CHUNK
    row = lax.broadcasted_iota(jnp.int32, (L, L), 0)
    col = lax.broadcasted_iota(jnp.int32, (L, L), 1)
    scale = ML_DIM ** -0.5
    ones_v = jnp.ones((L, ML_DIM), BF16)
    pending = []
    for d, (qv_ref, kt_ref, gt_ref, h_ref) in enumerate(((qvf_ref, ktf_ref, gtf_ref, hf_ref),
                                                          (qvb_ref, ktb_ref, gtb_ref, hb_ref))):
        seen = (col <= row) if d == 0 else (col >= row)
        g0 = d * ML_HEADS
        ic = gt_ref[g0:g0 + ML_HEADS, :]
        lf = gt_ref[N_GATE + g0:N_GATE + g0 + ML_HEADS, :]
        r = ic - _scan_lanes(lf, jnp.add, 0.0, d == 1)
        r_max = jnp.max(r, axis=1, keepdims=True)
        b_end = jnp.sum(lf, axis=1, keepdims=True)
        for hh in range(ML_HEADS):
            idx = g0 + hh
            lanes = slice(hh * ML_DIM, (hh + 1) * ML_DIM)
            r_row = r[hh:hh + 1, :]
            m_old = m_ref[idx]
            m_end = jnp.maximum(m_old, r_max[hh:hh + 1, :])
            q = qv_ref[:, lanes]
            vo = jnp.concatenate([qv_ref[:, WIDTH + hh * ML_DIM:WIDTH + (hh + 1) * ML_DIM], ones_v], axis=1)
            kt = kt_ref[lanes, :]
            qsb = (q.astype(F32) * scale).astype(BF16)
            cn = cn_ref[idx]
            s_raw = jnp.dot(qsb, kt, preferred_element_type=F32)
            q_cn = jnp.dot(qsb, cn.astype(BF16), preferred_element_type=F32)
            kwt = (kt.astype(F32) * jnp.exp(r_row - m_end)).astype(BF16)
            cn_ref[idx] = jnp.exp(m_old - m_end) * cn + jnp.dot(kwt, vo, preferred_element_type=F32)
            m_ref[idx] = b_end[hh:hh + 1, :] + m_end
            pending.append((h_ref, lanes, seen, r_row, lf[hh:hh + 1, :], m_old, s_raw, q_cn, vo))

    for h_ref, lanes, seen, r_row, lf_row, m_old, s_raw, q_cn, vo in pending:
        b_col = jnp.sum(jnp.where(seen, lf_row, 0.0), axis=1, keepdims=True)
        m_col = jnp.maximum(m_old, jnp.max(jnp.where(seen, r_row, -jnp.inf), axis=1, keepdims=True))
        w = jnp.exp(jnp.where(seen, r_row - m_col, -jnp.inf))
        a = jnp.exp(m_old - m_col)
        s_vo = jnp.dot((s_raw * w).astype(BF16), vo, preferred_element_type=F32)
        num = a * q_cn[:, :ML_DIM] + s_vo[:, :ML_DIM]
        den = a * q_cn[:, ML_DIM:] + s_vo[:, ML_DIM:]
        hc = num / jnp.maximum(jnp.abs(den), jnp.exp(-(b_col + m_col)))
        h_ref[:, lanes] = hc.astype(h_ref.dtype)


def _mlstm(p, kt, gt, batch, n_lat, n_ctx):
    rows = p.shape[0]
    cl, cc = n_lat // CHUNK, n_ctx // CHUNK
    lat_base, ctx_base = 0, batch * cl

    def fwd_chunk(b, i):
        return jnp.where(i < cc, ctx_base + b * cc + i, lat_base + b * cl + (i - cc))

    def bwd_chunk(b, i):
        return jnp.where(i < cc, ctx_base + b * cc + (cc - 1 - i), lat_base + b * cl + (cl - 1 - (i - cc)))

    def specs(chunk):
        return [pl.BlockSpec((CHUNK, 2 * WIDTH), lambda b, i: (chunk(b, i), C_MLQ // (2 * WIDTH))),
                pl.BlockSpec((WIDTH, CHUNK), lambda b, i: (0, chunk(b, i))),
                pl.BlockSpec((2 * N_GATE, CHUNK), lambda b, i: (0, chunk(b, i)))]

    n_state = 2 * ML_HEADS
    return pl.pallas_call(
        _mlstm_kernel,
        out_shape=(jax.ShapeDtypeStruct((rows, WIDTH), BF16), jax.ShapeDtypeStruct((rows, WIDTH), BF16)),
        grid=(batch, cl + cc),
        in_specs=specs(fwd_chunk) + specs(bwd_chunk),
        out_specs=(pl.BlockSpec((CHUNK, WIDTH), lambda b, i: (fwd_chunk(b, i), 0)),
                   pl.BlockSpec((CHUNK, WIDTH), lambda b, i: (bwd_chunk(b, i), 0))),
        scratch_shapes=[pltpu.VMEM((n_state, ML_DIM, 2 * ML_DIM), F32),
                        pltpu.VMEM((n_state, 1, 1), F32)],
        compiler_params=_params(("parallel", "arbitrary")),
        name="mlstm_scan",
    )(p, kt, gt, p, kt, gt)


def _half_mask(e):
    lane = lax.broadcasted_iota(jnp.int32, (1, LANES), 1)
    return (lane < HEAD_DIM) if e == 0 else (lane >= HEAD_DIM)


def _win_kernel(sink_ref, q_ref, kc_ref, vtc_ref, k0_ref, k1_ref, k2_ref, k3_ref,
                vt0_ref, vt1_ref, vt2_ref, vt3_ref, o_ref, *, n_lat_tiles, n_lat_blocks):
    t = pl.program_id(1)
    tq = q_ref.shape[0]
    n_ctx = kc_ref.shape[0]
    is_lat = t < n_lat_tiles
    kk = lax.broadcasted_iota(jnp.int32, (CHUNK, tq), 0)
    qq = lax.broadcasted_iota(jnp.int32, (CHUNK, tq), 1)
    band = (kk >= qq, qq <= kk + CHUNK, kk <= qq, kk + CHUNK <= qq)
    first = 2 * t - 1
    present = [jnp.logical_and(is_lat, jnp.logical_and(first + i >= 0, first + i < n_lat_blocks))
               for i in range(4)]
    visible = [jnp.logical_and(band[i], present[i]) for i in range(4)]
    k_refs = (k0_ref, k1_ref, k2_ref, k3_ref)
    vt_refs = (vt0_ref, vt1_ref, vt2_ref, vt3_ref)
    ones = jnp.ones((16, n_ctx + 4 * CHUNK), BF16)

    def group_operands(g):
        lanes = slice(g * LANES, (g + 1) * LANES)
        k_all = jnp.concatenate([kc_ref[:, lanes]] + [r[:, lanes] for r in k_refs], axis=0)
        vrows = slice(g * HEAD_DIM, (g + 1) * HEAD_DIM)
        vta = jnp.concatenate(
            [jnp.concatenate([vtc_ref[vrows, :]] + [r[vrows, :] for r in vt_refs], axis=1), ones], axis=0)
        return k_all, vta

    operands = [group_operands(g) for g in range(KV_HEADS)]

    def score(h):
        qf = q_ref[:, (h // 2) * LANES:(h // 2 + 1) * LANES].astype(F32)
        qt = jnp.where(_half_mask(h % 2), qf, 0.0).T.astype(BF16)
        return jnp.dot(operands[h // (Q_HEADS // KV_HEADS)][0], qt, preferred_element_type=F32)

    scores = {h: score(h) for h in range(WIN_LEAD)}
    outs = []
    for h in range(Q_HEADS):
        if h + WIN_LEAD < Q_HEADS:
            scores[h + WIN_LEAD] = score(h + WIN_LEAD)
        s = scores.pop(h)
        sink = sink_ref[h] * LOG2E
        parts = [s[:n_ctx, :]]
        for i in range(4):
            blk = s[n_ctx + i * CHUNK:n_ctx + (i + 1) * CHUNK, :]
            parts.append(jnp.where(visible[i], blk, -jnp.inf))
        s = jnp.concatenate(parts, axis=0)
        m = jnp.maximum(jnp.max(s, axis=0, keepdims=True), sink)
        pt = jnp.exp2(s - m).astype(BF16)
        ol = jnp.dot(operands[h // (Q_HEADS // KV_HEADS)][1], pt, preferred_element_type=F32)
        den = ol[HEAD_DIM:HEAD_DIM + 1, :] + jnp.exp2(sink - m)
        outs.append(ol[:HEAD_DIM, :] / den)
    for tt in range(Q_HEADS // 2):
        o_ref[:, tt * LANES:(tt + 1) * LANES] = (
            jnp.concatenate(outs[2 * tt:2 * tt + 2], axis=0).T.astype(o_ref.dtype))


def _window_attention(p, vt, sink, batch, n_lat, n_ctx, ctx_queries):
    tq = Q_TILE
    nlt, nct = n_lat // tq, n_ctx // tq
    nlb = n_lat // CHUNK
    nq = nlt + (nct if ctx_queries else 0)
    out_rows = batch * (n_lat + (n_ctx if ctx_queries else 0))
    ctx_blk = batch * n_lat // n_ctx
    k_col = C_WK // KV_COLS

    def q_block(b, t):
        return jnp.where(t < nlt, b * nlt + t, batch * nlt + b * nct + (t - nlt))

    def near(i):
        return lambda b, t: b * nlb + jnp.clip(2 * t - 1 + i, 0, nlb - 1)

    k_specs = [pl.BlockSpec((CHUNK, KV_COLS), (lambda f: lambda b, t: (f(b, t), k_col))(near(i))) for i in range(4)]
    vt_specs = [pl.BlockSpec((LANES, CHUNK), (lambda f: lambda b, t: (VT_WINDOW, f(b, t)))(near(i)))
                for i in range(4)]
    return pl.pallas_call(
        functools.partial(_win_kernel, n_lat_tiles=nlt, n_lat_blocks=nlb),
        out_shape=jax.ShapeDtypeStruct((out_rows, WIDTH), BF16),
        grid=(batch, nq),
        in_specs=[pl.BlockSpec(memory_space=pltpu.SMEM),
                  pl.BlockSpec((tq, WIDTH), lambda b, t: (q_block(b, t), C_WQ // WIDTH)),
                  pl.BlockSpec((n_ctx, KV_COLS), lambda b, t: (ctx_blk + b, k_col)),
                  pl.BlockSpec((LANES, n_ctx), lambda b, t: (VT_WINDOW, ctx_blk + b))]
                 + k_specs + vt_specs,
        out_specs=pl.BlockSpec((tq, WIDTH), lambda b, t: (q_block(b, t), 0)),
        compiler_params=_params(("parallel", "parallel")),
        name="window_attention",
    )(sink, p, p, vt, p, p, p, p, vt, vt, vt, vt)


def _glb_kernel(q_ref, qn_ref, kc_ref, vtc_ref, kl_ref, vtl_ref, o_ref, qt_ref, knorm_ref, *scratch,
                n_lat_tiles, tk, span_chunks):
    heads = Q_HEADS // KV_HEADS
    acc_refs = scratch[0:heads]
    sc_refs = scratch[heads:2 * heads]
    ring = [scratch[(2 + i) * heads:(3 + i) * heads] for i in range(N_SCORE_BUFS)]
    base = (2 + N_SCORE_BUFS) * heads
    pc_refs = scratch[base:base + heads]
    p_ring = scratch[base + heads:base + heads + 2]
    ahead_ref = scratch[base + heads + 2]
    qi = pl.program_id(2)
    tq = q_ref.shape[0]
    n_chunks = kl_ref.shape[0] // tk

    def k_chunk(c):
        off = pl.multiple_of(jnp.minimum(c, n_chunks - 1) * tk, tk)
        return kl_ref[pl.ds(off, tk), :]

    def vt_chunk(c):
        return vtl_ref[:, pl.ds(pl.multiple_of(c * tk, tk), tk)]

    def with_ones(vt):
        ones = jnp.ones((acc_refs[0].shape[0] - HEAD_DIM, vt.shape[1]), BF16)
        return jnp.concatenate([vt, ones], axis=0)

    @pl.when(qi == 0)
    def _():
        def sq_norm(k):
            kf = k.astype(F32)
            return jnp.max(jnp.sum(kf * kf, axis=1, keepdims=True), axis=0, keepdims=True)

        def body(c, best):
            return jnp.maximum(best, sq_norm(k_chunk(c)))
        best = lax.fori_loop(0, n_chunks, body, sq_norm(kc_ref[...]))
        knorm_ref[...] = jnp.sqrt(0.5 * best)

    for t in range(heads // 2):
        qf = q_ref[:, t * LANES:(t + 1) * LANES].astype(F32)
        for e in range(2):
            h = 2 * t + e
            qt_ref[:, h * tq:(h + 1) * tq] = jnp.where(_half_mask(e), qf, 0.0).T.astype(BF16)
    for h in range(heads):
        acc_refs[h][...] = jnp.zeros_like(acc_refs[h])

    def q_t(h):
        return qt_ref[:, h * tq:(h + 1) * tq]

    bound = [jnp.sqrt(jnp.sum(jnp.square(q_t(h).astype(F32)), axis=0, keepdims=True)) * knorm_ref[...]
             for h in range(heads)]
    bounded = jnp.max(functools.reduce(jnp.maximum, bound)) <= SAFE_LOG2_BOUND

    span = span_chunks * tk
    n_spans = n_chunks // span_chunks
    n_steps = n_spans * heads

    def produce_ctx():
        for h in range(heads):
            s = jnp.dot(kc_ref[...], q_t(h), preferred_element_type=F32)
            pc_refs[h][...] = jnp.exp2(s - bound[h]).astype(BF16)

    def consume_ctx():
        vta = with_ones(vtc_ref[...])
        for h in range(heads):
            acc_refs[h][...] += jnp.dot(vta, pc_refs[h][...], preferred_element_type=F32)

    def produce(p_ref, sp, h):
        k = kl_ref[pl.ds(pl.multiple_of(sp * span, span), span), :]
        s = jnp.dot(k, q_t(h), preferred_element_type=F32)
        p_ref[...] = jnp.exp2(s - bound[h]).astype(BF16)

    def consume(p_ref, sp, h):
        vta = with_ones(vtl_ref[:, pl.ds(pl.multiple_of(sp * span, span), span)])
        acc_refs[h][...] += jnp.dot(vta, p_ref[...], preferred_element_type=F32)

    def produce_next_tile(p_ref):
        qn = jnp.where(_half_mask(0), qn_ref[:, 0:LANES].astype(F32), 0.0).T.astype(BF16)
        bound_n = jnp.sqrt(jnp.sum(jnp.square(qn.astype(F32)), axis=0, keepdims=True)) * knorm_ref[...]
        s = jnp.dot(kl_ref[0:span, :], qn, preferred_element_type=F32)
        p_ref[...] = jnp.exp2(s - bound_n).astype(BF16)

    @pl.when(qi == 0)
    def _():
        ahead_ref[0] = 0
    ahead = ahead_ref[0] == 1
    ahead_ref[0] = 0

    @pl.when(bounded)
    def _():
        produce_ctx()

        @pl.when(qi >= n_lat_tiles)
        def _():
            consume_ctx()

        @pl.when(qi < n_lat_tiles)
        def _():
            pl.when(jnp.logical_not(ahead))(lambda: produce(p_ring[0], 0, 0))
            consume_ctx()

            unroll = heads * (2 if n_spans % 2 == 0 else 1)

            def steps(j, last):
                for i in range(unroll):
                    step = unroll * j + i
                    if last and i == unroll - 1:
                        produce_next_tile(p_ring[(i + 1) % 2])
                    else:
                        produce(p_ring[(i + 1) % 2], (step + 1) // heads, (i + 1) % heads)
                    consume(p_ring[i % 2], step // heads, i % heads)

            def body(j, carry):
                steps(j, False)
                return carry
            n_iter = n_steps // unroll
            lax.fori_loop(0, n_iter - 1, body, 0)
            steps(n_iter - 1, True)
            ahead_ref[0] = 1

    neg_inf = (jnp.full((1, tq), -jnp.inf, F32),) * heads

    def fill(s_refs, k, m_run):
        out = []
        for h in range(heads):
            s = jnp.dot(k, q_t(h), preferred_element_type=F32)
            s_refs[h][...] = s
            out.append(jnp.maximum(m_run[h], jnp.max(s, axis=0, keepdims=True)))
        return tuple(out)

    def drain(s_refs, vt, m_run, m_acc):
        vta = with_ones(vt)
        for h in range(heads):
            alpha = jnp.exp2(m_acc[h] - m_run[h])
            pt = jnp.exp2(s_refs[h][...] - m_run[h]).astype(BF16)
            acc_refs[h][...] = alpha * acc_refs[h][...] + jnp.dot(vta, pt, preferred_element_type=F32)
        return m_run

    @pl.when(jnp.logical_not(bounded))
    def _():
        m_ctx = fill(sc_refs, kc_ref[...], neg_inf)

        @pl.when(qi >= n_lat_tiles)
        def _():
            drain(sc_refs, vtc_ref[...], m_ctx, neg_inf)

        @pl.when(qi < n_lat_tiles)
        def _():
            m_one = fill(ring[0], k_chunk(0), m_ctx)
            m_two = fill(ring[1], k_chunk(1), m_one)
            m_acc = drain(sc_refs, vtc_ref[...], m_one, neg_inf)

            def body(j, carry):
                m_run, m_acc = carry
                for i in range(N_SCORE_BUFS):
                    c = N_SCORE_BUFS * j + i
                    m_next = fill(ring[(i + 2) % N_SCORE_BUFS], k_chunk(c + 2), m_run)
                    m_acc = drain(ring[i], vt_chunk(c), m_run, m_acc)
                    m_run = m_next
                return m_run, m_acc
            lax.fori_loop(0, n_chunks // N_SCORE_BUFS, body, (m_two, m_acc))

    for t in range(heads // 2):
        pair = []
        for e in range(2):
            a = acc_refs[2 * t + e][...]
            pair.append(a[:HEAD_DIM, :] / a[HEAD_DIM:HEAD_DIM + 1, :])
        o_ref[:, t * LANES:(t + 1) * LANES] = jnp.concatenate(pair, axis=0).T.astype(o_ref.dtype)


def _global_attention(p, vt, batch, n_lat, n_ctx, ctx_queries):
    tq = Q_TILE
    tk = min(KEY_CHUNK, n_lat // N_SCORE_BUFS)
    assert n_lat % (tk * N_SCORE_BUFS) == 0 and tk % LANES == 0
    span_chunks = math.gcd(SPAN_CHUNKS, n_lat // tk)
    nl, nc = n_lat // tq, n_ctx // tq
    nq = nl + (nc if ctx_queries else 0)
    out_rows = batch * (n_lat + (n_ctx if ctx_queries else 0))
    ctx_base = batch * nl
    heads = Q_HEADS // KV_HEADS
    acc_rows = HEAD_DIM + 16

    def q_block(b, qi):
        return jnp.where(qi < nl, b * nl + qi, ctx_base + b * nc + (qi - nl))

    ctx_blk = batch * n_lat // n_ctx
    return pl.pallas_call(
        functools.partial(_glb_kernel, n_lat_tiles=nl, tk=tk, span_chunks=span_chunks),
        out_shape=jax.ShapeDtypeStruct((out_rows, WIDTH), BF16),
        grid=(batch, KV_HEADS, nq),
        in_specs=[pl.BlockSpec((tq, 2 * LANES), lambda b, g, qi: (q_block(b, qi), C_GQ // (2 * LANES) + g)),
                  pl.BlockSpec((tq, 2 * LANES), lambda b, g, qi: (q_block(b, jnp.minimum(qi + 1, nq - 1)),
                                                                  C_GQ // (2 * LANES) + g)),
                  pl.BlockSpec((n_ctx, LANES), lambda b, g, qi: (ctx_blk + b, C_GK // LANES + g)),
                  pl.BlockSpec((HEAD_DIM, n_ctx), lambda b, g, qi: (KV_HEADS * VT_GLOBAL + g, ctx_blk + b)),
                  pl.BlockSpec((n_lat, LANES), lambda b, g, qi: (b, C_GK // LANES + g)),
                  pl.BlockSpec((HEAD_DIM, n_lat), lambda b, g, qi: (KV_HEADS * VT_GLOBAL + g, b))],
        out_specs=pl.BlockSpec((tq, 2 * LANES), lambda b, g, qi: (q_block(b, qi), g)),
        scratch_shapes=([pltpu.VMEM((LANES, heads * tq), BF16), pltpu.VMEM((1, 1), F32)]
                        + [pltpu.VMEM((acc_rows, tq), F32)] * heads
                        + [pltpu.VMEM((n_ctx, tq), F32)] * heads
                        + [pltpu.VMEM((tk, tq), F32)] * (N_SCORE_BUFS * heads)
                        + [pltpu.VMEM((n_ctx, tq), BF16)] * heads
                        + [pltpu.VMEM((span_chunks * tk, tq), BF16)] * 2
                        + [pltpu.SMEM((1,), jnp.int32)]),
        compiler_params=_params(("parallel", "parallel", "arbitrary")),
        name="global_attention",
    )(p, p, p, vt, p, vt)


def _merge_kernel(*refs, lat_tiles):
    n_x = 1 if lat_tiles is None else 2
    x_refs = refs[:n_x]
    (mod_ref, hf_ref, hb_ref, og_ref, yb_ref, yc_ref, ga_ref, gb_ref, gc_ref,
     mlg_ref, wbr_ref, wo_ref, o_ref) = refs[n_x:]
    tm = o_ref.shape[0]
    proj_b = jnp.dot(yb_ref[...], wbr_ref[1], preferred_element_type=F32)
    proj_c = jnp.dot(yc_ref[...], wbr_ref[2], preferred_element_type=F32)
    hs = hf_ref[...].astype(F32) + hb_ref[...].astype(F32)
    parts = []
    for t in range(ML_HEADS):
        ht = hs[:, t * ML_DIM:(t + 1) * ML_DIM]
        ms = jnp.mean(ht * ht, axis=-1, keepdims=True)
        parts.append(ht * lax.rsqrt(ms + EPS))
    ya = (jnp.concatenate(parts, axis=1) * mlg_ref[...] * _sigmoid(og_ref[...].astype(F32))).astype(BF16)
    proj_a = jnp.dot(ya, wbr_ref[0], preferred_element_type=F32)
    merged = (_sigmoid(gb_ref[...].astype(F32)) * proj_b + _sigmoid(gc_ref[...].astype(F32)) * proj_c
              + _sigmoid(ga_ref[...].astype(F32)) * proj_a)
    out = jnp.dot(merged.astype(BF16), wo_ref[...], preferred_element_type=F32)

    def gated_residual(x_ref):
        for sb in range(tm // ROW_BLOCK):
            rows = slice(sb * ROW_BLOCK, (sb + 1) * ROW_BLOCK)
            o_ref[rows, :] = x_ref[rows, :] + mod_ref[sb, 2:3, :] * out[rows, :]

    if lat_tiles is None:
        gated_residual(x_refs[0])
    else:
        pl.when(pl.program_id(0) < lat_tiles)(lambda: gated_residual(x_refs[0]))
        pl.when(pl.program_id(0) >= lat_tiles)(lambda: gated_residual(x_refs[1]))


def _merge(xs, modtab, p, hf, hb, yb, yc, mlg, wbr, wo, n_rows, lat_tiles):
    tm = ROW_TILE
    nb = tm // ROW_BLOCK
    rows = sum(x.shape[0] for x in xs)
    row = lambda i: (i, 0)
    gate = lambda k: pl.BlockSpec((tm, D_MODEL), lambda i: (i, C_GATE // D_MODEL + k))
    return pl.pallas_call(
        functools.partial(_merge_kernel, lat_tiles=lat_tiles if len(xs) == 2 else None),
        out_shape=jax.ShapeDtypeStruct((rows, D_MODEL), F32),
        grid=(n_rows // tm,),
        in_specs=_token_tile_specs(xs, tm, lat_tiles) + [
                  pl.BlockSpec((nb, 6, D_MODEL), lambda i: (i, 0, 0)),
                  pl.BlockSpec((tm, WIDTH), row),
                  pl.BlockSpec((tm, WIDTH), row),
                  pl.BlockSpec((tm, WIDTH), lambda i: (i, C_MLO // WIDTH)),
                  pl.BlockSpec((tm, WIDTH), row),
                  pl.BlockSpec((tm, WIDTH), row),
                  gate(0), gate(1), gate(2),
                  _resident((1, WIDTH)),
                  _resident((3, WIDTH, D_MODEL)),
                  _resident((D_MODEL, D_MODEL))],
        out_specs=pl.BlockSpec((tm, D_MODEL), row),
        input_output_aliases={0: 0} if len(xs) == 1 else {},
        compiler_params=_params(("parallel",)),
        name="merge",
    )(*xs, modtab, hf, hb, p, yb, yc, p, p, p, mlg, wbr, wo)


def _ffn_kernel(x_ref, mod_ref, g2_ref, w1_ref, w3_ref, w2_ref, gf_ref, o_ref, *, final):
    tm = x_ref.shape[0]
    d_ff = w1_ref.shape[1]
    split = pl.cdiv(d_ff // MXU_TILE, 2) * MXU_TILE
    for sb in range(tm // ROW_BLOCK):
        rows = slice(sb * ROW_BLOCK, (sb + 1) * ROW_BLOCK)
        xs = x_ref[rows, :]
        ms = jnp.mean(xs * xs, axis=-1, keepdims=True)
        y = xs * lax.rsqrt(ms + EPS) * g2_ref[...]
        h = (y * (1.0 + mod_ref[sb, 4:5, :]) + mod_ref[sb, 3:4, :]).astype(BF16)
        out = None
        for c0, c1 in ((0, split), (split, d_ff)):
            a = jnp.dot(h, w1_ref[:, c0:c1], preferred_element_type=F32)
            b = jnp.dot(h, w3_ref[:, c0:c1], preferred_element_type=F32)
            z = (a * _sigmoid(a) * b).astype(BF16)
            part = jnp.dot(z, w2_ref[c0:c1, :], preferred_element_type=F32)
            out = part if out is None else out + part
        xn = xs + mod_ref[sb, 5:6, :] * out
        if final:
            ms = jnp.mean(xn * xn, axis=-1, keepdims=True)
            xn = xn * lax.rsqrt(ms + EPS) * gf_ref[...]
        o_ref[rows, :] = xn


def _ffn(x, modtab, g2, w1, w3, w2, gfin, n_rows, final):
    tm = ROW_TILE
    nb = tm // ROW_BLOCK
    d_ff = w1.shape[1]
    row = lambda i: (i, 0)
    out_rows = n_rows if final else x.shape[0]
    return pl.pallas_call(
        functools.partial(_ffn_kernel, final=final),
        out_shape=jax.ShapeDtypeStruct((out_rows, D_MODEL), F32),
        grid=(n_rows // tm,),
        in_specs=[pl.BlockSpec((tm, D_MODEL), row),
                  pl.BlockSpec((nb, 6, D_MODEL), lambda i: (i, 0, 0)),
                  _resident((1, D_MODEL)),
                  _resident((D_MODEL, d_ff)),
                  _resident((D_MODEL, d_ff)),
                  _resident((d_ff, D_MODEL)),
                  _resident((1, D_MODEL))],
        out_specs=pl.BlockSpec((tm, D_MODEL), row),
        input_output_aliases={} if final else {0: 0},
        compiler_params=_params(("parallel",)),
        name="ffn",
    )(x, modtab, g2, w1, w3, w2, gfin)


def _dup_halves(w, base):
    h0 = w[..., base:base + HEAD_DIM]
    h1 = w[..., base + HEAD_DIM:base + 2 * HEAD_DIM]
    return [h0, h0, h1, h1]


def _arrange_in_proj(w):
    o_gate_ml = 4 * WIDTH
    o_wq = o_gate_ml + 2 * N_GATE
    o_wk, o_wv = o_wq + WIDTH, o_wq + WIDTH + 128
    o_gq = o_wv + 128
    o_gk, o_gv = o_gq + WIDTH, o_gq + WIDTH + 128
    o_gate = o_gv + 128
    main = jnp.concatenate(
        [w[..., o_gate:o_gate + 3 * D_MODEL], w[..., 0:WIDTH], w[..., 2 * WIDTH:4 * WIDTH],
         w[..., o_wq:o_wq + WIDTH]]
        + _dup_halves(w, o_wk) + [w[..., o_gq:o_gq + WIDTH]] + _dup_halves(w, o_gk), axis=-1)
    values = jnp.concatenate([w[..., o_gv:o_gv + 128], w[..., o_wv:o_wv + 128]], axis=-1)
    return main, w[..., o_gate_ml:o_gate_ml + 2 * N_GATE], w[..., WIDTH:2 * WIDTH], values


def _rope_tables(n_lat):
    t = jnp.arange(n_lat)
    quarter = HEAD_DIM // 4
    inv = ROPE_THETA ** (-jnp.arange(0, 2 * quarter, 2, dtype=F32) / (2 * quarter))
    ang_r = (t // GRID_W).astype(F32)[:, None] * inv
    ang_c = (t % GRID_W).astype(F32)[:, None] * inv
    cos = jnp.concatenate([jnp.cos(ang_r)] * 2 + [jnp.cos(ang_c)] * 2, axis=1)
    sin = jnp.concatenate([-jnp.sin(ang_r), jnp.sin(ang_r), -jnp.sin(ang_c), jnp.sin(ang_c)], axis=1)
    reps = LANES // HEAD_DIM
    return (jnp.concatenate([jnp.tile(cos, (1, reps)), jnp.ones((ROW_TILE, LANES), F32)], axis=0),
            jnp.concatenate([jnp.tile(sin, (1, reps)), jnp.zeros((ROW_TILE, LANES), F32)], axis=0))


def kernel(x, c, ctx, c_ctx, w_mod, b_mod, norm1_g, w_in, b_in, ml_norm_g, win_sink, qn_g, kn_g,
           w_br, w_o, norm2_g, w_ff1, w_ff3, w_ff2, final_g):
    batch, n_lat, d = x.shape
    n_ctx = ctx.shape[1]
    depth = w_mod.shape[0]
    assert d == D_MODEL and n_lat % ROW_TILE == 0 and n_lat % GRID_W == 0
    assert (batch * n_ctx) % ROW_TILE == 0 and n_ctx % ROW_BLOCK == 0 and (batch * n_lat) % n_ctx == 0
    lat_rows, ctx_rows = batch * n_lat, batch * n_ctx

    xs = (x.reshape(lat_rows, d), ctx.reshape(ctx_rows, d))
    cvec = jnp.concatenate([c, c_ctx[None, :], jnp.zeros((8 - batch - 1, d), F32)], axis=0)
    block_class = np.concatenate([np.repeat(np.arange(batch), n_lat // ROW_BLOCK),
                                  np.full(ctx_rows // ROW_BLOCK, batch)])
    cos, sin = _rope_tables(n_lat)
    lat_tiles = lat_rows // ROW_TILE
    avg = jnp.asarray(np.kron(np.eye(LANES // HEAD_DIM), np.full((HEAD_DIM, HEAD_DIM), 1.0 / HEAD_DIM)), BF16)

    out = None
    for l in range(depth):
        last = l == depth - 1
        mod = _mod_vectors(cvec, w_mod[l], b_mod[l]).reshape(8, 6, d)
        modtab = mod[block_class]

        w_main, w_gate, w_mlk, w_val = _arrange_in_proj(w_in[l])
        b_main, b_gate, b_mlk, b_val = _arrange_in_proj(b_in[l][None, :])
        p, gt, kt, vt = _in_projection(
            xs, modtab, norm1_g[l][None, :], w_main.astype(BF16), b_main,
            w_gate.T.astype(BF16), b_gate.T, w_mlk.T.astype(BF16), b_mlk.T, w_val.T.astype(BF16), b_val.T,
            cos, sin, jnp.tile(qn_g[l], 2)[None, :], jnp.tile(kn_g[l], 2)[None, :], avg, batch, n_lat)

        hf, hb = _mlstm(p, kt, gt, batch, n_lat, n_ctx)
        yb = _window_attention(p, vt, win_sink[l], batch, n_lat, n_ctx, ctx_queries=not last)
        yc = _global_attention(p, vt, batch, n_lat, n_ctx, ctx_queries=not last)

        n_rows = lat_rows if last else lat_rows + ctx_rows
        merged = _merge(xs, modtab, p, hf, hb, yb, yc, ml_norm_g[l].reshape(1, WIDTH),
                        w_br[l].astype(BF16), w_o[l].astype(BF16), n_rows, lat_tiles)
        out = _ffn(merged, modtab, norm2_g[l][None, :], w_ff1[l].astype(BF16), w_ff3[l].astype(BF16),
                   w_ff2[l].astype(BF16), final_g[None, :], n_rows, final=last)
        xs = (out,)
    return out.reshape(batch, n_lat, d)
```

```python
import functools
import math

import jax
import jax.numpy as jnp
import numpy as np
from jax import lax
from jax.experimental import pallas as pl
from jax.experimental.pallas import tpu as pltpu

F32 = jnp.float32
BF16 = jnp.bfloat16

D_MODEL = 1024
GRID_W = 64
CHUNK = 128
HEAD_DIM = 64
ROPE_THETA = 10000.0
EPS = 1e-6
ML_HEADS = 4
ML_DIM = 128
Q_HEADS = 8
KV_HEADS = 2
WIDTH = 512
N_GATE = 2 * ML_HEADS

LANES = 128
MXU_TILE = 256
ROW_BLOCK = 256
ROW_TILE = 512
VMEM_LIMIT = 56 * 1024 * 1024

C_GATE = 0
C_MLQ, C_MLV, C_MLO = 3072, 3584, 4096
C_WQ, C_WK = 4608, 5120
C_GQ, C_GK = 5376, 5888
KV_COLS = KV_HEADS * LANES
N_PROJ = C_GK + KV_COLS
Q_TILE = 256
VT_GLOBAL, VT_WINDOW = 0, 1
LOG2E = 1.4426950408889634
WIN_LEAD = 3
KEY_CHUNK = 512
MOD_TILE = 1536
N_SCORE_BUFS = 4
SPAN_CHUNKS = 4
SAFE_LOG2_BOUND = 50.0

NT_DIMS = (((1,), (1,)), ((), ()))


def _params(sem, vmem=VMEM_LIMIT):
    return pltpu.CompilerParams(dimension_semantics=sem, vmem_limit_bytes=vmem)


def _resident(shape):
    nd = len(shape)
    return pl.BlockSpec(shape, lambda *_: (0,) * nd, pipeline_mode=pl.Buffered(1))


def _sigmoid(x):
    return 1.0 / (1.0 + jnp.exp2(x * -LOG2E))


def _log_sigmoid(x):
    return jnp.minimum(x, 0.0) - jnp.log(1.0 + jnp.exp(-jnp.abs(x)))


def _mod_kernel(c_ref, w_ref, b_ref, o_ref):
    c = c_ref[...]
    s = c * _sigmoid(c)
    o_ref[...] = jnp.dot(s, w_ref[...], preferred_element_type=F32) + b_ref[...]


def _mod_vectors(cvec, w_mod, b_mod):
    n_out = w_mod.shape[1]
    tn = MOD_TILE
    return pl.pallas_call(
        _mod_kernel,
        out_shape=jax.ShapeDtypeStruct((cvec.shape[0], n_out), F32),
        grid=(n_out // tn,),
        in_specs=[pl.BlockSpec(cvec.shape, lambda j: (0, 0)),
                  pl.BlockSpec((D_MODEL, tn), lambda j: (0, j)),
                  pl.BlockSpec((1, tn), lambda j: (0, j))],
        out_specs=pl.BlockSpec((cvec.shape[0], tn), lambda j: (0, j)),
        compiler_params=_params(("parallel",)),
        name="mod_vectors",
    )(cvec, w_mod, b_mod.reshape(1, n_out))


def _rope(acc, cos, sin, first_half):
    w = acc.shape[1]
    reps = w // LANES
    if reps > 1:
        cos = jnp.concatenate([cos] * reps, axis=1)
        sin = jnp.concatenate([sin] * reps, axis=1)
    ahead = pltpu.roll(acc, w - 16, axis=1)
    behind = pltpu.roll(acc, 16, axis=1)
    return acc * cos + jnp.where(first_half, ahead, behind) * sin


def _head_rms(acc, avg, gain):
    sq = (acc * acc).astype(BF16)
    outs = []
    for t in range(acc.shape[1] // LANES):
        sl = slice(t * LANES, (t + 1) * LANES)
        ms = jnp.dot(sq[:, sl], avg, preferred_element_type=F32)
        outs.append(acc[:, sl] * lax.rsqrt(ms + EPS) * gain)
    return jnp.concatenate(outs, axis=1)


def _inproj_kernel(*refs, lat_tiles):
    n_x = 1 if lat_tiles is None else 2
    x_refs = refs[:n_x]
    (mod_ref, g1_ref, w_ref, b_ref, wgt_ref, bgt_ref, wkt_ref, bkt_ref, wvt_ref, bvt_ref,
     cos_ref, sin_ref, qg_ref, kg_ref, avg_ref, p_ref, gt_ref, kt_ref, vt_ref, h_ref) = refs[n_x:]
    tm = h_ref.shape[0]

    def modulated_norm(x_ref):
        for sb in range(tm // ROW_BLOCK):
            r0 = sb * ROW_BLOCK
            xs = x_ref[r0:r0 + ROW_BLOCK, :]
            ms = jnp.mean(xs * xs, axis=-1, keepdims=True)
            y = xs * lax.rsqrt(ms + EPS) * g1_ref[...]
            shift = mod_ref[sb, 0:1, :]
            scale = mod_ref[sb, 1:2, :]
            h_ref[r0:r0 + ROW_BLOCK, :] = (y * (1.0 + scale) + shift).astype(BF16)

    if lat_tiles is None:
        modulated_norm(x_refs[0])
    else:
        pl.when(pl.program_id(0) < lat_tiles)(lambda: modulated_norm(x_refs[0]))
        pl.when(pl.program_id(0) >= lat_tiles)(lambda: modulated_norm(x_refs[1]))

    def first_half(width):
        return (lax.broadcasted_iota(jnp.int32, (1, width), 1) % 32) < 16

    avg = avg_ref[...]
    qg = qg_ref[...]
    kg = kg_ref[...]
    q_scale = HEAD_DIM ** -0.5

    h = h_ref[...]
    cos = cos_ref[...]
    sin = sin_ref[...]

    def proj(c0, width):
        return (jnp.dot(h, w_ref[:, c0:c0 + width], preferred_element_type=F32)
                + b_ref[:, c0:c0 + width])

    def store_plain(cols):
        for c0 in cols:
            p_ref[:, c0:c0 + WIDTH] = proj(c0, WIDTH).astype(BF16)

    plain = list(range(C_GATE, C_GATE + 3 * D_MODEL, WIDTH)) + [C_MLQ, C_MLV, C_MLO]
    raw_gq, raw_gk = proj(C_GQ, WIDTH), proj(C_GK, KV_COLS)
    raw_wq, raw_wk = proj(C_WQ, WIDTH), proj(C_WK, KV_COLS)
    store_plain(plain[:3])
    gq = _rope(_head_rms(raw_gq, avg, qg), cos, sin, first_half(WIDTH)) * (q_scale * LOG2E)
    p_ref[:, C_GQ:C_GQ + WIDTH] = gq.astype(BF16)
    gk = _rope(_head_rms(raw_gk, avg, kg), cos, sin, first_half(KV_COLS))
    p_ref[:, C_GK:C_GK + KV_COLS] = gk.astype(BF16)
    p_ref[:, C_WQ:C_WQ + WIDTH] = (_rope(raw_wq, cos, sin, first_half(WIDTH)) * (q_scale * LOG2E)).astype(BF16)
    p_ref[:, C_WK:C_WK + KV_COLS] = _rope(raw_wk, cos, sin, first_half(KV_COLS)).astype(BF16)
    store_plain(plain[3:])

    wt = jnp.concatenate([wgt_ref[...], wkt_ref[...], wvt_ref[...]], axis=0)
    bt = jnp.concatenate([bgt_ref[...], bkt_ref[...], bvt_ref[...]], axis=0)
    pt = lax.dot_general(wt, h, NT_DIMS, preferred_element_type=F32) + bt
    n_g, n_k = 2 * N_GATE, 2 * N_GATE + WIDTH
    grow = lax.broadcasted_iota(jnp.int32, (n_g, 1), 0)
    gt_ref[...] = jnp.where(grow >= N_GATE, _log_sigmoid(pt[:n_g, :]), pt[:n_g, :])
    kt_ref[...] = pt[n_g:n_k, :].astype(BF16)
    vt_ref[...] = pt[n_k:, :].astype(BF16)


def _token_tile_specs(xs, tm, lat_tiles):
    if len(xs) == 1:
        return [pl.BlockSpec((tm, D_MODEL), lambda i: (i, 0))]
    ctx_tiles = xs[1].shape[0] // tm
    return [pl.BlockSpec((tm, D_MODEL), lambda i: (jnp.minimum(i, lat_tiles - 1), 0)),
            pl.BlockSpec((tm, D_MODEL), lambda i: (jnp.clip(i - lat_tiles, 0, ctx_tiles - 1), 0))]


def _in_projection(xs, modtab, g1, w, b, wgt, bgt, wkt, bkt, wvt, bvt, cos, sin, qg, kg, avg, batch, n_lat):
    rows = sum(x.shape[0] for x in xs)
    tm = ROW_TILE
    nb = tm // ROW_BLOCK
    kv = 2 * KV_HEADS * HEAD_DIM
    seq_tiles = n_lat // tm
    lat_tiles = batch * seq_tiles

    def rope_block(i):
        return (jnp.where(i < lat_tiles, i % seq_tiles, seq_tiles), 0)

    return pl.pallas_call(
        functools.partial(_inproj_kernel, lat_tiles=lat_tiles if len(xs) == 2 else None),
        out_shape=(jax.ShapeDtypeStruct((rows, N_PROJ), BF16),
                   jax.ShapeDtypeStruct((2 * N_GATE, rows), F32),
                   jax.ShapeDtypeStruct((WIDTH, rows), BF16),
                   jax.ShapeDtypeStruct((kv, rows), BF16)),
        grid=(rows // tm,),
        in_specs=_token_tile_specs(xs, tm, lat_tiles) + [
                  pl.BlockSpec((nb, 6, D_MODEL), lambda i: (i, 0, 0)),
                  _resident((1, D_MODEL)),
                  _resident((D_MODEL, N_PROJ)),
                  _resident((1, N_PROJ)),
                  _resident((2 * N_GATE, D_MODEL)),
                  _resident((2 * N_GATE, 1)),
                  _resident((WIDTH, D_MODEL)),
                  _resident((WIDTH, 1)),
                  _resident((kv, D_MODEL)),
                  _resident((kv, 1)),
                  pl.BlockSpec((tm, LANES), rope_block),
                  pl.BlockSpec((tm, LANES), rope_block),
                  _resident((1, LANES)),
                  _resident((1, LANES)),
                  _resident((LANES, LANES))],
        out_specs=(pl.BlockSpec((tm, N_PROJ), lambda i: (i, 0)),
                   pl.BlockSpec((2 * N_GATE, tm), lambda i: (0, i)),
                   pl.BlockSpec((WIDTH, tm), lambda i: (0, i)),
                   pl.BlockSpec((kv, tm), lambda i: (0, i))),
        scratch_shapes=[pltpu.VMEM((tm, D_MODEL), BF16)],
        compiler_params=_params(("parallel",)),
        name="in_projection",
    )(*xs, modtab, g1, w, b, wgt, bgt, wkt, bkt, wvt, bvt, cos, sin, qg, kg, avg)


def _scan_lanes(x, op, fill, reverse):
    n = x.shape[1]
    lane = lax.broadcasted_iota(jnp.int32, x.shape, 1)
    sh = 1
    while sh < n:
        if reverse:
            moved = jnp.where(lane < n - sh, pltpu.roll(x, n - sh, axis=1), fill)
        else:
            moved = jnp.where(lane >= sh, pltpu.roll(x, sh, axis=1), fill)
        x = op(x, moved)
        sh *= 2
    return x


def _mlstm_kernel(qvf_ref, ktf_ref, gtf_ref, qvb_ref, ktb_ref, gtb_ref, hf_ref, hb_ref, cn_ref, m_ref):
    @pl.when(pl.program_id(1) == 0)
    def _():
        cn_ref[...] = jnp.zeros_like(cn_ref)
        m_ref[...] = jnp.zeros_like(m_ref)

    L = CHUNK
    row = lax.broadcasted_iota(jnp.int32, (L, L), 0)
    col = lax.broadcasted_iota(jnp.int32, (L, L), 1)
    scale = ML_DIM ** -0.5
    ones_v = jnp.ones((L, ML_DIM), BF16)
    pending = []
    for d, (qv_ref, kt_ref, gt_ref, h_ref) in enumerate(((qvf_ref, ktf_ref, gtf_ref, hf_ref),
                                                          (qvb_ref, ktb_ref, gtb_ref, hb_ref))):
        seen = (col <= row) if d == 0 else (col >= row)
        g0 = d * ML_HEADS
        ic = gt_ref[g0:g0 + ML_HEADS, :]
        lf = gt_ref[N_GATE + g0:N_GATE + g0 + ML_HEADS, :]
        r = ic - _scan_lanes(lf, jnp.add, 0.0, d == 1)
        r_max = jnp.max(r, axis=1, keepdims=True)
        b_end = jnp.sum(lf, axis=1, keepdims=True)
        for hh in range(ML_HEADS):
            idx = g0 + hh
            lanes = slice(hh * ML_DIM, (hh + 1) * ML_DIM)
            r_row = r[hh:hh + 1, :]
            m_old = m_ref[idx]
            m_end = jnp.maximum(m_old, r_max[hh:hh + 1, :])
            q = qv_ref[:, lanes]
            vo = jnp.concatenate([qv_ref[:, WIDTH + hh * ML_DIM:WIDTH + (hh + 1) * ML_DIM], ones_v], axis=1)
            kt = kt_ref[lanes, :]
            qsb = (q.astype(F32) * scale).astype(BF16)
            cn = cn_ref[idx]
            s_raw = jnp.dot(qsb, kt, preferred_element_type=F32)
            q_cn = jnp.dot(qsb, cn.astype(BF16), preferred_element_type=F32)
            kwt = (kt.astype(F32) * jnp.exp(r_row - m_end)).astype(BF16)
            cn_ref[idx] = jnp.exp(m_old - m_end) * cn + jnp.dot(kwt, vo, preferred_element_type=F32)
            m_ref[idx] = b_end[hh:hh + 1, :] + m_end
            pending.append((h_ref, lanes, seen, r_row, lf[hh:hh + 1, :], m_old, s_raw, q_cn, vo))

    for h_ref, lanes, seen, r_row, lf_row, m_old, s_raw, q_cn, vo in pending:
        b_col = jnp.sum(jnp.where(seen, lf_row, 0.0), axis=1, keepdims=True)
        m_col = jnp.maximum(m_old, jnp.max(jnp.where(seen, r_row, -jnp.inf), axis=1, keepdims=True))
        w = jnp.exp(jnp.where(seen, r_row - m_col, -jnp.inf))
        a = jnp.exp(m_old - m_col)
        s_vo = jnp.dot((s_raw * w).astype(BF16), vo, preferred_element_type=F32)
        num = a * q_cn[:, :ML_DIM] + s_vo[:, :ML_DIM]
        den = a * q_cn[:, ML_DIM:] + s_vo[:, ML_DIM:]
        hc = num / jnp.maximum(jnp.abs(den), jnp.exp(-(b_col + m_col)))
        h_ref[:, lanes] = hc.astype(h_ref.dtype)


def _mlstm(p, kt, gt, batch, n_lat, n_ctx):
    rows = p.shape[0]
    cl, cc = n_lat // CHUNK, n_ctx // CHUNK
    lat_base, ctx_base = 0, batch * cl

    def fwd_chunk(b, i):
        return jnp.where(i < cc, ctx_base + b * cc + i, lat_base + b * cl + (i - cc))

    def bwd_chunk(b, i):
        return jnp.where(i < cc, ctx_base + b * cc + (cc - 1 - i), lat_base + b * cl + (cl - 1 - (i - cc)))

    def specs(chunk):
        return [pl.BlockSpec((CHUNK, 2 * WIDTH), lambda b, i: (chunk(b, i), C_MLQ // (2 * WIDTH))),
                pl.BlockSpec((WIDTH, CHUNK), lambda b, i: (0, chunk(b, i))),
                pl.BlockSpec((2 * N_GATE, CHUNK), lambda b, i: (0, chunk(b, i)))]

    n_state = 2 * ML_HEADS
    return pl.pallas_call(
        _mlstm_kernel,
        out_shape=(jax.ShapeDtypeStruct((rows, WIDTH), BF16), jax.ShapeDtypeStruct((rows, WIDTH), BF16)),
        grid=(batch, cl + cc),
        in_specs=specs(fwd_chunk) + specs(bwd_chunk),
        out_specs=(pl.BlockSpec((CHUNK, WIDTH), lambda b, i: (fwd_chunk(b, i), 0)),
                   pl.BlockSpec((CHUNK, WIDTH), lambda b, i: (bwd_chunk(b, i), 0))),
        scratch_shapes=[pltpu.VMEM((n_state, ML_DIM, 2 * ML_DIM), F32),
                        pltpu.VMEM((n_state, 1, 1), F32)],
        compiler_params=_params(("parallel", "arbitrary")),
        name="mlstm_scan",
    )(p, kt, gt, p, kt, gt)


def _half_mask(e):
    lane = lax.broadcasted_iota(jnp.int32, (1, LANES), 1)
    return (lane < HEAD_DIM) if e == 0 else (lane >= HEAD_DIM)


def _win_kernel(sink_ref, q_ref, kc_ref, vtc_ref, k0_ref, k1_ref, k2_ref, k3_ref,
                vt0_ref, vt1_ref, vt2_ref, vt3_ref, o_ref, *, n_lat_tiles, n_lat_blocks):
    t = pl.program_id(1)
    tq = q_ref.shape[0]
    n_ctx = kc_ref.shape[0]
    is_lat = t < n_lat_tiles
    kk = lax.broadcasted_iota(jnp.int32, (CHUNK, tq), 0)
    qq = lax.broadcasted_iota(jnp.int32, (CHUNK, tq), 1)
    band = (kk >= qq, qq <= kk + CHUNK, kk <= qq, kk + CHUNK <= qq)
    first = 2 * t - 1
    present = [jnp.logical_and(is_lat, jnp.logical_and(first + i >= 0, first + i < n_lat_blocks))
               for i in range(4)]
    visible = [jnp.logical_and(band[i], present[i]) for i in range(4)]
    k_refs = (k0_ref, k1_ref, k2_ref, k3_ref)
    vt_refs = (vt0_ref, vt1_ref, vt2_ref, vt3_ref)
    ones = jnp.ones((16, n_ctx + 4 * CHUNK), BF16)

    def group_operands(g):
        lanes = slice(g * LANES, (g + 1) * LANES)
        k_all = jnp.concatenate([kc_ref[:, lanes]] + [r[:, lanes] for r in k_refs], axis=0)
        vrows = slice(g * HEAD_DIM, (g + 1) * HEAD_DIM)
        vta = jnp.concatenate(
            [jnp.concatenate([vtc_ref[vrows, :]] + [r[vrows, :] for r in vt_refs], axis=1), ones], axis=0)
        return k_all, vta

    operands = [group_operands(g) for g in range(KV_HEADS)]

    def score(h):
        qf = q_ref[:, (h // 2) * LANES:(h // 2 + 1) * LANES].astype(F32)
        qt = jnp.where(_half_mask(h % 2), qf, 0.0).T.astype(BF16)
        return jnp.dot(operands[h // (Q_HEADS // KV_HEADS)][0], qt, preferred_element_type=F32)

    scores = {h: score(h) for h in range(WIN_LEAD)}
    outs = []
    for h in range(Q_HEADS):
        if h + WIN_LEAD < Q_HEADS:
            scores[h + WIN_LEAD] = score(h + WIN_LEAD)
        s = scores.pop(h)
        sink = sink_ref[h] * LOG2E
        parts = [s[:n_ctx, :]]
        for i in range(4):
            blk = s[n_ctx + i * CHUNK:n_ctx + (i + 1) * CHUNK, :]
            parts.append(jnp.where(visible[i], blk, -jnp.inf))
        s = jnp.concatenate(parts, axis=0)
        m = jnp.maximum(jnp.max(s, axis=0, keepdims=True), sink)
        pt = jnp.exp2(s - m).astype(BF16)
        ol = jnp.dot(operands[h // (Q_HEADS // KV_HEADS)][1], pt, preferred_element_type=F32)
        den = ol[HEAD_DIM:HEAD_DIM + 1, :] + jnp.exp2(sink - m)
        outs.append(ol[:HEAD_DIM, :] / den)
    for tt in range(Q_HEADS // 2):
        o_ref[:, tt * LANES:(tt + 1) * LANES] = (
            jnp.concatenate(outs[2 * tt:2 * tt + 2], axis=0).T.astype(o_ref.dtype))


def _window_attention(p, vt, sink, batch, n_lat, n_ctx, ctx_queries):
    tq = Q_TILE
    nlt, nct = n_lat // tq, n_ctx // tq
    nlb = n_lat // CHUNK
    nq = nlt + (nct if ctx_queries else 0)
    out_rows = batch * (n_lat + (n_ctx if ctx_queries else 0))
    ctx_blk = batch * n_lat // n_ctx
    k_col = C_WK // KV_COLS

    def q_block(b, t):
        return jnp.where(t < nlt, b * nlt + t, batch * nlt + b * nct + (t - nlt))

    def near(i):
        return lambda b, t: b * nlb + jnp.clip(2 * t - 1 + i, 0, nlb - 1)

    k_specs = [pl.BlockSpec((CHUNK, KV_COLS), (lambda f: lambda b, t: (f(b, t), k_col))(near(i))) for i in range(4)]
    vt_specs = [pl.BlockSpec((LANES, CHUNK), (lambda f: lambda b, t: (VT_WINDOW, f(b, t)))(near(i)))
                for i in range(4)]
    return pl.pallas_call(
        functools.partial(_win_kernel, n_lat_tiles=nlt, n_lat_blocks=nlb),
        out_shape=jax.ShapeDtypeStruct((out_rows, WIDTH), BF16),
        grid=(batch, nq),
        in_specs=[pl.BlockSpec(memory_space=pltpu.SMEM),
                  pl.BlockSpec((tq, WIDTH), lambda b, t: (q_block(b, t), C_WQ // WIDTH)),
                  pl.BlockSpec((n_ctx, KV_COLS), lambda b, t: (ctx_blk + b, k_col)),
                  pl.BlockSpec((LANES, n_ctx), lambda b, t: (VT_WINDOW, ctx_blk + b))]
                 + k_specs + vt_specs,
        out_specs=pl.BlockSpec((tq, WIDTH), lambda b, t: (q_block(b, t), 0)),
        compiler_params=_params(("parallel", "parallel")),
        name="window_attention",
    )(sink, p, p, vt, p, p, p, p, vt, vt, vt, vt)


def _glb_kernel(q_ref, kc_ref, vtc_ref, kl_ref, vtl_ref, o_ref, qt_ref, knorm_ref, *scratch, n_lat_tiles, tk, span_chunks):
    heads = Q_HEADS // KV_HEADS
    acc_refs = scratch[0:heads]
    sc_refs = scratch[heads:2 * heads]
    ring = [scratch[(2 + i) * heads:(3 + i) * heads] for i in range(N_SCORE_BUFS)]
    base = (2 + N_SCORE_BUFS) * heads
    pc_refs = scratch[base:base + heads]
    p_ring = scratch[base + heads:base + heads + 2]
    qi = pl.program_id(2)
    tq = q_ref.shape[0]
    n_chunks = kl_ref.shape[0] // tk

    def k_chunk(c):
        off = pl.multiple_of(jnp.minimum(c, n_chunks - 1) * tk, tk)
        return kl_ref[pl.ds(off, tk), :]

    def vt_chunk(c):
        return vtl_ref[:, pl.ds(pl.multiple_of(c * tk, tk), tk)]

    def with_ones(vt):
        ones = jnp.ones((acc_refs[0].shape[0] - HEAD_DIM, vt.shape[1]), BF16)
        return jnp.concatenate([vt, ones], axis=0)

    @pl.when(qi == 0)
    def _():
        def sq_norm(k):
            kf = k.astype(F32)
            return jnp.max(jnp.sum(kf * kf, axis=1, keepdims=True), axis=0, keepdims=True)

        def body(c, best):
            return jnp.maximum(best, sq_norm(k_chunk(c)))
        best = lax.fori_loop(0, n_chunks, body, sq_norm(kc_ref[...]))
        knorm_ref[...] = jnp.sqrt(0.5 * best)

    for t in range(heads // 2):
        qf = q_ref[:, t * LANES:(t + 1) * LANES].astype(F32)
        for e in range(2):
            h = 2 * t + e
            qt_ref[:, h * tq:(h + 1) * tq] = jnp.where(_half_mask(e), qf, 0.0).T.astype(BF16)
    for h in range(heads):
        acc_refs[h][...] = jnp.zeros_like(acc_refs[h])

    def q_t(h):
        return qt_ref[:, h * tq:(h + 1) * tq]

    bound = [jnp.sqrt(jnp.sum(jnp.square(q_t(h).astype(F32)), axis=0, keepdims=True)) * knorm_ref[...]
             for h in range(heads)]
    bounded = jnp.max(functools.reduce(jnp.maximum, bound)) <= SAFE_LOG2_BOUND

    span = span_chunks * tk
    n_spans = n_chunks // span_chunks
    n_steps = n_spans * heads

    def produce_ctx():
        for h in range(heads):
            s = jnp.dot(kc_ref[...], q_t(h), preferred_element_type=F32)
            pc_refs[h][...] = jnp.exp2(s - bound[h]).astype(BF16)

    def consume_ctx():
        vta = with_ones(vtc_ref[...])
        for h in range(heads):
            acc_refs[h][...] += jnp.dot(vta, pc_refs[h][...], preferred_element_type=F32)

    def produce(p_ref, sp, h):
        k = kl_ref[pl.ds(pl.multiple_of(sp * span, span), span), :]
        s = jnp.dot(k, q_t(h), preferred_element_type=F32)
        p_ref[...] = jnp.exp2(s - bound[h]).astype(BF16)

    def consume(p_ref, sp, h):
        vta = with_ones(vtl_ref[:, pl.ds(pl.multiple_of(sp * span, span), span)])
        acc_refs[h][...] += jnp.dot(vta, p_ref[...], preferred_element_type=F32)

    @pl.when(bounded)
    def _():
        produce_ctx()

        @pl.when(qi >= n_lat_tiles)
        def _():
            consume_ctx()

        @pl.when(qi < n_lat_tiles)
        def _():
            produce(p_ring[0], 0, 0)
            consume_ctx()

            unroll = heads * (2 if n_spans % 2 == 0 else 1)

            def steps(j, last):
                for i in range(unroll):
                    step = unroll * j + i
                    if not (last and i == unroll - 1):
                        produce(p_ring[(i + 1) % 2], (step + 1) // heads, (i + 1) % heads)
                    consume(p_ring[i % 2], step // heads, i % heads)

            def body(j, carry):
                steps(j, False)
                return carry
            n_iter = n_steps // unroll
            lax.fori_loop(0, n_iter - 1, body, 0)
            steps(n_iter - 1, True)

    neg_inf = (jnp.full((1, tq), -jnp.inf, F32),) * heads

    def fill(s_refs, k, m_run):
        out = []
        for h in range(heads):
            s = jnp.dot(k, q_t(h), preferred_element_type=F32)
            s_refs[h][...] = s
            out.append(jnp.maximum(m_run[h], jnp.max(s, axis=0, keepdims=True)))
        return tuple(out)

    def drain(s_refs, vt, m_run, m_acc):
        vta = with_ones(vt)
        for h in range(heads):
            alpha = jnp.exp2(m_acc[h] - m_run[h])
            pt = jnp.exp2(s_refs[h][...] - m_run[h]).astype(BF16)
            acc_refs[h][...] = alpha * acc_refs[h][...] + jnp.dot(vta, pt, preferred_element_type=F32)
        return m_run

    @pl.when(jnp.logical_not(bounded))
    def _():
        m_ctx = fill(sc_refs, kc_ref[...], neg_inf)

        @pl.when(qi >= n_lat_tiles)
        def _():
            drain(sc_refs, vtc_ref[...], m_ctx, neg_inf)

        @pl.when(qi < n_lat_tiles)
        def _():
            m_one = fill(ring[0], k_chunk(0), m_ctx)
            m_two = fill(ring[1], k_chunk(1), m_one)
            m_acc = drain(sc_refs, vtc_ref[...], m_one, neg_inf)

            def body(j, carry):
                m_run, m_acc = carry
                for i in range(N_SCORE_BUFS):
                    c = N_SCORE_BUFS * j + i
                    m_next = fill(ring[(i + 2) % N_SCORE_BUFS], k_chunk(c + 2), m_run)
                    m_acc = drain(ring[i], vt_chunk(c), m_run, m_acc)
                    m_run = m_next
                return m_run, m_acc
            lax.fori_loop(0, n_chunks // N_SCORE_BUFS, body, (m_two, m_acc))

    for t in range(heads // 2):
        pair = []
        for e in range(2):
            a = acc_refs[2 * t + e][...]
            pair.append(a[:HEAD_DIM, :] / a[HEAD_DIM:HEAD_DIM + 1, :])
        o_ref[:, t * LANES:(t + 1) * LANES] = jnp.concatenate(pair, axis=0).T.astype(o_ref.dtype)


def _global_attention(p, vt, batch, n_lat, n_ctx, ctx_queries):
    tq = Q_TILE
    tk = min(KEY_CHUNK, n_lat // N_SCORE_BUFS)
    assert n_lat % (tk * N_SCORE_BUFS) == 0 and tk % LANES == 0
    span_chunks = math.gcd(SPAN_CHUNKS, n_lat // tk)
    nl, nc = n_lat // tq, n_ctx // tq
    nq = nl + (nc if ctx_queries else 0)
    out_rows = batch * (n_lat + (n_ctx if ctx_queries else 0))
    ctx_base = batch * nl
    heads = Q_HEADS // KV_HEADS
    acc_rows = HEAD_DIM + 16

    def q_block(b, qi):
        return jnp.where(qi < nl, b * nl + qi, ctx_base + b * nc + (qi - nl))

    ctx_blk = batch * n_lat // n_ctx
    return pl.pallas_call(
        functools.partial(_glb_kernel, n_lat_tiles=nl, tk=tk, span_chunks=span_chunks),
        out_shape=jax.ShapeDtypeStruct((out_rows, WIDTH), BF16),
        grid=(batch, KV_HEADS, nq),
        in_specs=[pl.BlockSpec((tq, 2 * LANES), lambda b, g, qi: (q_block(b, qi), C_GQ // (2 * LANES) + g)),
                  pl.BlockSpec((n_ctx, LANES), lambda b, g, qi: (ctx_blk + b, C_GK // LANES + g)),
                  pl.BlockSpec((HEAD_DIM, n_ctx), lambda b, g, qi: (KV_HEADS * VT_GLOBAL + g, ctx_blk + b)),
                  pl.BlockSpec((n_lat, LANES), lambda b, g, qi: (b, C_GK // LANES + g),
                               pipeline_mode=pl.Buffered(1)),
                  pl.BlockSpec((HEAD_DIM, n_lat), lambda b, g, qi: (KV_HEADS * VT_GLOBAL + g, b),
                               pipeline_mode=pl.Buffered(1))],
        out_specs=pl.BlockSpec((tq, 2 * LANES), lambda b, g, qi: (q_block(b, qi), g)),
        scratch_shapes=([pltpu.VMEM((LANES, heads * tq), BF16), pltpu.VMEM((1, 1), F32)]
                        + [pltpu.VMEM((acc_rows, tq), F32)] * heads
                        + [pltpu.VMEM((n_ctx, tq), F32)] * heads
                        + [pltpu.VMEM((tk, tq), F32)] * (N_SCORE_BUFS * heads)
                        + [pltpu.VMEM((n_ctx, tq), BF16)] * heads
                        + [pltpu.VMEM((span_chunks * tk, tq), BF16)] * 2),
        compiler_params=_params(("parallel", "parallel", "arbitrary")),
        name="global_attention",
    )(p, p, vt, p, vt)


def _merge_kernel(*refs, lat_tiles):
    n_x = 1 if lat_tiles is None else 2
    x_refs = refs[:n_x]
    (mod_ref, hf_ref, hb_ref, og_ref, yb_ref, yc_ref, ga_ref, gb_ref, gc_ref,
     mlg_ref, wbr_ref, wo_ref, o_ref) = refs[n_x:]
    tm = o_ref.shape[0]
    proj_b = jnp.dot(yb_ref[...], wbr_ref[1], preferred_element_type=F32)
    proj_c = jnp.dot(yc_ref[...], wbr_ref[2], preferred_element_type=F32)
    hs = hf_ref[...].astype(F32) + hb_ref[...].astype(F32)
    parts = []
    for t in range(ML_HEADS):
        ht = hs[:, t * ML_DIM:(t + 1) * ML_DIM]
        ms = jnp.mean(ht * ht, axis=-1, keepdims=True)
        parts.append(ht * lax.rsqrt(ms + EPS))
    ya = (jnp.concatenate(parts, axis=1) * mlg_ref[...] * _sigmoid(og_ref[...].astype(F32))).astype(BF16)
    proj_a = jnp.dot(ya, wbr_ref[0], preferred_element_type=F32)
    merged = (_sigmoid(gb_ref[...].astype(F32)) * proj_b + _sigmoid(gc_ref[...].astype(F32)) * proj_c
              + _sigmoid(ga_ref[...].astype(F32)) * proj_a)
    out = jnp.dot(merged.astype(BF16), wo_ref[...], preferred_element_type=F32)

    def gated_residual(x_ref):
        for sb in range(tm // ROW_BLOCK):
            rows = slice(sb * ROW_BLOCK, (sb + 1) * ROW_BLOCK)
            o_ref[rows, :] = x_ref[rows, :] + mod_ref[sb, 2:3, :] * out[rows, :]

    if lat_tiles is None:
        gated_residual(x_refs[0])
    else:
        pl.when(pl.program_id(0) < lat_tiles)(lambda: gated_residual(x_refs[0]))
        pl.when(pl.program_id(0) >= lat_tiles)(lambda: gated_residual(x_refs[1]))


def _merge(xs, modtab, p, hf, hb, yb, yc, mlg, wbr, wo, n_rows, lat_tiles):
    tm = ROW_TILE
    nb = tm // ROW_BLOCK
    rows = sum(x.shape[0] for x in xs)
    row = lambda i: (i, 0)
    gate = lambda k: pl.BlockSpec((tm, D_MODEL), lambda i: (i, C_GATE // D_MODEL + k))
    return pl.pallas_call(
        functools.partial(_merge_kernel, lat_tiles=lat_tiles if len(xs) == 2 else None),
        out_shape=jax.ShapeDtypeStruct((rows, D_MODEL), F32),
        grid=(n_rows // tm,),
        in_specs=_token_tile_specs(xs, tm, lat_tiles) + [
                  pl.BlockSpec((nb, 6, D_MODEL), lambda i: (i, 0, 0)),
                  pl.BlockSpec((tm, WIDTH), row),
                  pl.BlockSpec((tm, WIDTH), row),
                  pl.BlockSpec((tm, WIDTH), lambda i: (i, C_MLO // WIDTH)),
                  pl.BlockSpec((tm, WIDTH), row),
                  pl.BlockSpec((tm, WIDTH), row),
                  gate(0), gate(1), gate(2),
                  _resident((1, WIDTH)),
                  _resident((3, WIDTH, D_MODEL)),
                  _resident((D_MODEL, D_MODEL))],
        out_specs=pl.BlockSpec((tm, D_MODEL), row),
        input_output_aliases={0: 0} if len(xs) == 1 else {},
        compiler_params=_params(("parallel",)),
        name="merge",
    )(*xs, modtab, hf, hb, p, yb, yc, p, p, p, mlg, wbr, wo)


def _ffn_kernel(x_ref, mod_ref, g2_ref, w1_ref, w3_ref, w2_ref, gf_ref, o_ref, *, final):
    tm = x_ref.shape[0]
    d_ff = w1_ref.shape[1]
    split = pl.cdiv(d_ff // MXU_TILE, 2) * MXU_TILE
    for sb in range(tm // ROW_BLOCK):
        rows = slice(sb * ROW_BLOCK, (sb + 1) * ROW_BLOCK)
        xs = x_ref[rows, :]
        ms = jnp.mean(xs * xs, axis=-1, keepdims=True)
        y = xs * lax.rsqrt(ms + EPS) * g2_ref[...]
        h = (y * (1.0 + mod_ref[sb, 4:5, :]) + mod_ref[sb, 3:4, :]).astype(BF16)
        out = None
        for c0, c1 in ((0, split), (split, d_ff)):
            a = jnp.dot(h, w1_ref[:, c0:c1], preferred_element_type=F32)
            b = jnp.dot(h, w3_ref[:, c0:c1], preferred_element_type=F32)
            z = (a * _sigmoid(a) * b).astype(BF16)
            part = jnp.dot(z, w2_ref[c0:c1, :], preferred_element_type=F32)
            out = part if out is None else out + part
        xn = xs + mod_ref[sb, 5:6, :] * out
        if final:
            ms = jnp.mean(xn * xn, axis=-1, keepdims=True)
            xn = xn * lax.rsqrt(ms + EPS) * gf_ref[...]
        o_ref[rows, :] = xn


def _ffn(x, modtab, g2, w1, w3, w2, gfin, n_rows, final):
    tm = ROW_TILE
    nb = tm // ROW_BLOCK
    d_ff = w1.shape[1]
    row = lambda i: (i, 0)
    out_rows = n_rows if final else x.shape[0]
    return pl.pallas_call(
        functools.partial(_ffn_kernel, final=final),
        out_shape=jax.ShapeDtypeStruct((out_rows, D_MODEL), F32),
        grid=(n_rows // tm,),
        in_specs=[pl.BlockSpec((tm, D_MODEL), row),
                  pl.BlockSpec((nb, 6, D_MODEL), lambda i: (i, 0, 0)),
                  _resident((1, D_MODEL)),
                  _resident((D_MODEL, d_ff)),
                  _resident((D_MODEL, d_ff)),
                  _resident((d_ff, D_MODEL)),
                  _resident((1, D_MODEL))],
        out_specs=pl.BlockSpec((tm, D_MODEL), row),
        input_output_aliases={} if final else {0: 0},
        compiler_params=_params(("parallel",)),
        name="ffn",
    )(x, modtab, g2, w1, w3, w2, gfin)


def _dup_halves(w, base):
    h0 = w[..., base:base + HEAD_DIM]
    h1 = w[..., base + HEAD_DIM:base + 2 * HEAD_DIM]
    return [h0, h0, h1, h1]


def _arrange_in_proj(w):
    o_gate_ml = 4 * WIDTH
    o_wq = o_gate_ml + 2 * N_GATE
    o_wk, o_wv = o_wq + WIDTH, o_wq + WIDTH + 128
    o_gq = o_wv + 128
    o_gk, o_gv = o_gq + WIDTH, o_gq + WIDTH + 128
    o_gate = o_gv + 128
    main = jnp.concatenate(
        [w[..., o_gate:o_gate + 3 * D_MODEL], w[..., 0:WIDTH], w[..., 2 * WIDTH:4 * WIDTH],
         w[..., o_wq:o_wq + WIDTH]]
        + _dup_halves(w, o_wk) + [w[..., o_gq:o_gq + WIDTH]] + _dup_halves(w, o_gk), axis=-1)
    values = jnp.concatenate([w[..., o_gv:o_gv + 128], w[..., o_wv:o_wv + 128]], axis=-1)
    return main, w[..., o_gate_ml:o_gate_ml + 2 * N_GATE], w[..., WIDTH:2 * WIDTH], values


def _rope_tables(n_lat):
    t = jnp.arange(n_lat)
    quarter = HEAD_DIM // 4
    inv = ROPE_THETA ** (-jnp.arange(0, 2 * quarter, 2, dtype=F32) / (2 * quarter))
    ang_r = (t // GRID_W).astype(F32)[:, None] * inv
    ang_c = (t % GRID_W).astype(F32)[:, None] * inv
    cos = jnp.concatenate([jnp.cos(ang_r)] * 2 + [jnp.cos(ang_c)] * 2, axis=1)
    sin = jnp.concatenate([-jnp.sin(ang_r), jnp.sin(ang_r), -jnp.sin(ang_c), jnp.sin(ang_c)], axis=1)
    reps = LANES // HEAD_DIM
    return (jnp.concatenate([jnp.tile(cos, (1, reps)), jnp.ones((ROW_TILE, LANES), F32)], axis=0),
            jnp.concatenate([jnp.tile(sin, (1, reps)), jnp.zeros((ROW_TILE, LANES), F32)], axis=0))


def kernel(x, c, ctx, c_ctx, w_mod, b_mod, norm1_g, w_in, b_in, ml_norm_g, win_sink, qn_g, kn_g,
           w_br, w_o, norm2_g, w_ff1, w_ff3, w_ff2, final_g):
    batch, n_lat, d = x.shape
    n_ctx = ctx.shape[1]
    depth = w_mod.shape[0]
    assert d == D_MODEL and n_lat % ROW_TILE == 0 and n_lat % GRID_W == 0
    assert (batch * n_ctx) % ROW_TILE == 0 and n_ctx % ROW_BLOCK == 0 and (batch * n_lat) % n_ctx == 0
    lat_rows, ctx_rows = batch * n_lat, batch * n_ctx

    xs = (x.reshape(lat_rows, d), ctx.reshape(ctx_rows, d))
    cvec = jnp.concatenate([c, c_ctx[None, :], jnp.zeros((8 - batch - 1, d), F32)], axis=0)
    block_class = np.concatenate([np.repeat(np.arange(batch), n_lat // ROW_BLOCK),
                                  np.full(ctx_rows // ROW_BLOCK, batch)])
    cos, sin = _rope_tables(n_lat)
    lat_tiles = lat_rows // ROW_TILE
    avg = jnp.asarray(np.kron(np.eye(LANES // HEAD_DIM), np.full((HEAD_DIM, HEAD_DIM), 1.0 / HEAD_DIM)), BF16)

    out = None
    for l in range(depth):
        last = l == depth - 1
        mod = _mod_vectors(cvec, w_mod[l], b_mod[l]).reshape(8, 6, d)
        modtab = mod[block_class]

        w_main, w_gate, w_mlk, w_val = _arrange_in_proj(w_in[l])
        b_main, b_gate, b_mlk, b_val = _arrange_in_proj(b_in[l][None, :])
        p, gt, kt, vt = _in_projection(
            xs, modtab, norm1_g[l][None, :], w_main.astype(BF16), b_main,
            w_gate.T.astype(BF16), b_gate.T, w_mlk.T.astype(BF16), b_mlk.T, w_val.T.astype(BF16), b_val.T,
            cos, sin, jnp.tile(qn_g[l], 2)[None, :], jnp.tile(kn_g[l], 2)[None, :], avg, batch, n_lat)

        hf, hb = _mlstm(p, kt, gt, batch, n_lat, n_ctx)
        yb = _window_attention(p, vt, win_sink[l], batch, n_lat, n_ctx, ctx_queries=not last)
        yc = _global_attention(p, vt, batch, n_lat, n_ctx, ctx_queries=not last)

        n_rows = lat_rows if last else lat_rows + ctx_rows
        merged = _merge(xs, modtab, p, hf, hb, yb, yc, ml_norm_g[l].reshape(1, WIDTH),
                        w_br[l].astype(BF16), w_o[l].astype(BF16), n_rows, lat_tiles)
        out = _ffn(merged, modtab, norm2_g[l][None, :], w_ff1[l].astype(BF16), w_ff3[l].astype(BF16),
                   w_ff2[l].astype(BF16), final_g[None, :], n_rows, final=last)
        xs = (out,)
    return out.reshape(batch, n_lat, d)
```
